```python
import jax, jax.numpy as jnp
from jax import lax
import numpy as np

D_MODEL = 2048
BATCH = 4
SEQ = 2048
DEPTH = 2

MEM_LEN = 256
EPS = 1e-6
HG_KDIM = 128
HG_VDIM = 128
HG_HEADS = (D_MODEL // 4) // HG_VDIM
HG_CHUNK = 64
CONV_WIDTH = D_MODEL // 4
CONV_K = 3
FOX_HEAD_DIM = 128
FOX_HEADS = (D_MODEL // 2) // FOX_HEAD_DIM
FOX_BLOCK = 128
CROSS_HEADS = 4
CROSS_HEAD_DIM = D_MODEL // CROSS_HEADS
MOE_GROUPS = 4
MOE_EXPERTS_PER_GROUP = 8
MOE_TOPK = 2
MOE_HIDDEN = D_MODEL // 4

HG_WIDTH = HG_HEADS * HG_VDIM
FOX_WIDTH = FOX_HEADS * FOX_HEAD_DIM
MIX_WIDTH = HG_WIDTH + CONV_WIDTH + FOX_WIDTH
IN_SPLITS = (HG_HEADS * HG_KDIM, HG_HEADS * HG_KDIM, HG_WIDTH, HG_WIDTH,
             CONV_WIDTH, CONV_WIDTH, CONV_WIDTH,
             FOX_WIDTH, FOX_WIDTH, FOX_WIDTH, FOX_HEADS)
IN_WIDTH = sum(IN_SPLITS)

kernel_name = "hymba_hgrn2_conv_fox_hmoe"


def rmsnorm(x, g):
    xf = x.astype(jnp.float32)
    y = xf * lax.rsqrt(jnp.mean(xf * xf, axis=-1, keepdims=True) + EPS)
    return (y * g.astype(jnp.float32)).astype(x.dtype)


def split_in_proj(proj):
    offsets = []
    acc = 0
    for w in IN_SPLITS[:-1]:
        acc += w
        offsets.append(acc)
    return jnp.split(proj, offsets, axis=-1)


def hgrn_lower_bound(lb_param, layer):
    p = jax.nn.softmax(lb_param.astype(jnp.float32), axis=0)
    c = jnp.cumsum(p, axis=0)
    return c[layer] - c[0]


def hgrn2_mixer(q, f_raw, i, g, lb, norm_g):
    dt = q.dtype
    B, S, _ = q.shape
    H, dk, dv, C = HG_HEADS, HG_KDIM, HG_VDIM, HG_CHUNK
    nc = S // C
    z = f_raw.astype(jnp.float32)
    log_f = jnp.logaddexp(jnp.log(lb), jnp.log1p(-lb) + jax.nn.log_sigmoid(z))
    k = -jnp.expm1(log_f)

    def to_chunks(t, d):
        return t.astype(jnp.float32).reshape(B, nc, C, H, d).transpose(1, 0, 3, 2, 4)

    qc, kc, vc, lfc = to_chunks(q, dk), to_chunks(k, dk), to_chunks(i, dv), to_chunks(log_f, dk)
    mask = jnp.tril(jnp.ones((C, C), dtype=bool))[:, :, None]

    def step(state, inp):
        qb, kb, vb, lfb = inp
        b = jnp.cumsum(lfb, axis=2)
        diff = b[:, :, :, None, :] - b[:, :, None, :, :]
        decay = jnp.exp(jnp.where(mask, diff, -jnp.inf))
        attn = jnp.einsum('bhtd,bhsd,bhtsd->bhts', qb, kb, decay)
        o = jnp.einsum('bhts,bhsv->bhtv', attn, vb) + jnp.einsum('bhtd,bhdv->bhtv', qb * jnp.exp(b), state)
        b_last = b[:, :, -1:, :]
        state = jnp.exp(b_last[:, :, 0, :])[..., None] * state + jnp.einsum(
            'bhsd,bhsv->bhdv', kb * jnp.exp(b_last - b), vb)
        return state, o

    state0 = jnp.zeros((B, H, dk, dv), jnp.float32)
    _, o = lax.scan(step, state0, (qc, kc, vc, lfc))
    o = o.transpose(1, 0, 3, 2, 4).reshape(B, S, H, dv)
    o = rmsnorm(o, norm_g).reshape(B, S, H * dv)
    return (o * jax.nn.silu(g.astype(jnp.float32))).astype(dt)


def short_conv_mixer(b_gate, c_gate, h_in, w):
    u = c_gate * h_in
    y = lax.conv_general_dilated(
        u, w[:, None, :].astype(u.dtype), window_strides=(1,),
        padding=[(CONV_K - 1, 0)], dimension_numbers=('NWC', 'WIO', 'NWC'),
        feature_group_count=u.shape[-1])
    return b_gate * y


def fox_mixer(q, k, v, f_raw, f_bias, qn, kn):
    dt = q.dtype
    B, S, _ = q.shape
    H, dh = FOX_HEADS, FOX_HEAD_DIM
    qh = rmsnorm(q.reshape(B, S, H, dh), qn).astype(jnp.float32)
    kh = rmsnorm(k.reshape(B, S, H, dh), kn).astype(jnp.float32)
    vh = v.reshape(B, S, H, dh).astype(jnp.float32)
    log_f = jax.nn.log_sigmoid(f_raw.astype(jnp.float32) + f_bias.astype(jnp.float32))
    c = jnp.cumsum(log_f, axis=1).transpose(0, 2, 1)
    nb = S // FOX_BLOCK
    qb = qh.reshape(B, nb, FOX_BLOCK, H, dh).transpose(1, 0, 3, 2, 4)
    cq = c.reshape(B, H, nb, FOX_BLOCK).transpose(2, 0, 1, 3)
    kt = kh.transpose(0, 2, 1, 3)
    vt = vh.transpose(0, 2, 1, 3)
    kpos = jnp.arange(S)
    scale = dh ** -0.5

    def block(args):
        qi, cqi, bi = args
        qpos = bi * FOX_BLOCK + jnp.arange(FOX_BLOCK)
        logits = jnp.einsum('bhqd,bhkd->bhqk', qi, kt) * scale + cqi[..., None] - c[:, :, None, :]
        logits = jnp.where(kpos[None, :] <= qpos[:, None], logits, -jnp.inf)
        p = jax.nn.softmax(logits, axis=-1)
        return jnp.einsum('bhqk,bhkd->bhqd', p, vt)

    o = lax.map(block, (qb, cq, jnp.arange(nb)))
    return o.transpose(1, 0, 3, 2, 4).reshape(B, S, H * dh).astype(dt)


def memory_cross_attention(h, memn, wq, wk, wv, wo, qn, kn):
    B, S, D = h.shape
    M = memn.shape[1]
    q = rmsnorm((h @ wq).reshape(B, S, CROSS_HEADS, CROSS_HEAD_DIM), qn).astype(jnp.float32)
    k = rmsnorm((memn @ wk).reshape(B, M, CROSS_HEADS, CROSS_HEAD_DIM), kn).astype(jnp.float32)
    v = (memn @ wv).reshape(B, M, CROSS_HEADS, CROSS_HEAD_DIM).astype(jnp.float32)
    logits = jnp.einsum('bqhd,bmhd->bhqm', q, k) * (CROSS_HEAD_DIM ** -0.5)
    p = jax.nn.softmax(logits, axis=-1)
    o = jnp.einsum('bhqm,bmhd->bqhd', p, v).reshape(B, S, D).astype(h.dtype)
    return o @ wo


def hier_moe(x, wg, bg, we, be, w1, w3, w2):
    B, S, D = x.shape
    G, E = MOE_GROUPS, MOE_EXPERTS_PER_GROUP
    t = x.reshape(-1, D)
    T = t.shape[0]
    glog = (t @ wg + bg).astype(jnp.float32)
    gp = jax.nn.softmax(glog, axis=-1)
    gidx = jnp.argmax(glog, axis=-1)
    gw = jnp.take_along_axis(gp, gidx[:, None], axis=-1)[:, 0]
    elog = (t @ we + be).astype(jnp.float32).reshape(T, G, E)
    elog_sel = jnp.take_along_axis(elog, gidx[:, None, None], axis=1)[:, 0]
    ep = jax.nn.softmax(elog_sel, axis=-1)
    topv, topi = lax.top_k(ep, MOE_TOPK)
    topv = topv / jnp.sum(topv, axis=-1, keepdims=True)
    ew = jnp.sum(jax.nn.one_hot(topi, E, dtype=jnp.float32) * topv[..., None], axis=1)
    comb = (gw[:, None, None] * jax.nn.one_hot(gidx, G, dtype=jnp.float32)[:, :, None]
            * ew[:, None, :]).astype(x.dtype)
    y = jnp.zeros_like(t)
    for g in range(G):
        hg = jnp.einsum('td,edf->tef', t, w1[g])
        hu = jnp.einsum('td,edf->tef', t, w3[g])
        hmid = jax.nn.silu(hg) * hu * comb[:, g, :, None]
        y = y + jnp.einsum('tef,efd->td', hmid, w2[g])
    return y.reshape(B, S, D)


def setup_inputs(seed: int = 0) -> dict:
    key = jax.random.key(seed)
    ks = jax.random.split(key, 27)
    D, L = D_MODEL, DEPTH
    G, E, F = MOE_GROUPS, MOE_EXPERTS_PER_GROUP, MOE_HIDDEN
    nrm = jax.random.normal
    f32 = jnp.float32

    def gain(k, shape):
        return 1.0 + 0.02 * nrm(k, shape, f32)

    return {
        "x": nrm(ks[0], (BATCH, SEQ, D), f32),
        "mem": nrm(ks[1], (BATCH, MEM_LEN, D), f32),
        "norm_mix": gain(ks[2], (L, D)),
        "w_in": nrm(ks[3], (L, D, IN_WIDTH), f32) * D ** -0.5,
        "hg_lower_bounds": nrm(ks[4], (L, HG_HEADS * HG_KDIM), f32),
        "hg_norm": gain(ks[5], (L, HG_VDIM)),
        "conv_w": nrm(ks[6], (L, CONV_K, CONV_WIDTH), f32) * CONV_K ** -0.5,
        "fox_f_bias": 1.0 + 0.5 * nrm(ks[7], (L, FOX_HEADS), f32),
        "fox_q_norm": gain(ks[8], (L, FOX_HEAD_DIM)),
        "fox_k_norm": gain(ks[9], (L, FOX_HEAD_DIM)),
        "w_out": nrm(ks[10], (L, MIX_WIDTH, D), f32) * MIX_WIDTH ** -0.5,
        "norm_cross": gain(ks[11], (L, D)),
        "norm_mem": gain(ks[12], (L, D)),
        "w_cq": nrm(ks[13], (L, D, D), f32) * D ** -0.5,
        "w_ck": nrm(ks[14], (L, D, D), f32) * D ** -0.5,
        "w_cv": nrm(ks[15], (L, D, D), f32) * D ** -0.5,
        "w_co": nrm(ks[16], (L, D, D), f32) * D ** -0.5,
        "cross_q_norm": gain(ks[17], (L, CROSS_HEAD_DIM)),
        "cross_k_norm": gain(ks[18], (L, CROSS_HEAD_DIM)),
        "norm_ffn": gain(ks[19], (L, D)),
        "router_group_w": nrm(ks[20], (L, D, G), f32) * D ** -0.5,
        "router_group_b": 0.01 * nrm(ks[21], (L, G), f32),
        "router_expert_w": nrm(ks[22], (L, D, G * E), f32) * D ** -0.5,
        "router_expert_b": 0.01 * nrm(ks[23], (L, G * E), f32),
        "moe_w1": nrm(ks[24], (L, G, E, D, F), f32) * D ** -0.5,
        "moe_w3": nrm(ks[25], (L, G, E, D, F), f32) * D ** -0.5,
        "moe_w2": nrm(ks[26], (L, G, E, F, D), f32) * F ** -0.5,
    }


def reference(x, mem, norm_mix, w_in, hg_lower_bounds, hg_norm, conv_w, fox_f_bias,
              fox_q_norm, fox_k_norm, w_out, norm_cross, norm_mem, w_cq, w_ck, w_cv, w_co,
              cross_q_norm, cross_k_norm, norm_ffn, router_group_w, router_group_b,
              router_expert_w, router_expert_b, moe_w1, moe_w3, moe_w2):
    for l in range(DEPTH):
        hn = rmsnorm(x, norm_mix[l])
        proj = hn @ w_in[l]
        hq, hf, hi, hgate, cb, cc, ch, fq, fk, fv, ff = split_in_proj(proj)
        lb = hgrn_lower_bound(hg_lower_bounds, l)
        y_hg = hgrn2_mixer(hq, hf, hi, hgate, lb, hg_norm[l])
        y_conv = short_conv_mixer(cb, cc, ch, conv_w[l])
        y_fox = fox_mixer(fq, fk, fv, ff, fox_f_bias[l], fox_q_norm[l], fox_k_norm[l])
        x = x + jnp.concatenate([y_hg, y_conv, y_fox], axis=-1) @ w_out[l]
        memn = rmsnorm(mem, norm_mem[l])
        x = x + memory_cross_attention(rmsnorm(x, norm_cross[l]), memn, w_cq[l], w_ck[l],
                                       w_cv[l], w_co[l], cross_q_norm[l], cross_k_norm[l])
        x = x + hier_moe(rmsnorm(x, norm_ffn[l]), router_group_w[l], router_group_b[l],
                         router_expert_w[l], router_expert_b[l], moe_w1[l], moe_w3[l], moe_w2[l])
    return x
```

```python
import functools

import numpy as np
import jax
import jax.numpy as jnp
from jax import lax
from jax.experimental import pallas as pl
from jax.experimental.pallas import tpu as pltpu

F32 = jnp.float32
BF16 = jnp.bfloat16
EPS = 1e-6

HG_HEADS = 4
HG_DIM = 128
HG_WIDTH = HG_HEADS * HG_DIM
CONV_WIDTH = 512
FOX_HEADS = 8
FOX_DIM = 128
FOX_WIDTH = FOX_HEADS * FOX_DIM
CROSS_HEADS = 4
MOE_GROUPS = 4
MOE_EXPERTS = 8
N_EXPERTS = MOE_GROUPS * MOE_EXPERTS
ROUTER_LANE0 = MOE_GROUPS

LANES = 128
VMEM_LIMIT = 56 * 1024 * 1024

NORM_ROWS = 256
MM_TM = 1024
MM_TN = 512
HG_CHUNK = 128
FOX_TQ = 512
CROSS_TS = 512
ROUTE_ROWS = 512
MOE_ROWS = 256
COMB_ROWS = 256


def _params(*sem):
    return pltpu.CompilerParams(dimension_semantics=sem, vmem_limit_bytes=VMEM_LIMIT)


def _sigmoid(x):
    return 1.0 / (1.0 + jnp.exp(-x))


def _log_sigmoid(x):
    return jnp.minimum(x, 0.0) - jnp.log1p(jnp.exp(-jnp.abs(x)))


def _rmsnorm_kernel(x_ref, g_ref, o_ref, of_ref=None):
    x = x_ref[...]
    y = x * lax.rsqrt(jnp.mean(x * x, axis=-1, keepdims=True) + EPS) * g_ref[...]
    o_ref[...] = y.astype(o_ref.dtype)
    if of_ref is not None:
        of_ref[...] = y


def _rmsnorm_router_kernel(x_ref, g_ref, w_ref, b_ref, o_ref, lg_ref):
    x = x_ref[...]
    y = x * lax.rsqrt(jnp.mean(x * x, axis=-1, keepdims=True) + EPS) * g_ref[...]
    o_ref[...] = y
    lg_ref[...] = jnp.dot(y, w_ref[...], precision=lax.Precision.HIGHEST,
                          preferred_element_type=F32) + b_ref[...]


def rmsnorm(x, g, out_dtype=BF16):
    R, D = x.shape
    return pl.pallas_call(
        _rmsnorm_kernel,
        grid=(R // NORM_ROWS,),
        in_specs=[pl.BlockSpec((NORM_ROWS, D), lambda i: (i, 0)),
                  pl.BlockSpec((1, D), lambda i: (0, 0))],
        out_specs=pl.BlockSpec((NORM_ROWS, D), lambda i: (i, 0)),
        out_shape=jax.ShapeDtypeStruct((R, D), out_dtype),
        compiler_params=_params("parallel"),
        name="rmsnorm",
    )(x, g.reshape(1, D))


def rmsnorm_router(x, g, w, b):
    R, D = x.shape
    return pl.pallas_call(
        _rmsnorm_router_kernel,
        grid=(R // NORM_ROWS,),
        in_specs=[pl.BlockSpec((NORM_ROWS, D), lambda i: (i, 0)),
                  pl.BlockSpec((1, D), lambda i: (0, 0)),
                  pl.BlockSpec((D, LANES), lambda i: (0, 0)),
                  pl.BlockSpec((1, LANES), lambda i: (0, 0))],
        out_specs=[pl.BlockSpec((NORM_ROWS, D), lambda i: (i, 0)),
                   pl.BlockSpec((NORM_ROWS, LANES), lambda i: (i, 0))],
        out_shape=[jax.ShapeDtypeStruct((R, D), F32),
                   jax.ShapeDtypeStruct((R, LANES), F32)],
        compiler_params=_params("parallel"),
        name="rmsnorm_router",
    )(x, g.reshape(1, D), w, b.reshape(1, LANES))


def _matmul_kernel(a_ref, w_ref, *rest, has_res):
    if has_res:
        r_ref, o_ref, wb_ref = rest
    else:
        o_ref, wb_ref = rest

    @pl.when(pl.program_id(1) == 0)
    def _():
        wb_ref[...] = w_ref[...].astype(BF16)

    acc = jnp.dot(a_ref[...], wb_ref[...], preferred_element_type=F32)
    if has_res:
        acc = r_ref[...] + acc
    o_ref[...] = acc.astype(o_ref.dtype)


def matmul(a, w, col0, n, out_dtype, residual=None, tm=MM_TM, tn=MM_TN):
    M, K = a.shape
    tm = min(tm, M)
    assert col0 % tn == 0 and n % tn == 0 and M % tm == 0
    cb = col0 // tn
    in_specs = [pl.BlockSpec((tm, K), lambda j, i: (i, 0)),
                pl.BlockSpec((K, tn), lambda j, i: (0, cb + j))]
    args = [a, w]
    if residual is not None:
        in_specs.append(pl.BlockSpec((tm, tn), lambda j, i: (i, j)))
        args.append(residual)
    return pl.pallas_call(
        functools.partial(_matmul_kernel, has_res=residual is not None),
        grid=(n // tn, M // tm),
        in_specs=in_specs,
        out_specs=pl.BlockSpec((tm, tn), lambda j, i: (i, j)),
        out_shape=jax.ShapeDtypeStruct((M, n), out_dtype),
        scratch_shapes=[pltpu.VMEM((K, tn), BF16)],
        compiler_params=_params("arbitrary", "arbitrary"),
        name="matmul",
    )(*args)


def _headnorm_kernel(x_ref, g_ref, o_ref, *, nh, dh, scale):
    for h in range(nh):
        x = x_ref[:, h * dh:(h + 1) * dh].astype(F32)
        y = x * lax.rsqrt(jnp.mean(x * x, axis=-1, keepdims=True) + EPS) * g_ref[...]
        o_ref[:, h * dh:(h + 1) * dh] = (y * scale).astype(o_ref.dtype)


def headnorm(x, g, col0, n, dh, scale=1.0, bw=512):
    R = x.shape[0]
    tr = min(NORM_ROWS * 2, R)
    assert col0 % bw == 0 and n % bw == 0 and bw % dh == 0
    cb = col0 // bw
    return pl.pallas_call(
        functools.partial(_headnorm_kernel, nh=bw // dh, dh=dh, scale=scale),
        grid=(R // tr, n // bw),
        in_specs=[pl.BlockSpec((tr, bw), lambda i, j: (i, cb + j)),
                  pl.BlockSpec((1, dh), lambda i, j: (0, 0))],
        out_specs=pl.BlockSpec((tr, bw), lambda i, j: (i, j)),
        out_shape=jax.ShapeDtypeStruct((R, n), BF16),
        compiler_params=_params("parallel", "parallel"),
        name="headnorm",
    )(x, g.reshape(1, dh))


def _hgrn_consts(C):
    nlev = int(np.log2(C))
    t = np.arange(C)[:, None]
    u = np.arange(C)[None, :]
    blocks = [(u <= t)]
    for j in range(nlev):
        m = 1 << j
        boundary = (t // (2 * m)) * (2 * m) + m - 1
        second = (t & m) != 0
        blocks.append(np.where(second, (u > boundary) & (u <= t), (u > t) & (u <= boundary)))
    mall = np.concatenate(blocks, axis=0).astype(np.float32)
    mall2 = np.concatenate([mall, mall], axis=1)
    tt = np.arange(C)[:, None]
    ss = np.arange(C)[None, :]
    lev = np.full((C, C), nlev + 1, np.int32)
    x = tt ^ ss
    hb = np.zeros_like(x)
    for j in range(nlev):
        hb = np.where((x >> j) & 1, j, hb)
    lev = np.where(tt > ss, hb, lev)
    lev = np.where(tt == ss, nlev, lev)
    return jnp.asarray(mall2, BF16), jnp.asarray(lev, jnp.int32)


def _hgrn_kernel(x_ref, mall_ref, lev_ref, loglb_ref, l1mlb_ref, ng_ref, o_ref, st_ref, *, C, nlev):
    @pl.when(pl.program_id(1) == 0)
    def _():
        st_ref[...] = jnp.zeros_like(st_ref)

    W = HG_WIDTH
    d = HG_DIM
    lev = lev_ref[...]
    mall = mall_ref[...]
    row = lax.broadcasted_iota(jnp.int32, (C, d), 0)
    nt = (((1,), (1,)), ((), ()))
    for h in range(HG_HEADS):
        sl = slice(h * d, (h + 1) * d)
        q = x_ref[0, :, h * d:(h + 1) * d]
        z = x_ref[0, :, W + h * d:W + (h + 1) * d]
        v = x_ref[0, :, 2 * W + h * d:2 * W + (h + 1) * d]
        g = x_ref[0, :, 3 * W + h * d:3 * W + (h + 1) * d]
        a = loglb_ref[:, sl]
        bt = l1mlb_ref[:, sl] + _log_sigmoid(z)
        lf = jnp.maximum(a, bt) + jnp.log1p(jnp.exp(-jnp.abs(a - bt)))
        kk = 1.0 - jnp.exp(lf)
        hi = lf.astype(BF16)
        lo = (lf - hi.astype(F32)).astype(BF16)
        sums = jnp.dot(mall, jnp.concatenate([hi, lo], axis=0), preferred_element_type=F32)
        b = sums[0:C]
        vb = v.astype(BF16)
        st = st_ref[h]
        o = lax.dot_general((q * jnp.exp(b)).astype(BF16), st.astype(BF16), nt,
                            preferred_element_type=F32)
        att = jnp.where(lev == nlev, jnp.sum(q * kk, axis=-1, keepdims=True), 0.0)
        for j in range(nlev):
            second = (row & (1 << j)) != 0
            e = jnp.exp(sums[(j + 1) * C:(j + 2) * C])
            xq = jnp.where(second, q * e, 0.0).astype(BF16)
            xk = jnp.where(second, 0.0, kk * e).astype(BF16)
            am = lax.dot_general(xq, xk, nt, preferred_element_type=F32)
            att = jnp.where(lev == j, am, att)
        o = o + jnp.dot(att.astype(BF16), vb, preferred_element_type=F32)
        bl = b[C - 1:C, :]
        kh = (kk * jnp.exp(bl - b)).astype(BF16)
        st_ref[h] = st * jnp.exp(bl) + lax.dot_general(
            vb, kh, (((0,), (0,)), ((), ())), preferred_element_type=F32)
        y = o * lax.rsqrt(jnp.mean(o * o, axis=-1, keepdims=True) + EPS) * ng_ref[...]
        o_ref[0, :, sl] = (y * (g * _sigmoid(g))).astype(o_ref.dtype)


def hgrn2(proj_a, loglb, l1mlb, norm_g):
    B, S, _ = proj_a.shape
    C = HG_CHUNK
    nlev = int(np.log2(C))
    mall, lev = _hgrn_consts(C)
    return pl.pallas_call(
        functools.partial(_hgrn_kernel, C=C, nlev=nlev),
        grid=(B, S // C),
        in_specs=[pl.BlockSpec((1, C, 4 * HG_WIDTH), lambda b, c: (b, c, 0)),
                  pl.BlockSpec(mall.shape, lambda b, c: (0, 0)),
                  pl.BlockSpec((C, C), lambda b, c: (0, 0)),
                  pl.BlockSpec((1, HG_WIDTH), lambda b, c: (0, 0)),
                  pl.BlockSpec((1, HG_WIDTH), lambda b, c: (0, 0)),
                  pl.BlockSpec((1, HG_DIM), lambda b, c: (0, 0))],
        out_specs=pl.BlockSpec((1, C, HG_WIDTH), lambda b, c: (b, c, 0)),
        out_shape=jax.ShapeDtypeStruct((B, S, HG_WIDTH), BF16),
        scratch_shapes=[pltpu.VMEM((HG_HEADS, HG_DIM, HG_DIM), F32)],
        compiler_params=_params("parallel", "arbitrary"),
        name="hgrn2",
    )(proj_a, mall, lev, loglb.reshape(1, HG_WIDTH), l1mlb.reshape(1, HG_WIDTH),
      norm_g.reshape(1, HG_DIM))


def _conv_kernel(b_ref, c_ref, h_ref, w_ref, o_ref):
    u = c_ref[0].astype(F32) * h_ref[0].astype(F32)
    row = lax.broadcasted_iota(jnp.int32, u.shape, 0)
    u1 = jnp.where(row >= 1, pltpu.roll(u, 1, axis=0), 0.0)
    u2 = jnp.where(row >= 2, pltpu.roll(u, 2, axis=0), 0.0)
    y = w_ref[0:1, :] * u2 + w_ref[1:2, :] * u1 + w_ref[2:3, :] * u
    o_ref[0] = (b_ref[0].astype(F32) * y).astype(o_ref.dtype)


def short_conv(proj_b, w):
    B, S, _ = proj_b.shape
    cw = CONV_WIDTH
    spec = lambda k: pl.BlockSpec((1, S, cw), lambda b: (b, 0, k))
    return pl.pallas_call(
        _conv_kernel,
        grid=(B,),
        in_specs=[spec(0), spec(1), spec(2), pl.BlockSpec((3, cw), lambda b: (0, 0))],
        out_specs=pl.BlockSpec((1, S, cw), lambda b: (b, 0, 0)),
        out_shape=jax.ShapeDtypeStruct((B, S, cw), BF16),
        compiler_params=_params("parallel"),
        name="short_conv",
    )(proj_b, proj_b, proj_b, w)


def _fox_gate_kernel(hn_ref, wf_ref, bias_ref, c_ref):
    gl = lax.dot_general(wf_ref[...].astype(BF16), hn_ref[0], (((1,), (1,)), ((), ())),
                         preferred_element_type=F32)
    c = _log_sigmoid(gl + bias_ref[...])
    S = c.shape[1]
    lane = lax.broadcasted_iota(jnp.int32, c.shape, 1)
    sh = 1
    while sh < S:
        c = c + jnp.where(lane >= sh, pltpu.roll(c, sh, axis=1), 0.0)
        sh *= 2
    c_ref[0] = c


def fox_gate(hn, wf_t, bias):
    B, S, D = hn.shape
    H = wf_t.shape[0]
    return pl.pallas_call(
        _fox_gate_kernel,
        grid=(B,),
        in_specs=[pl.BlockSpec((1, S, D), lambda b: (b, 0, 0)),
                  pl.BlockSpec((H, D), lambda b: (0, 0)),
                  pl.BlockSpec((H, 1), lambda b: (0, 0))],
        out_specs=pl.BlockSpec((1, H, S), lambda b: (b, 0, 0)),
        out_shape=jax.ShapeDtypeStruct((B, H, S), F32),
        compiler_params=_params("parallel"),
        name="fox_gate",
    )(hn, wf_t, bias.reshape(H, 1))


def _fox_kernel(q_ref, k_ref, v_ref, c_ref, o_ref, *, tq):
    qi = pl.program_id(2)
    q = q_ref[0]
    nt = (((1,), (1,)), ((), ()))

    def step(ki, carry, masked):
        m, l, acc = carry
        k = k_ref[0, pl.ds(ki * tq, tq), :]
        v = v_ref[0, pl.ds(ki * tq, tq), :]
        ck = c_ref[0, 0, pl.ds(ki, 1), :]
        s = lax.dot_general(q, k, nt, preferred_element_type=F32) - ck
        if masked:
            r = lax.broadcasted_iota(jnp.int32, s.shape, 0)
            c = lax.broadcasted_iota(jnp.int32, s.shape, 1)
            s = jnp.where(c <= r, s, -jnp.inf)
        m_new = jnp.maximum(m, jnp.max(s, axis=-1, keepdims=True))
        alpha = jnp.exp(m - m_new)
        p = jnp.exp(s - m_new)
        l = alpha * l + jnp.sum(p, axis=-1, keepdims=True)
        acc = alpha * acc + jnp.dot(p.astype(BF16), v, preferred_element_type=F32)
        return m_new, l, acc

    init = (jnp.full((tq, 1), -jnp.inf, F32), jnp.zeros((tq, 1), F32),
            jnp.zeros((tq, FOX_DIM), F32))
    carry = lax.fori_loop(0, qi, lambda ki, cr: step(ki, cr, False), init)
    _, l, acc = step(qi, carry, True)
    o_ref[0] = (acc / l).astype(o_ref.dtype)


def fox_attention(qn, kn, proj_b, v_col0, c):
    B, S, _ = qn.shape
    tq = FOX_TQ
    H = FOX_HEADS
    vb = v_col0 // FOX_DIM
    c4 = c.reshape(B, H, S // tq, tq)
    return pl.pallas_call(
        functools.partial(_fox_kernel, tq=tq),
        grid=(B, H, S // tq),
        in_specs=[pl.BlockSpec((1, tq, FOX_DIM), lambda b, h, i: (b, i, h)),
                  pl.BlockSpec((1, S, FOX_DIM), lambda b, h, i: (b, 0, h)),
                  pl.BlockSpec((1, S, FOX_DIM), lambda b, h, i: (b, 0, vb + h)),
                  pl.BlockSpec((1, 1, S // tq, tq), lambda b, h, i: (b, h, 0, 0))],
        out_specs=pl.BlockSpec((1, tq, FOX_DIM), lambda b, h, i: (b, i, h)),
        out_shape=jax.ShapeDtypeStruct((B, S, FOX_WIDTH), BF16),
        compiler_params=_params("parallel", "parallel", "arbitrary"),
        name="fox_attention",
    )(qn, kn, proj_b, c4)


def _cross_kernel(q_ref, k_ref, v_ref, o_ref, *, dh):
    nt = (((1,), (1,)), ((), ()))
    for h in range(CROSS_HEADS):
        sl = slice(h * dh, (h + 1) * dh)
        s = lax.dot_general(q_ref[0, :, sl], k_ref[0, :, sl], nt, preferred_element_type=F32)
        m = jnp.max(s, axis=-1, keepdims=True)
        p = jnp.exp(s - m)
        l = jnp.sum(p, axis=-1, keepdims=True)
        o = jnp.dot(p.astype(BF16), v_ref[0, :, sl], preferred_element_type=F32)
        o_ref[0, :, sl] = (o / l).astype(o_ref.dtype)


def cross_attention(qn, kn, v):
    B, S, D = qn.shape
    M = kn.shape[1]
    ts = CROSS_TS
    return pl.pallas_call(
        functools.partial(_cross_kernel, dh=D // CROSS_HEADS),
        grid=(B, S // ts),
        in_specs=[pl.BlockSpec((1, ts, D), lambda b, i: (b, i, 0)),
                  pl.BlockSpec((1, M, D), lambda b, i: (b, 0, 0)),
                  pl.BlockSpec((1, M, D), lambda b, i: (b, 0, 0))],
        out_specs=pl.BlockSpec((1, ts, D), lambda b, i: (b, i, 0)),
        out_shape=jax.ShapeDtypeStruct((B, S, D), BF16),
        compiler_params=_params("parallel", "parallel"),
        name="cross_attention",
    )(qn, kn, v)


def _route_kernel(lg_ref, tri_ref, meta_ref, cnt_ref, carry_ref):
    @pl.when(pl.program_id(0) == 0)
    def _():
        carry_ref[...] = jnp.zeros_like(carry_ref)

    x = lg_ref[...]
    lane = lax.broadcasted_iota(jnp.int32, x.shape, 1)
    ninf = -jnp.inf
    is_g = lane < MOE_GROUPS
    gl = jnp.where(is_g, x, ninf)
    gmax = jnp.max(gl, axis=-1, keepdims=True)
    gidx = jnp.min(jnp.where(gl == gmax, lane, LANES), axis=-1, keepdims=True)
    gw = 1.0 / jnp.sum(jnp.where(is_g, jnp.exp(x - gmax), 0.0), axis=-1, keepdims=True)
    e_lane = lane - ROUTER_LANE0
    in_grp = (e_lane >= 0) & (e_lane < N_EXPERTS) & ((e_lane >> 3) == gidx)
    el = jnp.where(in_grp, x, ninf)
    m1 = jnp.max(el, axis=-1, keepdims=True)
    i1 = jnp.min(jnp.where(el == m1, lane, LANES), axis=-1, keepdims=True)
    el2 = jnp.where(lane == i1, ninf, el)
    m2 = jnp.max(el2, axis=-1, keepdims=True)
    i2 = jnp.min(jnp.where(el2 == m2, lane, LANES), axis=-1, keepdims=True)
    r = jnp.exp(m2 - m1)
    p1 = 1.0 / (1.0 + r)
    p2 = r * p1
    oh = ((lane == i1) | (lane == i2)).astype(F32)
    prefix = jnp.dot(tri_ref[...], oh.astype(BF16), preferred_element_type=F32) + carry_ref[...]
    rank1 = jnp.sum(jnp.where(lane == i1, prefix, 0.0), axis=-1, keepdims=True)
    rank2 = jnp.sum(jnp.where(lane == i2, prefix, 0.0), axis=-1, keepdims=True)
    carry_ref[...] += jnp.sum(oh, axis=0, keepdims=True)
    cnt_ref[...] = carry_ref[...]
    cols = [(i1 - ROUTER_LANE0).astype(F32), (i2 - ROUTER_LANE0).astype(F32),
            gw * p1, gw * p2, rank1, rank2]
    meta = jnp.zeros(x.shape, F32)
    for k, col in enumerate(cols):
        meta = jnp.where(lane == k, col, meta)
    meta_ref[...] = meta


def route(logits):
    T = logits.shape[0]
    tt = ROUTE_ROWS
    tri = jnp.asarray(np.tril(np.ones((tt, tt), np.float32), -1), BF16)
    return pl.pallas_call(
        _route_kernel,
        grid=(T // tt,),
        in_specs=[pl.BlockSpec((tt, LANES), lambda i: (i, 0)),
                  pl.BlockSpec((tt, tt), lambda i: (0, 0))],
        out_specs=[pl.BlockSpec((tt, LANES), lambda i: (i, 0)),
                   pl.BlockSpec((1, LANES), lambda i: (0, 0))],
        out_shape=[jax.ShapeDtypeStruct((T, LANES), F32),
                   jax.ShapeDtypeStruct((1, LANES), F32)],
        scratch_shapes=[pltpu.VMEM((1, LANES), F32)],
        compiler_params=_params("arbitrary"),
        name="moe_route",
    )(logits, tri)


def _row_copy(src_hbm, row, dst, slot, r, sem):
    return pltpu.make_async_copy(src_hbm.at[pl.ds(row, 1)], dst.at[slot, pl.ds(r, 1)], sem.at[slot])


def _expert_kernel(te_ref, first_ref, nact_ref, src_ref, hf_hbm, w1_ref, w3_ref, w2_ref, o_ref,
                   xbuf, w1b, w3b, w2b, sem, *, tr):
    i = pl.program_id(0)
    nact = nact_ref[0]
    slot = i % 2

    def issue(tile, slot):
        def body(r, _):
            _row_copy(hf_hbm, src_ref[tile * tr + r], xbuf, slot, r, sem).start()
            return 0
        lax.fori_loop(0, tr, body, 0)

    @pl.when(i == 0)
    def _():
        issue(0, 0)

    @pl.when(i < nact)
    def _():
        pltpu.make_async_copy(hf_hbm.at[pl.ds(0, tr)], xbuf.at[slot], sem.at[slot]).wait()

    @pl.when(i + 1 < nact)
    def _():
        issue(i + 1, 1 - slot)

    @pl.when(first_ref[i] == 1)
    def _():
        w1b[...] = w1_ref[0].astype(BF16)
        w3b[...] = w3_ref[0].astype(BF16)
        w2b[...] = w2_ref[0].astype(BF16)

    @pl.when(i < nact)
    def _():
        x = xbuf[slot].astype(BF16)
        h1 = jnp.dot(x, w1b[...], preferred_element_type=F32)
        h3 = jnp.dot(x, w3b[...], preferred_element_type=F32)
        hm = (h1 * _sigmoid(h1) * h3).astype(BF16)
        o_ref[...] = jnp.dot(hm, w2b[...], preferred_element_type=F32)

    @pl.when(i >= nact)
    def _():
        o_ref[...] = jnp.zeros_like(o_ref)


def moe_experts(hf, w1, w3, w2, layer, te, first, nact, src):
    T, D = hf.shape
    F = w1.shape[-1]
    tr = MOE_ROWS
    NT = te.shape[0]
    w1f = w1.reshape(-1, D, F)
    w3f = w3.reshape(-1, D, F)
    w2f = w2.reshape(-1, F, D)
    base = layer * N_EXPERTS
    wmap = lambda i, te, first, nact, src: (base + te[i], 0, 0)
    omap = lambda i, te, first, nact, src: (i, 0)
    grid_spec = pltpu.PrefetchScalarGridSpec(
        num_scalar_prefetch=4,
        grid=(NT,),
        in_specs=[pl.BlockSpec(memory_space=pl.ANY),
                  pl.BlockSpec((1, D, F), wmap),
                  pl.BlockSpec((1, D, F), wmap),
                  pl.BlockSpec((1, F, D), wmap)],
        out_specs=pl.BlockSpec((tr, D), omap),
        scratch_shapes=[pltpu.VMEM((2, tr, D), F32),
                        pltpu.VMEM((D, F), BF16),
                        pltpu.VMEM((D, F), BF16),
                        pltpu.VMEM((F, D), BF16),
                        pltpu.SemaphoreType.DMA((2,))],
    )
    return pl.pallas_call(
        functools.partial(_expert_kernel, tr=tr),
        grid_spec=grid_spec,
        out_shape=jax.ShapeDtypeStruct((NT * tr, D), F32),
        compiler_params=_params("arbitrary"),
        name="moe_experts",
    )(te, first, nact, src, hf, w1f, w3f, w2f)


def _combine_kernel(d1_ref, d2_ref, x_ref, meta_ref, ys_hbm, o_ref, buf, sem, *, tt):
    i = pl.program_id(0)
    n = pl.num_programs(0)
    slot = i % 2

    def issue(tile, slot):
        def body(r, _):
            t = tile * tt + r
            _row_copy(ys_hbm, d1_ref[t], buf, slot, r, sem).start()
            _row_copy(ys_hbm, d2_ref[t], buf, slot, tt + r, sem).start()
            return 0
        lax.fori_loop(0, tt, body, 0)

    @pl.when(i == 0)
    def _():
        issue(0, 0)

    pltpu.make_async_copy(ys_hbm.at[pl.ds(0, 2 * tt)], buf.at[slot], sem.at[slot]).wait()

    @pl.when(i + 1 < n)
    def _():
        issue(i + 1, 1 - slot)

    w1 = meta_ref[:, 2:3]
    w2 = meta_ref[:, 3:4]
    o_ref[...] = x_ref[...] + w1 * buf[slot, 0:tt, :] + w2 * buf[slot, tt:2 * tt, :]


def moe_combine(x, meta, ys, d1, d2):
    T, D = x.shape
    tt = COMB_ROWS
    grid_spec = pltpu.PrefetchScalarGridSpec(
        num_scalar_prefetch=2,
        grid=(T // tt,),
        in_specs=[pl.BlockSpec((tt, D), lambda i, d1, d2: (i, 0)),
                  pl.BlockSpec((tt, LANES), lambda i, d1, d2: (i, 0)),
                  pl.BlockSpec(memory_space=pl.ANY)],
        out_specs=pl.BlockSpec((tt, D), lambda i, d1, d2: (i, 0)),
        scratch_shapes=[pltpu.VMEM((2, 2 * tt, D), F32),
                        pltpu.SemaphoreType.DMA((2,))],
    )
    return pl.pallas_call(
        functools.partial(_combine_kernel, tt=tt),
        grid_spec=grid_spec,
        out_shape=jax.ShapeDtypeStruct((T, D), F32),
        compiler_params=_params("arbitrary"),
        name="moe_combine",
    )(d1, d2, x, meta, ys)


def _dispatch_tables(meta, cnt, T, tr):
    e1 = jnp.clip(meta[:, 0].astype(jnp.int32), 0, N_EXPERTS - 1)
    e2 = jnp.clip(meta[:, 1].astype(jnp.int32), 0, N_EXPERTS - 1)
    r1 = meta[:, 4].astype(jnp.int32)
    r2 = meta[:, 5].astype(jnp.int32)
    counts = cnt[0, ROUTER_LANE0:ROUTER_LANE0 + N_EXPERTS].astype(jnp.int32)
    tiles_e = (counts + tr - 1) // tr
    tile_end = jnp.cumsum(tiles_e)
    row_off = (tile_end - tiles_e) * tr
    nact = tile_end[-1]
    NT = (2 * T) // tr + N_EXPERTS
    R = NT * tr
    d1 = jnp.clip(row_off[e1] + r1, 0, R - 1)
    d2 = jnp.clip(row_off[e2] + r2, 0, R - 1)
    tile_id = jnp.minimum(jnp.arange(NT, dtype=jnp.int32), nact - 1)
    te = jnp.sum((tile_id[:, None] >= tile_end[None, :]).astype(jnp.int32), axis=1)
    te = jnp.minimum(te, N_EXPERTS - 1)
    prev = jnp.concatenate([jnp.full((1,), -1, jnp.int32), te[:-1]])
    first = ((te != prev) & (jnp.arange(NT) < nact)).astype(jnp.int32)
    tok = jnp.arange(T, dtype=jnp.int32)
    src = jnp.zeros((R,), jnp.int32).at[d1].set(tok).at[d2].set(tok)
    return te, first, nact.reshape(1), src, d1, d2


def _lower_bound_logs(lb_param, layer):
    p = jax.nn.softmax(lb_param.astype(F32), axis=0)
    c = jnp.cumsum(p, axis=0)
    lb = c[layer] - c[0]
    return jnp.log(lb), jnp.log1p(-lb)


def kernel(x, mem, norm_mix, w_in, hg_lower_bounds, hg_norm, conv_w, fox_f_bias, fox_q_norm,
           fox_k_norm, w_out, norm_cross, norm_mem, w_cq, w_ck, w_cv, w_co, cross_q_norm,
           cross_k_norm, norm_ffn, router_group_w, router_group_b, router_expert_w,
           router_expert_b, moe_w1, moe_w3, moe_w2):
    B, S, D = x.shape
    M = mem.shape[1]
    L = w_in.shape[0]
    T = B * S
    HGW = 4 * HG_WIDTH
    CONV3 = 3 * CONV_WIDTH
    FQ0 = CONV3
    FK0 = FQ0 + FOX_WIDTH
    FV0 = FK0 + FOX_WIDTH
    PB = CONV3 + 3 * FOX_WIDTH
    dhc = D // CROSS_HEADS
    x2 = x.reshape(T, D)
    mem2 = mem.reshape(B * M, D)
    for l in range(L):
        hn = rmsnorm(x2, norm_mix[l])
        proj_a = matmul(hn, w_in[l], 0, HGW, F32)
        proj_b = matmul(hn, w_in[l], HGW, PB, BF16)
        loglb, l1mlb = _lower_bound_logs(hg_lower_bounds, l)
        y_hg = hgrn2(proj_a.reshape(B, S, HGW), loglb, l1mlb, hg_norm[l])
        pb3 = proj_b.reshape(B, S, PB)
        y_conv = short_conv(pb3, conv_w[l])
        wf_t = w_in[l][:, HGW + PB:].T
        c = fox_gate(hn.reshape(B, S, D), wf_t, fox_f_bias[l])
        qn = headnorm(proj_b, fox_q_norm[l], FQ0, FOX_WIDTH, FOX_DIM, scale=FOX_DIM ** -0.5)
        kn = headnorm(proj_b, fox_k_norm[l], FK0, FOX_WIDTH, FOX_DIM)
        y_fox = fox_attention(qn.reshape(B, S, FOX_WIDTH), kn.reshape(B, S, FOX_WIDTH), pb3, FV0, c)
        mix = jnp.concatenate([y_hg, y_conv, y_fox], axis=-1).reshape(T, D)
        x2 = matmul(mix, w_out[l], 0, D, F32, residual=x2)
        memn = rmsnorm(mem2, norm_mem[l])
        kc = matmul(memn, w_ck[l], 0, D, F32)
        vc = matmul(memn, w_cv[l], 0, D, BF16)
        kcn = headnorm(kc, cross_k_norm[l], 0, D, dhc)
        hc = rmsnorm(x2, norm_cross[l])
        qc = matmul(hc, w_cq[l], 0, D, F32)
        qcn = headnorm(qc, cross_q_norm[l], 0, D, dhc, scale=dhc ** -0.5)
        oc = cross_attention(qcn.reshape(B, S, D), kcn.reshape(B, M, D), vc.reshape(B, M, D))
        x2 = matmul(oc.reshape(T, D), w_co[l], 0, D, F32, residual=x2)
        wr = jnp.concatenate([router_group_w[l], router_expert_w[l]], axis=1)
        wr = jnp.pad(wr, ((0, 0), (0, LANES - wr.shape[1])))
        br = jnp.concatenate([router_group_b[l], router_expert_b[l]])
        br = jnp.pad(br, (0, LANES - br.shape[0]))
        hf, logits = rmsnorm_router(x2, norm_ffn[l], wr, br)
        meta, cnt = route(logits)
        te, first, nact, src, d1, d2 = _dispatch_tables(meta, cnt, T, MOE_ROWS)
        ys = moe_experts(hf, moe_w1, moe_w3, moe_w2, l, te, first, nact, src)
        x2 = moe_combine(x2, meta, ys, d1, d2)
    return x2.reshape(B, S, D)
```

```python
import functools

import numpy as np
import jax
import jax.numpy as jnp
from jax import lax
from jax.experimental import pallas as pl
from jax.experimental.pallas import tpu as pltpu

F32 = jnp.float32
BF16 = jnp.bfloat16
U32 = jnp.uint32
I32 = jnp.int32
EPS = 1e-6

HG_HEADS = 4
HG_DIM = 128
HG_WIDTH = HG_HEADS * HG_DIM
CONV_WIDTH = 512
FOX_HEADS = 8
FOX_DIM = 128
FOX_WIDTH = FOX_HEADS * FOX_DIM
CROSS_HEADS = 4
MOE_GROUPS = 4
MOE_EXPERTS = 8
N_EXPERTS = MOE_GROUPS * MOE_EXPERTS
ROUTER_LANE0 = MOE_GROUPS

LANES = 128
SUBLANES = 8
VMEM_LIMIT = 56 * 1024 * 1024

NORM_ROWS = 256
MM_TM = 1024
MM_TN = 512
HG_CHUNK = 128
FOX_TQ = 512
CROSS_TS = 512
ROUTE_ROWS = 512
MOE_ROWS_LOG2 = 8
MOE_ROWS = 1 << MOE_ROWS_LOG2
COMB_ROWS = 256


def _params(*sem):
    return pltpu.CompilerParams(dimension_semantics=sem, vmem_limit_bytes=VMEM_LIMIT)


def _sigmoid(x):
    return 1.0 / (1.0 + jnp.exp(-x))


def _log_sigmoid(x):
    return jnp.minimum(x, 0.0) - jnp.log1p(jnp.exp(-jnp.abs(x)))


TOKEN_ROWS = 16


def _store_token_tiles(ref, y, rows):
    for j in range(TOKEN_ROWS):
        ref[pl.ds(j, rows, stride=TOKEN_ROWS), :] = y[:, j * LANES:(j + 1) * LANES]


def _load_token_tiles(ref, base, rows):
    return [ref[pl.ds(base + j, rows, stride=TOKEN_ROWS), :] for j in range(TOKEN_ROWS)]


def _rmsnorm_kernel(x_ref, g_ref, o_ref):
    x = x_ref[...]
    y = x * lax.rsqrt(jnp.mean(x * x, axis=-1, keepdims=True) + EPS) * g_ref[...]
    o_ref[...] = y.astype(o_ref.dtype)


def _rmsnorm_router_kernel(x_ref, g_ref, w_ref, b_ref, o_ref, lg_ref):
    x = x_ref[...]
    y = x * lax.rsqrt(jnp.mean(x * x, axis=-1, keepdims=True) + EPS) * g_ref[...]
    _store_token_tiles(o_ref, y, x.shape[0])
    lg_ref[...] = jnp.dot(y, w_ref[...], precision=lax.Precision.HIGHEST,
                          preferred_element_type=F32) + b_ref[...]


def rmsnorm(x, g, out_dtype=BF16):
    R, D = x.shape
    return pl.pallas_call(
        _rmsnorm_kernel,
        grid=(R // NORM_ROWS,),
        in_specs=[pl.BlockSpec((NORM_ROWS, D), lambda i: (i, 0)),
                  pl.BlockSpec((1, D), lambda i: (0, 0))],
        out_specs=pl.BlockSpec((NORM_ROWS, D), lambda i: (i, 0)),
        out_shape=jax.ShapeDtypeStruct((R, D), out_dtype),
        compiler_params=_params("parallel"),
        name="rmsnorm",
    )(x, g.reshape(1, D))


def rmsnorm_router(x, g, w, b):
    R, D = x.shape
    assert D == TOKEN_ROWS * LANES
    return pl.pallas_call(
        _rmsnorm_router_kernel,
        grid=(R // NORM_ROWS,),
        in_specs=[pl.BlockSpec((NORM_ROWS, D), lambda i: (i, 0)),
                  pl.BlockSpec((1, D), lambda i: (0, 0)),
                  pl.BlockSpec((D, LANES), lambda i: (0, 0)),
                  pl.BlockSpec((1, LANES), lambda i: (0, 0))],
        out_specs=[pl.BlockSpec((NORM_ROWS * TOKEN_ROWS, LANES), lambda i: (i, 0)),
                   pl.BlockSpec((NORM_ROWS, LANES), lambda i: (i, 0))],
        out_shape=[jax.ShapeDtypeStruct((R * TOKEN_ROWS, LANES), F32),
                   jax.ShapeDtypeStruct((R, LANES), F32)],
        compiler_params=_params("parallel"),
        name="rmsnorm_router",
    )(x, g.reshape(1, D), w, b.reshape(1, LANES))


def _matmul_kernel(*refs, n_parts, has_res, w_t):
    a_refs = refs[:n_parts]
    w_ref = refs[n_parts]
    rest = refs[n_parts + 1:]
    if has_res:
        r_ref, o_ref, wb_ref = rest
    else:
        o_ref, wb_ref = rest

    @pl.when(pl.program_id(1) == 0)
    def _():
        wb_ref[...] = w_ref[...].astype(BF16)

    if n_parts == 1:
        a = a_refs[0][...]
    else:
        a = jnp.concatenate([r[...] for r in a_refs], axis=1)
    if w_t:
        acc = lax.dot_general(a, wb_ref[...], (((1,), (1,)), ((), ())), preferred_element_type=F32)
    else:
        acc = jnp.dot(a, wb_ref[...], preferred_element_type=F32)
    if has_res:
        acc = r_ref[...] + acc
    o_ref[...] = acc.astype(o_ref.dtype)


def matmul(a_parts, w, layer, col0, n, out_dtype, residual=None, w_t=False, tm=MM_TM, tn=MM_TN):
    M = a_parts[0].shape[0]
    K = w.shape[2] if w_t else w.shape[1]
    assert sum(p.shape[1] for p in a_parts) == K
    tm = min(tm, M)
    assert col0 % tn == 0 and n % tn == 0 and M % tm == 0
    cb = col0 // tn
    in_specs = [pl.BlockSpec((tm, p.shape[1]), lambda j, i: (i, 0)) for p in a_parts]
    if w_t:
        in_specs.append(pl.BlockSpec((None, tn, K), lambda j, i: (layer, cb + j, 0)))
    else:
        in_specs.append(pl.BlockSpec((None, K, tn), lambda j, i: (layer, 0, cb + j)))
    args = list(a_parts) + [w]
    if residual is not None:
        in_specs.append(pl.BlockSpec((tm, tn), lambda j, i: (i, j)))
        args.append(residual)
    return pl.pallas_call(
        functools.partial(_matmul_kernel, n_parts=len(a_parts), has_res=residual is not None,
                          w_t=w_t),
        grid=(n // tn, M // tm),
        in_specs=in_specs,
        out_specs=pl.BlockSpec((tm, tn), lambda j, i: (i, j)),
        out_shape=jax.ShapeDtypeStruct((M, n), out_dtype),
        scratch_shapes=[pltpu.VMEM((tn, K) if w_t else (K, tn), BF16)],
        compiler_params=_params("arbitrary", "arbitrary"),
        name="matmul",
    )(*args)


def _headnorm_kernel(x_ref, g_ref, o_ref, *, nh, dh, scale):
    for h in range(nh):
        x = x_ref[:, h * dh:(h + 1) * dh].astype(F32)
        y = x * lax.rsqrt(jnp.mean(x * x, axis=-1, keepdims=True) + EPS) * g_ref[...]
        o_ref[:, h * dh:(h + 1) * dh] = (y * scale).astype(o_ref.dtype)


def headnorm(x, g, col0, n, dh, scale=1.0, bw=512):
    R = x.shape[0]
    tr = min(NORM_ROWS * 2, R)
    assert col0 % bw == 0 and n % bw == 0 and bw % dh == 0
    cb = col0 // bw
    return pl.pallas_call(
        functools.partial(_headnorm_kernel, nh=bw // dh, dh=dh, scale=scale),
        grid=(R // tr, n // bw),
        in_specs=[pl.BlockSpec((tr, bw), lambda i, j: (i, cb + j)),
                  pl.BlockSpec((1, dh), lambda i, j: (0, 0))],
        out_specs=pl.BlockSpec((tr, bw), lambda i, j: (i, j)),
        out_shape=jax.ShapeDtypeStruct((R, n), BF16),
        compiler_params=_params("parallel", "parallel"),
        name="headnorm",
    )(x, g.reshape(1, dh))


def _hgrn_consts(C):
    nlev = int(np.log2(C))
    t = np.arange(C)[:, None]
    u = np.arange(C)[None, :]
    blocks = [(u <= t)]
    for j in range(nlev):
        m = 1 << j
        boundary = (t // (2 * m)) * (2 * m) + m - 1
        second = (t & m) != 0
        blocks.append(np.where(second, (u > boundary) & (u <= t), (u > t) & (u <= boundary)))
    mall = np.concatenate(blocks, axis=0).astype(np.float32)
    mall2 = np.concatenate([mall, mall], axis=1)
    tt = np.arange(C)[:, None]
    ss = np.arange(C)[None, :]
    lev = np.full((C, C), nlev + 1, np.int32)
    x = tt ^ ss
    hb = np.zeros_like(x)
    for j in range(nlev):
        hb = np.where((x >> j) & 1, j, hb)
    lev = np.where(tt > ss, hb, lev)
    lev = np.where(tt == ss, nlev, lev)
    return jnp.asarray(mall2, BF16), jnp.asarray(lev, jnp.int32)


def _hgrn_kernel(x_ref, mall_ref, lev_ref, loglb_ref, l1mlb_ref, ng_ref, o_ref, st_ref, *, C, nlev):
    @pl.when(pl.program_id(1) == 0)
    def _():
        st_ref[...] = jnp.zeros_like(st_ref)

    W = HG_WIDTH
    d = HG_DIM
    lev = lev_ref[...]
    mall = mall_ref[...]
    row = lax.broadcasted_iota(jnp.int32, (C, d), 0)
    nt = (((1,), (1,)), ((), ()))
    for h in range(HG_HEADS):
        sl = slice(h * d, (h + 1) * d)
        q = x_ref[0, :, h * d:(h + 1) * d]
        z = x_ref[0, :, W + h * d:W + (h + 1) * d]
        v = x_ref[0, :, 2 * W + h * d:2 * W + (h + 1) * d]
        g = x_ref[0, :, 3 * W + h * d:3 * W + (h + 1) * d]
        a = loglb_ref[:, sl]
        bt = l1mlb_ref[:, sl] + _log_sigmoid(z)
        lf = jnp.maximum(a, bt) + jnp.log1p(jnp.exp(-jnp.abs(a - bt)))
        kk = 1.0 - jnp.exp(lf)
        hi = lf.astype(BF16)
        lo = (lf - hi.astype(F32)).astype(BF16)
        sums = jnp.dot(mall, jnp.concatenate([hi, lo], axis=0), preferred_element_type=F32)
        b = sums[0:C]
        vb = v.astype(BF16)
        st = st_ref[h]
        o = lax.dot_general((q * jnp.exp(b)).astype(BF16), st.astype(BF16), nt,
                            preferred_element_type=F32)
        att = jnp.where(lev == nlev, jnp.sum(q * kk, axis=-1, keepdims=True), 0.0)
        for j in range(nlev):
            second = (row & (1 << j)) != 0
            e = jnp.exp(sums[(j + 1) * C:(j + 2) * C])
            xq = jnp.where(second, q * e, 0.0).astype(BF16)
            xk = jnp.where(second, 0.0, kk * e).astype(BF16)
            am = lax.dot_general(xq, xk, nt, preferred_element_type=F32)
            att = jnp.where(lev == j, am, att)
        o = o + jnp.dot(att.astype(BF16), vb, preferred_element_type=F32)
        bl = b[C - 1:C, :]
        kh = (kk * jnp.exp(bl - b)).astype(BF16)
        st_ref[h] = st * jnp.exp(bl) + lax.dot_general(
            vb, kh, (((0,), (0,)), ((), ())), preferred_element_type=F32)
        y = o * lax.rsqrt(jnp.mean(o * o, axis=-1, keepdims=True) + EPS) * ng_ref[...]
        o_ref[0, :, sl] = (y * (g * _sigmoid(g))).astype(o_ref.dtype)


def hgrn2(proj_a, loglb, l1mlb, norm_g):
    B, S, _ = proj_a.shape
    C = HG_CHUNK
    nlev = int(np.log2(C))
    mall, lev = _hgrn_consts(C)
    return pl.pallas_call(
        functools.partial(_hgrn_kernel, C=C, nlev=nlev),
        grid=(B, S // C),
        in_specs=[pl.BlockSpec((1, C, 4 * HG_WIDTH), lambda b, c: (b, c, 0)),
                  pl.BlockSpec(mall.shape, lambda b, c: (0, 0)),
                  pl.BlockSpec((C, C), lambda b, c: (0, 0)),
                  pl.BlockSpec((1, HG_WIDTH), lambda b, c: (0, 0)),
                  pl.BlockSpec((1, HG_WIDTH), lambda b, c: (0, 0)),
                  pl.BlockSpec((1, HG_DIM), lambda b, c: (0, 0))],
        out_specs=pl.BlockSpec((1, C, HG_WIDTH), lambda b, c: (b, c, 0)),
        out_shape=jax.ShapeDtypeStruct((B, S, HG_WIDTH), BF16),
        scratch_shapes=[pltpu.VMEM((HG_HEADS, HG_DIM, HG_DIM), F32)],
        compiler_params=_params("parallel", "arbitrary"),
        name="hgrn2",
    )(proj_a, mall, lev, loglb.reshape(1, HG_WIDTH), l1mlb.reshape(1, HG_WIDTH),
      norm_g.reshape(1, HG_DIM))


def _conv_kernel(b_ref, c_ref, h_ref, w_ref, o_ref):
    u = c_ref[0].astype(F32) * h_ref[0].astype(F32)
    row = lax.broadcasted_iota(jnp.int32, u.shape, 0)
    u1 = jnp.where(row >= 1, pltpu.roll(u, 1, axis=0), 0.0)
    u2 = jnp.where(row >= 2, pltpu.roll(u, 2, axis=0), 0.0)
    y = w_ref[0:1, :] * u2 + w_ref[1:2, :] * u1 + w_ref[2:3, :] * u
    o_ref[0] = (b_ref[0].astype(F32) * y).astype(o_ref.dtype)


def short_conv(proj_b, w):
    B, S, _ = proj_b.shape
    cw = CONV_WIDTH
    spec = lambda k: pl.BlockSpec((1, S, cw), lambda b: (b, 0, k))
    return pl.pallas_call(
        _conv_kernel,
        grid=(B,),
        in_specs=[spec(0), spec(1), spec(2), pl.BlockSpec((3, cw), lambda b: (0, 0))],
        out_specs=pl.BlockSpec((1, S, cw), lambda b: (b, 0, 0)),
        out_shape=jax.ShapeDtypeStruct((B, S, cw), BF16),
        compiler_params=_params("parallel"),
        name="short_conv",
    )(proj_b, proj_b, proj_b, w)


def _fox_gate_kernel(hn_ref, wf_ref, bias_ref, c_ref):
    gl = lax.dot_general(wf_ref[...].astype(BF16), hn_ref[0], (((1,), (1,)), ((), ())),
                         preferred_element_type=F32)
    c = _log_sigmoid(gl + bias_ref[...])
    S = c.shape[1]
    lane = lax.broadcasted_iota(jnp.int32, c.shape, 1)
    sh = 1
    while sh < S:
        c = c + jnp.where(lane >= sh, pltpu.roll(c, sh, axis=1), 0.0)
        sh *= 2
    c_ref[0] = c


def fox_gate(hn, w_t, layer, row0, bias):
    B, S, D = hn.shape
    H = bias.shape[0]
    assert row0 % H == 0
    return pl.pallas_call(
        _fox_gate_kernel,
        grid=(B,),
        in_specs=[pl.BlockSpec((1, S, D), lambda b: (b, 0, 0)),
                  pl.BlockSpec((None, H, D), lambda b: (layer, row0 // H, 0)),
                  pl.BlockSpec((H, 1), lambda b: (0, 0))],
        out_specs=pl.BlockSpec((1, H, S), lambda b: (b, 0, 0)),
        out_shape=jax.ShapeDtypeStruct((B, H, S), F32),
        compiler_params=_params("parallel"),
        name="fox_gate",
    )(hn, w_t, bias.reshape(H, 1))


def _fox_kernel(q_ref, k_ref, v_ref, c_ref, o_ref, *, tq):
    qi = pl.program_id(2)
    q = q_ref[0]
    nt = (((1,), (1,)), ((), ()))

    def step(ki, carry, masked):
        m, l, acc = carry
        k = k_ref[0, pl.ds(ki * tq, tq), :]
        v = v_ref[0, pl.ds(ki * tq, tq), :]
        ck = c_ref[0, 0, pl.ds(ki, 1), :]
        s = lax.dot_general(q, k, nt, preferred_element_type=F32) - ck
        if masked:
            r = lax.broadcasted_iota(jnp.int32, s.shape, 0)
            c = lax.broadcasted_iota(jnp.int32, s.shape, 1)
            s = jnp.where(c <= r, s, -jnp.inf)
        m_new = jnp.maximum(m, jnp.max(s, axis=-1, keepdims=True))
        alpha = jnp.exp(m - m_new)
        p = jnp.exp(s - m_new)
        l = alpha * l + jnp.sum(p, axis=-1, keepdims=True)
        acc = alpha * acc + jnp.dot(p.astype(BF16), v, preferred_element_type=F32)
        return m_new, l, acc

    init = (jnp.full((tq, 1), -jnp.inf, F32), jnp.zeros((tq, 1), F32),
            jnp.zeros((tq, FOX_DIM), F32))
    carry = lax.fori_loop(0, qi, lambda ki, cr: step(ki, cr, False), init)
    _, l, acc = step(qi, carry, True)
    o_ref[0] = (acc / l).astype(o_ref.dtype)


def fox_attention(qn, kn, proj_b, v_col0, c):
    B, S, _ = qn.shape
    tq = FOX_TQ
    H = FOX_HEADS
    vb = v_col0 // FOX_DIM
    c4 = c.reshape(B, H, S // tq, tq)
    return pl.pallas_call(
        functools.partial(_fox_kernel, tq=tq),
        grid=(B, H, S // tq),
        in_specs=[pl.BlockSpec((1, tq, FOX_DIM), lambda b, h, i: (b, i, h)),
                  pl.BlockSpec((1, S, FOX_DIM), lambda b, h, i: (b, 0, h)),
                  pl.BlockSpec((1, S, FOX_DIM), lambda b, h, i: (b, 0, vb + h)),
                  pl.BlockSpec((1, 1, S // tq, tq), lambda b, h, i: (b, h, 0, 0))],
        out_specs=pl.BlockSpec((1, tq, FOX_DIM), lambda b, h, i: (b, i, h)),
        out_shape=jax.ShapeDtypeStruct((B, S, FOX_WIDTH), BF16),
        compiler_params=_params("parallel", "parallel", "arbitrary"),
        name="fox_attention",
    )(qn, kn, proj_b, c4)


def _cross_kernel(q_ref, k_ref, v_ref, o_ref, *, dh):
    nt = (((1,), (1,)), ((), ()))
    for h in range(CROSS_HEADS):
        sl = slice(h * dh, (h + 1) * dh)
        s = lax.dot_general(q_ref[0, :, sl], k_ref[0, :, sl], nt, preferred_element_type=F32)
        m = jnp.max(s, axis=-1, keepdims=True)
        p = jnp.exp(s - m)
        l = jnp.sum(p, axis=-1, keepdims=True)
        o = jnp.dot(p.astype(BF16), v_ref[0, :, sl], preferred_element_type=F32)
        o_ref[0, :, sl] = (o / l).astype(o_ref.dtype)


def cross_attention(qn, kn, v):
    B, S, D = qn.shape
    M = kn.shape[1]
    ts = CROSS_TS
    return pl.pallas_call(
        functools.partial(_cross_kernel, dh=D // CROSS_HEADS),
        grid=(B, S // ts),
        in_specs=[pl.BlockSpec((1, ts, D), lambda b, i: (b, i, 0)),
                  pl.BlockSpec((1, M, D), lambda b, i: (b, 0, 0)),
                  pl.BlockSpec((1, M, D), lambda b, i: (b, 0, 0))],
        out_specs=pl.BlockSpec((1, ts, D), lambda b, i: (b, i, 0)),
        out_shape=jax.ShapeDtypeStruct((B, S, D), BF16),
        compiler_params=_params("parallel", "parallel"),
        name="cross_attention",
    )(qn, kn, v)


def _route_kernel(lg_ref, tri_ref, meta_ref, cnt_ref, carry_ref):
    @pl.when(pl.program_id(0) == 0)
    def _():
        carry_ref[...] = jnp.zeros_like(carry_ref)

    x = lg_ref[...]
    lane = lax.broadcasted_iota(jnp.int32, x.shape, 1)
    ninf = -jnp.inf
    is_g = lane < MOE_GROUPS
    gl = jnp.where(is_g, x, ninf)
    gmax = jnp.max(gl, axis=-1, keepdims=True)
    gidx = jnp.min(jnp.where(gl == gmax, lane, LANES), axis=-1, keepdims=True)
    gw = 1.0 / jnp.sum(jnp.where(is_g, jnp.exp(x - gmax), 0.0), axis=-1, keepdims=True)
    e_lane = lane - ROUTER_LANE0
    in_grp = (e_lane >= 0) & (e_lane < N_EXPERTS) & ((e_lane >> 3) == gidx)
    el = jnp.where(in_grp, x, ninf)
    m1 = jnp.max(el, axis=-1, keepdims=True)
    i1 = jnp.min(jnp.where(el == m1, lane, LANES), axis=-1, keepdims=True)
    el2 = jnp.where(lane == i1, ninf, el)
    m2 = jnp.max(el2, axis=-1, keepdims=True)
    i2 = jnp.min(jnp.where(el2 == m2, lane, LANES), axis=-1, keepdims=True)
    r = jnp.exp(m2 - m1)
    p1 = 1.0 / (1.0 + r)
    p2 = r * p1
    oh = ((lane == i1) | (lane == i2)).astype(F32)
    prefix = jnp.dot(tri_ref[...], oh.astype(BF16), preferred_element_type=F32) + carry_ref[...]
    rank1 = jnp.sum(jnp.where(lane == i1, prefix, 0.0), axis=-1, keepdims=True)
    rank2 = jnp.sum(jnp.where(lane == i2, prefix, 0.0), axis=-1, keepdims=True)
    carry_ref[...] += jnp.sum(oh, axis=0, keepdims=True)
    cnt_ref[...] = carry_ref[...]
    cols = [i1.astype(F32), i2.astype(F32), gw * p1, gw * p2, rank1, rank2]
    meta = jnp.zeros(x.shape, F32)
    for k, col in enumerate(cols):
        meta = jnp.where(lane == k, col, meta)
    meta_ref[...] = meta


def route(logits):
    T = logits.shape[0]
    tt = ROUTE_ROWS
    tri = jnp.asarray(np.tril(np.ones((tt, tt), np.float32), -1), BF16)
    return pl.pallas_call(
        _route_kernel,
        grid=(T // tt,),
        in_specs=[pl.BlockSpec((tt, LANES), lambda i: (i, 0)),
                  pl.BlockSpec((tt, tt), lambda i: (0, 0))],
        out_specs=[pl.BlockSpec((tt, LANES), lambda i: (i, 0)),
                   pl.BlockSpec((1, LANES), lambda i: (0, 0))],
        out_shape=[jax.ShapeDtypeStruct((T, LANES), F32),
                   jax.ShapeDtypeStruct((1, LANES), F32)],
        scratch_shapes=[pltpu.VMEM((1, LANES), F32)],
        compiler_params=_params("arbitrary"),
        name="moe_route",
    )(logits, tri)


def _plan_kernel(meta_ref, cnt_ref, d_ref):
    cnt = jnp.broadcast_to(cnt_ref[...], (SUBLANES, LANES)).astype(I32)
    pad = (((cnt + (MOE_ROWS - 1)) >> MOE_ROWS_LOG2) << MOE_ROWS_LOG2).astype(F32)
    lane8 = lax.broadcasted_iota(I32, pad.shape, 1)
    incl = pad
    sh = 1
    while sh < LANES:
        incl = incl + jnp.where(lane8 >= sh, pltpu.roll(incl, sh, axis=1), 0.0)
        sh *= 2
    off = (incl - pad)[0:1, :]
    m = meta_ref[...]
    lane = lax.broadcasted_iota(I32, m.shape, 1)
    i1 = m[:, 0:1].astype(I32)
    i2 = m[:, 1:2].astype(I32)
    d1 = jnp.sum(jnp.where(lane == i1, off, 0.0), axis=-1, keepdims=True) + m[:, 4:5]
    d2 = jnp.sum(jnp.where(lane == i2, off, 0.0), axis=-1, keepdims=True) + m[:, 5:6]
    dm = jnp.where(lane == 0, d1, jnp.where(lane == 1, d2, 0.0))
    d_ref[...] = jnp.transpose(dm)[0:2, :].astype(I32)


def plan(meta, cnt):
    T = meta.shape[0]
    tt = ROUTE_ROWS
    return pl.pallas_call(
        _plan_kernel,
        grid=(T // tt,),
        in_specs=[pl.BlockSpec((tt, LANES), lambda i: (i, 0)),
                  pl.BlockSpec((1, LANES), lambda i: (0, 0))],
        out_specs=pl.BlockSpec((2, tt), lambda i: (0, i)),
        out_shape=jax.ShapeDtypeStruct((2, T), I32),
        compiler_params=_params("parallel"),
        name="moe_plan",
    )(meta, cnt)


def _invert_kernel(d1_ref, d2_ref, src_ref, *, T, R):
    def init(r, _):
        src_ref[r] = 0
        return 0
    lax.fori_loop(0, R, init, 0, unroll=32)

    def body(t, _):
        src_ref[d1_ref[t]] = t
        src_ref[d2_ref[t]] = t
        return 0
    lax.fori_loop(0, T, body, 0, unroll=8)


def invert(d1, d2, R):
    T = d1.shape[0]
    grid_spec = pltpu.PrefetchScalarGridSpec(
        num_scalar_prefetch=2, grid=(1,), in_specs=[],
        out_specs=pl.BlockSpec(memory_space=pltpu.SMEM))
    return pl.pallas_call(
        functools.partial(_invert_kernel, T=T, R=R),
        grid_spec=grid_spec,
        out_shape=jax.ShapeDtypeStruct((R,), I32),
        compiler_params=_params("arbitrary"),
        name="moe_invert",
    )(d1, d2)


GATHER_CHUNK = 8


def _tile_copy(src_hbm, row, dst, slot, r, sem):
    return pltpu.make_async_copy(src_hbm.at[pl.ds(row * TOKEN_ROWS, TOKEN_ROWS)],
                                 dst.at[slot, pl.ds(r * TOKEN_ROWS, TOKEN_ROWS)], sem.at[slot])


def _expert_kernel(te_ref, first_ref, nxt_ref, wsl_ref, nch_ref, src_ref,
                   hp_hbm, w1_hbm, w3_hbm, w2_hbm, o_ref,
                   xbuf, r1, r3, r2, w1b, w3b, w2b, gsem, wsem, *, tr, base):
    i = pl.program_id(0)
    n = pl.num_programs(0)
    slot = i % 2

    def weight_copies(e, ws):
        return [pltpu.make_async_copy(w1_hbm.at[base + e], r1.at[ws], wsem.at[ws, 0]),
                pltpu.make_async_copy(w3_hbm.at[base + e], r3.at[ws], wsem.at[ws, 1]),
                pltpu.make_async_copy(w2_hbm.at[base + e], r2.at[ws], wsem.at[ws, 2])]

    def issue_rows(tile, slot):
        def body(c, _):
            for u in range(GATHER_CHUNK):
                r = c * GATHER_CHUNK + u
                _tile_copy(hp_hbm, src_ref[tile * tr + r], xbuf, slot, r, gsem).start()
            return 0
        lax.fori_loop(0, nch_ref[tile], body, 0)

    def wait_rows(tile, slot):
        rows = GATHER_CHUNK * TOKEN_ROWS
        def body(c, _):
            pltpu.make_async_copy(hp_hbm.at[pl.ds(0, rows)], xbuf.at[slot, pl.ds(0, rows)],
                                  gsem.at[slot]).wait()
            return 0
        lax.fori_loop(0, nch_ref[tile], body, 0)

    @pl.when(i == 0)
    def _():
        xbuf[...] = jnp.zeros_like(xbuf)
        for cp in weight_copies(te_ref[0], 0):
            cp.start(priority=1)
        issue_rows(0, 0)

    wait_rows(i, slot)

    @pl.when(i + 1 < n)
    def _():
        issue_rows(i + 1, 1 - slot)

    @pl.when(first_ref[i] == 1)
    def _():
        ws = wsl_ref[i]
        for cp in weight_copies(te_ref[i], ws):
            cp.wait()

        @pl.when(nxt_ref[i] >= 0)
        def _():
            for cp in weight_copies(nxt_ref[i], 1 - ws):
                cp.start(priority=1)

        w1b[...] = r1[ws].astype(BF16)
        w3b[...] = r3[ws].astype(BF16)
        w2b[...] = r2[ws].astype(BF16)

    @pl.when(nch_ref[i] > 0)
    def _():
        x = jnp.concatenate([p.astype(BF16) for p in _load_token_tiles(xbuf.at[slot], 0, tr)],
                            axis=1)
        h1 = jnp.dot(x, w1b[...], preferred_element_type=F32)
        h3 = jnp.dot(x, w3b[...], preferred_element_type=F32)
        hm = (h1 * _sigmoid(h1) * h3).astype(BF16)
        _store_token_tiles(o_ref, jnp.dot(hm, w2b[...], preferred_element_type=F32), tr)

    @pl.when(nch_ref[i] == 0)
    def _():
        o_ref[...] = jnp.zeros_like(o_ref)


def moe_experts(hp, w1, w3, w2, layer, tables, src):
    D, F = w1.shape[-2:]
    tr = MOE_ROWS
    te, first, nxt, wsl, nch = tables
    NT = te.shape[0]
    w1f = w1.reshape(-1, D, F)
    w3f = w3.reshape(-1, D, F)
    w2f = w2.reshape(-1, F, D)
    hbm = pl.BlockSpec(memory_space=pl.ANY)
    grid_spec = pltpu.PrefetchScalarGridSpec(
        num_scalar_prefetch=6,
        grid=(NT,),
        in_specs=[hbm, hbm, hbm, hbm],
        out_specs=pl.BlockSpec((tr * TOKEN_ROWS, LANES), lambda i, *_: (i, 0)),
        scratch_shapes=[pltpu.VMEM((2, tr * TOKEN_ROWS, LANES), F32),
                        pltpu.VMEM((2, D, F), F32),
                        pltpu.VMEM((2, D, F), F32),
                        pltpu.VMEM((2, F, D), F32),
                        pltpu.VMEM((D, F), BF16),
                        pltpu.VMEM((D, F), BF16),
                        pltpu.VMEM((F, D), BF16),
                        pltpu.SemaphoreType.DMA((2,)),
                        pltpu.SemaphoreType.DMA((2, 3))],
    )
    return pl.pallas_call(
        functools.partial(_expert_kernel, tr=tr, base=layer * N_EXPERTS),
        grid_spec=grid_spec,
        out_shape=jax.ShapeDtypeStruct((NT * tr * TOKEN_ROWS, LANES), F32),
        compiler_params=_params("arbitrary"),
        name="moe_experts",
    )(te, first, nxt, wsl, nch, src, hp, w1f, w3f, w2f)


def _combine_kernel(d1_ref, d2_ref, x_ref, meta_ref, ys_hbm, o_ref, buf, sem, *, tt):
    i = pl.program_id(0)
    n = pl.num_programs(0)
    slot = i % 2

    def issue(tile, slot):
        def body(c, _):
            for u in range(GATHER_CHUNK):
                r = c * GATHER_CHUNK + u
                t = tile * tt + r
                _tile_copy(ys_hbm, d1_ref[t], buf, slot, r, sem).start(priority=0)
                _tile_copy(ys_hbm, d2_ref[t], buf, slot, tt + r, sem).start(priority=1)
            return 0
        lax.fori_loop(0, tt // GATHER_CHUNK, body, 0)

    @pl.when(i == 0)
    def _():
        issue(0, 0)

    pltpu.make_async_copy(ys_hbm.at[pl.ds(0, 2 * tt * TOKEN_ROWS)], buf.at[slot], sem.at[slot]).wait()

    @pl.when(i + 1 < n)
    def _():
        issue(i + 1, 1 - slot)

    w1 = meta_ref[:, 2:3]
    w2 = meta_ref[:, 3:4]
    pa = _load_token_tiles(buf.at[slot], 0, tt)
    pb = _load_token_tiles(buf.at[slot], tt * TOKEN_ROWS, tt)
    for j in range(TOKEN_ROWS):
        sl = slice(j * LANES, (j + 1) * LANES)
        o_ref[:, sl] = x_ref[:, sl] + w1 * pa[j] + w2 * pb[j]


def moe_combine(x, meta, ys, d1, d2):
    T, D = x.shape
    tt = COMB_ROWS
    grid_spec = pltpu.PrefetchScalarGridSpec(
        num_scalar_prefetch=2,
        grid=(T // tt,),
        in_specs=[pl.BlockSpec((tt, D), lambda i, d1, d2: (i, 0)),
                  pl.BlockSpec((tt, LANES), lambda i, d1, d2: (i, 0)),
                  pl.BlockSpec(memory_space=pl.ANY)],
        out_specs=pl.BlockSpec((tt, D), lambda i, d1, d2: (i, 0)),
        scratch_shapes=[pltpu.VMEM((2, 2 * tt * TOKEN_ROWS, LANES), F32),
                        pltpu.SemaphoreType.DMA((2,))],
    )
    return pl.pallas_call(
        functools.partial(_combine_kernel, tt=tt),
        grid_spec=grid_spec,
        out_shape=jax.ShapeDtypeStruct((T, D), F32),
        compiler_params=_params("arbitrary"),
        name="moe_combine",
    )(d1, d2, x, meta, ys)


def _tile_tables(cnt, T):
    E = N_EXPERTS
    counts = cnt[0, ROUTER_LANE0:ROUTER_LANE0 + E].astype(I32)
    tiles_e = (counts + MOE_ROWS - 1) // MOE_ROWS
    tile_end = jnp.cumsum(tiles_e)
    tile_start = tile_end - tiles_e
    nact = tile_end[-1]
    NT = (2 * T) // MOE_ROWS + E
    tid = jnp.arange(NT, dtype=I32)
    te = jnp.sum((jnp.minimum(tid, nact - 1)[:, None] >= tile_end[None, :]).astype(I32), axis=1)
    te = jnp.minimum(te, E - 1)
    active = tid < nact
    first = (active & (tid == tile_start[te])).astype(I32)
    eid = jnp.arange(E, dtype=I32)
    later = (eid[None, :] > eid[:, None]) & (tiles_e[None, :] > 0)
    nxt_e = jnp.min(jnp.where(later, eid[None, :], E), axis=1)
    nxt_e = jnp.where(nxt_e == E, -1, nxt_e)
    ordinal = jnp.cumsum((tiles_e > 0).astype(I32)) - 1
    valid = jnp.clip(counts[te] - (tid - tile_start[te]) * MOE_ROWS, 0, MOE_ROWS)
    nch = jnp.where(active, (valid + GATHER_CHUNK - 1) // GATHER_CHUNK, 0)
    return (te, first, nxt_e[te], ordinal[te] % 2, nch), NT * MOE_ROWS


def _lower_bound_logs(lb_param, layer):
    p = jax.nn.softmax(lb_param.astype(F32), axis=0)
    c = jnp.cumsum(p, axis=0)
    lb = c[layer] - c[0]
    return jnp.log(lb), jnp.log1p(-lb)


def kernel(x, mem, norm_mix, w_in, hg_lower_bounds, hg_norm, conv_w, fox_f_bias, fox_q_norm,
           fox_k_norm, w_out, norm_cross, norm_mem, w_cq, w_ck, w_cv, w_co, cross_q_norm,
           cross_k_norm, norm_ffn, router_group_w, router_group_b, router_expert_w,
           router_expert_b, moe_w1, moe_w3, moe_w2):
    B, S, D = x.shape
    M = mem.shape[1]
    L = w_in.shape[0]
    T = B * S
    HGW = 4 * HG_WIDTH
    CONV3 = 3 * CONV_WIDTH
    FQ0 = CONV3
    FK0 = FQ0 + FOX_WIDTH
    FV0 = FK0 + FOX_WIDTH
    PB = CONV3 + 3 * FOX_WIDTH
    dhc = D // CROSS_HEADS
    x2 = x.reshape(T, D)
    mem2 = mem.reshape(B * M, D)
    w_in_t = jnp.swapaxes(w_in, 1, 2)
    for l in range(L):
        hn = rmsnorm(x2, norm_mix[l])
        proj_a = matmul([hn], w_in_t, l, 0, HGW, F32, w_t=True)
        proj_b = matmul([hn], w_in_t, l, HGW, PB, BF16, w_t=True)
        loglb, l1mlb = _lower_bound_logs(hg_lower_bounds, l)
        y_hg = hgrn2(proj_a.reshape(B, S, HGW), loglb, l1mlb, hg_norm[l])
        pb3 = proj_b.reshape(B, S, PB)
        y_conv = short_conv(pb3, conv_w[l])
        c = fox_gate(hn.reshape(B, S, D), w_in_t, l, HGW + PB, fox_f_bias[l])
        qn = headnorm(proj_b, fox_q_norm[l], FQ0, FOX_WIDTH, FOX_DIM, scale=FOX_DIM ** -0.5)
        kn = headnorm(proj_b, fox_k_norm[l], FK0, FOX_WIDTH, FOX_DIM)
        y_fox = fox_attention(qn.reshape(B, S, FOX_WIDTH), kn.reshape(B, S, FOX_WIDTH), pb3, FV0, c)
        mix = [y_hg.reshape(T, HG_WIDTH), y_conv.reshape(T, CONV_WIDTH), y_fox.reshape(T, FOX_WIDTH)]
        x2 = matmul(mix, w_out, l, 0, D, F32, residual=x2)
        memn = rmsnorm(mem2, norm_mem[l])
        kc = matmul([memn], w_ck, l, 0, D, F32)
        vc = matmul([memn], w_cv, l, 0, D, BF16)
        kcn = headnorm(kc, cross_k_norm[l], 0, D, dhc)
        hc = rmsnorm(x2, norm_cross[l])
        qc = matmul([hc], w_cq, l, 0, D, F32)
        qcn = headnorm(qc, cross_q_norm[l], 0, D, dhc, scale=dhc ** -0.5)
        oc = cross_attention(qcn.reshape(B, S, D), kcn.reshape(B, M, D), vc.reshape(B, M, D))
        x2 = matmul([oc.reshape(T, D)], w_co, l, 0, D, F32, residual=x2)
        wr = jnp.concatenate([router_group_w[l], router_expert_w[l]], axis=1)
        wr = jnp.pad(wr, ((0, 0), (0, LANES - wr.shape[1])))
        br = jnp.concatenate([router_group_b[l], router_expert_b[l]])
        br = jnp.pad(br, (0, LANES - br.shape[0]))
        hp, logits = rmsnorm_router(x2, norm_ffn[l], wr, br)
        meta, cnt = route(logits)
        d = plan(meta, cnt)
        tables, R = _tile_tables(cnt, T)
        src = invert(d[0], d[1], R)
        ys = moe_experts(hp, moe_w1, moe_w3, moe_w2, l, tables, src)
        x2 = moe_combine(x2, meta, ys, d[0], d[1])
    return x2.reshape(B, S, D)
```

```python
import functools

import numpy as np
import jax
import jax.numpy as jnp
from jax import lax
from jax.experimental import pallas as pl
from jax.experimental.pallas import tpu as pltpu

F32 = jnp.float32
BF16 = jnp.bfloat16
I32 = jnp.int32
EPS = 1e-6

HG_HEADS = 4
HG_DIM = 128
HG_WIDTH = HG_HEADS * HG_DIM
CONV_WIDTH = 512
FOX_HEADS = 8
FOX_DIM = 128
FOX_WIDTH = FOX_HEADS * FOX_DIM
CROSS_HEADS = 4
MOE_GROUPS = 4
MOE_EXPERTS = 8
N_EXPERTS = MOE_GROUPS * MOE_EXPERTS
ROUTER_LANE0 = MOE_GROUPS

LANES = 128
SUBLANES = 8
VMEM_LIMIT = 56 * 1024 * 1024

NORM_ROWS = 256
MM_TM = 1024
MM_TN = 512
HG_CHUNK = 128
FOX_TQ = 512
CROSS_TS = 512
ROUTE_ROWS = 512
MOE_ROWS_LOG2 = 8
MOE_ROWS = 1 << MOE_ROWS_LOG2
COMB_ROWS = 256


def _params(*sem):
    return pltpu.CompilerParams(dimension_semantics=sem, vmem_limit_bytes=VMEM_LIMIT)


def _sigmoid(x):
    return 1.0 / (1.0 + jnp.exp(-x))


def _log_sigmoid(x):
    return jnp.minimum(x, 0.0) - jnp.log1p(jnp.exp(-jnp.abs(x)))


TOKEN_ROWS = 16


def _store_token_tiles(ref, y, rows):
    for j in range(TOKEN_ROWS):
        ref[pl.ds(j, rows, stride=TOKEN_ROWS), :] = y[:, j * LANES:(j + 1) * LANES]


def _load_token_tiles(ref, base, rows):
    return [ref[pl.ds(base + j, rows, stride=TOKEN_ROWS), :] for j in range(TOKEN_ROWS)]


def _rms(x):
    return x * lax.rsqrt(jnp.mean(x * x, axis=-1, keepdims=True) + EPS)


def _rmsnorm_kernel(x_ref, g_ref, o_ref):
    o_ref[...] = (_rms(x_ref[...]) * g_ref[...]).astype(o_ref.dtype)


def rmsnorm(x, g, out_dtype=BF16):
    R, D = x.shape
    return pl.pallas_call(
        _rmsnorm_kernel,
        grid=(R // NORM_ROWS,),
        in_specs=[pl.BlockSpec((NORM_ROWS, D), lambda i: (i, 0)),
                  pl.BlockSpec((1, D), lambda i: (0, 0))],
        out_specs=pl.BlockSpec((NORM_ROWS, D), lambda i: (i, 0)),
        out_shape=jax.ShapeDtypeStruct((R, D), out_dtype),
        compiler_params=_params("parallel"),
        name="rmsnorm",
    )(x, g.reshape(1, D))


def _head_rmsnorm(x, g, dh, scale):
    heads = [_rms(x[:, h * dh:(h + 1) * dh]) * g * scale for h in range(x.shape[1] // dh)]
    return heads[0] if len(heads) == 1 else jnp.concatenate(heads, axis=1)


def _matmul_kernel(*refs, n_parts, w_t, norm_tiles, dh):
    a_refs = refs[:n_parts]
    w_ref = refs[n_parts]
    if norm_tiles:
        g_ref, o_ref, wb_ref = refs[n_parts + 1:]
    else:
        o_ref, wb_ref = refs[n_parts + 1:]
    j = pl.program_id(0)

    @pl.when(pl.program_id(1) == 0)
    def _():
        wb_ref[...] = w_ref[...].astype(BF16)

    if n_parts == 1:
        a = a_refs[0][...]
    else:
        a = jnp.concatenate([r[...] for r in a_refs], axis=1)
    if w_t:
        acc = lax.dot_general(a, wb_ref[...], (((1,), (1,)), ((), ())), preferred_element_type=F32)
    else:
        acc = jnp.dot(a, wb_ref[...], preferred_element_type=F32)
    if not norm_tiles:
        o_ref[...] = acc.astype(o_ref.dtype)
        return
    plain = True
    for k, (j0, j1, scale) in enumerate(norm_tiles):
        hit = (j >= j0) & (j < j1)
        plain = plain & jnp.logical_not(hit)

        @pl.when(hit)
        def _(k=k, scale=scale):
            o_ref[...] = _head_rmsnorm(acc, g_ref[k:k + 1, :], dh, scale).astype(o_ref.dtype)

    @pl.when(plain)
    def _():
        o_ref[...] = acc.astype(o_ref.dtype)


def matmul(a_parts, w, layer, col0, n, out_dtype, w_t=False, head_norms=(), tm=MM_TM, tn=MM_TN):
    M = a_parts[0].shape[0]
    K = w.shape[2] if w_t else w.shape[1]
    assert sum(p.shape[1] for p in a_parts) == K
    tm = min(tm, M)
    assert col0 % tn == 0 and n % tn == 0 and M % tm == 0
    cb = col0 // tn
    in_specs = [pl.BlockSpec((tm, p.shape[1]), lambda j, i: (i, 0)) for p in a_parts]
    if w_t:
        in_specs.append(pl.BlockSpec((None, tn, K), lambda j, i: (layer, cb + j, 0)))
    else:
        in_specs.append(pl.BlockSpec((None, K, tn), lambda j, i: (layer, 0, cb + j)))
    args = list(a_parts) + [w]
    norm_tiles, dh = (), 0
    if head_norms:
        dh = head_norms[0][2].shape[0]
        assert all(c0 % tn == 0 and c1 % tn == 0 and g.shape[0] == dh and tn % dh == 0
                   for c0, c1, g, _ in head_norms)
        norm_tiles = tuple((c0 // tn, c1 // tn, s) for c0, c1, _, s in head_norms)
        gains = jnp.stack([g for _, _, g, _ in head_norms])
        in_specs.append(pl.BlockSpec(gains.shape, lambda j, i: (0, 0)))
        args.append(gains)
    return pl.pallas_call(
        functools.partial(_matmul_kernel, n_parts=len(a_parts), w_t=w_t, norm_tiles=norm_tiles,
                          dh=dh),
        grid=(n // tn, M // tm),
        in_specs=in_specs,
        out_specs=pl.BlockSpec((tm, tn), lambda j, i: (i, j)),
        out_shape=jax.ShapeDtypeStruct((M, n), out_dtype),
        scratch_shapes=[pltpu.VMEM((tn, K) if w_t else (K, tn), BF16)],
        compiler_params=_params("arbitrary", "arbitrary"),
        name="matmul",
    )(*args)


W_CHUNK = 256


def _load_weight(w_hbm, layer, wb, stg, sem):
    n_chunks = wb.shape[0] // W_CHUNK

    def copy(c):
        return pltpu.make_async_copy(w_hbm.at[layer, pl.ds(c * W_CHUNK, W_CHUNK)], stg.at[c % 2],
                                     sem.at[c % 2])
    copy(0).start()
    for c in range(n_chunks):
        if c + 1 < n_chunks:
            copy(c + 1).start()
        copy(c).wait()
        wb[c * W_CHUNK:(c + 1) * W_CHUNK, :] = stg[c % 2].astype(BF16)


def _outq_kernel(a1_ref, a2_ref, a3_ref, x_ref, wo_hbm, wq_hbm, gn_ref, gq_ref, xo_ref, q_ref,
                 wo_b, wq_b, stg, sem, *, layer, dh, scale):
    @pl.when(pl.program_id(0) == 0)
    def _():
        _load_weight(wo_hbm, layer, wo_b, stg, sem)
        _load_weight(wq_hbm, layer, wq_b, stg, sem)

    a = jnp.concatenate([a1_ref[...], a2_ref[...], a3_ref[...]], axis=1)
    xn = x_ref[...] + jnp.dot(a, wo_b[...], preferred_element_type=F32)
    xo_ref[...] = xn
    hc = (_rms(xn) * gn_ref[...]).astype(BF16)
    q = jnp.dot(hc, wq_b[...], preferred_element_type=F32)
    q_ref[...] = _head_rmsnorm(q, gq_ref[...], dh, scale).astype(q_ref.dtype)


def out_proj_cross_q(mix, x, w_out, w_cq, layer, g_cross, g_q, tm=512):
    T, D = x.shape
    dh = g_q.shape[0]
    row = lambda w: pl.BlockSpec((tm, w), lambda i: (i, 0))
    hbm = pl.BlockSpec(memory_space=pl.ANY)
    return pl.pallas_call(
        functools.partial(_outq_kernel, layer=layer, dh=dh, scale=dh ** -0.5),
        grid=(T // tm,),
        in_specs=[row(mix[0].shape[1]), row(mix[1].shape[1]), row(mix[2].shape[1]), row(D), hbm, hbm,
                  pl.BlockSpec((1, D), lambda i: (0, 0)), pl.BlockSpec((1, dh), lambda i: (0, 0))],
        out_specs=[row(D), row(D)],
        out_shape=[jax.ShapeDtypeStruct((T, D), F32), jax.ShapeDtypeStruct((T, D), BF16)],
        scratch_shapes=[pltpu.VMEM((D, D), BF16), pltpu.VMEM((D, D), BF16),
                        pltpu.VMEM((2, W_CHUNK, D), F32), pltpu.SemaphoreType.DMA((2,))],
        compiler_params=_params("arbitrary"),
        name="out_proj_cross_q",
    )(*mix, x, w_out, w_cq, g_cross.reshape(1, D), g_q.reshape(1, dh))


def _hgrn_consts(C):
    nlev = int(np.log2(C))
    t = np.arange(C)[:, None]
    u = np.arange(C)[None, :]
    blocks = [(u <= t)]
    for j in range(nlev):
        m = 1 << j
        boundary = (t // (2 * m)) * (2 * m) + m - 1
        second = (t & m) != 0
        blocks.append(np.where(second, (u > boundary) & (u <= t), (u > t) & (u <= boundary)))
    mall = np.concatenate(blocks, axis=0).astype(np.float32)
    mall2 = np.concatenate([mall, mall], axis=1)
    tt = np.arange(C)[:, None]
    ss = np.arange(C)[None, :]
    lev = np.full((C, C), nlev + 1, np.int32)
    x = tt ^ ss
    hb = np.zeros_like(x)
    for j in range(nlev):
        hb = np.where((x >> j) & 1, j, hb)
    lev = np.where(tt > ss, hb, lev)
    lev = np.where(tt == ss, nlev, lev)
    return jnp.asarray(mall2, BF16), jnp.asarray(lev, jnp.int32)


def _hgrn_kernel(x_ref, mall_ref, lev_ref, loglb_ref, l1mlb_ref, ng_ref, o_ref, st_ref, *, C, nlev):
    @pl.when(pl.program_id(1) == 0)
    def _():
        st_ref[...] = jnp.zeros_like(st_ref)

    W = HG_WIDTH
    d = HG_DIM
    lev = lev_ref[...]
    mall = mall_ref[...]
    row = lax.broadcasted_iota(jnp.int32, (C, d), 0)
    nt = (((1,), (1,)), ((), ()))
    for h in range(HG_HEADS):
        sl = slice(h * d, (h + 1) * d)
        q = x_ref[0, :, h * d:(h + 1) * d]
        z = x_ref[0, :, W + h * d:W + (h + 1) * d]
        v = x_ref[0, :, 2 * W + h * d:2 * W + (h + 1) * d]
        g = x_ref[0, :, 3 * W + h * d:3 * W + (h + 1) * d]
        a = loglb_ref[:, sl]
        bt = l1mlb_ref[:, sl] + _log_sigmoid(z)
        lf = jnp.maximum(a, bt) + jnp.log1p(jnp.exp(-jnp.abs(a - bt)))
        kk = 1.0 - jnp.exp(lf)
        hi = lf.astype(BF16)
        lo = (lf - hi.astype(F32)).astype(BF16)
        sums = jnp.dot(mall, jnp.concatenate([hi, lo], axis=0), preferred_element_type=F32)
        b = sums[0:C]
        vb = v.astype(BF16)
        st = st_ref[h]
        o = lax.dot_general((q * jnp.exp(b)).astype(BF16), st.astype(BF16), nt,
                            preferred_element_type=F32)
        att = jnp.where(lev == nlev, jnp.sum(q * kk, axis=-1, keepdims=True), 0.0)
        for j in range(nlev):
            second = (row & (1 << j)) != 0
            e = jnp.exp(sums[(j + 1) * C:(j + 2) * C])
            xq = jnp.where(second, q * e, 0.0).astype(BF16)
            xk = jnp.where(second, 0.0, kk * e).astype(BF16)
            am = lax.dot_general(xq, xk, nt, preferred_element_type=F32)
            att = jnp.where(lev == j, am, att)
        o = o + jnp.dot(att.astype(BF16), vb, preferred_element_type=F32)
        bl = b[C - 1:C, :]
        kh = (kk * jnp.exp(bl - b)).astype(BF16)
        st_ref[h] = st * jnp.exp(bl) + lax.dot_general(
            vb, kh, (((0,), (0,)), ((), ())), preferred_element_type=F32)
        y = o * lax.rsqrt(jnp.mean(o * o, axis=-1, keepdims=True) + EPS) * ng_ref[...]
        o_ref[0, :, sl] = (y * (g * _sigmoid(g))).astype(o_ref.dtype)


def hgrn2(proj_a, loglb, l1mlb, norm_g):
    B, S, _ = proj_a.shape
    C = HG_CHUNK
    nlev = int(np.log2(C))
    mall, lev = _hgrn_consts(C)
    return pl.pallas_call(
        functools.partial(_hgrn_kernel, C=C, nlev=nlev),
        grid=(B, S // C),
        in_specs=[pl.BlockSpec((1, C, 4 * HG_WIDTH), lambda b, c: (b, c, 0)),
                  pl.BlockSpec(mall.shape, lambda b, c: (0, 0)),
                  pl.BlockSpec((C, C), lambda b, c: (0, 0)),
                  pl.BlockSpec((1, HG_WIDTH), lambda b, c: (0, 0)),
                  pl.BlockSpec((1, HG_WIDTH), lambda b, c: (0, 0)),
                  pl.BlockSpec((1, HG_DIM), lambda b, c: (0, 0))],
        out_specs=pl.BlockSpec((1, C, HG_WIDTH), lambda b, c: (b, c, 0)),
        out_shape=jax.ShapeDtypeStruct((B, S, HG_WIDTH), BF16),
        scratch_shapes=[pltpu.VMEM((HG_HEADS, HG_DIM, HG_DIM), F32)],
        compiler_params=_params("parallel", "arbitrary"),
        name="hgrn2",
    )(proj_a, mall, lev, loglb.reshape(1, HG_WIDTH), l1mlb.reshape(1, HG_WIDTH),
      norm_g.reshape(1, HG_DIM))


def _conv_kernel(b_ref, c_ref, h_ref, w_ref, o_ref):
    u = c_ref[0].astype(F32) * h_ref[0].astype(F32)
    row = lax.broadcasted_iota(jnp.int32, u.shape, 0)
    u1 = jnp.where(row >= 1, pltpu.roll(u, 1, axis=0), 0.0)
    u2 = jnp.where(row >= 2, pltpu.roll(u, 2, axis=0), 0.0)
    y = w_ref[0:1, :] * u2 + w_ref[1:2, :] * u1 + w_ref[2:3, :] * u
    o_ref[0] = (b_ref[0].astype(F32) * y).astype(o_ref.dtype)


def short_conv(proj_b, w):
    B, S, _ = proj_b.shape
    cw = CONV_WIDTH
    spec = lambda k: pl.BlockSpec((1, S, cw), lambda b: (b, 0, k))
    return pl.pallas_call(
        _conv_kernel,
        grid=(B,),
        in_specs=[spec(0), spec(1), spec(2), pl.BlockSpec((3, cw), lambda b: (0, 0))],
        out_specs=pl.BlockSpec((1, S, cw), lambda b: (b, 0, 0)),
        out_shape=jax.ShapeDtypeStruct((B, S, cw), BF16),
        compiler_params=_params("parallel"),
        name="short_conv",
    )(proj_b, proj_b, proj_b, w)


def _fox_gate_kernel(hn_ref, wf_ref, bias_ref, c_ref):
    gl = lax.dot_general(wf_ref[...].astype(BF16), hn_ref[0], (((1,), (1,)), ((), ())),
                         preferred_element_type=F32)
    c = _log_sigmoid(gl + bias_ref[...])
    S = c.shape[1]
    lane = lax.broadcasted_iota(jnp.int32, c.shape, 1)
    sh = 1
    while sh < S:
        c = c + jnp.where(lane >= sh, pltpu.roll(c, sh, axis=1), 0.0)
        sh *= 2
    c_ref[0] = c


def fox_gate(hn, w_t, layer, row0, bias):
    B, S, D = hn.shape
    H = bias.shape[0]
    assert row0 % H == 0
    return pl.pallas_call(
        _fox_gate_kernel,
        grid=(B,),
        in_specs=[pl.BlockSpec((1, S, D), lambda b: (b, 0, 0)),
                  pl.BlockSpec((None, H, D), lambda b: (layer, row0 // H, 0)),
                  pl.BlockSpec((H, 1), lambda b: (0, 0))],
        out_specs=pl.BlockSpec((1, H, S), lambda b: (b, 0, 0)),
        out_shape=jax.ShapeDtypeStruct((B, H, S), F32),
        compiler_params=_params("parallel"),
        name="fox_gate",
    )(hn, w_t, bias.reshape(H, 1))


def _fox_kernel(q_ref, k_ref, v_ref, c_ref, o_ref, *, tq):
    qi = pl.program_id(2)
    q = q_ref[0]
    nt = (((1,), (1,)), ((), ()))

    def step(ki, carry, masked):
        m, l, acc = carry
        k = k_ref[0, pl.ds(ki * tq, tq), :]
        v = v_ref[0, pl.ds(ki * tq, tq), :]
        ck = c_ref[0, 0, pl.ds(ki, 1), :]
        s = lax.dot_general(q, k, nt, preferred_element_type=F32) - ck
        if masked:
            r = lax.broadcasted_iota(jnp.int32, s.shape, 0)
            c = lax.broadcasted_iota(jnp.int32, s.shape, 1)
            s = jnp.where(c <= r, s, -jnp.inf)
        m_new = jnp.maximum(m, jnp.max(s, axis=-1, keepdims=True))
        alpha = jnp.exp(m - m_new)
        p = jnp.exp(s - m_new)
        l = alpha * l + jnp.sum(p, axis=-1, keepdims=True)
        acc = alpha * acc + jnp.dot(p.astype(BF16), v, preferred_element_type=F32)
        return m_new, l, acc

    init = (jnp.full((tq, 1), -jnp.inf, F32), jnp.zeros((tq, 1), F32),
            jnp.zeros((tq, FOX_DIM), F32))
    carry = lax.fori_loop(0, qi, lambda ki, cr: step(ki, cr, False), init)
    _, l, acc = step(qi, carry, True)
    o_ref[0] = (acc / l).astype(o_ref.dtype)


def fox_attention(proj_b, q_col0, k_col0, v_col0, c):
    B, S, _ = proj_b.shape
    tq = FOX_TQ
    H = FOX_HEADS
    qb, kb, vb = q_col0 // FOX_DIM, k_col0 // FOX_DIM, v_col0 // FOX_DIM
    c4 = c.reshape(B, H, S // tq, tq)
    return pl.pallas_call(
        functools.partial(_fox_kernel, tq=tq),
        grid=(B, H, S // tq),
        in_specs=[pl.BlockSpec((1, tq, FOX_DIM), lambda b, h, i: (b, i, qb + h)),
                  pl.BlockSpec((1, S, FOX_DIM), lambda b, h, i: (b, 0, kb + h)),
                  pl.BlockSpec((1, S, FOX_DIM), lambda b, h, i: (b, 0, vb + h)),
                  pl.BlockSpec((1, 1, S // tq, tq), lambda b, h, i: (b, h, 0, 0))],
        out_specs=pl.BlockSpec((1, tq, FOX_DIM), lambda b, h, i: (b, i, h)),
        out_shape=jax.ShapeDtypeStruct((B, S, FOX_WIDTH), BF16),
        compiler_params=_params("parallel", "parallel", "arbitrary"),
        name="fox_attention",
    )(proj_b, proj_b, proj_b, c4)


def _cross_kernel(q_ref, k_ref, v_ref, x_ref, wo_hbm, gn_ref, wr_ref, br_ref,
                  xo_ref, hp_ref, lg_ref, wo_b, stg, sem, *, layer, dh):
    @pl.when((pl.program_id(0) == 0) & (pl.program_id(1) == 0))
    def _():
        _load_weight(wo_hbm, layer, wo_b, stg, sem)

    nt = (((1,), (1,)), ((), ()))
    heads = []
    for h in range(CROSS_HEADS):
        sl = slice(h * dh, (h + 1) * dh)
        s = lax.dot_general(q_ref[0, :, sl], k_ref[0, :, sl], nt, preferred_element_type=F32)
        m = jnp.max(s, axis=-1, keepdims=True)
        p = jnp.exp(s - m)
        l = jnp.sum(p, axis=-1, keepdims=True)
        o = jnp.dot(p.astype(BF16), v_ref[0, :, sl], preferred_element_type=F32)
        heads.append((o / l).astype(BF16))
    xn = x_ref[0] + jnp.dot(jnp.concatenate(heads, axis=1), wo_b[...], preferred_element_type=F32)
    xo_ref[0] = xn
    y = _rms(xn) * gn_ref[...]
    _store_token_tiles(hp_ref, y, y.shape[0])
    lg_ref[...] = jnp.dot(y, wr_ref[...], precision=lax.Precision.HIGHEST,
                          preferred_element_type=F32) + br_ref[...]


def cross_attention_ffn_in(qn, kn, v, x, w_co, layer, g_ffn, wr, br):
    B, S, D = qn.shape
    M = kn.shape[1]
    ts = CROSS_TS
    ns = S // ts
    assert D == TOKEN_ROWS * LANES
    tile = pl.BlockSpec((1, ts, D), lambda b, i: (b, i, 0))
    memb = pl.BlockSpec((1, M, D), lambda b, i: (b, 0, 0))
    const = lambda shape: pl.BlockSpec(shape, lambda b, i: (0, 0))
    return pl.pallas_call(
        functools.partial(_cross_kernel, layer=layer, dh=D // CROSS_HEADS),
        grid=(B, ns),
        in_specs=[tile, memb, memb, tile, pl.BlockSpec(memory_space=pl.ANY),
                  const((1, D)), const((D, LANES)), const((1, LANES))],
        out_specs=[tile,
                   pl.BlockSpec((ts * TOKEN_ROWS, LANES), lambda b, i: (b * ns + i, 0)),
                   pl.BlockSpec((ts, LANES), lambda b, i: (b * ns + i, 0))],
        out_shape=[jax.ShapeDtypeStruct((B, S, D), F32),
                   jax.ShapeDtypeStruct((B * S * TOKEN_ROWS, LANES), F32),
                   jax.ShapeDtypeStruct((B * S, LANES), F32)],
        scratch_shapes=[pltpu.VMEM((D, D), BF16), pltpu.VMEM((2, W_CHUNK, D), F32),
                        pltpu.SemaphoreType.DMA((2,))],
        compiler_params=_params("arbitrary", "arbitrary"),
        name="cross_attention_ffn_in",
    )(qn, kn, v, x, w_co, g_ffn.reshape(1, D), wr, br.reshape(1, LANES))


def _route_kernel(lg_ref, tri_ref, meta_ref, cnt_ref, carry_ref):
    @pl.when(pl.program_id(0) == 0)
    def _():
        carry_ref[...] = jnp.zeros_like(carry_ref)

    x = lg_ref[...]
    lane = lax.broadcasted_iota(jnp.int32, x.shape, 1)
    ninf = -jnp.inf
    is_g = lane < MOE_GROUPS
    gl = jnp.where(is_g, x, ninf)
    gmax = jnp.max(gl, axis=-1, keepdims=True)
    gidx = jnp.min(jnp.where(gl == gmax, lane, LANES), axis=-1, keepdims=True)
    gw = 1.0 / jnp.sum(jnp.where(is_g, jnp.exp(x - gmax), 0.0), axis=-1, keepdims=True)
    e_lane = lane - ROUTER_LANE0
    in_grp = (e_lane >= 0) & (e_lane < N_EXPERTS) & ((e_lane >> 3) == gidx)
    el = jnp.where(in_grp, x, ninf)
    m1 = jnp.max(el, axis=-1, keepdims=True)
    i1 = jnp.min(jnp.where(el == m1, lane, LANES), axis=-1, keepdims=True)
    el2 = jnp.where(lane == i1, ninf, el)
    m2 = jnp.max(el2, axis=-1, keepdims=True)
    i2 = jnp.min(jnp.where(el2 == m2, lane, LANES), axis=-1, keepdims=True)
    r = jnp.exp(m2 - m1)
    p1 = 1.0 / (1.0 + r)
    p2 = r * p1
    oh = ((lane == i1) | (lane == i2)).astype(F32)
    prefix = jnp.dot(tri_ref[...], oh.astype(BF16), preferred_element_type=F32) + carry_ref[...]
    rank1 = jnp.sum(jnp.where(lane == i1, prefix, 0.0), axis=-1, keepdims=True)
    rank2 = jnp.sum(jnp.where(lane == i2, prefix, 0.0), axis=-1, keepdims=True)
    carry_ref[...] += jnp.sum(oh, axis=0, keepdims=True)
    cnt_ref[...] = carry_ref[...]
    cols = [i1.astype(F32), i2.astype(F32), gw * p1, gw * p2, rank1, rank2]
    meta = jnp.zeros(x.shape, F32)
    for k, col in enumerate(cols):
        meta = jnp.where(lane == k, col, meta)
    meta_ref[...] = meta


def route(logits):
    T = logits.shape[0]
    tt = ROUTE_ROWS
    tri = jnp.asarray(np.tril(np.ones((tt, tt), np.float32), -1), BF16)
    return pl.pallas_call(
        _route_kernel,
        grid=(T // tt,),
        in_specs=[pl.BlockSpec((tt, LANES), lambda i: (i, 0)),
                  pl.BlockSpec((tt, tt), lambda i: (0, 0))],
        out_specs=[pl.BlockSpec((tt, LANES), lambda i: (i, 0)),
                   pl.BlockSpec((1, LANES), lambda i: (0, 0))],
        out_shape=[jax.ShapeDtypeStruct((T, LANES), F32),
                   jax.ShapeDtypeStruct((1, LANES), F32)],
        scratch_shapes=[pltpu.VMEM((1, LANES), F32)],
        compiler_params=_params("arbitrary"),
        name="moe_route",
    )(logits, tri)


def _plan_kernel(meta_ref, cnt_ref, d_ref):
    cnt = jnp.broadcast_to(cnt_ref[...], (SUBLANES, LANES)).astype(I32)
    pad = (((cnt + (MOE_ROWS - 1)) >> MOE_ROWS_LOG2) << MOE_ROWS_LOG2).astype(F32)
    lane8 = lax.broadcasted_iota(I32, pad.shape, 1)
    incl = pad
    sh = 1
    while sh < LANES:
        incl = incl + jnp.where(lane8 >= sh, pltpu.roll(incl, sh, axis=1), 0.0)
        sh *= 2
    off = (incl - pad)[0:1, :]
    m = meta_ref[...]
    lane = lax.broadcasted_iota(I32, m.shape, 1)
    i1 = m[:, 0:1].astype(I32)
    i2 = m[:, 1:2].astype(I32)
    d1 = jnp.sum(jnp.where(lane == i1, off, 0.0), axis=-1, keepdims=True) + m[:, 4:5]
    d2 = jnp.sum(jnp.where(lane == i2, off, 0.0), axis=-1, keepdims=True) + m[:, 5:6]
    dm = jnp.where(lane == 0, d1, jnp.where(lane == 1, d2, 0.0))
    d_ref[...] = jnp.transpose(dm)[0:2, :].astype(I32)


def plan(meta, cnt):
    T = meta.shape[0]
    tt = ROUTE_ROWS
    return pl.pallas_call(
        _plan_kernel,
        grid=(T // tt,),
        in_specs=[pl.BlockSpec((tt, LANES), lambda i: (i, 0)),
                  pl.BlockSpec((1, LANES), lambda i: (0, 0))],
        out_specs=pl.BlockSpec((2, tt), lambda i: (0, i)),
        out_shape=jax.ShapeDtypeStruct((2, T), I32),
        compiler_params=_params("parallel"),
        name="moe_plan",
    )(meta, cnt)


def _invert_kernel(d1_ref, d2_ref, src_ref, *, T, R):
    def init(r, _):
        src_ref[r] = 0
        return 0
    lax.fori_loop(0, R, init, 0, unroll=32)

    def body(t, _):
        src_ref[d1_ref[t]] = t
        src_ref[d2_ref[t]] = t
        return 0
    lax.fori_loop(0, T, body, 0, unroll=8)


def invert(d1, d2, R):
    T = d1.shape[0]
    grid_spec = pltpu.PrefetchScalarGridSpec(
        num_scalar_prefetch=2, grid=(1,), in_specs=[],
        out_specs=pl.BlockSpec(memory_space=pltpu.SMEM))
    return pl.pallas_call(
        functools.partial(_invert_kernel, T=T, R=R),
        grid_spec=grid_spec,
        out_shape=jax.ShapeDtypeStruct((R,), I32),
        compiler_params=_params("arbitrary"),
        name="moe_invert",
    )(d1, d2)


GATHER_CHUNK = 8


def _tile_copy(src_hbm, row, dst, slot, r, sem):
    return pltpu.make_async_copy(src_hbm.at[pl.ds(row * TOKEN_ROWS, TOKEN_ROWS)],
                                 dst.at[slot, pl.ds(r * TOKEN_ROWS, TOKEN_ROWS)], sem.at[slot])


def _expert_kernel(te_ref, first_ref, nxt_ref, wsl_ref, nch_ref, src_ref,
                   hp_hbm, w1_hbm, w3_hbm, w2_hbm, o_ref,
                   xbuf, r1, r3, r2, w1b, w3b, w2b, gsem, wsem, *, tr, base):
    i = pl.program_id(0)
    n = pl.num_programs(0)
    slot = i % 2

    def weight_copies(e, ws):
        return [pltpu.make_async_copy(w1_hbm.at[base + e], r1.at[ws], wsem.at[ws, 0]),
                pltpu.make_async_copy(w3_hbm.at[base + e], r3.at[ws], wsem.at[ws, 1]),
                pltpu.make_async_copy(w2_hbm.at[base + e], r2.at[ws], wsem.at[ws, 2])]

    def issue_rows(tile, slot):
        def body(c, _):
            for u in range(GATHER_CHUNK):
                r = c * GATHER_CHUNK + u
                _tile_copy(hp_hbm, src_ref[tile * tr + r], xbuf, slot, r, gsem).start()
            return 0
        lax.fori_loop(0, nch_ref[tile], body, 0)

    def wait_rows(tile, slot):
        rows = GATHER_CHUNK * TOKEN_ROWS
        def body(c, _):
            pltpu.make_async_copy(hp_hbm.at[pl.ds(0, rows)], xbuf.at[slot, pl.ds(0, rows)],
                                  gsem.at[slot]).wait()
            return 0
        lax.fori_loop(0, nch_ref[tile], body, 0)

    @pl.when(i == 0)
    def _():
        xbuf[...] = jnp.zeros_like(xbuf)
        for cp in weight_copies(te_ref[0], 0):
            cp.start(priority=1)
        issue_rows(0, 0)

    wait_rows(i, slot)

    @pl.when(i + 1 < n)
    def _():
        issue_rows(i + 1, 1 - slot)

    @pl.when(first_ref[i] == 1)
    def _():
        ws = wsl_ref[i]
        for cp in weight_copies(te_ref[i], ws):
            cp.wait()

        @pl.when(nxt_ref[i] >= 0)
        def _():
            for cp in weight_copies(nxt_ref[i], 1 - ws):
                cp.start(priority=1)

        w1b[...] = r1[ws].astype(BF16)
        w3b[...] = r3[ws].astype(BF16)
        w2b[...] = r2[ws].astype(BF16)

    @pl.when(nch_ref[i] > 0)
    def _():
        x = jnp.concatenate([p.astype(BF16) for p in _load_token_tiles(xbuf.at[slot], 0, tr)],
                            axis=1)
        h1 = jnp.dot(x, w1b[...], preferred_element_type=F32)
        h3 = jnp.dot(x, w3b[...], preferred_element_type=F32)
        hm = (h1 * _sigmoid(h1) * h3).astype(BF16)
        _store_token_tiles(o_ref, jnp.dot(hm, w2b[...], preferred_element_type=F32), tr)

    @pl.when(nch_ref[i] == 0)
    def _():
        o_ref[...] = jnp.zeros_like(o_ref)


def moe_experts(hp, w1, w3, w2, layer, tables, src):
    D, F = w1.shape[-2:]
    tr = MOE_ROWS
    te, first, nxt, wsl, nch = tables
    NT = te.shape[0]
    w1f = w1.reshape(-1, D, F)
    w3f = w3.reshape(-1, D, F)
    w2f = w2.reshape(-1, F, D)
    hbm = pl.BlockSpec(memory_space=pl.ANY)
    grid_spec = pltpu.PrefetchScalarGridSpec(
        num_scalar_prefetch=6,
        grid=(NT,),
        in_specs=[hbm, hbm, hbm, hbm],
        out_specs=pl.BlockSpec((tr * TOKEN_ROWS, LANES), lambda i, *_: (i, 0)),
        scratch_shapes=[pltpu.VMEM((2, tr * TOKEN_ROWS, LANES), F32),
                        pltpu.VMEM((2, D, F), F32),
                        pltpu.VMEM((2, D, F), F32),
                        pltpu.VMEM((2, F, D), F32),
                        pltpu.VMEM((D, F), BF16),
                        pltpu.VMEM((D, F), BF16),
                        pltpu.VMEM((F, D), BF16),
                        pltpu.SemaphoreType.DMA((2,)),
                        pltpu.SemaphoreType.DMA((2, 3))],
    )
    return pl.pallas_call(
        functools.partial(_expert_kernel, tr=tr, base=layer * N_EXPERTS),
        grid_spec=grid_spec,
        out_shape=jax.ShapeDtypeStruct((NT * tr * TOKEN_ROWS, LANES), F32),
        compiler_params=_params("arbitrary"),
        name="moe_experts",
    )(te, first, nxt, wsl, nch, src, hp, w1f, w3f, w2f)


def _combine_kernel(d1_ref, d2_ref, x_ref, meta_ref, ys_hbm, *rest, tt, with_norm):
    if with_norm:
        g_ref, o_ref, hn_ref, buf, sem = rest
    else:
        o_ref, buf, sem = rest
    i = pl.program_id(0)
    n = pl.num_programs(0)
    slot = i % 2

    def issue(tile, slot):
        def body(c, _):
            for u in range(GATHER_CHUNK):
                r = c * GATHER_CHUNK + u
                t = tile * tt + r
                _tile_copy(ys_hbm, d1_ref[t], buf, slot, r, sem).start(priority=0)
                _tile_copy(ys_hbm, d2_ref[t], buf, slot, tt + r, sem).start(priority=1)
            return 0
        lax.fori_loop(0, tt // GATHER_CHUNK, body, 0)

    @pl.when(i == 0)
    def _():
        issue(0, 0)

    pltpu.make_async_copy(ys_hbm.at[pl.ds(0, 2 * tt * TOKEN_ROWS)], buf.at[slot], sem.at[slot]).wait()

    @pl.when(i + 1 < n)
    def _():
        issue(i + 1, 1 - slot)

    w1 = meta_ref[:, 2:3]
    w2 = meta_ref[:, 3:4]
    pa = _load_token_tiles(buf.at[slot], 0, tt)
    pb = _load_token_tiles(buf.at[slot], tt * TOKEN_ROWS, tt)
    cols = [x_ref[:, j * LANES:(j + 1) * LANES] + w1 * pa[j] + w2 * pb[j]
            for j in range(TOKEN_ROWS)]
    xn = jnp.concatenate(cols, axis=1)
    o_ref[...] = xn
    if with_norm:
        hn_ref[...] = (_rms(xn) * g_ref[...]).astype(hn_ref.dtype)


def moe_combine(x, meta, ys, d1, d2, g_next=None):
    T, D = x.shape
    tt = COMB_ROWS
    with_norm = g_next is not None
    row = pl.BlockSpec((tt, D), lambda i, d1, d2: (i, 0))
    in_specs = [row, pl.BlockSpec((tt, LANES), lambda i, d1, d2: (i, 0)),
                pl.BlockSpec(memory_space=pl.ANY)]
    args = [d1, d2, x, meta, ys]
    out_specs, out_shape = row, jax.ShapeDtypeStruct((T, D), F32)
    if with_norm:
        in_specs.append(pl.BlockSpec((1, D), lambda i, d1, d2: (0, 0)))
        args.append(g_next.reshape(1, D))
        out_specs, out_shape = [row, row], [out_shape, jax.ShapeDtypeStruct((T, D), BF16)]
    grid_spec = pltpu.PrefetchScalarGridSpec(
        num_scalar_prefetch=2,
        grid=(T // tt,),
        in_specs=in_specs,
        out_specs=out_specs,
        scratch_shapes=[pltpu.VMEM((2, 2 * tt * TOKEN_ROWS, LANES), F32),
                        pltpu.SemaphoreType.DMA((2,))],
    )
    return pl.pallas_call(
        functools.partial(_combine_kernel, tt=tt, with_norm=with_norm),
        grid_spec=grid_spec,
        out_shape=out_shape,
        compiler_params=_params("arbitrary"),
        name="moe_combine",
    )(*args)


def _tile_tables(cnt, T):
    E = N_EXPERTS
    counts = cnt[0, ROUTER_LANE0:ROUTER_LANE0 + E].astype(I32)
    tiles_e = (counts + MOE_ROWS - 1) // MOE_ROWS
    tile_end = jnp.cumsum(tiles_e)
    tile_start = tile_end - tiles_e
    nact = tile_end[-1]
    NT = (2 * T) // MOE_ROWS + E
    tid = jnp.arange(NT, dtype=I32)
    te = jnp.sum((jnp.minimum(tid, nact - 1)[:, None] >= tile_end[None, :]).astype(I32), axis=1)
    te = jnp.minimum(te, E - 1)
    active = tid < nact
    first = (active & (tid == tile_start[te])).astype(I32)
    eid = jnp.arange(E, dtype=I32)
    later = (eid[None, :] > eid[:, None]) & (tiles_e[None, :] > 0)
    nxt_e = jnp.min(jnp.where(later, eid[None, :], E), axis=1)
    nxt_e = jnp.where(nxt_e == E, -1, nxt_e)
    ordinal = jnp.cumsum((tiles_e > 0).astype(I32)) - 1
    valid = jnp.clip(counts[te] - (tid - tile_start[te]) * MOE_ROWS, 0, MOE_ROWS)
    nch = jnp.where(active, (valid + GATHER_CHUNK - 1) // GATHER_CHUNK, 0)
    return (te, first, nxt_e[te], ordinal[te] % 2, nch), NT * MOE_ROWS


def _lower_bound_logs(lb_param, layer):
    p = jax.nn.softmax(lb_param.astype(F32), axis=0)
    c = jnp.cumsum(p, axis=0)
    lb = c[layer] - c[0]
    return jnp.log(lb), jnp.log1p(-lb)


def kernel(x, mem, norm_mix, w_in, hg_lower_bounds, hg_norm, conv_w, fox_f_bias, fox_q_norm,
           fox_k_norm, w_out, norm_cross, norm_mem, w_cq, w_ck, w_cv, w_co, cross_q_norm,
           cross_k_norm, norm_ffn, router_group_w, router_group_b, router_expert_w,
           router_expert_b, moe_w1, moe_w3, moe_w2):
    B, S, D = x.shape
    M = mem.shape[1]
    L = w_in.shape[0]
    T = B * S
    HGW = 4 * HG_WIDTH
    CONV3 = 3 * CONV_WIDTH
    FQ0 = CONV3
    FK0 = FQ0 + FOX_WIDTH
    FV0 = FK0 + FOX_WIDTH
    PB = CONV3 + 3 * FOX_WIDTH
    dhc = D // CROSS_HEADS
    x2 = x.reshape(T, D)
    mem2 = mem.reshape(B * M, D)
    w_in_t = jnp.swapaxes(w_in, 1, 2)
    hn = rmsnorm(x2, norm_mix[0])
    for l in range(L):
        proj_a = matmul([hn], w_in_t, l, 0, HGW, F32, w_t=True)
        proj_b = matmul([hn], w_in_t, l, HGW, PB, BF16, w_t=True, head_norms=(
            (FQ0, FK0, fox_q_norm[l], FOX_DIM ** -0.5), (FK0, FV0, fox_k_norm[l], 1.0)))
        loglb, l1mlb = _lower_bound_logs(hg_lower_bounds, l)
        y_hg = hgrn2(proj_a.reshape(B, S, HGW), loglb, l1mlb, hg_norm[l])
        pb3 = proj_b.reshape(B, S, PB)
        y_conv = short_conv(pb3, conv_w[l])
        c = fox_gate(hn.reshape(B, S, D), w_in_t, l, HGW + PB, fox_f_bias[l])
        y_fox = fox_attention(pb3, FQ0, FK0, FV0, c)
        mix = [y_hg.reshape(T, HG_WIDTH), y_conv.reshape(T, CONV_WIDTH), y_fox.reshape(T, FOX_WIDTH)]
        x2, qcn = out_proj_cross_q(mix, x2, w_out, w_cq, l, norm_cross[l], cross_q_norm[l])
        memn = rmsnorm(mem2, norm_mem[l])
        kcn = matmul([memn], w_ck, l, 0, D, BF16, head_norms=((0, D, cross_k_norm[l], 1.0),))
        vc = matmul([memn], w_cv, l, 0, D, BF16)
        wr = jnp.concatenate([router_group_w[l], router_expert_w[l]], axis=1)
        wr = jnp.pad(wr, ((0, 0), (0, LANES - wr.shape[1])))
        br = jnp.concatenate([router_group_b[l], router_expert_b[l]])
        br = jnp.pad(br, (0, LANES - br.shape[0]))
        x3, hp, logits = cross_attention_ffn_in(
            qcn.reshape(B, S, D), kcn.reshape(B, M, D), vc.reshape(B, M, D), x2.reshape(B, S, D),
            w_co, l, norm_ffn[l], wr, br)
        x2 = x3.reshape(T, D)
        meta, cnt = route(logits)
        d = plan(meta, cnt)
        tables, R = _tile_tables(cnt, T)
        src = invert(d[0], d[1], R)
        ys = moe_experts(hp, moe_w1, moe_w3, moe_w2, l, tables, src)
        if l + 1 < L:
            x2, hn = moe_combine(x2, meta, ys, d[0], d[1], norm_mix[l + 1])
        else:
            x2 = moe_combine(x2, meta, ys, d[0], d[1])
    return x2.reshape(B, S, D)
```

```python
import functools

import numpy as np
import jax
import jax.numpy as jnp
from jax import lax
from jax.experimental import pallas as pl
from jax.experimental.pallas import tpu as pltpu

F32 = jnp.float32
BF16 = jnp.bfloat16
I32 = jnp.int32
EPS = 1e-6

HG_HEADS = 4
HG_DIM = 128
HG_WIDTH = HG_HEADS * HG_DIM
CONV_WIDTH = 512
FOX_HEADS = 8
FOX_DIM = 128
FOX_WIDTH = FOX_HEADS * FOX_DIM
CROSS_HEADS = 4
MOE_GROUPS = 4
MOE_EXPERTS = 8
N_EXPERTS = MOE_GROUPS * MOE_EXPERTS
ROUTER_LANE0 = MOE_GROUPS

LANES = 128
SUBLANES = 8
VMEM_LIMIT = 56 * 1024 * 1024

NORM_ROWS = 256
MM_TM = 1024
MM_TN = 512
HG_CHUNK = 128
FOX_TQ = 512
FOX_TK = 512
FOX_HEADS_PER_STEP = 2
CROSS_TS = 512
ROUTE_ROWS = 512
MOE_ROWS_LOG2 = 8
MOE_ROWS = 1 << MOE_ROWS_LOG2
COMB_ROWS = 256


def _params(*sem):
    return pltpu.CompilerParams(dimension_semantics=sem, vmem_limit_bytes=VMEM_LIMIT)


def _sigmoid(x):
    return 1.0 / (1.0 + jnp.exp(-x))


def _log_sigmoid(x):
    return jnp.minimum(x, 0.0) - jnp.log1p(jnp.exp(-jnp.abs(x)))


TOKEN_ROWS = 16


def _store_token_tiles(ref, y, rows):
    for j in range(TOKEN_ROWS):
        ref[pl.ds(j, rows, stride=TOKEN_ROWS), :] = y[:, j * LANES:(j + 1) * LANES]


def _load_token_tiles(ref, base, rows):
    return [ref[pl.ds(base + j, rows, stride=TOKEN_ROWS), :] for j in range(TOKEN_ROWS)]


def _rms(x):
    return x * lax.rsqrt(jnp.mean(x * x, axis=-1, keepdims=True) + EPS)


def _rmsnorm_kernel(x_ref, g_ref, o_ref):
    o_ref[...] = (_rms(x_ref[...]) * g_ref[...]).astype(o_ref.dtype)


def rmsnorm(x, g, out_dtype=BF16):
    R, D = x.shape
    return pl.pallas_call(
        _rmsnorm_kernel,
        grid=(R // NORM_ROWS,),
        in_specs=[pl.BlockSpec((NORM_ROWS, D), lambda i: (i, 0)),
                  pl.BlockSpec((1, D), lambda i: (0, 0))],
        out_specs=pl.BlockSpec((NORM_ROWS, D), lambda i: (i, 0)),
        out_shape=jax.ShapeDtypeStruct((R, D), out_dtype),
        compiler_params=_params("parallel"),
        name="rmsnorm",
    )(x, g.reshape(1, D))


def _head_rmsnorm(x, g, dh, scale):
    heads = [_rms(x[:, h * dh:(h + 1) * dh]) * g * scale for h in range(x.shape[1] // dh)]
    return heads[0] if len(heads) == 1 else jnp.concatenate(heads, axis=1)


def _matmul_kernel(*refs, n_parts, w_t, norm_tiles, dh):
    a_refs = refs[:n_parts]
    w_ref = refs[n_parts]
    if norm_tiles:
        g_ref, o_ref, wb_ref = refs[n_parts + 1:]
    else:
        o_ref, wb_ref = refs[n_parts + 1:]
    j = pl.program_id(0)

    @pl.when(pl.program_id(1) == 0)
    def _():
        wb_ref[...] = w_ref[...].astype(BF16)

    if n_parts == 1:
        a = a_refs[0][...]
    else:
        a = jnp.concatenate([r[...] for r in a_refs], axis=1)
    if w_t:
        acc = lax.dot_general(a, wb_ref[...], (((1,), (1,)), ((), ())), preferred_element_type=F32)
    else:
        acc = jnp.dot(a, wb_ref[...], preferred_element_type=F32)
    if not norm_tiles:
        o_ref[...] = acc.astype(o_ref.dtype)
        return
    plain = True
    for k, (j0, j1, scale) in enumerate(norm_tiles):
        hit = (j >= j0) & (j < j1)
        plain = plain & jnp.logical_not(hit)

        @pl.when(hit)
        def _(k=k, scale=scale):
            o_ref[...] = _head_rmsnorm(acc, g_ref[k:k + 1, :], dh, scale).astype(o_ref.dtype)

    @pl.when(plain)
    def _():
        o_ref[...] = acc.astype(o_ref.dtype)


def matmul(a_parts, w, layer, col0, n, out_dtype, w_t=False, head_norms=(), tm=MM_TM, tn=MM_TN):
    M = a_parts[0].shape[0]
    K = w.shape[2] if w_t else w.shape[1]
    assert sum(p.shape[1] for p in a_parts) == K
    tm = min(tm, M)
    assert col0 % tn == 0 and n % tn == 0 and M % tm == 0
    cb = col0 // tn
    in_specs = [pl.BlockSpec((tm, p.shape[1]), lambda j, i: (i, 0)) for p in a_parts]
    if w_t:
        in_specs.append(pl.BlockSpec((None, tn, K), lambda j, i: (layer, cb + j, 0)))
    else:
        in_specs.append(pl.BlockSpec((None, K, tn), lambda j, i: (layer, 0, cb + j)))
    args = list(a_parts) + [w]
    norm_tiles, dh = (), 0
    if head_norms:
        dh = head_norms[0][2].shape[0]
        assert all(c0 % tn == 0 and c1 % tn == 0 and g.shape[0] == dh and tn % dh == 0
                   for c0, c1, g, _ in head_norms)
        norm_tiles = tuple((c0 // tn, c1 // tn, s) for c0, c1, _, s in head_norms)
        gains = jnp.stack([g for _, _, g, _ in head_norms])
        in_specs.append(pl.BlockSpec(gains.shape, lambda j, i: (0, 0)))
        args.append(gains)
    return pl.pallas_call(
        functools.partial(_matmul_kernel, n_parts=len(a_parts), w_t=w_t, norm_tiles=norm_tiles,
                          dh=dh),
        grid=(n // tn, M // tm),
        in_specs=in_specs,
        out_specs=pl.BlockSpec((tm, tn), lambda j, i: (i, j)),
        out_shape=jax.ShapeDtypeStruct((M, n), out_dtype),
        scratch_shapes=[pltpu.VMEM((tn, K) if w_t else (K, tn), BF16)],
        compiler_params=_params("arbitrary", "arbitrary"),
        name="matmul",
    )(*args)


W_CHUNK = 256


def _load_weight(w_hbm, layer, wb, stg, sem):
    n_chunks = wb.shape[0] // W_CHUNK

    def copy(c):
        return pltpu.make_async_copy(w_hbm.at[layer, pl.ds(c * W_CHUNK, W_CHUNK)], stg.at[c % 2],
                                     sem.at[c % 2])
    copy(0).start()
    for c in range(n_chunks):
        if c + 1 < n_chunks:
            copy(c + 1).start()
        copy(c).wait()
        wb[c * W_CHUNK:(c + 1) * W_CHUNK, :] = stg[c % 2].astype(BF16)


def _outq_kernel(a1_ref, a2_ref, a3_ref, x_ref, wo_hbm, wq_hbm, gn_ref, gq_ref, xo_ref, q_ref,
                 wo_b, wq_b, stg, sem, *, layer, dh, scale):
    @pl.when(pl.program_id(0) == 0)
    def _():
        _load_weight(wo_hbm, layer, wo_b, stg, sem)
        _load_weight(wq_hbm, layer, wq_b, stg, sem)

    a = jnp.concatenate([a1_ref[...], a2_ref[...], a3_ref[...]], axis=1)
    xn = x_ref[...] + jnp.dot(a, wo_b[...], preferred_element_type=F32)
    xo_ref[...] = xn
    hc = (_rms(xn) * gn_ref[...]).astype(BF16)
    q = jnp.dot(hc, wq_b[...], preferred_element_type=F32)
    q_ref[...] = _head_rmsnorm(q, gq_ref[...], dh, scale).astype(q_ref.dtype)


def out_proj_cross_q(mix, x, w_out, w_cq, layer, g_cross, g_q, tm=512):
    T, D = x.shape
    dh = g_q.shape[0]
    row = lambda w: pl.BlockSpec((tm, w), lambda i: (i, 0))
    hbm = pl.BlockSpec(memory_space=pl.ANY)
    return pl.pallas_call(
        functools.partial(_outq_kernel, layer=layer, dh=dh, scale=dh ** -0.5),
        grid=(T // tm,),
        in_specs=[row(mix[0].shape[1]), row(mix[1].shape[1]), row(mix[2].shape[1]), row(D), hbm, hbm,
                  pl.BlockSpec((1, D), lambda i: (0, 0)), pl.BlockSpec((1, dh), lambda i: (0, 0))],
        out_specs=[row(D), row(D)],
        out_shape=[jax.ShapeDtypeStruct((T, D), F32), jax.ShapeDtypeStruct((T, D), BF16)],
        scratch_shapes=[pltpu.VMEM((D, D), BF16), pltpu.VMEM((D, D), BF16),
                        pltpu.VMEM((2, W_CHUNK, D), F32), pltpu.SemaphoreType.DMA((2,))],
        compiler_params=_params("arbitrary"),
        name="out_proj_cross_q",
    )(*mix, x, w_out, w_cq, g_cross.reshape(1, D), g_q.reshape(1, dh))


def _hgrn_consts(C):
    nlev = int(np.log2(C))
    t = np.arange(C)[:, None]
    u = np.arange(C)[None, :]
    blocks = [(u <= t)]
    for j in range(nlev):
        m = 1 << j
        boundary = (t // (2 * m)) * (2 * m) + m - 1
        second = (t & m) != 0
        blocks.append(np.where(second, (u > boundary) & (u <= t), (u > t) & (u <= boundary)))
    mall = np.concatenate(blocks, axis=0).astype(np.float32)
    mall2 = np.concatenate([mall, mall], axis=1)
    tt = np.arange(C)[:, None]
    ss = np.arange(C)[None, :]
    lev = np.full((C, C), nlev + 1, np.int32)
    x = tt ^ ss
    hb = np.zeros_like(x)
    for j in range(nlev):
        hb = np.where((x >> j) & 1, j, hb)
    lev = np.where(tt > ss, hb, lev)
    lev = np.where(tt == ss, nlev, lev)
    return jnp.asarray(mall2, BF16), jnp.asarray(lev, jnp.int32)


def _hgrn_kernel(x_ref, mall_ref, lev_ref, loglb_ref, l1mlb_ref, ng_ref, o_ref, st_ref, *, C, nlev):
    @pl.when(pl.program_id(1) == 0)
    def _():
        st_ref[...] = jnp.zeros_like(st_ref)

    W = HG_WIDTH
    d = HG_DIM
    lev = lev_ref[...]
    mall = mall_ref[...]
    row = lax.broadcasted_iota(jnp.int32, (C, d), 0)
    nt = (((1,), (1,)), ((), ()))
    for h in range(HG_HEADS):
        sl = slice(h * d, (h + 1) * d)
        q = x_ref[0, :, h * d:(h + 1) * d]
        z = x_ref[0, :, W + h * d:W + (h + 1) * d]
        v = x_ref[0, :, 2 * W + h * d:2 * W + (h + 1) * d]
        g = x_ref[0, :, 3 * W + h * d:3 * W + (h + 1) * d]
        a = loglb_ref[:, sl]
        bt = l1mlb_ref[:, sl] + _log_sigmoid(z)
        lf = jnp.maximum(a, bt) + jnp.log1p(jnp.exp(-jnp.abs(a - bt)))
        kk = 1.0 - jnp.exp(lf)
        hi = lf.astype(BF16)
        lo = (lf - hi.astype(F32)).astype(BF16)
        sums = jnp.dot(mall, jnp.concatenate([hi, lo], axis=0), preferred_element_type=F32)
        b = sums[0:C]
        vb = v.astype(BF16)
        st = st_ref[h]
        o = lax.dot_general((q * jnp.exp(b)).astype(BF16), st.astype(BF16), nt,
                            preferred_element_type=F32)
        att = jnp.where(lev == nlev, jnp.sum(q * kk, axis=-1, keepdims=True), 0.0)
        for j in range(nlev):
            second = (row & (1 << j)) != 0
            e = jnp.exp(sums[(j + 1) * C:(j + 2) * C])
            xq = jnp.where(second, q * e, 0.0).astype(BF16)
            xk = jnp.where(second, 0.0, kk * e).astype(BF16)
            am = lax.dot_general(xq, xk, nt, preferred_element_type=F32)
            att = jnp.where(lev == j, am, att)
        o = o + jnp.dot(att.astype(BF16), vb, preferred_element_type=F32)
        bl = b[C - 1:C, :]
        kh = (kk * jnp.exp(bl - b)).astype(BF16)
        st_ref[h] = st * jnp.exp(bl) + lax.dot_general(
            vb, kh, (((0,), (0,)), ((), ())), preferred_element_type=F32)
        y = o * lax.rsqrt(jnp.mean(o * o, axis=-1, keepdims=True) + EPS) * ng_ref[...]
        o_ref[0, :, sl] = (y * (g * _sigmoid(g))).astype(o_ref.dtype)


def hgrn2(proj_a, loglb, l1mlb, norm_g):
    B, S, _ = proj_a.shape
    C = HG_CHUNK
    nlev = int(np.log2(C))
    mall, lev = _hgrn_consts(C)
    return pl.pallas_call(
        functools.partial(_hgrn_kernel, C=C, nlev=nlev),
        grid=(B, S // C),
        in_specs=[pl.BlockSpec((1, C, 4 * HG_WIDTH), lambda b, c: (b, c, 0)),
                  pl.BlockSpec(mall.shape, lambda b, c: (0, 0)),
                  pl.BlockSpec((C, C), lambda b, c: (0, 0)),
                  pl.BlockSpec((1, HG_WIDTH), lambda b, c: (0, 0)),
                  pl.BlockSpec((1, HG_WIDTH), lambda b, c: (0, 0)),
                  pl.BlockSpec((1, HG_DIM), lambda b, c: (0, 0))],
        out_specs=pl.BlockSpec((1, C, HG_WIDTH), lambda b, c: (b, c, 0)),
        out_shape=jax.ShapeDtypeStruct((B, S, HG_WIDTH), BF16),
        scratch_shapes=[pltpu.VMEM((HG_HEADS, HG_DIM, HG_DIM), F32)],
        compiler_params=_params("parallel", "arbitrary"),
        name="hgrn2",
    )(proj_a, mall, lev, loglb.reshape(1, HG_WIDTH), l1mlb.reshape(1, HG_WIDTH),
      norm_g.reshape(1, HG_DIM))


def _conv_kernel(b_ref, c_ref, h_ref, w_ref, o_ref):
    u = c_ref[0].astype(F32) * h_ref[0].astype(F32)
    row = lax.broadcasted_iota(jnp.int32, u.shape, 0)
    u1 = jnp.where(row >= 1, pltpu.roll(u, 1, axis=0), 0.0)
    u2 = jnp.where(row >= 2, pltpu.roll(u, 2, axis=0), 0.0)
    y = w_ref[0:1, :] * u2 + w_ref[1:2, :] * u1 + w_ref[2:3, :] * u
    o_ref[0] = (b_ref[0].astype(F32) * y).astype(o_ref.dtype)


def short_conv(proj_b, w):
    B, S, _ = proj_b.shape
    cw = CONV_WIDTH
    spec = lambda k: pl.BlockSpec((1, S, cw), lambda b: (b, 0, k))
    return pl.pallas_call(
        _conv_kernel,
        grid=(B,),
        in_specs=[spec(0), spec(1), spec(2), pl.BlockSpec((3, cw), lambda b: (0, 0))],
        out_specs=pl.BlockSpec((1, S, cw), lambda b: (b, 0, 0)),
        out_shape=jax.ShapeDtypeStruct((B, S, cw), BF16),
        compiler_params=_params("parallel"),
        name="short_conv",
    )(proj_b, proj_b, proj_b, w)


def _fox_gate_kernel(hn_ref, wf_ref, bias_ref, c_ref):
    gl = lax.dot_general(wf_ref[...].astype(BF16), hn_ref[0], (((1,), (1,)), ((), ())),
                         preferred_element_type=F32)
    c = _log_sigmoid(gl + bias_ref[...])
    S = c.shape[1]
    lane = lax.broadcasted_iota(jnp.int32, c.shape, 1)
    sh = 1
    while sh < S:
        c = c + jnp.where(lane >= sh, pltpu.roll(c, sh, axis=1), 0.0)
        sh *= 2
    c_ref[0] = c


def fox_gate(hn, w_t, layer, row0, bias):
    B, S, D = hn.shape
    H = bias.shape[0]
    assert row0 % H == 0
    return pl.pallas_call(
        _fox_gate_kernel,
        grid=(B,),
        in_specs=[pl.BlockSpec((1, S, D), lambda b: (b, 0, 0)),
                  pl.BlockSpec((None, H, D), lambda b: (layer, row0 // H, 0)),
                  pl.BlockSpec((H, 1), lambda b: (0, 0))],
        out_specs=pl.BlockSpec((1, H, S), lambda b: (b, 0, 0)),
        out_shape=jax.ShapeDtypeStruct((B, H, S), F32),
        compiler_params=_params("parallel"),
        name="fox_gate",
    )(hn, w_t, bias.reshape(H, 1))


def _fox_kernel(q_ref, k_ref, v_ref, c_ref, o_ref, *, tq, tk, nh):
    qi = pl.program_id(2)
    nt = (((1,), (1,)), ((), ()))
    ones = jnp.ones((tk, FOX_DIM), BF16)
    heads = [slice(h * FOX_DIM, (h + 1) * FOX_DIM) for h in range(nh)]
    qs = [q_ref[0, :, sl] for sl in heads]

    def step(ki, carry, diag):
        out = []
        for h, sl in enumerate(heads):
            m, acc = carry[h]
            k = k_ref[0, pl.ds(ki * tk, tk), sl]
            v1 = jnp.concatenate([v_ref[0, pl.ds(ki * tk, tk), sl], ones], axis=1)
            ck = c_ref[0, h, pl.ds(ki, 1), :]
            s = lax.dot_general(qs[h], k, nt, preferred_element_type=F32) - ck
            if diag is not None:
                r = lax.broadcasted_iota(jnp.int32, s.shape, 0)
                c = lax.broadcasted_iota(jnp.int32, s.shape, 1)
                s = jnp.where(c + diag * tk <= r, s, -jnp.inf)
            m_new = jnp.maximum(m, jnp.max(s, axis=-1, keepdims=True))
            alpha = jnp.exp(m - m_new)
            p = jnp.exp((s - m_new).astype(BF16))
            out.append((m_new, alpha * acc + jnp.dot(p, v1, preferred_element_type=F32)))
        return tuple(out)

    init = tuple((jnp.full((tq, 1), -jnp.inf, F32), jnp.zeros((tq, 2 * FOX_DIM), F32))
                 for _ in heads)
    n_full = qi * (tq // tk)
    carry = lax.fori_loop(0, n_full, lambda ki, cr: step(ki, cr, None), init)
    for d in range(tq // tk):
        carry = step(n_full + d, carry, d)
    for h, sl in enumerate(heads):
        acc = carry[h][1]
        o_ref[0, :, sl] = (acc[:, :FOX_DIM] / acc[:, FOX_DIM:]).astype(o_ref.dtype)


def fox_attention(proj_b, q_col0, k_col0, v_col0, c):
    B, S, _ = proj_b.shape
    tq, tk, nh = FOX_TQ, FOX_TK, FOX_HEADS_PER_STEP
    H = FOX_HEADS
    w = nh * FOX_DIM
    assert tq % tk == 0 and H % nh == 0 and all(c0 % w == 0 for c0 in (q_col0, k_col0, v_col0))
    qb, kb, vb = q_col0 // w, k_col0 // w, v_col0 // w
    c4 = c.reshape(B, H, S // tk, tk)
    return pl.pallas_call(
        functools.partial(_fox_kernel, tq=tq, tk=tk, nh=nh),
        grid=(B, H // nh, S // tq),
        in_specs=[pl.BlockSpec((1, tq, w), lambda b, h, i: (b, i, qb + h)),
                  pl.BlockSpec((1, S, w), lambda b, h, i: (b, 0, kb + h)),
                  pl.BlockSpec((1, S, w), lambda b, h, i: (b, 0, vb + h)),
                  pl.BlockSpec((1, nh, S // tk, tk), lambda b, h, i: (b, h, 0, 0))],
        out_specs=pl.BlockSpec((1, tq, w), lambda b, h, i: (b, i, h)),
        out_shape=jax.ShapeDtypeStruct((B, S, FOX_WIDTH), BF16),
        compiler_params=_params("parallel", "parallel", "arbitrary"),
        name="fox_attention",
    )(proj_b, proj_b, proj_b, c4)


def _cross_kernel(q_ref, k_ref, v_ref, x_ref, wo_hbm, gn_ref, wr_ref, br_ref,
                  xo_ref, hp_ref, lg_ref, wo_b, stg, sem, *, layer, dh):
    @pl.when((pl.program_id(0) == 0) & (pl.program_id(1) == 0))
    def _():
        _load_weight(wo_hbm, layer, wo_b, stg, sem)

    nt = (((1,), (1,)), ((), ()))
    heads = []
    for h in range(CROSS_HEADS):
        sl = slice(h * dh, (h + 1) * dh)
        s = lax.dot_general(q_ref[0, :, sl], k_ref[0, :, sl], nt, preferred_element_type=F32)
        m = jnp.max(s, axis=-1, keepdims=True)
        p = jnp.exp(s - m)
        l = jnp.sum(p, axis=-1, keepdims=True)
        o = jnp.dot(p.astype(BF16), v_ref[0, :, sl], preferred_element_type=F32)
        heads.append((o / l).astype(BF16))
    xn = x_ref[0] + jnp.dot(jnp.concatenate(heads, axis=1), wo_b[...], preferred_element_type=F32)
    xo_ref[0] = xn
    y = _rms(xn) * gn_ref[...]
    _store_token_tiles(hp_ref, y, y.shape[0])
    yh = y.astype(BF16)
    yl = (y - yh.astype(F32)).astype(BF16)
    wr = wr_ref[...]
    wh = wr.astype(BF16)
    wl = (wr - wh.astype(F32)).astype(BF16)
    lg_ref[...] = (jnp.dot(yh, wh, preferred_element_type=F32)
                   + jnp.dot(yl, wh, preferred_element_type=F32)
                   + jnp.dot(yh, wl, preferred_element_type=F32)) + br_ref[...]


def cross_attention_ffn_in(qn, kn, v, x, w_co, layer, g_ffn, wr, br):
    B, S, D = qn.shape
    M = kn.shape[1]
    ts = CROSS_TS
    ns = S // ts
    assert D == TOKEN_ROWS * LANES
    tile = pl.BlockSpec((1, ts, D), lambda b, i: (b, i, 0))
    memb = pl.BlockSpec((1, M, D), lambda b, i: (b, 0, 0))
    const = lambda shape: pl.BlockSpec(shape, lambda b, i: (0, 0))
    return pl.pallas_call(
        functools.partial(_cross_kernel, layer=layer, dh=D // CROSS_HEADS),
        grid=(B, ns),
        in_specs=[tile, memb, memb, tile, pl.BlockSpec(memory_space=pl.ANY),
                  const((1, D)), const((D, LANES)), const((1, LANES))],
        out_specs=[tile,
                   pl.BlockSpec((ts * TOKEN_ROWS, LANES), lambda b, i: (b * ns + i, 0)),
                   pl.BlockSpec((ts, LANES), lambda b, i: (b * ns + i, 0))],
        out_shape=[jax.ShapeDtypeStruct((B, S, D), F32),
                   jax.ShapeDtypeStruct((B * S * TOKEN_ROWS, LANES), F32),
                   jax.ShapeDtypeStruct((B * S, LANES), F32)],
        scratch_shapes=[pltpu.VMEM((D, D), BF16), pltpu.VMEM((2, W_CHUNK, D), F32),
                        pltpu.SemaphoreType.DMA((2,))],
        compiler_params=_params("arbitrary", "arbitrary"),
        name="cross_attention_ffn_in",
    )(qn, kn, v, x, w_co, g_ffn.reshape(1, D), wr, br.reshape(1, LANES))


def _route_kernel(lg_ref, tri_ref, meta_ref, cnt_ref, carry_ref):
    @pl.when(pl.program_id(0) == 0)
    def _():
        carry_ref[...] = jnp.zeros_like(carry_ref)

    x = lg_ref[...]
    lane = lax.broadcasted_iota(jnp.int32, x.shape, 1)
    ninf = -jnp.inf
    is_g = lane < MOE_GROUPS
    gl = jnp.where(is_g, x, ninf)
    gmax = jnp.max(gl, axis=-1, keepdims=True)
    gidx = jnp.min(jnp.where(gl == gmax, lane, LANES), axis=-1, keepdims=True)
    gw = 1.0 / jnp.sum(jnp.where(is_g, jnp.exp(x - gmax), 0.0), axis=-1, keepdims=True)
    e_lane = lane - ROUTER_LANE0
    in_grp = (e_lane >= 0) & (e_lane < N_EXPERTS) & ((e_lane >> 3) == gidx)
    el = jnp.where(in_grp, x, ninf)
    m1 = jnp.max(el, axis=-1, keepdims=True)
    i1 = jnp.min(jnp.where(el == m1, lane, LANES), axis=-1, keepdims=True)
    el2 = jnp.where(lane == i1, ninf, el)
    m2 = jnp.max(el2, axis=-1, keepdims=True)
    i2 = jnp.min(jnp.where(el2 == m2, lane, LANES), axis=-1, keepdims=True)
    r = jnp.exp(m2 - m1)
    p1 = 1.0 / (1.0 + r)
    p2 = r * p1
    oh = ((lane == i1) | (lane == i2)).astype(F32)
    prefix = jnp.dot(tri_ref[...], oh.astype(BF16), preferred_element_type=F32) + carry_ref[...]
    rank1 = jnp.sum(jnp.where(lane == i1, prefix, 0.0), axis=-1, keepdims=True)
    rank2 = jnp.sum(jnp.where(lane == i2, prefix, 0.0), axis=-1, keepdims=True)
    carry_ref[...] += jnp.sum(oh, axis=0, keepdims=True)
    cnt_ref[...] = carry_ref[...]
    cols = [i1.astype(F32), i2.astype(F32), gw * p1, gw * p2, rank1, rank2]
    meta = jnp.zeros(x.shape, F32)
    for k, col in enumerate(cols):
        meta = jnp.where(lane == k, col, meta)
    meta_ref[...] = meta


def route(logits):
    T = logits.shape[0]
    tt = ROUTE_ROWS
    tri = jnp.asarray(np.tril(np.ones((tt, tt), np.float32), -1), BF16)
    return pl.pallas_call(
        _route_kernel,
        grid=(T // tt,),
        in_specs=[pl.BlockSpec((tt, LANES), lambda i: (i, 0)),
                  pl.BlockSpec((tt, tt), lambda i: (0, 0))],
        out_specs=[pl.BlockSpec((tt, LANES), lambda i: (i, 0)),
                   pl.BlockSpec((1, LANES), lambda i: (0, 0))],
        out_shape=[jax.ShapeDtypeStruct((T, LANES), F32),
                   jax.ShapeDtypeStruct((1, LANES), F32)],
        scratch_shapes=[pltpu.VMEM((1, LANES), F32)],
        compiler_params=_params("arbitrary"),
        name="moe_route",
    )(logits, tri)


def _plan_kernel(meta_ref, cnt_ref, d_ref):
    cnt = jnp.broadcast_to(cnt_ref[...], (SUBLANES, LANES)).astype(I32)
    pad = (((cnt + (MOE_ROWS - 1)) >> MOE_ROWS_LOG2) << MOE_ROWS_LOG2).astype(F32)
    lane8 = lax.broadcasted_iota(I32, pad.shape, 1)
    incl = pad
    sh = 1
    while sh < LANES:
        incl = incl + jnp.where(lane8 >= sh, pltpu.roll(incl, sh, axis=1), 0.0)
        sh *= 2
    off = (incl - pad)[0:1, :]
    m = meta_ref[...]
    lane = lax.broadcasted_iota(I32, m.shape, 1)
    i1 = m[:, 0:1].astype(I32)
    i2 = m[:, 1:2].astype(I32)
    d1 = jnp.sum(jnp.where(lane == i1, off, 0.0), axis=-1, keepdims=True) + m[:, 4:5]
    d2 = jnp.sum(jnp.where(lane == i2, off, 0.0), axis=-1, keepdims=True) + m[:, 5:6]
    dm = jnp.where(lane == 0, d1, jnp.where(lane == 1, d2, 0.0))
    d_ref[...] = jnp.transpose(dm)[0:2, :].astype(I32)


def plan(meta, cnt):
    T = meta.shape[0]
    tt = ROUTE_ROWS
    return pl.pallas_call(
        _plan_kernel,
        grid=(T // tt,),
        in_specs=[pl.BlockSpec((tt, LANES), lambda i: (i, 0)),
                  pl.BlockSpec((1, LANES), lambda i: (0, 0))],
        out_specs=pl.BlockSpec((2, tt), lambda i: (0, i)),
        out_shape=jax.ShapeDtypeStruct((2, T), I32),
        compiler_params=_params("parallel"),
        name="moe_plan",
    )(meta, cnt)


def _invert_kernel(d1_ref, d2_ref, src_ref, *, T, R):
    def init(r, _):
        src_ref[r] = 0
        return 0
    lax.fori_loop(0, R, init, 0, unroll=32)

    def body(t, _):
        src_ref[d1_ref[t]] = t
        src_ref[d2_ref[t]] = t
        return 0
    lax.fori_loop(0, T, body, 0, unroll=8)


def invert(d1, d2, R):
    T = d1.shape[0]
    grid_spec = pltpu.PrefetchScalarGridSpec(
        num_scalar_prefetch=2, grid=(1,), in_specs=[],
        out_specs=pl.BlockSpec(memory_space=pltpu.SMEM))
    return pl.pallas_call(
        functools.partial(_invert_kernel, T=T, R=R),
        grid_spec=grid_spec,
        out_shape=jax.ShapeDtypeStruct((R,), I32),
        compiler_params=_params("arbitrary"),
        name="moe_invert",
    )(d1, d2)


GATHER_CHUNK = 8


def _tile_copy(src_hbm, row, dst, slot, r, sem):
    return pltpu.make_async_copy(src_hbm.at[pl.ds(row * TOKEN_ROWS, TOKEN_ROWS)],
                                 dst.at[slot, pl.ds(r * TOKEN_ROWS, TOKEN_ROWS)], sem.at[slot])


def _expert_kernel(te_ref, first_ref, nxt_ref, wsl_ref, act_ref, src_ref,
                   hp_hbm, w1_hbm, w3_hbm, w2_hbm, o_ref,
                   xa, xb, r1, r3, r2, w1b, w3b, w2b, gsem, wsem, *, tr, base):
    i = pl.program_id(0)
    n = pl.num_programs(0)
    bufs = (xa, xb)

    def weight_copies(e, ws):
        return [pltpu.make_async_copy(w1_hbm.at[base + e], r1.at[ws], wsem.at[ws, 0]),
                pltpu.make_async_copy(w3_hbm.at[base + e], r3.at[ws], wsem.at[ws, 1]),
                pltpu.make_async_copy(w2_hbm.at[base + e], r2.at[ws], wsem.at[ws, 2])]

    def row_copy(tile, r, slot):
        tok = src_ref[tile * tr + r]
        return pltpu.make_async_copy(hp_hbm.at[pl.ds(tok * TOKEN_ROWS, TOKEN_ROWS)],
                                     bufs[slot].at[pl.ds(r * TOKEN_ROWS, TOKEN_ROWS)],
                                     gsem.at[slot])

    @pl.when(i == 0)
    def _():
        for cp in weight_copies(te_ref[0], 0):
            cp.start(priority=1)

        def body(r, _):
            row_copy(0, r, 0).start()
            return 0
        lax.fori_loop(0, tr, body, 0, unroll=8)

    @pl.when(first_ref[i] == 1)
    def _():
        ws = wsl_ref[i]
        for cp in weight_copies(te_ref[i], ws):
            cp.wait()

        @pl.when(nxt_ref[i] >= 0)
        def _():
            for cp in weight_copies(nxt_ref[i], 1 - ws):
                cp.start(priority=1)

        w1b[...] = r1[ws].astype(BF16)
        w3b[...] = r3[ws].astype(BF16)
        w2b[...] = r2[ws].astype(BF16)

    def tile_body(slot, fetch_next):
        xbuf = bufs[slot]
        pltpu.make_async_copy(hp_hbm.at[pl.ds(0, tr * TOKEN_ROWS)], xbuf, gsem.at[slot]).wait()
        quarter = tr // 4

        def fetch(part):
            if fetch_next:
                for r in range(part * quarter, (part + 1) * quarter):
                    row_copy(i + 1, r, 1 - slot).start()

        fetch(0)
        x = jnp.concatenate([p.astype(BF16) for p in _load_token_tiles(xbuf, 0, tr)], axis=1)
        fetch(1)
        h1 = jnp.dot(x, w1b[...], preferred_element_type=F32)
        fetch(2)
        h3 = jnp.dot(x, w3b[...], preferred_element_type=F32)
        fetch(3)
        hm = (h1 * _sigmoid(h1) * h3).astype(BF16)
        _store_token_tiles(o_ref, jnp.dot(hm, w2b[...], preferred_element_type=F32), tr)

    active = act_ref[i] == 1
    next_active = act_ref[jnp.minimum(i + 1, n - 1)] == 1
    next_active = next_active & (i + 1 < n)
    for slot in (0, 1):
        for fetch_next in (True, False):
            cond = active & (i % 2 == slot) & (next_active == fetch_next)
            pl.when(cond)(functools.partial(tile_body, slot, fetch_next))

    @pl.when(jnp.logical_not(active))
    def _():
        o_ref[...] = jnp.zeros_like(o_ref)


def moe_experts(hp, w1, w3, w2, layer, tables, src):
    D, F = w1.shape[-2:]
    tr = MOE_ROWS
    te, first, nxt, wsl, act = tables
    NT = te.shape[0]
    w1f = w1.reshape(-1, D, F)
    w3f = w3.reshape(-1, D, F)
    w2f = w2.reshape(-1, F, D)
    hbm = pl.BlockSpec(memory_space=pl.ANY)
    grid_spec = pltpu.PrefetchScalarGridSpec(
        num_scalar_prefetch=6,
        grid=(NT,),
        in_specs=[hbm, hbm, hbm, hbm],
        out_specs=pl.BlockSpec((tr * TOKEN_ROWS, LANES), lambda i, *_: (i, 0)),
        scratch_shapes=[pltpu.VMEM((tr * TOKEN_ROWS, LANES), F32),
                        pltpu.VMEM((tr * TOKEN_ROWS, LANES), F32),
                        pltpu.VMEM((2, D, F), F32),
                        pltpu.VMEM((2, D, F), F32),
                        pltpu.VMEM((2, F, D), F32),
                        pltpu.VMEM((D, F), BF16),
                        pltpu.VMEM((D, F), BF16),
                        pltpu.VMEM((F, D), BF16),
                        pltpu.SemaphoreType.DMA((2,)),
                        pltpu.SemaphoreType.DMA((2, 3))],
    )
    return pl.pallas_call(
        functools.partial(_expert_kernel, tr=tr, base=layer * N_EXPERTS),
        grid_spec=grid_spec,
        out_shape=jax.ShapeDtypeStruct((NT * tr * TOKEN_ROWS, LANES), F32),
        compiler_params=_params("arbitrary"),
        name="moe_experts",
    )(te, first, nxt, wsl, act, src, hp, w1f, w3f, w2f)


def _combine_kernel(d1_ref, d2_ref, x_ref, meta_ref, ys_hbm, *rest, tt, with_norm):
    if with_norm:
        g_ref, o_ref, hn_ref, buf, sem = rest
    else:
        o_ref, buf, sem = rest
    i = pl.program_id(0)
    n = pl.num_programs(0)
    slot = i % 2

    def issue(tile, slot):
        def body(c, _):
            for u in range(GATHER_CHUNK):
                r = c * GATHER_CHUNK + u
                t = tile * tt + r
                _tile_copy(ys_hbm, d1_ref[t], buf, slot, r, sem).start(priority=0)
                _tile_copy(ys_hbm, d2_ref[t], buf, slot, tt + r, sem).start(priority=1)
            return 0
        lax.fori_loop(0, tt // GATHER_CHUNK, body, 0)

    @pl.when(i == 0)
    def _():
        issue(0, 0)

    pltpu.make_async_copy(ys_hbm.at[pl.ds(0, 2 * tt * TOKEN_ROWS)], buf.at[slot], sem.at[slot]).wait()

    @pl.when(i + 1 < n)
    def _():
        issue(i + 1, 1 - slot)

    w1 = meta_ref[:, 2:3]
    w2 = meta_ref[:, 3:4]
    pa = _load_token_tiles(buf.at[slot], 0, tt)
    pb = _load_token_tiles(buf.at[slot], tt * TOKEN_ROWS, tt)
    cols = [x_ref[:, j * LANES:(j + 1) * LANES] + w1 * pa[j] + w2 * pb[j]
            for j in range(TOKEN_ROWS)]
    xn = jnp.concatenate(cols, axis=1)
    o_ref[...] = xn
    if with_norm:
        hn_ref[...] = (_rms(xn) * g_ref[...]).astype(hn_ref.dtype)


def moe_combine(x, meta, ys, d1, d2, g_next=None):
    T, D = x.shape
    tt = COMB_ROWS
    with_norm = g_next is not None
    row = pl.BlockSpec((tt, D), lambda i, d1, d2: (i, 0))
    in_specs = [row, pl.BlockSpec((tt, LANES), lambda i, d1, d2: (i, 0)),
                pl.BlockSpec(memory_space=pl.ANY)]
    args = [d1, d2, x, meta, ys]
    out_specs, out_shape = row, jax.ShapeDtypeStruct((T, D), F32)
    if with_norm:
        in_specs.append(pl.BlockSpec((1, D), lambda i, d1, d2: (0, 0)))
        args.append(g_next.reshape(1, D))
        out_specs, out_shape = [row, row], [out_shape, jax.ShapeDtypeStruct((T, D), BF16)]
    grid_spec = pltpu.PrefetchScalarGridSpec(
        num_scalar_prefetch=2,
        grid=(T // tt,),
        in_specs=in_specs,
        out_specs=out_specs,
        scratch_shapes=[pltpu.VMEM((2, 2 * tt * TOKEN_ROWS, LANES), F32),
                        pltpu.SemaphoreType.DMA((2,))],
    )
    return pl.pallas_call(
        functools.partial(_combine_kernel, tt=tt, with_norm=with_norm),
        grid_spec=grid_spec,
        out_shape=out_shape,
        compiler_params=_params("arbitrary"),
        name="moe_combine",
    )(*args)


def _tile_tables(cnt, T):
    E = N_EXPERTS
    counts = cnt[0, ROUTER_LANE0:ROUTER_LANE0 + E].astype(I32)
    tiles_e = (counts + MOE_ROWS - 1) // MOE_ROWS
    tile_end = jnp.cumsum(tiles_e)
    tile_start = tile_end - tiles_e
    nact = tile_end[-1]
    NT = (2 * T) // MOE_ROWS + E
    tid = jnp.arange(NT, dtype=I32)
    te = jnp.sum((jnp.minimum(tid, nact - 1)[:, None] >= tile_end[None, :]).astype(I32), axis=1)
    te = jnp.minimum(te, E - 1)
    active = tid < nact
    first = (active & (tid == tile_start[te])).astype(I32)
    eid = jnp.arange(E, dtype=I32)
    later = (eid[None, :] > eid[:, None]) & (tiles_e[None, :] > 0)
    nxt_e = jnp.min(jnp.where(later, eid[None, :], E), axis=1)
    nxt_e = jnp.where(nxt_e == E, -1, nxt_e)
    ordinal = jnp.cumsum((tiles_e > 0).astype(I32)) - 1
    return (te, first, nxt_e[te], ordinal[te] % 2, active.astype(I32)), NT * MOE_ROWS


def _lower_bound_logs(lb_param, layer):
    p = jax.nn.softmax(lb_param.astype(F32), axis=0)
    c = jnp.cumsum(p, axis=0)
    lb = c[layer] - c[0]
    return jnp.log(lb), jnp.log1p(-lb)


def kernel(x, mem, norm_mix, w_in, hg_lower_bounds, hg_norm, conv_w, fox_f_bias, fox_q_norm,
           fox_k_norm, w_out, norm_cross, norm_mem, w_cq, w_ck, w_cv, w_co, cross_q_norm,
           cross_k_norm, norm_ffn, router_group_w, router_group_b, router_expert_w,
           router_expert_b, moe_w1, moe_w3, moe_w2):
    B, S, D = x.shape
    M = mem.shape[1]
    L = w_in.shape[0]
    T = B * S
    HGW = 4 * HG_WIDTH
    CONV3 = 3 * CONV_WIDTH
    FQ0 = CONV3
    FK0 = FQ0 + FOX_WIDTH
    FV0 = FK0 + FOX_WIDTH
    PB = CONV3 + 3 * FOX_WIDTH
    dhc = D // CROSS_HEADS
    x2 = x.reshape(T, D)
    mem2 = mem.reshape(B * M, D)
    w_in_t = jnp.swapaxes(w_in, 1, 2)
    hn = rmsnorm(x2, norm_mix[0])
    for l in range(L):
        proj_a = matmul([hn], w_in_t, l, 0, HGW, F32, w_t=True)
        proj_b = matmul([hn], w_in_t, l, HGW, PB, BF16, w_t=True, head_norms=(
            (FQ0, FK0, fox_q_norm[l], FOX_DIM ** -0.5), (FK0, FV0, fox_k_norm[l], 1.0)))
        loglb, l1mlb = _lower_bound_logs(hg_lower_bounds, l)
        y_hg = hgrn2(proj_a.reshape(B, S, HGW), loglb, l1mlb, hg_norm[l])
        pb3 = proj_b.reshape(B, S, PB)
        y_conv = short_conv(pb3, conv_w[l])
        c = fox_gate(hn.reshape(B, S, D), w_in_t, l, HGW + PB, fox_f_bias[l])
        y_fox = fox_attention(pb3, FQ0, FK0, FV0, c)
        mix = [y_hg.reshape(T, HG_WIDTH), y_conv.reshape(T, CONV_WIDTH), y_fox.reshape(T, FOX_WIDTH)]
        x2, qcn = out_proj_cross_q(mix, x2, w_out, w_cq, l, norm_cross[l], cross_q_norm[l])
        memn = rmsnorm(mem2, norm_mem[l])
        kcn = matmul([memn], w_ck, l, 0, D, BF16, head_norms=((0, D, cross_k_norm[l], 1.0),))
        vc = matmul([memn], w_cv, l, 0, D, BF16)
        wr = jnp.concatenate([router_group_w[l], router_expert_w[l]], axis=1)
        wr = jnp.pad(wr, ((0, 0), (0, LANES - wr.shape[1])))
        br = jnp.concatenate([router_group_b[l], router_expert_b[l]])
        br = jnp.pad(br, (0, LANES - br.shape[0]))
        x3, hp, logits = cross_attention_ffn_in(
            qcn.reshape(B, S, D), kcn.reshape(B, M, D), vc.reshape(B, M, D), x2.reshape(B, S, D),
            w_co, l, norm_ffn[l], wr, br)
        x2 = x3.reshape(T, D)
        meta, cnt = route(logits)
        d = plan(meta, cnt)
        tables, R = _tile_tables(cnt, T)
        src = invert(d[0], d[1], R)
        ys = moe_experts(hp, moe_w1, moe_w3, moe_w2, l, tables, src)
        if l + 1 < L:
            x2, hn = moe_combine(x2, meta, ys, d[0], d[1], norm_mix[l + 1])
        else:
            x2 = moe_combine(x2, meta, ys, d[0], d[1])
    return x2.reshape(B, S, D)
```

```python
import functools

import numpy as np
import jax
import jax.numpy as jnp
from jax import lax
from jax.experimental import pallas as pl
from jax.experimental.pallas import tpu as pltpu

F32 = jnp.float32
BF16 = jnp.bfloat16
I32 = jnp.int32
EPS = 1e-6

HG_HEADS = 4
HG_DIM = 128
HG_WIDTH = HG_HEADS * HG_DIM
CONV_WIDTH = 512
FOX_HEADS = 8
FOX_DIM = 128
FOX_WIDTH = FOX_HEADS * FOX_DIM
CROSS_HEADS = 4
MOE_GROUPS = 4
MOE_EXPERTS = 8
N_EXPERTS = MOE_GROUPS * MOE_EXPERTS
ROUTER_LANE0 = MOE_GROUPS

LANES = 128
SUBLANES = 8
VMEM_LIMIT = 56 * 1024 * 1024

NORM_ROWS = 256
MM_TM = 1024
MM_TN = 512
HG_CHUNK = 128
FOX_TQ = 512
FOX_TK = 512
FOX_HEADS_PER_STEP = 2
CROSS_TS = 512
ROUTE_ROWS = 512
MOE_ROWS_LOG2 = 8
MOE_ROWS = 1 << MOE_ROWS_LOG2
COMB_ROWS = 256


def _params(*sem):
    return pltpu.CompilerParams(dimension_semantics=sem, vmem_limit_bytes=VMEM_LIMIT)


def _sigmoid(x):
    return 1.0 / (1.0 + jnp.exp(-x))


def _log_sigmoid(x):
    return jnp.minimum(x, 0.0) - jnp.log1p(jnp.exp(-jnp.abs(x)))


TOKEN_ROWS = 16
TOKEN_LANES = 128


def _store_token_tiles(ref, y, rows):
    for j in range(TOKEN_ROWS):
        ref[pl.ds(j, rows, stride=TOKEN_ROWS), :] = y[:, j * TOKEN_LANES:(j + 1) * TOKEN_LANES]


def _load_token_tiles(ref, base, rows):
    return [ref[pl.ds(base + j, rows, stride=TOKEN_ROWS), :] for j in range(TOKEN_ROWS)]


def _rms(x):
    return x * lax.rsqrt(jnp.mean(x * x, axis=-1, keepdims=True) + EPS)


def _rmsnorm_kernel(x_ref, g_ref, o_ref):
    o_ref[...] = (_rms(x_ref[...]) * g_ref[...]).astype(o_ref.dtype)


def rmsnorm(x, g, out_dtype=BF16):
    R, D = x.shape
    return pl.pallas_call(
        _rmsnorm_kernel,
        grid=(R // NORM_ROWS,),
        in_specs=[pl.BlockSpec((NORM_ROWS, D), lambda i: (i, 0)),
                  pl.BlockSpec((1, D), lambda i: (0, 0))],
        out_specs=pl.BlockSpec((NORM_ROWS, D), lambda i: (i, 0)),
        out_shape=jax.ShapeDtypeStruct((R, D), out_dtype),
        compiler_params=_params("parallel"),
        name="rmsnorm",
    )(x, g.reshape(1, D))


def _head_rmsnorm(x, g, dh, scale):
    heads = [_rms(x[:, h * dh:(h + 1) * dh]) * g * scale for h in range(x.shape[1] // dh)]
    return heads[0] if len(heads) == 1 else jnp.concatenate(heads, axis=1)


def _matmul_kernel(*refs, n_parts, w_t, norm_tiles, dh):
    a_refs = refs[:n_parts]
    w_ref = refs[n_parts]
    if norm_tiles:
        g_ref, o_ref, wb_ref = refs[n_parts + 1:]
    else:
        o_ref, wb_ref = refs[n_parts + 1:]
    j = pl.program_id(0)

    @pl.when(pl.program_id(1) == 0)
    def _():
        wb_ref[...] = w_ref[...].astype(BF16)

    if n_parts == 1:
        a = a_refs[0][...]
    else:
        a = jnp.concatenate([r[...] for r in a_refs], axis=1)
    if w_t:
        acc = lax.dot_general(a, wb_ref[...], (((1,), (1,)), ((), ())), preferred_element_type=F32)
    else:
        acc = jnp.dot(a, wb_ref[...], preferred_element_type=F32)
    if not norm_tiles:
        o_ref[...] = acc.astype(o_ref.dtype)
        return
    plain = True
    for k, (j0, j1, scale) in enumerate(norm_tiles):
        hit = (j >= j0) & (j < j1)
        plain = plain & jnp.logical_not(hit)

        @pl.when(hit)
        def _(k=k, scale=scale):
            o_ref[...] = _head_rmsnorm(acc, g_ref[k:k + 1, :], dh, scale).astype(o_ref.dtype)

    @pl.when(plain)
    def _():
        o_ref[...] = acc.astype(o_ref.dtype)


def matmul(a_parts, w, layer, col0, n, out_dtype, w_t=False, head_norms=(), tm=MM_TM, tn=MM_TN):
    M = a_parts[0].shape[0]
    K = w.shape[2] if w_t else w.shape[1]
    assert sum(p.shape[1] for p in a_parts) == K
    tm = min(tm, M)
    assert col0 % tn == 0 and n % tn == 0 and M % tm == 0
    cb = col0 // tn
    in_specs = [pl.BlockSpec((tm, p.shape[1]), lambda j, i: (i, 0)) for p in a_parts]
    if w_t:
        in_specs.append(pl.BlockSpec((None, tn, K), lambda j, i: (layer, cb + j, 0)))
    else:
        in_specs.append(pl.BlockSpec((None, K, tn), lambda j, i: (layer, 0, cb + j)))
    args = list(a_parts) + [w]
    norm_tiles, dh = (), 0
    if head_norms:
        dh = head_norms[0][2].shape[0]
        assert all(c0 % tn == 0 and c1 % tn == 0 and g.shape[0] == dh and tn % dh == 0
                   for c0, c1, g, _ in head_norms)
        norm_tiles = tuple((c0 // tn, c1 // tn, s) for c0, c1, _, s in head_norms)
        gains = jnp.stack([g for _, _, g, _ in head_norms])
        in_specs.append(pl.BlockSpec(gains.shape, lambda j, i: (0, 0)))
        args.append(gains)
    return pl.pallas_call(
        functools.partial(_matmul_kernel, n_parts=len(a_parts), w_t=w_t, norm_tiles=norm_tiles,
                          dh=dh),
        grid=(n // tn, M // tm),
        in_specs=in_specs,
        out_specs=pl.BlockSpec((tm, tn), lambda j, i: (i, j)),
        out_shape=jax.ShapeDtypeStruct((M, n), out_dtype),
        scratch_shapes=[pltpu.VMEM((tn, K) if w_t else (K, tn), BF16)],
        compiler_params=_params("arbitrary", "arbitrary"),
        name="matmul",
    )(*args)


W_CHUNK = 256


def _load_weight(w_hbm, layer, wb, stg, sem):
    n_chunks = wb.shape[0] // W_CHUNK

    def copy(c):
        return pltpu.make_async_copy(w_hbm.at[layer, pl.ds(c * W_CHUNK, W_CHUNK)], stg.at[c % 2],
                                     sem.at[c % 2])
    copy(0).start()
    for c in range(n_chunks):
        if c + 1 < n_chunks:
            copy(c + 1).start()
        copy(c).wait()
        wb[c * W_CHUNK:(c + 1) * W_CHUNK, :] = stg[c % 2].astype(BF16)


def _outq_kernel(a1_ref, a2_ref, a3_ref, x_ref, wo_hbm, wq_hbm, gn_ref, gq_ref, xo_ref, q_ref,
                 wo_b, wq_b, stg, sem, *, layer, dh, scale):
    @pl.when(pl.program_id(0) == 0)
    def _():
        _load_weight(wo_hbm, layer, wo_b, stg, sem)
        _load_weight(wq_hbm, layer, wq_b, stg, sem)

    a = jnp.concatenate([a1_ref[...], a2_ref[...], a3_ref[...]], axis=1)
    xn = x_ref[...] + jnp.dot(a, wo_b[...], preferred_element_type=F32)
    xo_ref[...] = xn
    hc = (_rms(xn) * gn_ref[...]).astype(BF16)
    q = jnp.dot(hc, wq_b[...], preferred_element_type=F32)
    q_ref[...] = _head_rmsnorm(q, gq_ref[...], dh, scale).astype(q_ref.dtype)


def out_proj_cross_q(mix, x, w_out, w_cq, layer, g_cross, g_q, tm=512):
    T, D = x.shape
    dh = g_q.shape[0]
    row = lambda w: pl.BlockSpec((tm, w), lambda i: (i, 0))
    hbm = pl.BlockSpec(memory_space=pl.ANY)
    return pl.pallas_call(
        functools.partial(_outq_kernel, layer=layer, dh=dh, scale=dh ** -0.5),
        grid=(T // tm,),
        in_specs=[row(mix[0].shape[1]), row(mix[1].shape[1]), row(mix[2].shape[1]), row(D), hbm, hbm,
                  pl.BlockSpec((1, D), lambda i: (0, 0)), pl.BlockSpec((1, dh), lambda i: (0, 0))],
        out_specs=[row(D), row(D)],
        out_shape=[jax.ShapeDtypeStruct((T, D), F32), jax.ShapeDtypeStruct((T, D), BF16)],
        scratch_shapes=[pltpu.VMEM((D, D), BF16), pltpu.VMEM((D, D), BF16),
                        pltpu.VMEM((2, W_CHUNK, D), F32), pltpu.SemaphoreType.DMA((2,))],
        compiler_params=_params("arbitrary"),
        name="out_proj_cross_q",
    )(*mix, x, w_out, w_cq, g_cross.reshape(1, D), g_q.reshape(1, dh))


def _hgrn_consts(C):
    nlev = int(np.log2(C))
    t = np.arange(C)[:, None]
    u = np.arange(C)[None, :]
    blocks = [(u <= t)]
    for j in range(nlev):
        m = 1 << j
        boundary = (t // (2 * m)) * (2 * m) + m - 1
        second = (t & m) != 0
        blocks.append(np.where(second, (u > boundary) & (u <= t), (u > t) & (u <= boundary)))
    mall = np.concatenate(blocks, axis=0).astype(np.float32)
    mall2 = np.concatenate([mall, mall], axis=1)
    tt = np.arange(C)[:, None]
    ss = np.arange(C)[None, :]
    lev = np.full((C, C), nlev + 1, np.int32)
    x = tt ^ ss
    hb = np.zeros_like(x)
    for j in range(nlev):
        hb = np.where((x >> j) & 1, j, hb)
    lev = np.where(tt > ss, hb, lev)
    lev = np.where(tt == ss, nlev, lev)
    return jnp.asarray(mall2, BF16), jnp.asarray(lev, jnp.int32)


def _hgrn_kernel(x_ref, mall_ref, lev_ref, loglb_ref, l1mlb_ref, ng_ref, o_ref, st_ref, *, C, nlev):
    @pl.when(pl.program_id(1) == 0)
    def _():
        st_ref[...] = jnp.zeros_like(st_ref)

    W = HG_WIDTH
    d = HG_DIM
    lev = lev_ref[...]
    mall = mall_ref[...]
    row = lax.broadcasted_iota(jnp.int32, (C, d), 0)
    nt = (((1,), (1,)), ((), ()))
    for h in range(HG_HEADS):
        sl = slice(h * d, (h + 1) * d)
        q = x_ref[0, :, h * d:(h + 1) * d]
        z = x_ref[0, :, W + h * d:W + (h + 1) * d]
        v = x_ref[0, :, 2 * W + h * d:2 * W + (h + 1) * d]
        g = x_ref[0, :, 3 * W + h * d:3 * W + (h + 1) * d]
        a = loglb_ref[:, sl]
        bt = l1mlb_ref[:, sl] + _log_sigmoid(z)
        lf = jnp.maximum(a, bt) + jnp.log1p(jnp.exp(-jnp.abs(a - bt)))
        kk = 1.0 - jnp.exp(lf)
        hi = lf.astype(BF16)
        lo = (lf - hi.astype(F32)).astype(BF16)
        sums = jnp.dot(mall, jnp.concatenate([hi, lo], axis=0), preferred_element_type=F32)
        b = sums[0:C]
        vb = v.astype(BF16)
        st = st_ref[h]
        o = lax.dot_general((q * jnp.exp(b)).astype(BF16), st.astype(BF16), nt,
                            preferred_element_type=F32)
        att = jnp.where(lev == nlev, jnp.sum(q * kk, axis=-1, keepdims=True), 0.0)
        for j in range(nlev):
            second = (row & (1 << j)) != 0
            e = jnp.exp(sums[(j + 1) * C:(j + 2) * C])
            xq = jnp.where(second, q * e, 0.0).astype(BF16)
            xk = jnp.where(second, 0.0, kk * e).astype(BF16)
            am = lax.dot_general(xq, xk, nt, preferred_element_type=F32)
            att = jnp.where(lev == j, am, att)
        o = o + jnp.dot(att.astype(BF16), vb, preferred_element_type=F32)
        bl = b[C - 1:C, :]
        kh = (kk * jnp.exp(bl - b)).astype(BF16)
        st_ref[h] = st * jnp.exp(bl) + lax.dot_general(
            vb, kh, (((0,), (0,)), ((), ())), preferred_element_type=F32)
        y = o * lax.rsqrt(jnp.mean(o * o, axis=-1, keepdims=True) + EPS) * ng_ref[...]
        o_ref[0, :, sl] = (y * (g * _sigmoid(g))).astype(o_ref.dtype)


def hgrn2(proj_a, loglb, l1mlb, norm_g):
    B, S, _ = proj_a.shape
    C = HG_CHUNK
    nlev = int(np.log2(C))
    mall, lev = _hgrn_consts(C)
    return pl.pallas_call(
        functools.partial(_hgrn_kernel, C=C, nlev=nlev),
        grid=(B, S // C),
        in_specs=[pl.BlockSpec((1, C, 4 * HG_WIDTH), lambda b, c: (b, c, 0)),
                  pl.BlockSpec(mall.shape, lambda b, c: (0, 0)),
                  pl.BlockSpec((C, C), lambda b, c: (0, 0)),
                  pl.BlockSpec((1, HG_WIDTH), lambda b, c: (0, 0)),
                  pl.BlockSpec((1, HG_WIDTH), lambda b, c: (0, 0)),
                  pl.BlockSpec((1, HG_DIM), lambda b, c: (0, 0))],
        out_specs=pl.BlockSpec((1, C, HG_WIDTH), lambda b, c: (b, c, 0)),
        out_shape=jax.ShapeDtypeStruct((B, S, HG_WIDTH), BF16),
        scratch_shapes=[pltpu.VMEM((HG_HEADS, HG_DIM, HG_DIM), F32)],
        compiler_params=_params("parallel", "arbitrary"),
        name="hgrn2",
    )(proj_a, mall, lev, loglb.reshape(1, HG_WIDTH), l1mlb.reshape(1, HG_WIDTH),
      norm_g.reshape(1, HG_DIM))


def _conv_kernel(b_ref, c_ref, h_ref, w_ref, o_ref):
    u = c_ref[0].astype(F32) * h_ref[0].astype(F32)
    row = lax.broadcasted_iota(jnp.int32, u.shape, 0)
    u1 = jnp.where(row >= 1, pltpu.roll(u, 1, axis=0), 0.0)
    u2 = jnp.where(row >= 2, pltpu.roll(u, 2, axis=0), 0.0)
    y = w_ref[0:1, :] * u2 + w_ref[1:2, :] * u1 + w_ref[2:3, :] * u
    o_ref[0] = (b_ref[0].astype(F32) * y).astype(o_ref.dtype)


def short_conv(proj_b, w):
    B, S, _ = proj_b.shape
    cw = CONV_WIDTH
    spec = lambda k: pl.BlockSpec((1, S, cw), lambda b: (b, 0, k))
    return pl.pallas_call(
        _conv_kernel,
        grid=(B,),
        in_specs=[spec(0), spec(1), spec(2), pl.BlockSpec((3, cw), lambda b: (0, 0))],
        out_specs=pl.BlockSpec((1, S, cw), lambda b: (b, 0, 0)),
        out_shape=jax.ShapeDtypeStruct((B, S, cw), BF16),
        compiler_params=_params("parallel"),
        name="short_conv",
    )(proj_b, proj_b, proj_b, w)


def _fox_gate_kernel(hn_ref, wf_ref, bias_ref, c_ref):
    gl = lax.dot_general(wf_ref[...].astype(BF16), hn_ref[0], (((1,), (1,)), ((), ())),
                         preferred_element_type=F32)
    c = _log_sigmoid(gl + bias_ref[...])
    S = c.shape[1]
    lane = lax.broadcasted_iota(jnp.int32, c.shape, 1)
    sh = 1
    while sh < S:
        c = c + jnp.where(lane >= sh, pltpu.roll(c, sh, axis=1), 0.0)
        sh *= 2
    c_ref[0] = c


def fox_gate(hn, w_t, layer, row0, bias):
    B, S, D = hn.shape
    H = bias.shape[0]
    assert row0 % H == 0
    return pl.pallas_call(
        _fox_gate_kernel,
        grid=(B,),
        in_specs=[pl.BlockSpec((1, S, D), lambda b: (b, 0, 0)),
                  pl.BlockSpec((None, H, D), lambda b: (layer, row0 // H, 0)),
                  pl.BlockSpec((H, 1), lambda b: (0, 0))],
        out_specs=pl.BlockSpec((1, H, S), lambda b: (b, 0, 0)),
        out_shape=jax.ShapeDtypeStruct((B, H, S), F32),
        compiler_params=_params("parallel"),
        name="fox_gate",
    )(hn, w_t, bias.reshape(H, 1))


def _fox_kernel(q_ref, k_ref, v_ref, c_ref, o_ref, *, tq, tk, nh):
    qi = pl.program_id(2)
    nt = (((1,), (1,)), ((), ()))
    ones = jnp.ones((tk, FOX_DIM), BF16)
    heads = [slice(h * FOX_DIM, (h + 1) * FOX_DIM) for h in range(nh)]
    qs = [q_ref[0, :, sl] for sl in heads]

    def step(ki, carry, diag):
        out = []
        for h, sl in enumerate(heads):
            m, acc = carry[h]
            k = k_ref[0, pl.ds(ki * tk, tk), sl]
            v1 = jnp.concatenate([v_ref[0, pl.ds(ki * tk, tk), sl], ones], axis=1)
            ck = c_ref[0, h, pl.ds(ki, 1), :]
            s = lax.dot_general(qs[h], k, nt, preferred_element_type=F32) - ck
            if diag is not None:
                r = lax.broadcasted_iota(jnp.int32, s.shape, 0)
                c = lax.broadcasted_iota(jnp.int32, s.shape, 1)
                s = jnp.where(c + diag * tk <= r, s, -jnp.inf)
            m_new = jnp.maximum(m, jnp.max(s, axis=-1, keepdims=True))
            alpha = jnp.exp(m - m_new)
            p = jnp.exp((s - m_new).astype(BF16))
            out.append((m_new, alpha * acc + jnp.dot(p, v1, preferred_element_type=F32)))
        return tuple(out)

    init = tuple((jnp.full((tq, 1), -jnp.inf, F32), jnp.zeros((tq, 2 * FOX_DIM), F32))
                 for _ in heads)
    n_full = qi * (tq // tk)
    carry = lax.fori_loop(0, n_full, lambda ki, cr: step(ki, cr, None), init)
    for d in range(tq // tk):
        carry = step(n_full + d, carry, d)
    for h, sl in enumerate(heads):
        acc = carry[h][1]
        o_ref[0, :, sl] = (acc[:, :FOX_DIM] / acc[:, FOX_DIM:]).astype(o_ref.dtype)


def fox_attention(proj_b, q_col0, k_col0, v_col0, c):
    B, S, _ = proj_b.shape
    tq, tk, nh = FOX_TQ, FOX_TK, FOX_HEADS_PER_STEP
    H = FOX_HEADS
    w = nh * FOX_DIM
    assert tq % tk == 0 and H % nh == 0 and all(c0 % w == 0 for c0 in (q_col0, k_col0, v_col0))
    qb, kb, vb = q_col0 // w, k_col0 // w, v_col0 // w
    c4 = c.reshape(B, H, S // tk, tk)
    return pl.pallas_call(
        functools.partial(_fox_kernel, tq=tq, tk=tk, nh=nh),
        grid=(B, H // nh, S // tq),
        in_specs=[pl.BlockSpec((1, tq, w), lambda b, h, i: (b, i, qb + h)),
                  pl.BlockSpec((1, S, w), lambda b, h, i: (b, 0, kb + h)),
                  pl.BlockSpec((1, S, w), lambda b, h, i: (b, 0, vb + h)),
                  pl.BlockSpec((1, nh, S // tk, tk), lambda b, h, i: (b, h, 0, 0))],
        out_specs=pl.BlockSpec((1, tq, w), lambda b, h, i: (b, i, h)),
        out_shape=jax.ShapeDtypeStruct((B, S, FOX_WIDTH), BF16),
        compiler_params=_params("parallel", "parallel", "arbitrary"),
        name="fox_attention",
    )(proj_b, proj_b, proj_b, c4)


def _cross_kernel(q_ref, k_ref, v_ref, x_ref, wo_hbm, gn_ref, wr_ref, br_ref,
                  xo_ref, hp_ref, lg_ref, wo_b, stg, sem, *, layer, dh):
    @pl.when((pl.program_id(0) == 0) & (pl.program_id(1) == 0))
    def _():
        _load_weight(wo_hbm, layer, wo_b, stg, sem)

    nt = (((1,), (1,)), ((), ()))
    heads = []
    for h in range(CROSS_HEADS):
        sl = slice(h * dh, (h + 1) * dh)
        s = lax.dot_general(q_ref[0, :, sl], k_ref[0, :, sl], nt, preferred_element_type=F32)
        m = jnp.max(s, axis=-1, keepdims=True)
        p = jnp.exp(s - m)
        l = jnp.sum(p, axis=-1, keepdims=True)
        o = jnp.dot(p.astype(BF16), v_ref[0, :, sl], preferred_element_type=F32)
        heads.append((o / l).astype(BF16))
    xn = x_ref[0] + jnp.dot(jnp.concatenate(heads, axis=1), wo_b[...], preferred_element_type=F32)
    xo_ref[0] = xn
    y = _rms(xn) * gn_ref[...]
    _store_token_tiles(hp_ref, y, y.shape[0])
    yh = y.astype(BF16)
    yl = (y - yh.astype(F32)).astype(BF16)
    wr = wr_ref[...]
    wh = wr.astype(BF16)
    wl = (wr - wh.astype(F32)).astype(BF16)
    lg_ref[...] = (jnp.dot(yh, wh, preferred_element_type=F32)
                   + jnp.dot(yl, wh, preferred_element_type=F32)
                   + jnp.dot(yh, wl, preferred_element_type=F32)) + br_ref[...]


def cross_attention_ffn_in(qn, kn, v, x, w_co, layer, g_ffn, wr, br):
    B, S, D = qn.shape
    M = kn.shape[1]
    ts = CROSS_TS
    ns = S // ts
    assert D == TOKEN_ROWS * TOKEN_LANES
    tile = pl.BlockSpec((1, ts, D), lambda b, i: (b, i, 0))
    memb = pl.BlockSpec((1, M, D), lambda b, i: (b, 0, 0))
    const = lambda shape: pl.BlockSpec(shape, lambda b, i: (0, 0))
    return pl.pallas_call(
        functools.partial(_cross_kernel, layer=layer, dh=D // CROSS_HEADS),
        grid=(B, ns),
        in_specs=[tile, memb, memb, tile, pl.BlockSpec(memory_space=pl.ANY),
                  const((1, D)), const((D, LANES)), const((1, LANES))],
        out_specs=[tile,
                   pl.BlockSpec((ts * TOKEN_ROWS, TOKEN_LANES), lambda b, i: (b * ns + i, 0)),
                   pl.BlockSpec((ts, LANES), lambda b, i: (b * ns + i, 0))],
        out_shape=[jax.ShapeDtypeStruct((B, S, D), F32),
                   jax.ShapeDtypeStruct((B * S * TOKEN_ROWS, TOKEN_LANES), F32),
                   jax.ShapeDtypeStruct((B * S, LANES), F32)],
        scratch_shapes=[pltpu.VMEM((D, D), BF16), pltpu.VMEM((2, W_CHUNK, D), F32),
                        pltpu.SemaphoreType.DMA((2,))],
        compiler_params=_params("arbitrary", "arbitrary"),
        name="cross_attention_ffn_in",
    )(qn, kn, v, x, w_co, g_ffn.reshape(1, D), wr, br.reshape(1, LANES))


def _route_kernel(lg_ref, tri_ref, meta_ref, cnt_ref, carry_ref):
    @pl.when(pl.program_id(0) == 0)
    def _():
        carry_ref[...] = jnp.zeros_like(carry_ref)

    x = lg_ref[...]
    lane = lax.broadcasted_iota(jnp.int32, x.shape, 1)
    ninf = -jnp.inf
    is_g = lane < MOE_GROUPS
    gl = jnp.where(is_g, x, ninf)
    gmax = jnp.max(gl, axis=-1, keepdims=True)
    gidx = jnp.min(jnp.where(gl == gmax, lane, LANES), axis=-1, keepdims=True)
    gw = 1.0 / jnp.sum(jnp.where(is_g, jnp.exp(x - gmax), 0.0), axis=-1, keepdims=True)
    e_lane = lane - ROUTER_LANE0
    in_grp = (e_lane >= 0) & (e_lane < N_EXPERTS) & ((e_lane >> 3) == gidx)
    el = jnp.where(in_grp, x, ninf)
    m1 = jnp.max(el, axis=-1, keepdims=True)
    i1 = jnp.min(jnp.where(el == m1, lane, LANES), axis=-1, keepdims=True)
    el2 = jnp.where(lane == i1, ninf, el)
    m2 = jnp.max(el2, axis=-1, keepdims=True)
    i2 = jnp.min(jnp.where(el2 == m2, lane, LANES), axis=-1, keepdims=True)
    r = jnp.exp(m2 - m1)
    p1 = 1.0 / (1.0 + r)
    p2 = r * p1
    oh = ((lane == i1) | (lane == i2)).astype(F32)
    prefix = jnp.dot(tri_ref[...], oh.astype(BF16), preferred_element_type=F32) + carry_ref[...]
    rank1 = jnp.sum(jnp.where(lane == i1, prefix, 0.0), axis=-1, keepdims=True)
    rank2 = jnp.sum(jnp.where(lane == i2, prefix, 0.0), axis=-1, keepdims=True)
    carry_ref[...] += jnp.sum(oh, axis=0, keepdims=True)
    cnt_ref[...] = carry_ref[...]
    cols = [i1.astype(F32), i2.astype(F32), gw * p1, gw * p2, rank1, rank2]
    meta = jnp.zeros(x.shape, F32)
    for k, col in enumerate(cols):
        meta = jnp.where(lane == k, col, meta)
    meta_ref[...] = meta


def route(logits):
    T = logits.shape[0]
    tt = ROUTE_ROWS
    tri = jnp.asarray(np.tril(np.ones((tt, tt), np.float32), -1), BF16)
    return pl.pallas_call(
        _route_kernel,
        grid=(T // tt,),
        in_specs=[pl.BlockSpec((tt, LANES), lambda i: (i, 0)),
                  pl.BlockSpec((tt, tt), lambda i: (0, 0))],
        out_specs=[pl.BlockSpec((tt, LANES), lambda i: (i, 0)),
                   pl.BlockSpec((1, LANES), lambda i: (0, 0))],
        out_shape=[jax.ShapeDtypeStruct((T, LANES), F32),
                   jax.ShapeDtypeStruct((1, LANES), F32)],
        scratch_shapes=[pltpu.VMEM((1, LANES), F32)],
        compiler_params=_params("arbitrary"),
        name="moe_route",
    )(logits, tri)


def _plan_kernel(meta_ref, cnt_ref, d_ref):
    cnt = jnp.broadcast_to(cnt_ref[...], (SUBLANES, LANES)).astype(I32)
    pad = (((cnt + (MOE_ROWS - 1)) >> MOE_ROWS_LOG2) << MOE_ROWS_LOG2).astype(F32)
    lane8 = lax.broadcasted_iota(I32, pad.shape, 1)
    incl = pad
    sh = 1
    while sh < LANES:
        incl = incl + jnp.where(lane8 >= sh, pltpu.roll(incl, sh, axis=1), 0.0)
        sh *= 2
    off = (incl - pad)[0:1, :]
    m = meta_ref[...]
    lane = lax.broadcasted_iota(I32, m.shape, 1)
    i1 = m[:, 0:1].astype(I32)
    i2 = m[:, 1:2].astype(I32)
    d1 = jnp.sum(jnp.where(lane == i1, off, 0.0), axis=-1, keepdims=True) + m[:, 4:5]
    d2 = jnp.sum(jnp.where(lane == i2, off, 0.0), axis=-1, keepdims=True) + m[:, 5:6]
    dm = jnp.where(lane == 0, d1, jnp.where(lane == 1, d2, 0.0))
    d_ref[...] = jnp.transpose(dm)[0:2, :].astype(I32)


def plan(meta, cnt):
    T = meta.shape[0]
    tt = ROUTE_ROWS
    return pl.pallas_call(
        _plan_kernel,
        grid=(T // tt,),
        in_specs=[pl.BlockSpec((tt, LANES), lambda i: (i, 0)),
                  pl.BlockSpec((1, LANES), lambda i: (0, 0))],
        out_specs=pl.BlockSpec((2, tt), lambda i: (0, i)),
        out_shape=jax.ShapeDtypeStruct((2, T), I32),
        compiler_params=_params("parallel"),
        name="moe_plan",
    )(meta, cnt)


def _invert_kernel(d1_ref, d2_ref, src_ref, *, T, R):
    def init(r, _):
        src_ref[r] = 0
        return 0
    lax.fori_loop(0, R, init, 0, unroll=32)

    def body(t, _):
        src_ref[d1_ref[t]] = t
        src_ref[d2_ref[t]] = t
        return 0
    lax.fori_loop(0, T, body, 0, unroll=8)


def invert(d1, d2, R):
    T = d1.shape[0]
    grid_spec = pltpu.PrefetchScalarGridSpec(
        num_scalar_prefetch=2, grid=(1,), in_specs=[],
        out_specs=pl.BlockSpec(memory_space=pltpu.SMEM))
    return pl.pallas_call(
        functools.partial(_invert_kernel, T=T, R=R),
        grid_spec=grid_spec,
        out_shape=jax.ShapeDtypeStruct((R,), I32),
        compiler_params=_params("arbitrary"),
        name="moe_invert",
    )(d1, d2)


GATHER_CHUNK = 8


def _tile_copy(src_hbm, row, dst, slot, r, sem):
    return pltpu.make_async_copy(src_hbm.at[pl.ds(row * TOKEN_ROWS, TOKEN_ROWS)],
                                 dst.at[slot, pl.ds(r * TOKEN_ROWS, TOKEN_ROWS)], sem.at[slot])


def _expert_kernel(te_ref, first_ref, nxt_ref, wsl_ref, nch_ref, src_ref,
                   hp_hbm, w1_hbm, w3_hbm, w2_hbm, o_ref,
                   xbuf, r1, r3, r2, w1b, w3b, w2b, gsem, wsem, *, tr, base):
    i = pl.program_id(0)
    n = pl.num_programs(0)
    slot = i % 2

    def weight_copies(e, ws):
        return [pltpu.make_async_copy(w1_hbm.at[base + e], r1.at[ws], wsem.at[ws, 0]),
                pltpu.make_async_copy(w3_hbm.at[base + e], r3.at[ws], wsem.at[ws, 1]),
                pltpu.make_async_copy(w2_hbm.at[base + e], r2.at[ws], wsem.at[ws, 2])]

    def issue_rows(tile, slot, c0, c1):
        def body(c, _):
            for u in range(GATHER_CHUNK):
                r = c * GATHER_CHUNK + u
                _tile_copy(hp_hbm, src_ref[tile * tr + r], xbuf, slot, r, gsem).start()
            return 0
        lax.fori_loop(c0, c1, body, 0)

    def wait_rows(tile, slot):
        rows = GATHER_CHUNK * TOKEN_ROWS
        def body(c, _):
            pltpu.make_async_copy(hp_hbm.at[pl.ds(0, rows)], xbuf.at[slot, pl.ds(0, rows)],
                                  gsem.at[slot]).wait()
            return 0
        lax.fori_loop(0, nch_ref[tile], body, 0)

    @pl.when(i == 0)
    def _():
        xbuf[...] = jnp.zeros_like(xbuf)
        for cp in weight_copies(te_ref[0], 0):
            cp.start(priority=1)
        issue_rows(0, 0, 0, nch_ref[0])

    wait_rows(i, slot)

    nxt_tile = jnp.minimum(i + 1, n - 1)
    nch_next = jnp.where(i + 1 < n, nch_ref[nxt_tile], 0)
    issue_rows(nxt_tile, 1 - slot, 0, nch_next // 2)

    @pl.when(first_ref[i] == 1)
    def _():
        ws = wsl_ref[i]
        for cp in weight_copies(te_ref[i], ws):
            cp.wait()

        @pl.when(nxt_ref[i] >= 0)
        def _():
            for cp in weight_copies(nxt_ref[i], 1 - ws):
                cp.start(priority=1)

        w1b[...] = r1[ws].astype(BF16)
        w3b[...] = r3[ws].astype(BF16)
        w2b[...] = r2[ws].astype(BF16)

    @pl.when(nch_ref[i] > 0)
    def _():
        x = jnp.concatenate([p.astype(BF16) for p in _load_token_tiles(xbuf.at[slot], 0, tr)],
                            axis=1)
        h1 = jnp.dot(x, w1b[...], preferred_element_type=F32)
        h3 = jnp.dot(x, w3b[...], preferred_element_type=F32)
        hm = (h1 * _sigmoid(h1) * h3).astype(BF16)
        issue_rows(nxt_tile, 1 - slot, nch_next // 2, nch_next)
        _store_token_tiles(o_ref, jnp.dot(hm, w2b[...], preferred_element_type=F32), tr)

    @pl.when(nch_ref[i] == 0)
    def _():
        o_ref[...] = jnp.zeros_like(o_ref)


def moe_experts(hp, w1, w3, w2, layer, tables, src):
    D, F = w1.shape[-2:]
    tr = MOE_ROWS
    te, first, nxt, wsl, act = tables
    NT = te.shape[0]
    w1f = w1.reshape(-1, D, F)
    w3f = w3.reshape(-1, D, F)
    w2f = w2.reshape(-1, F, D)
    hbm = pl.BlockSpec(memory_space=pl.ANY)
    grid_spec = pltpu.PrefetchScalarGridSpec(
        num_scalar_prefetch=6,
        grid=(NT,),
        in_specs=[hbm, hbm, hbm, hbm],
        out_specs=pl.BlockSpec((tr * TOKEN_ROWS, TOKEN_LANES), lambda i, *_: (i, 0)),
        scratch_shapes=[pltpu.VMEM((2, tr * TOKEN_ROWS, TOKEN_LANES), F32),
                        pltpu.VMEM((2, D, F), F32),
                        pltpu.VMEM((2, D, F), F32),
                        pltpu.VMEM((2, F, D), F32),
                        pltpu.VMEM((D, F), BF16),
                        pltpu.VMEM((D, F), BF16),
                        pltpu.VMEM((F, D), BF16),
                        pltpu.SemaphoreType.DMA((2,)),
                        pltpu.SemaphoreType.DMA((2, 3))],
    )
    return pl.pallas_call(
        functools.partial(_expert_kernel, tr=tr, base=layer * N_EXPERTS),
        grid_spec=grid_spec,
        out_shape=jax.ShapeDtypeStruct((NT * tr * TOKEN_ROWS, TOKEN_LANES), F32),
        compiler_params=_params("arbitrary"),
        name="moe_experts",
    )(te, first, nxt, wsl, act, src, hp, w1f, w3f, w2f)


def _combine_kernel(d1_ref, d2_ref, x_ref, meta_ref, ys_hbm, *rest, tt, with_norm):
    if with_norm:
        g_ref, o_ref, hn_ref, buf, sem = rest
    else:
        o_ref, buf, sem = rest
    i = pl.program_id(0)
    n = pl.num_programs(0)
    slot = i % 2

    def issue(tile, slot):
        def body(c, _):
            for u in range(GATHER_CHUNK):
                r = c * GATHER_CHUNK + u
                t = tile * tt + r
                _tile_copy(ys_hbm, d1_ref[t], buf, slot, r, sem).start(priority=0)
                _tile_copy(ys_hbm, d2_ref[t], buf, slot, tt + r, sem).start(priority=1)
            return 0
        lax.fori_loop(0, tt // GATHER_CHUNK, body, 0)

    @pl.when(i == 0)
    def _():
        issue(0, 0)

    pltpu.make_async_copy(ys_hbm.at[pl.ds(0, 2 * tt * TOKEN_ROWS)], buf.at[slot], sem.at[slot]).wait()

    @pl.when(i + 1 < n)
    def _():
        issue(i + 1, 1 - slot)

    w1 = meta_ref[:, 2:3]
    w2 = meta_ref[:, 3:4]
    pa = _load_token_tiles(buf.at[slot], 0, tt)
    pb = _load_token_tiles(buf.at[slot], tt * TOKEN_ROWS, tt)
    cols = [x_ref[:, j * TOKEN_LANES:(j + 1) * TOKEN_LANES] + w1 * pa[j] + w2 * pb[j]
            for j in range(TOKEN_ROWS)]
    xn = jnp.concatenate(cols, axis=1)
    o_ref[...] = xn
    if with_norm:
        hn_ref[...] = (_rms(xn) * g_ref[...]).astype(hn_ref.dtype)


def moe_combine(x, meta, ys, d1, d2, g_next=None):
    T, D = x.shape
    tt = COMB_ROWS
    with_norm = g_next is not None
    row = pl.BlockSpec((tt, D), lambda i, d1, d2: (i, 0))
    in_specs = [row, pl.BlockSpec((tt, LANES), lambda i, d1, d2: (i, 0)),
                pl.BlockSpec(memory_space=pl.ANY)]
    args = [d1, d2, x, meta, ys]
    out_specs, out_shape = row, jax.ShapeDtypeStruct((T, D), F32)
    if with_norm:
        in_specs.append(pl.BlockSpec((1, D), lambda i, d1, d2: (0, 0)))
        args.append(g_next.reshape(1, D))
        out_specs, out_shape = [row, row], [out_shape, jax.ShapeDtypeStruct((T, D), BF16)]
    grid_spec = pltpu.PrefetchScalarGridSpec(
        num_scalar_prefetch=2,
        grid=(T // tt,),
        in_specs=in_specs,
        out_specs=out_specs,
        scratch_shapes=[pltpu.VMEM((2, 2 * tt * TOKEN_ROWS, TOKEN_LANES), F32),
                        pltpu.SemaphoreType.DMA((2,))],
    )
    return pl.pallas_call(
        functools.partial(_combine_kernel, tt=tt, with_norm=with_norm),
        grid_spec=grid_spec,
        out_shape=out_shape,
        compiler_params=_params("arbitrary"),
        name="moe_combine",
    )(*args)


def _tile_tables(cnt, T):
    E = N_EXPERTS
    counts = cnt[0, ROUTER_LANE0:ROUTER_LANE0 + E].astype(I32)
    tiles_e = (counts + MOE_ROWS - 1) // MOE_ROWS
    tile_end = jnp.cumsum(tiles_e)
    tile_start = tile_end - tiles_e
    nact = tile_end[-1]
    NT = (2 * T) // MOE_ROWS + E
    tid = jnp.arange(NT, dtype=I32)
    te = jnp.sum((jnp.minimum(tid, nact - 1)[:, None] >= tile_end[None, :]).astype(I32), axis=1)
    te = jnp.minimum(te, E - 1)
    active = tid < nact
    first = (active & (tid == tile_start[te])).astype(I32)
    eid = jnp.arange(E, dtype=I32)
    later = (eid[None, :] > eid[:, None]) & (tiles_e[None, :] > 0)
    nxt_e = jnp.min(jnp.where(later, eid[None, :], E), axis=1)
    nxt_e = jnp.where(nxt_e == E, -1, nxt_e)
    ordinal = jnp.cumsum((tiles_e > 0).astype(I32)) - 1
    valid = jnp.clip(counts[te] - (tid - tile_start[te]) * MOE_ROWS, 0, MOE_ROWS)
    nch = jnp.where(active, (valid + GATHER_CHUNK - 1) // GATHER_CHUNK, 0)
    return (te, first, nxt_e[te], ordinal[te] % 2, nch), NT * MOE_ROWS


def _lower_bound_logs(lb_param, layer):
    p = jax.nn.softmax(lb_param.astype(F32), axis=0)
    c = jnp.cumsum(p, axis=0)
    lb = c[layer] - c[0]
    return jnp.log(lb), jnp.log1p(-lb)


def kernel(x, mem, norm_mix, w_in, hg_lower_bounds, hg_norm, conv_w, fox_f_bias, fox_q_norm,
           fox_k_norm, w_out, norm_cross, norm_mem, w_cq, w_ck, w_cv, w_co, cross_q_norm,
           cross_k_norm, norm_ffn, router_group_w, router_group_b, router_expert_w,
           router_expert_b, moe_w1, moe_w3, moe_w2):
    B, S, D = x.shape
    M = mem.shape[1]
    L = w_in.shape[0]
    T = B * S
    HGW = 4 * HG_WIDTH
    CONV3 = 3 * CONV_WIDTH
    FQ0 = CONV3
    FK0 = FQ0 + FOX_WIDTH
    FV0 = FK0 + FOX_WIDTH
    PB = CONV3 + 3 * FOX_WIDTH
    dhc = D // CROSS_HEADS
    x2 = x.reshape(T, D)
    mem2 = mem.reshape(B * M, D)
    w_in_t = jnp.swapaxes(w_in, 1, 2)
    hn = rmsnorm(x2, norm_mix[0])
    for l in range(L):
        proj_a = matmul([hn], w_in_t, l, 0, HGW, F32, w_t=True)
        proj_b = matmul([hn], w_in_t, l, HGW, PB, BF16, w_t=True, head_norms=(
            (FQ0, FK0, fox_q_norm[l], FOX_DIM ** -0.5), (FK0, FV0, fox_k_norm[l], 1.0)))
        loglb, l1mlb = _lower_bound_logs(hg_lower_bounds, l)
        y_hg = hgrn2(proj_a.reshape(B, S, HGW), loglb, l1mlb, hg_norm[l])
        pb3 = proj_b.reshape(B, S, PB)
        y_conv = short_conv(pb3, conv_w[l])
        c = fox_gate(hn.reshape(B, S, D), w_in_t, l, HGW + PB, fox_f_bias[l])
        y_fox = fox_attention(pb3, FQ0, FK0, FV0, c)
        mix = [y_hg.reshape(T, HG_WIDTH), y_conv.reshape(T, CONV_WIDTH), y_fox.reshape(T, FOX_WIDTH)]
        x2, qcn = out_proj_cross_q(mix, x2, w_out, w_cq, l, norm_cross[l], cross_q_norm[l])
        memn = rmsnorm(mem2, norm_mem[l])
        kcn = matmul([memn], w_ck, l, 0, D, BF16, head_norms=((0, D, cross_k_norm[l], 1.0),))
        vc = matmul([memn], w_cv, l, 0, D, BF16)
        wr = jnp.concatenate([router_group_w[l], router_expert_w[l]], axis=1)
        wr = jnp.pad(wr, ((0, 0), (0, LANES - wr.shape[1])))
        br = jnp.concatenate([router_group_b[l], router_expert_b[l]])
        br = jnp.pad(br, (0, LANES - br.shape[0]))
        x3, hp, logits = cross_attention_ffn_in(
            qcn.reshape(B, S, D), kcn.reshape(B, M, D), vc.reshape(B, M, D), x2.reshape(B, S, D),
            w_co, l, norm_ffn[l], wr, br)
        x2 = x3.reshape(T, D)
        meta, cnt = route(logits)
        d = plan(meta, cnt)
        tables, R = _tile_tables(cnt, T)
        src = invert(d[0], d[1], R)
        ys = moe_experts(hp, moe_w1, moe_w3, moe_w2, l, tables, src)
        if l + 1 < L:
            x2, hn = moe_combine(x2, meta, ys, d[0], d[1], norm_mix[l + 1])
        else:
            x2 = moe_combine(x2, meta, ys, d[0], d[1])
    return x2.reshape(B, S, D)
```

```python
import functools

import numpy as np
import jax
import jax.numpy as jnp
from jax import lax
from jax.experimental import pallas as pl
from jax.experimental.pallas import tpu as pltpu

F32 = jnp.float32
BF16 = jnp.bfloat16
I32 = jnp.int32
EPS = 1e-6

HG_HEADS = 4
HG_DIM = 128
HG_WIDTH = HG_HEADS * HG_DIM
CONV_WIDTH = 512
FOX_HEADS = 8
FOX_DIM = 128
FOX_WIDTH = FOX_HEADS * FOX_DIM
CROSS_HEADS = 4
MOE_GROUPS = 4
MOE_EXPERTS = 8
N_EXPERTS = MOE_GROUPS * MOE_EXPERTS
ROUTER_LANE0 = MOE_GROUPS

LANES = 128
SUBLANES = 8
VMEM_LIMIT = 56 * 1024 * 1024

NORM_ROWS = 256
MM_TM = 2048
MM_TN = 512
HG_CHUNK = 128
FOX_TQ = 512
FOX_TK = 512
FOX_HEADS_PER_STEP = 2
CROSS_TS = 512
ROUTE_ROWS = 512
MOE_ROWS_LOG2 = 8
MOE_ROWS = 1 << MOE_ROWS_LOG2
COMB_ROWS = 256


def _params(*sem):
    return pltpu.CompilerParams(dimension_semantics=sem, vmem_limit_bytes=VMEM_LIMIT)


def _sigmoid(x):
    return 1.0 / (1.0 + jnp.exp(-x))


def _log_sigmoid(x):
    return jnp.minimum(x, 0.0) - jnp.log1p(jnp.exp(-jnp.abs(x)))


TOKEN_ROWS = 16
TOKEN_LANES = 128


def _store_token_tiles(ref, y, rows):
    for j in range(TOKEN_ROWS):
        ref[pl.ds(j, rows, stride=TOKEN_ROWS), :] = y[:, j * TOKEN_LANES:(j + 1) * TOKEN_LANES]


def _load_token_tiles(ref, base, rows):
    return [ref[pl.ds(base + j, rows, stride=TOKEN_ROWS), :] for j in range(TOKEN_ROWS)]


def _rms(x):
    return x * lax.rsqrt(jnp.mean(x * x, axis=-1, keepdims=True) + EPS)


def _rmsnorm_kernel(x_ref, g_ref, o_ref):
    o_ref[...] = (_rms(x_ref[...]) * g_ref[...]).astype(o_ref.dtype)


def rmsnorm(x, g, out_dtype=BF16):
    R, D = x.shape
    return pl.pallas_call(
        _rmsnorm_kernel,
        grid=(R // NORM_ROWS,),
        in_specs=[pl.BlockSpec((NORM_ROWS, D), lambda i: (i, 0)),
                  pl.BlockSpec((1, D), lambda i: (0, 0))],
        out_specs=pl.BlockSpec((NORM_ROWS, D), lambda i: (i, 0)),
        out_shape=jax.ShapeDtypeStruct((R, D), out_dtype),
        compiler_params=_params("parallel"),
        name="rmsnorm",
    )(x, g.reshape(1, D))


def _head_rmsnorm(x, g, dh, scale):
    heads = [_rms(x[:, h * dh:(h + 1) * dh]) * g * scale for h in range(x.shape[1] // dh)]
    return heads[0] if len(heads) == 1 else jnp.concatenate(heads, axis=1)


def _matmul_kernel(*refs, n_parts, w_t, norm_tiles, dh):
    a_refs = refs[:n_parts]
    w_ref = refs[n_parts]
    if norm_tiles:
        g_ref, o_ref, wb_ref = refs[n_parts + 1:]
    else:
        o_ref, wb_ref = refs[n_parts + 1:]
    j = pl.program_id(0)

    @pl.when(pl.program_id(1) == 0)
    def _():
        wb_ref[...] = w_ref[...].astype(BF16)

    if n_parts == 1:
        a = a_refs[0][...]
    else:
        a = jnp.concatenate([r[...] for r in a_refs], axis=1)
    if w_t:
        acc = lax.dot_general(a, wb_ref[...], (((1,), (1,)), ((), ())), preferred_element_type=F32)
    else:
        acc = jnp.dot(a, wb_ref[...], preferred_element_type=F32)
    if not norm_tiles:
        o_ref[...] = acc.astype(o_ref.dtype)
        return
    plain = True
    for k, (j0, j1, scale) in enumerate(norm_tiles):
        hit = (j >= j0) & (j < j1)
        plain = plain & jnp.logical_not(hit)

        @pl.when(hit)
        def _(k=k, scale=scale):
            o_ref[...] = _head_rmsnorm(acc, g_ref[k:k + 1, :], dh, scale).astype(o_ref.dtype)

    @pl.when(plain)
    def _():
        o_ref[...] = acc.astype(o_ref.dtype)


def matmul(a_parts, w, layer, col0, n, out_dtype, w_t=False, head_norms=(), tm=MM_TM, tn=MM_TN):
    M = a_parts[0].shape[0]
    K = w.shape[2] if w_t else w.shape[1]
    assert sum(p.shape[1] for p in a_parts) == K
    tm = min(tm, M)
    assert col0 % tn == 0 and n % tn == 0 and M % tm == 0
    cb = col0 // tn
    in_specs = [pl.BlockSpec((tm, p.shape[1]), lambda j, i: (i, 0)) for p in a_parts]
    if w_t:
        in_specs.append(pl.BlockSpec((None, tn, K), lambda j, i: (layer, cb + j, 0)))
    else:
        in_specs.append(pl.BlockSpec((None, K, tn), lambda j, i: (layer, 0, cb + j)))
    args = list(a_parts) + [w]
    norm_tiles, dh = (), 0
    if head_norms:
        dh = head_norms[0][2].shape[0]
        assert all(c0 % tn == 0 and c1 % tn == 0 and g.shape[0] == dh and tn % dh == 0
                   for c0, c1, g, _ in head_norms)
        norm_tiles = tuple((c0 // tn, c1 // tn, s) for c0, c1, _, s in head_norms)
        gains = jnp.stack([g for _, _, g, _ in head_norms])
        in_specs.append(pl.BlockSpec(gains.shape, lambda j, i: (0, 0)))
        args.append(gains)
    return pl.pallas_call(
        functools.partial(_matmul_kernel, n_parts=len(a_parts), w_t=w_t, norm_tiles=norm_tiles,
                          dh=dh),
        grid=(n // tn, M // tm),
        in_specs=in_specs,
        out_specs=pl.BlockSpec((tm, tn), lambda j, i: (i, j)),
        out_shape=jax.ShapeDtypeStruct((M, n), out_dtype),
        scratch_shapes=[pltpu.VMEM((tn, K) if w_t else (K, tn), BF16)],
        compiler_params=_params("arbitrary", "arbitrary"),
        name="matmul",
    )(*args)


W_CHUNK = 256


def _load_weight(w_hbm, layer, wb, stg, sem):
    n_chunks = wb.shape[0] // W_CHUNK

    def copy(c):
        return pltpu.make_async_copy(w_hbm.at[layer, pl.ds(c * W_CHUNK, W_CHUNK)], stg.at[c % 2],
                                     sem.at[c % 2])
    copy(0).start()
    for c in range(n_chunks):
        if c + 1 < n_chunks:
            copy(c + 1).start()
        copy(c).wait()
        wb[c * W_CHUNK:(c + 1) * W_CHUNK, :] = stg[c % 2].astype(BF16)


def _outq_kernel(a1_ref, a2_ref, a3_ref, x_ref, wo_hbm, wq_hbm, gn_ref, gq_ref, xo_ref, q_ref,
                 wo_b, wq_b, stg, sem, *, layer, dh, scale):
    @pl.when(pl.program_id(0) == 0)
    def _():
        _load_weight(wo_hbm, layer, wo_b, stg, sem)
        _load_weight(wq_hbm, layer, wq_b, stg, sem)

    a = jnp.concatenate([a1_ref[...], a2_ref[...], a3_ref[...]], axis=1)
    xn = x_ref[...] + jnp.dot(a, wo_b[...], preferred_element_type=F32)
    xo_ref[...] = xn
    hc = (_rms(xn) * gn_ref[...]).astype(BF16)
    q = jnp.dot(hc, wq_b[...], preferred_element_type=F32)
    q_ref[...] = _head_rmsnorm(q, gq_ref[...], dh, scale).astype(q_ref.dtype)


def out_proj_cross_q(mix, x, w_out, w_cq, layer, g_cross, g_q, tm=512):
    T, D = x.shape
    dh = g_q.shape[0]
    row = lambda w: pl.BlockSpec((tm, w), lambda i: (i, 0))
    hbm = pl.BlockSpec(memory_space=pl.ANY)
    return pl.pallas_call(
        functools.partial(_outq_kernel, layer=layer, dh=dh, scale=dh ** -0.5),
        grid=(T // tm,),
        in_specs=[row(mix[0].shape[1]), row(mix[1].shape[1]), row(mix[2].shape[1]), row(D), hbm, hbm,
                  pl.BlockSpec((1, D), lambda i: (0, 0)), pl.BlockSpec((1, dh), lambda i: (0, 0))],
        out_specs=[row(D), row(D)],
        out_shape=[jax.ShapeDtypeStruct((T, D), F32), jax.ShapeDtypeStruct((T, D), BF16)],
        scratch_shapes=[pltpu.VMEM((D, D), BF16), pltpu.VMEM((D, D), BF16),
                        pltpu.VMEM((2, W_CHUNK, D), F32), pltpu.SemaphoreType.DMA((2,))],
        compiler_params=_params("arbitrary"),
        name="out_proj_cross_q",
    )(*mix, x, w_out, w_cq, g_cross.reshape(1, D), g_q.reshape(1, dh))


def _hgrn_consts(C):
    nlev = int(np.log2(C))
    t = np.arange(C)[:, None]
    u = np.arange(C)[None, :]
    tri = (u <= t).astype(np.float32)
    mall2 = np.concatenate([tri, tri], axis=1)
    tt = np.arange(C)[:, None]
    ss = np.arange(C)[None, :]
    lev = np.full((C, C), nlev + 1, np.int32)
    x = tt ^ ss
    hb = np.zeros_like(x)
    for j in range(nlev):
        hb = np.where((x >> j) & 1, j, hb)
    lev = np.where(tt > ss, hb, lev)
    lev = np.where(tt == ss, nlev, lev)
    return jnp.asarray(mall2, BF16), jnp.asarray(lev, jnp.int32)


def _level_sums(lf, b, row, j, C):
    m = 1 << j
    second = (row & m) != 0
    if j == 0:
        return jnp.where(second, lf, 0.0)
    if j == 1:
        prev = pltpu.roll(lf, 1, axis=0)
        nxt = pltpu.roll(lf, C - 1, axis=0)
        first = jnp.where((row & 1) == 0, nxt, 0.0)
        return jnp.where(second, jnp.where((row & 1) != 0, lf + prev, lf), first)
    g = 2 * m
    d = b.shape[1]
    bsel = jnp.broadcast_to(b.reshape(C // g, g, d)[:, m - 1:m, :], (C // g, g, d)).reshape(C, d)
    return jnp.where(second, b - bsel, bsel - b)


def _hgrn_kernel(x_ref, mall_ref, lev_ref, loglb_ref, l1mlb_ref, ng_ref, o_ref, st_ref, *, C, nlev):
    @pl.when(pl.program_id(1) == 0)
    def _():
        st_ref[...] = jnp.zeros_like(st_ref)

    W = HG_WIDTH
    d = HG_DIM
    lev = lev_ref[...]
    mall = mall_ref[...]
    row = lax.broadcasted_iota(jnp.int32, (C, d), 0)
    nt = (((1,), (1,)), ((), ()))
    for h in range(HG_HEADS):
        sl = slice(h * d, (h + 1) * d)
        q = x_ref[0, :, h * d:(h + 1) * d]
        z = x_ref[0, :, W + h * d:W + (h + 1) * d]
        v = x_ref[0, :, 2 * W + h * d:2 * W + (h + 1) * d]
        g = x_ref[0, :, 3 * W + h * d:3 * W + (h + 1) * d]
        a = loglb_ref[:, sl]
        bt = l1mlb_ref[:, sl] + _log_sigmoid(z)
        lf = jnp.maximum(a, bt) + jnp.log1p(jnp.exp(-jnp.abs(a - bt)))
        kk = 1.0 - jnp.exp(lf)
        hi = lf.astype(BF16)
        lo = (lf - hi.astype(F32)).astype(BF16)
        b = jnp.dot(mall, jnp.concatenate([hi, lo], axis=0), preferred_element_type=F32)
        vb = v.astype(BF16)
        st = st_ref[h]
        o = lax.dot_general((q * jnp.exp(b)).astype(BF16), st.astype(BF16), nt,
                            preferred_element_type=F32)
        att = jnp.where(lev == nlev, jnp.sum(q * kk, axis=-1, keepdims=True), 0.0)
        for j in range(nlev):
            second = (row & (1 << j)) != 0
            e = jnp.exp(_level_sums(lf, b, row, j, C))
            xq = jnp.where(second, q * e, 0.0).astype(BF16)
            xk = jnp.where(second, 0.0, kk * e).astype(BF16)
            am = lax.dot_general(xq, xk, nt, preferred_element_type=F32)
            att = jnp.where(lev == j, am, att)
        o = o + jnp.dot(att.astype(BF16), vb, preferred_element_type=F32)
        bl = b[C - 1:C, :]
        kh = (kk * jnp.exp(bl - b)).astype(BF16)
        st_ref[h] = st * jnp.exp(bl) + lax.dot_general(
            vb, kh, (((0,), (0,)), ((), ())), preferred_element_type=F32)
        y = o * lax.rsqrt(jnp.mean(o * o, axis=-1, keepdims=True) + EPS) * ng_ref[...]
        o_ref[0, :, sl] = (y * (g * _sigmoid(g))).astype(o_ref.dtype)


def hgrn2(proj_a, loglb, l1mlb, norm_g):
    B, S, _ = proj_a.shape
    C = HG_CHUNK
    nlev = int(np.log2(C))
    mall, lev = _hgrn_consts(C)
    return pl.pallas_call(
        functools.partial(_hgrn_kernel, C=C, nlev=nlev),
        grid=(B, S // C),
        in_specs=[pl.BlockSpec((1, C, 4 * HG_WIDTH), lambda b, c: (b, c, 0)),
                  pl.BlockSpec(mall.shape, lambda b, c: (0, 0)),
                  pl.BlockSpec((C, C), lambda b, c: (0, 0)),
                  pl.BlockSpec((1, HG_WIDTH), lambda b, c: (0, 0)),
                  pl.BlockSpec((1, HG_WIDTH), lambda b, c: (0, 0)),
                  pl.BlockSpec((1, HG_DIM), lambda b, c: (0, 0))],
        out_specs=pl.BlockSpec((1, C, HG_WIDTH), lambda b, c: (b, c, 0)),
        out_shape=jax.ShapeDtypeStruct((B, S, HG_WIDTH), BF16),
        scratch_shapes=[pltpu.VMEM((HG_HEADS, HG_DIM, HG_DIM), F32)],
        compiler_params=_params("parallel", "arbitrary"),
        name="hgrn2",
    )(proj_a, mall, lev, loglb.reshape(1, HG_WIDTH), l1mlb.reshape(1, HG_WIDTH),
      norm_g.reshape(1, HG_DIM))


def _conv_kernel(b_ref, c_ref, h_ref, w_ref, o_ref):
    u = c_ref[0].astype(F32) * h_ref[0].astype(F32)
    row = lax.broadcasted_iota(jnp.int32, u.shape, 0)
    u1 = jnp.where(row >= 1, pltpu.roll(u, 1, axis=0), 0.0)
    u2 = jnp.where(row >= 2, pltpu.roll(u, 2, axis=0), 0.0)
    y = w_ref[0:1, :] * u2 + w_ref[1:2, :] * u1 + w_ref[2:3, :] * u
    o_ref[0] = (b_ref[0].astype(F32) * y).astype(o_ref.dtype)


def short_conv(proj_b, w):
    B, S, _ = proj_b.shape
    cw = CONV_WIDTH
    spec = lambda k: pl.BlockSpec((1, S, cw), lambda b: (b, 0, k))
    return pl.pallas_call(
        _conv_kernel,
        grid=(B,),
        in_specs=[spec(0), spec(1), spec(2), pl.BlockSpec((3, cw), lambda b: (0, 0))],
        out_specs=pl.BlockSpec((1, S, cw), lambda b: (b, 0, 0)),
        out_shape=jax.ShapeDtypeStruct((B, S, cw), BF16),
        compiler_params=_params("parallel"),
        name="short_conv",
    )(proj_b, proj_b, proj_b, w)


def _fox_gate_kernel(hn_ref, wf_ref, bias_ref, c_ref):
    gl = lax.dot_general(wf_ref[...].astype(BF16), hn_ref[0], (((1,), (1,)), ((), ())),
                         preferred_element_type=F32)
    c = _log_sigmoid(gl + bias_ref[...])
    S = c.shape[1]
    lane = lax.broadcasted_iota(jnp.int32, c.shape, 1)
    sh = 1
    while sh < S:
        c = c + jnp.where(lane >= sh, pltpu.roll(c, sh, axis=1), 0.0)
        sh *= 2
    c_ref[0] = c


def fox_gate(hn, w_t, layer, row0, bias):
    B, S, D = hn.shape
    H = bias.shape[0]
    assert row0 % H == 0
    return pl.pallas_call(
        _fox_gate_kernel,
        grid=(B,),
        in_specs=[pl.BlockSpec((1, S, D), lambda b: (b, 0, 0)),
                  pl.BlockSpec((None, H, D), lambda b: (layer, row0 // H, 0)),
                  pl.BlockSpec((H, 1), lambda b: (0, 0))],
        out_specs=pl.BlockSpec((1, H, S), lambda b: (b, 0, 0)),
        out_shape=jax.ShapeDtypeStruct((B, H, S), F32),
        compiler_params=_params("parallel"),
        name="fox_gate",
    )(hn, w_t, bias.reshape(H, 1))


def _fox_kernel(q_ref, k_ref, v_ref, c_ref, o_ref, *, tq, tk, nh):
    qi = pl.program_id(2)
    nt = (((1,), (1,)), ((), ()))
    ones = jnp.ones((tk, FOX_DIM), BF16)
    heads = [slice(h * FOX_DIM, (h + 1) * FOX_DIM) for h in range(nh)]
    qs = [q_ref[0, :, sl] for sl in heads]

    def step(ki, carry, diag):
        out = []
        for h, sl in enumerate(heads):
            m, acc = carry[h]
            k = k_ref[0, pl.ds(ki * tk, tk), sl]
            v1 = jnp.concatenate([v_ref[0, pl.ds(ki * tk, tk), sl], ones], axis=1)
            ck = c_ref[0, h, pl.ds(ki, 1), :]
            s = lax.dot_general(qs[h], k, nt, preferred_element_type=F32) - ck
            if diag is not None:
                r = lax.broadcasted_iota(jnp.int32, s.shape, 0)
                c = lax.broadcasted_iota(jnp.int32, s.shape, 1)
                s = jnp.where(c + diag * tk <= r, s, -jnp.inf)
            m_new = jnp.maximum(m, jnp.max(s, axis=-1, keepdims=True))
            alpha = jnp.exp(m - m_new)
            p = jnp.exp((s - m_new).astype(BF16))
            out.append((m_new, alpha * acc + jnp.dot(p, v1, preferred_element_type=F32)))
        return tuple(out)

    init = tuple((jnp.full((tq, 1), -jnp.inf, F32), jnp.zeros((tq, 2 * FOX_DIM), F32))
                 for _ in heads)
    n_full = qi * (tq // tk)
    carry = lax.fori_loop(0, n_full, lambda ki, cr: step(ki, cr, None), init)
    for d in range(tq // tk):
        carry = step(n_full + d, carry, d)
    for h, sl in enumerate(heads):
        acc = carry[h][1]
        o_ref[0, :, sl] = (acc[:, :FOX_DIM] / acc[:, FOX_DIM:]).astype(o_ref.dtype)


def fox_attention(proj_b, q_col0, k_col0, v_col0, c):
    B, S, _ = proj_b.shape
    tq, tk, nh = FOX_TQ, FOX_TK, FOX_HEADS_PER_STEP
    H = FOX_HEADS
    w = nh * FOX_DIM
    assert tq % tk == 0 and H % nh == 0 and all(c0 % w == 0 for c0 in (q_col0, k_col0, v_col0))
    qb, kb, vb = q_col0 // w, k_col0 // w, v_col0 // w
    c4 = c.reshape(B, H, S // tk, tk)
    return pl.pallas_call(
        functools.partial(_fox_kernel, tq=tq, tk=tk, nh=nh),
        grid=(B, H // nh, S // tq),
        in_specs=[pl.BlockSpec((1, tq, w), lambda b, h, i: (b, i, qb + h)),
                  pl.BlockSpec((1, S, w), lambda b, h, i: (b, 0, kb + h)),
                  pl.BlockSpec((1, S, w), lambda b, h, i: (b, 0, vb + h)),
                  pl.BlockSpec((1, nh, S // tk, tk), lambda b, h, i: (b, h, 0, 0))],
        out_specs=pl.BlockSpec((1, tq, w), lambda b, h, i: (b, i, h)),
        out_shape=jax.ShapeDtypeStruct((B, S, FOX_WIDTH), BF16),
        compiler_params=_params("parallel", "parallel", "arbitrary"),
        name="fox_attention",
    )(proj_b, proj_b, proj_b, c4)


def _cross_kernel(q_ref, k_ref, v_ref, x_ref, wo_hbm, gn_ref, wr_ref, br_ref,
                  xo_ref, hp_ref, lg_ref, wo_b, stg, sem, *, layer, dh):
    @pl.when((pl.program_id(0) == 0) & (pl.program_id(1) == 0))
    def _():
        _load_weight(wo_hbm, layer, wo_b, stg, sem)

    nt = (((1,), (1,)), ((), ()))
    heads = []
    for h in range(CROSS_HEADS):
        sl = slice(h * dh, (h + 1) * dh)
        s = lax.dot_general(q_ref[0, :, sl], k_ref[0, :, sl], nt, preferred_element_type=F32)
        m = jnp.max(s, axis=-1, keepdims=True)
        p = jnp.exp(s - m)
        l = jnp.sum(p, axis=-1, keepdims=True)
        o = jnp.dot(p.astype(BF16), v_ref[0, :, sl], preferred_element_type=F32)
        heads.append((o / l).astype(BF16))
    xn = x_ref[0] + jnp.dot(jnp.concatenate(heads, axis=1), wo_b[...], preferred_element_type=F32)
    xo_ref[0] = xn
    y = _rms(xn) * gn_ref[...]
    _store_token_tiles(hp_ref, y, y.shape[0])
    yh = y.astype(BF16)
    yl = (y - yh.astype(F32)).astype(BF16)
    wr = wr_ref[...]
    wh = wr.astype(BF16)
    wl = (wr - wh.astype(F32)).astype(BF16)
    lg_ref[...] = (jnp.dot(yh, wh, preferred_element_type=F32)
                   + jnp.dot(yl, wh, preferred_element_type=F32)
                   + jnp.dot(yh, wl, preferred_element_type=F32)) + br_ref[...]


def cross_attention_ffn_in(qn, kn, v, x, w_co, layer, g_ffn, wr, br):
    B, S, D = qn.shape
    M = kn.shape[1]
    ts = CROSS_TS
    ns = S // ts
    assert D == TOKEN_ROWS * TOKEN_LANES
    tile = pl.BlockSpec((1, ts, D), lambda b, i: (b, i, 0))
    memb = pl.BlockSpec((1, M, D), lambda b, i: (b, 0, 0))
    const = lambda shape: pl.BlockSpec(shape, lambda b, i: (0, 0))
    return pl.pallas_call(
        functools.partial(_cross_kernel, layer=layer, dh=D // CROSS_HEADS),
        grid=(B, ns),
        in_specs=[tile, memb, memb, tile, pl.BlockSpec(memory_space=pl.ANY),
                  const((1, D)), const((D, LANES)), const((1, LANES))],
        out_specs=[tile,
                   pl.BlockSpec((ts * TOKEN_ROWS, TOKEN_LANES), lambda b, i: (b * ns + i, 0)),
                   pl.BlockSpec((ts, LANES), lambda b, i: (b * ns + i, 0))],
        out_shape=[jax.ShapeDtypeStruct((B, S, D), F32),
                   jax.ShapeDtypeStruct((B * S * TOKEN_ROWS, TOKEN_LANES), F32),
                   jax.ShapeDtypeStruct((B * S, LANES), F32)],
        scratch_shapes=[pltpu.VMEM((D, D), BF16), pltpu.VMEM((2, W_CHUNK, D), F32),
                        pltpu.SemaphoreType.DMA((2,))],
        compiler_params=_params("arbitrary", "arbitrary"),
        name="cross_attention_ffn_in",
    )(qn, kn, v, x, w_co, g_ffn.reshape(1, D), wr, br.reshape(1, LANES))


def _route_kernel(lg_ref, tri_ref, meta_ref, cnt_ref, carry_ref):
    @pl.when(pl.program_id(0) == 0)
    def _():
        carry_ref[...] = jnp.zeros_like(carry_ref)

    x = lg_ref[...]
    lane = lax.broadcasted_iota(jnp.int32, x.shape, 1)
    ninf = -jnp.inf
    is_g = lane < MOE_GROUPS
    gl = jnp.where(is_g, x, ninf)
    gmax = jnp.max(gl, axis=-1, keepdims=True)
    gidx = jnp.min(jnp.where(gl == gmax, lane, LANES), axis=-1, keepdims=True)
    gw = 1.0 / jnp.sum(jnp.where(is_g, jnp.exp(x - gmax), 0.0), axis=-1, keepdims=True)
    e_lane = lane - ROUTER_LANE0
    in_grp = (e_lane >= 0) & (e_lane < N_EXPERTS) & ((e_lane >> 3) == gidx)
    el = jnp.where(in_grp, x, ninf)
    m1 = jnp.max(el, axis=-1, keepdims=True)
    i1 = jnp.min(jnp.where(el == m1, lane, LANES), axis=-1, keepdims=True)
    el2 = jnp.where(lane == i1, ninf, el)
    m2 = jnp.max(el2, axis=-1, keepdims=True)
    i2 = jnp.min(jnp.where(el2 == m2, lane, LANES), axis=-1, keepdims=True)
    r = jnp.exp(m2 - m1)
    p1 = 1.0 / (1.0 + r)
    p2 = r * p1
    oh = ((lane == i1) | (lane == i2)).astype(F32)
    prefix = jnp.dot(tri_ref[...], oh.astype(BF16), preferred_element_type=F32) + carry_ref[...]
    rank1 = jnp.sum(jnp.where(lane == i1, prefix, 0.0), axis=-1, keepdims=True)
    rank2 = jnp.sum(jnp.where(lane == i2, prefix, 0.0), axis=-1, keepdims=True)
    carry_ref[...] += jnp.sum(oh, axis=0, keepdims=True)
    cnt_ref[...] = carry_ref[...]
    cols = [i1.astype(F32), i2.astype(F32), gw * p1, gw * p2, rank1, rank2]
    meta = jnp.zeros(x.shape, F32)
    for k, col in enumerate(cols):
        meta = jnp.where(lane == k, col, meta)
    meta_ref[...] = meta


def route(logits):
    T = logits.shape[0]
    tt = ROUTE_ROWS
    tri = jnp.asarray(np.tril(np.ones((tt, tt), np.float32), -1), BF16)
    return pl.pallas_call(
        _route_kernel,
        grid=(T // tt,),
        in_specs=[pl.BlockSpec((tt, LANES), lambda i: (i, 0)),
                  pl.BlockSpec((tt, tt), lambda i: (0, 0))],
        out_specs=[pl.BlockSpec((tt, LANES), lambda i: (i, 0)),
                   pl.BlockSpec((1, LANES), lambda i: (0, 0))],
        out_shape=[jax.ShapeDtypeStruct((T, LANES), F32),
                   jax.ShapeDtypeStruct((1, LANES), F32)],
        scratch_shapes=[pltpu.VMEM((1, LANES), F32)],
        compiler_params=_params("arbitrary"),
        name="moe_route",
    )(logits, tri)


def _plan_kernel(meta_ref, cnt_ref, d_ref):
    cnt = jnp.broadcast_to(cnt_ref[...], (SUBLANES, LANES)).astype(I32)
    pad = (((cnt + (MOE_ROWS - 1)) >> MOE_ROWS_LOG2) << MOE_ROWS_LOG2).astype(F32)
    lane8 = lax.broadcasted_iota(I32, pad.shape, 1)
    incl = pad
    sh = 1
    while sh < LANES:
        incl = incl + jnp.where(lane8 >= sh, pltpu.roll(incl, sh, axis=1), 0.0)
        sh *= 2
    off = (incl - pad)[0:1, :]
    m = meta_ref[...]
    lane = lax.broadcasted_iota(I32, m.shape, 1)
    i1 = m[:, 0:1].astype(I32)
    i2 = m[:, 1:2].astype(I32)
    d1 = jnp.sum(jnp.where(lane == i1, off, 0.0), axis=-1, keepdims=True) + m[:, 4:5]
    d2 = jnp.sum(jnp.where(lane == i2, off, 0.0), axis=-1, keepdims=True) + m[:, 5:6]
    dm = jnp.where(lane == 0, d1, jnp.where(lane == 1, d2, 0.0))
    d_ref[...] = jnp.transpose(dm)[0:2, :].astype(I32)


def plan(meta, cnt):
    T = meta.shape[0]
    tt = ROUTE_ROWS
    return pl.pallas_call(
        _plan_kernel,
        grid=(T // tt,),
        in_specs=[pl.BlockSpec((tt, LANES), lambda i: (i, 0)),
                  pl.BlockSpec((1, LANES), lambda i: (0, 0))],
        out_specs=pl.BlockSpec((2, tt), lambda i: (0, i)),
        out_shape=jax.ShapeDtypeStruct((2, T), I32),
        compiler_params=_params("parallel"),
        name="moe_plan",
    )(meta, cnt)


def _invert_kernel(d1_ref, d2_ref, src_ref, *, T, R):
    def init(r, _):
        src_ref[r] = 0
        return 0
    lax.fori_loop(0, R, init, 0, unroll=32)

    def body(t, _):
        src_ref[d1_ref[t]] = t
        src_ref[d2_ref[t]] = t
        return 0
    lax.fori_loop(0, T, body, 0, unroll=8)


def invert(d1, d2, R):
    T = d1.shape[0]
    grid_spec = pltpu.PrefetchScalarGridSpec(
        num_scalar_prefetch=2, grid=(1,), in_specs=[],
        out_specs=pl.BlockSpec(memory_space=pltpu.SMEM))
    return pl.pallas_call(
        functools.partial(_invert_kernel, T=T, R=R),
        grid_spec=grid_spec,
        out_shape=jax.ShapeDtypeStruct((R,), I32),
        compiler_params=_params("arbitrary"),
        name="moe_invert",
    )(d1, d2)


GATHER_CHUNK = 8


def _tile_copy(src_hbm, row, dst, slot, r, sem):
    return pltpu.make_async_copy(src_hbm.at[pl.ds(row * TOKEN_ROWS, TOKEN_ROWS)],
                                 dst.at[slot, pl.ds(r * TOKEN_ROWS, TOKEN_ROWS)], sem.at[slot])


def _expert_kernel(te_ref, first_ref, nxt_ref, wsl_ref, nch_ref, src_ref,
                   hp_hbm, w1_hbm, w3_hbm, w2_hbm, o_ref,
                   xbuf, r1, r3, r2, w1b, w3b, w2b, gsem, wsem, *, tr, base):
    i = pl.program_id(0)
    n = pl.num_programs(0)
    slot = i % 2

    def weight_copies(e, ws):
        return [pltpu.make_async_copy(w1_hbm.at[base + e], r1.at[ws], wsem.at[ws, 0]),
                pltpu.make_async_copy(w3_hbm.at[base + e], r3.at[ws], wsem.at[ws, 1]),
                pltpu.make_async_copy(w2_hbm.at[base + e], r2.at[ws], wsem.at[ws, 2])]

    def issue_rows(tile, slot, c0, c1):
        def body(c, _):
            for u in range(GATHER_CHUNK):
                r = c * GATHER_CHUNK + u
                _tile_copy(hp_hbm, src_ref[tile * tr + r], xbuf, slot, r, gsem).start()
            return 0
        lax.fori_loop(c0, c1, body, 0)

    def wait_rows(tile, slot):
        rows = GATHER_CHUNK * TOKEN_ROWS
        def body(c, _):
            pltpu.make_async_copy(hp_hbm.at[pl.ds(0, rows)], xbuf.at[slot, pl.ds(0, rows)],
                                  gsem.at[slot]).wait()
            return 0
        lax.fori_loop(0, nch_ref[tile], body, 0)

    @pl.when(i == 0)
    def _():
        xbuf[...] = jnp.zeros_like(xbuf)
        for cp in weight_copies(te_ref[0], 0):
            cp.start(priority=1)
        issue_rows(0, 0, 0, nch_ref[0])

    wait_rows(i, slot)

    nxt_tile = jnp.minimum(i + 1, n - 1)
    issue_rows(nxt_tile, 1 - slot, 0, jnp.where(i + 1 < n, nch_ref[nxt_tile], 0))

    @pl.when(first_ref[i] == 1)
    def _():
        ws = wsl_ref[i]
        for cp in weight_copies(te_ref[i], ws):
            cp.wait()

        @pl.when(nxt_ref[i] >= 0)
        def _():
            for cp in weight_copies(nxt_ref[i], 1 - ws):
                cp.start(priority=1)

        w1b[...] = r1[ws].astype(BF16)
        w3b[...] = r3[ws].astype(BF16)
        w2b[...] = r2[ws].astype(BF16)

    @pl.when(nch_ref[i] > 0)
    def _():
        x = jnp.concatenate([p.astype(BF16) for p in _load_token_tiles(xbuf.at[slot], 0, tr)],
                            axis=1)
        h1 = jnp.dot(x, w1b[...], preferred_element_type=F32)
        h3 = jnp.dot(x, w3b[...], preferred_element_type=F32)
        hm = (h1 * _sigmoid(h1) * h3).astype(BF16)
        _store_token_tiles(o_ref, jnp.dot(hm, w2b[...], preferred_element_type=F32), tr)

    @pl.when(nch_ref[i] == 0)
    def _():
        o_ref[...] = jnp.zeros_like(o_ref)


def moe_experts(hp, w1, w3, w2, layer, tables, src):
    D, F = w1.shape[-2:]
    tr = MOE_ROWS
    te, first, nxt, wsl, act = tables
    NT = te.shape[0]
    w1f = w1.reshape(-1, D, F)
    w3f = w3.reshape(-1, D, F)
    w2f = w2.reshape(-1, F, D)
    hbm = pl.BlockSpec(memory_space=pl.ANY)
    grid_spec = pltpu.PrefetchScalarGridSpec(
        num_scalar_prefetch=6,
        grid=(NT,),
        in_specs=[hbm, hbm, hbm, hbm],
        out_specs=pl.BlockSpec((tr * TOKEN_ROWS, TOKEN_LANES), lambda i, *_: (i, 0)),
        scratch_shapes=[pltpu.VMEM((2, tr * TOKEN_ROWS, TOKEN_LANES), F32),
                        pltpu.VMEM((2, D, F), F32),
                        pltpu.VMEM((2, D, F), F32),
                        pltpu.VMEM((2, F, D), F32),
                        pltpu.VMEM((D, F), BF16),
                        pltpu.VMEM((D, F), BF16),
                        pltpu.VMEM((F, D), BF16),
                        pltpu.SemaphoreType.DMA((2,)),
                        pltpu.SemaphoreType.DMA((2, 3))],
    )
    return pl.pallas_call(
        functools.partial(_expert_kernel, tr=tr, base=layer * N_EXPERTS),
        grid_spec=grid_spec,
        out_shape=jax.ShapeDtypeStruct((NT * tr * TOKEN_ROWS, TOKEN_LANES), F32),
        compiler_params=_params("arbitrary"),
        name="moe_experts",
    )(te, first, nxt, wsl, act, src, hp, w1f, w3f, w2f)


def _combine_kernel(d1_ref, d2_ref, x_ref, meta_ref, ys_hbm, *rest, tt, with_norm):
    if with_norm:
        g_ref, o_ref, hn_ref, buf, sem = rest
    else:
        o_ref, buf, sem = rest
    i = pl.program_id(0)
    n = pl.num_programs(0)
    slot = i % 2

    def issue(tile, slot):
        def body(c, _):
            for u in range(GATHER_CHUNK):
                r = c * GATHER_CHUNK + u
                t = tile * tt + r
                _tile_copy(ys_hbm, d1_ref[t], buf, slot, r, sem).start(priority=0)
                _tile_copy(ys_hbm, d2_ref[t], buf, slot, tt + r, sem).start(priority=1)
            return 0
        lax.fori_loop(0, tt // GATHER_CHUNK, body, 0)

    @pl.when(i == 0)
    def _():
        issue(0, 0)

    pltpu.make_async_copy(ys_hbm.at[pl.ds(0, 2 * tt * TOKEN_ROWS)], buf.at[slot], sem.at[slot]).wait()

    @pl.when(i + 1 < n)
    def _():
        issue(i + 1, 1 - slot)

    w1 = meta_ref[:, 2:3]
    w2 = meta_ref[:, 3:4]
    pa = _load_token_tiles(buf.at[slot], 0, tt)
    pb = _load_token_tiles(buf.at[slot], tt * TOKEN_ROWS, tt)
    cols = [x_ref[:, j * TOKEN_LANES:(j + 1) * TOKEN_LANES] + w1 * pa[j] + w2 * pb[j]
            for j in range(TOKEN_ROWS)]
    xn = jnp.concatenate(cols, axis=1)
    o_ref[...] = xn
    if with_norm:
        hn_ref[...] = (_rms(xn) * g_ref[...]).astype(hn_ref.dtype)


def moe_combine(x, meta, ys, d1, d2, g_next=None):
    T, D = x.shape
    tt = COMB_ROWS
    with_norm = g_next is not None
    row = pl.BlockSpec((tt, D), lambda i, d1, d2: (i, 0))
    in_specs = [row, pl.BlockSpec((tt, LANES), lambda i, d1, d2: (i, 0)),
                pl.BlockSpec(memory_space=pl.ANY)]
    args = [d1, d2, x, meta, ys]
    out_specs, out_shape = row, jax.ShapeDtypeStruct((T, D), F32)
    if with_norm:
        in_specs.append(pl.BlockSpec((1, D), lambda i, d1, d2: (0, 0)))
        args.append(g_next.reshape(1, D))
        out_specs, out_shape = [row, row], [out_shape, jax.ShapeDtypeStruct((T, D), BF16)]
    grid_spec = pltpu.PrefetchScalarGridSpec(
        num_scalar_prefetch=2,
        grid=(T // tt,),
        in_specs=in_specs,
        out_specs=out_specs,
        scratch_shapes=[pltpu.VMEM((2, 2 * tt * TOKEN_ROWS, TOKEN_LANES), F32),
                        pltpu.SemaphoreType.DMA((2,))],
    )
    return pl.pallas_call(
        functools.partial(_combine_kernel, tt=tt, with_norm=with_norm),
        grid_spec=grid_spec,
        out_shape=out_shape,
        compiler_params=_params("arbitrary"),
        name="moe_combine",
    )(*args)


def _tile_tables(cnt, T):
    E = N_EXPERTS
    counts = cnt[0, ROUTER_LANE0:ROUTER_LANE0 + E].astype(I32)
    tiles_e = (counts + MOE_ROWS - 1) // MOE_ROWS
    tile_end = jnp.cumsum(tiles_e)
    tile_start = tile_end - tiles_e
    nact = tile_end[-1]
    NT = (2 * T) // MOE_ROWS + E
    tid = jnp.arange(NT, dtype=I32)
    te = jnp.sum((jnp.minimum(tid, nact - 1)[:, None] >= tile_end[None, :]).astype(I32), axis=1)
    te = jnp.minimum(te, E - 1)
    active = tid < nact
    first = (active & (tid == tile_start[te])).astype(I32)
    eid = jnp.arange(E, dtype=I32)
    later = (eid[None, :] > eid[:, None]) & (tiles_e[None, :] > 0)
    nxt_e = jnp.min(jnp.where(later, eid[None, :], E), axis=1)
    nxt_e = jnp.where(nxt_e == E, -1, nxt_e)
    ordinal = jnp.cumsum((tiles_e > 0).astype(I32)) - 1
    valid = jnp.clip(counts[te] - (tid - tile_start[te]) * MOE_ROWS, 0, MOE_ROWS)
    nch = jnp.where(active, (valid + GATHER_CHUNK - 1) // GATHER_CHUNK, 0)
    return (te, first, nxt_e[te], ordinal[te] % 2, nch), NT * MOE_ROWS


def _lower_bound_logs(lb_param, layer):
    p = jax.nn.softmax(lb_param.astype(F32), axis=0)
    c = jnp.cumsum(p, axis=0)
    lb = c[layer] - c[0]
    return jnp.log(lb), jnp.log1p(-lb)


def kernel(x, mem, norm_mix, w_in, hg_lower_bounds, hg_norm, conv_w, fox_f_bias, fox_q_norm,
           fox_k_norm, w_out, norm_cross, norm_mem, w_cq, w_ck, w_cv, w_co, cross_q_norm,
           cross_k_norm, norm_ffn, router_group_w, router_group_b, router_expert_w,
           router_expert_b, moe_w1, moe_w3, moe_w2):
    B, S, D = x.shape
    M = mem.shape[1]
    L = w_in.shape[0]
    T = B * S
    HGW = 4 * HG_WIDTH
    CONV3 = 3 * CONV_WIDTH
    FQ0 = CONV3
    FK0 = FQ0 + FOX_WIDTH
    FV0 = FK0 + FOX_WIDTH
    PB = CONV3 + 3 * FOX_WIDTH
    dhc = D // CROSS_HEADS
    x2 = x.reshape(T, D)
    mem2 = mem.reshape(B * M, D)
    w_in_t = jnp.swapaxes(w_in, 1, 2)
    hn = rmsnorm(x2, norm_mix[0])
    for l in range(L):
        proj_a = matmul([hn], w_in_t, l, 0, HGW, F32, w_t=True)
        proj_b = matmul([hn], w_in_t, l, HGW, PB, BF16, w_t=True, head_norms=(
            (FQ0, FK0, fox_q_norm[l], FOX_DIM ** -0.5), (FK0, FV0, fox_k_norm[l], 1.0)))
        loglb, l1mlb = _lower_bound_logs(hg_lower_bounds, l)
        y_hg = hgrn2(proj_a.reshape(B, S, HGW), loglb, l1mlb, hg_norm[l])
        pb3 = proj_b.reshape(B, S, PB)
        y_conv = short_conv(pb3, conv_w[l])
        c = fox_gate(hn.reshape(B, S, D), w_in_t, l, HGW + PB, fox_f_bias[l])
        y_fox = fox_attention(pb3, FQ0, FK0, FV0, c)
        mix = [y_hg.reshape(T, HG_WIDTH), y_conv.reshape(T, CONV_WIDTH), y_fox.reshape(T, FOX_WIDTH)]
        x2, qcn = out_proj_cross_q(mix, x2, w_out, w_cq, l, norm_cross[l], cross_q_norm[l])
        memn = rmsnorm(mem2, norm_mem[l])
        kcn = matmul([memn], w_ck, l, 0, D, BF16, head_norms=((0, D, cross_k_norm[l], 1.0),))
        vc = matmul([memn], w_cv, l, 0, D, BF16)
        wr = jnp.concatenate([router_group_w[l], router_expert_w[l]], axis=1)
        wr = jnp.pad(wr, ((0, 0), (0, LANES - wr.shape[1])))
        br = jnp.concatenate([router_group_b[l], router_expert_b[l]])
        br = jnp.pad(br, (0, LANES - br.shape[0]))
        x3, hp, logits = cross_attention_ffn_in(
            qcn.reshape(B, S, D), kcn.reshape(B, M, D), vc.reshape(B, M, D), x2.reshape(B, S, D),
            w_co, l, norm_ffn[l], wr, br)
        x2 = x3.reshape(T, D)
        meta, cnt = route(logits)
        d = plan(meta, cnt)
        tables, R = _tile_tables(cnt, T)
        src = invert(d[0], d[1], R)
        ys = moe_experts(hp, moe_w1, moe_w3, moe_w2, l, tables, src)
        if l + 1 < L:
            x2, hn = moe_combine(x2, meta, ys, d[0], d[1], norm_mix[l + 1])
        else:
            x2 = moe_combine(x2, meta, ys, d[0], d[1])
    return x2.reshape(B, S, D)
```

```python
import functools

import numpy as np
import jax
import jax.numpy as jnp
from jax import lax
from jax.experimental import pallas as pl
from jax.experimental.pallas import tpu as pltpu

F32 = jnp.float32
BF16 = jnp.bfloat16
I32 = jnp.int32
EPS = 1e-6

HG_HEADS = 4
HG_DIM = 128
HG_WIDTH = HG_HEADS * HG_DIM
CONV_WIDTH = 512
FOX_HEADS = 8
FOX_DIM = 128
FOX_WIDTH = FOX_HEADS * FOX_DIM
CROSS_HEADS = 4
MOE_GROUPS = 4
MOE_EXPERTS = 8
N_EXPERTS = MOE_GROUPS * MOE_EXPERTS
ROUTER_LANE0 = MOE_GROUPS

LANES = 128
SUBLANES = 8
VMEM_LIMIT = 56 * 1024 * 1024

NORM_ROWS = 256
MM_TM = 2048
MM_TN = 512
HG_CHUNK = 128
FOX_TQ = 512
FOX_TK = 512
FOX_HEADS_PER_STEP = 2
CROSS_TS = 512
ROUTE_ROWS = 512
MOE_ROWS_LOG2 = 8
MOE_ROWS = 1 << MOE_ROWS_LOG2
COMB_ROWS = 256


def _params(*sem):
    return pltpu.CompilerParams(dimension_semantics=sem, vmem_limit_bytes=VMEM_LIMIT)


def _sigmoid(x):
    return 1.0 / (1.0 + jnp.exp(-x))


def _log_sigmoid(x):
    return jnp.minimum(x, 0.0) - jnp.log1p(jnp.exp(-jnp.abs(x)))


TOKEN_ROWS = 16
TOKEN_LANES = 128


def _store_token_tiles(ref, y, rows):
    for j in range(TOKEN_ROWS):
        ref[pl.ds(j, rows, stride=TOKEN_ROWS), :] = y[:, j * TOKEN_LANES:(j + 1) * TOKEN_LANES]


def _load_token_tiles(ref, base, rows):
    return [ref[pl.ds(base + j, rows, stride=TOKEN_ROWS), :] for j in range(TOKEN_ROWS)]


def _rms(x):
    return x * lax.rsqrt(jnp.mean(x * x, axis=-1, keepdims=True) + EPS)


def _rmsnorm_kernel(x_ref, g_ref, o_ref):
    o_ref[...] = (_rms(x_ref[...]) * g_ref[...]).astype(o_ref.dtype)


def rmsnorm(x, g, out_dtype=BF16):
    R, D = x.shape
    return pl.pallas_call(
        _rmsnorm_kernel,
        grid=(R // NORM_ROWS,),
        in_specs=[pl.BlockSpec((NORM_ROWS, D), lambda i: (i, 0)),
                  pl.BlockSpec((1, D), lambda i: (0, 0))],
        out_specs=pl.BlockSpec((NORM_ROWS, D), lambda i: (i, 0)),
        out_shape=jax.ShapeDtypeStruct((R, D), out_dtype),
        compiler_params=_params("parallel"),
        name="rmsnorm",
    )(x, g.reshape(1, D))


def _head_rmsnorm(x, g, dh, scale):
    heads = [_rms(x[:, h * dh:(h + 1) * dh]) * g * scale for h in range(x.shape[1] // dh)]
    return heads[0] if len(heads) == 1 else jnp.concatenate(heads, axis=1)


def _matmul_kernel(*refs, n_parts, w_t, norm_tiles, dh):
    a_refs = refs[:n_parts]
    w_ref = refs[n_parts]
    if norm_tiles:
        g_ref, o_ref, wb_ref = refs[n_parts + 1:]
    else:
        o_ref, wb_ref = refs[n_parts + 1:]
    j = pl.program_id(0)

    @pl.when(pl.program_id(1) == 0)
    def _():
        wb_ref[...] = w_ref[...].astype(BF16)

    if n_parts == 1:
        a = a_refs[0][...]
    else:
        a = jnp.concatenate([r[...] for r in a_refs], axis=1)
    if w_t:
        acc = lax.dot_general(a, wb_ref[...], (((1,), (1,)), ((), ())), preferred_element_type=F32)
    else:
        acc = jnp.dot(a, wb_ref[...], preferred_element_type=F32)
    if not norm_tiles:
        o_ref[...] = acc.astype(o_ref.dtype)
        return
    plain = True
    for k, (j0, j1, scale) in enumerate(norm_tiles):
        hit = (j >= j0) & (j < j1)
        plain = plain & jnp.logical_not(hit)

        @pl.when(hit)
        def _(k=k, scale=scale):
            o_ref[...] = _head_rmsnorm(acc, g_ref[k:k + 1, :], dh, scale).astype(o_ref.dtype)

    @pl.when(plain)
    def _():
        o_ref[...] = acc.astype(o_ref.dtype)


def matmul(a_parts, w, layer, col0, n, out_dtype, w_t=False, head_norms=(), tm=MM_TM, tn=MM_TN):
    M = a_parts[0].shape[0]
    K = w.shape[2] if w_t else w.shape[1]
    assert sum(p.shape[1] for p in a_parts) == K
    tm = min(tm, M)
    assert col0 % tn == 0 and n % tn == 0 and M % tm == 0
    cb = col0 // tn
    in_specs = [pl.BlockSpec((tm, p.shape[1]), lambda j, i: (i, 0)) for p in a_parts]
    if w_t:
        in_specs.append(pl.BlockSpec((None, tn, K), lambda j, i: (layer, cb + j, 0)))
    else:
        in_specs.append(pl.BlockSpec((None, K, tn), lambda j, i: (layer, 0, cb + j)))
    args = list(a_parts) + [w]
    norm_tiles, dh = (), 0
    if head_norms:
        dh = head_norms[0][2].shape[0]
        assert all(c0 % tn == 0 and c1 % tn == 0 and g.shape[0] == dh and tn % dh == 0
                   for c0, c1, g, _ in head_norms)
        norm_tiles = tuple((c0 // tn, c1 // tn, s) for c0, c1, _, s in head_norms)
        gains = jnp.stack([g for _, _, g, _ in head_norms])
        in_specs.append(pl.BlockSpec(gains.shape, lambda j, i: (0, 0)))
        args.append(gains)
    return pl.pallas_call(
        functools.partial(_matmul_kernel, n_parts=len(a_parts), w_t=w_t, norm_tiles=norm_tiles,
                          dh=dh),
        grid=(n // tn, M // tm),
        in_specs=in_specs,
        out_specs=pl.BlockSpec((tm, tn), lambda j, i: (i, j)),
        out_shape=jax.ShapeDtypeStruct((M, n), out_dtype),
        scratch_shapes=[pltpu.VMEM((tn, K) if w_t else (K, tn), BF16)],
        compiler_params=_params("arbitrary", "arbitrary"),
        name="matmul",
    )(*args)


W_CHUNK = 256


def _load_weight(w_hbm, layer, wb, stg, sem):
    n_chunks = wb.shape[0] // W_CHUNK

    def copy(c):
        return pltpu.make_async_copy(w_hbm.at[layer, pl.ds(c * W_CHUNK, W_CHUNK)], stg.at[c % 2],
                                     sem.at[c % 2])
    copy(0).start()
    for c in range(n_chunks):
        if c + 1 < n_chunks:
            copy(c + 1).start()
        copy(c).wait()
        wb[c * W_CHUNK:(c + 1) * W_CHUNK, :] = stg[c % 2].astype(BF16)


def _outq_kernel(a1_ref, a2_ref, a3_ref, x_ref, wo_hbm, wq_hbm, gn_ref, gq_ref, xo_ref, q_ref,
                 wo_b, wq_b, stg, sem, *, layer, dh, scale):
    @pl.when(pl.program_id(0) == 0)
    def _():
        _load_weight(wo_hbm, layer, wo_b, stg, sem)
        _load_weight(wq_hbm, layer, wq_b, stg, sem)

    a = jnp.concatenate([a1_ref[...], a2_ref[...], a3_ref[...]], axis=1)
    xn = x_ref[...] + jnp.dot(a, wo_b[...], preferred_element_type=F32)
    xo_ref[...] = xn
    hc = (_rms(xn) * gn_ref[...]).astype(BF16)
    q = jnp.dot(hc, wq_b[...], preferred_element_type=F32)
    q_ref[...] = _head_rmsnorm(q, gq_ref[...], dh, scale).astype(q_ref.dtype)


def out_proj_cross_q(mix, x, w_out, w_cq, layer, g_cross, g_q, tm=512):
    T, D = x.shape
    dh = g_q.shape[0]
    row = lambda w: pl.BlockSpec((tm, w), lambda i: (i, 0))
    hbm = pl.BlockSpec(memory_space=pl.ANY)
    return pl.pallas_call(
        functools.partial(_outq_kernel, layer=layer, dh=dh, scale=dh ** -0.5),
        grid=(T // tm,),
        in_specs=[row(mix[0].shape[1]), row(mix[1].shape[1]), row(mix[2].shape[1]), row(D), hbm, hbm,
                  pl.BlockSpec((1, D), lambda i: (0, 0)), pl.BlockSpec((1, dh), lambda i: (0, 0))],
        out_specs=[row(D), row(D)],
        out_shape=[jax.ShapeDtypeStruct((T, D), F32), jax.ShapeDtypeStruct((T, D), BF16)],
        scratch_shapes=[pltpu.VMEM((D, D), BF16), pltpu.VMEM((D, D), BF16),
                        pltpu.VMEM((2, W_CHUNK, D), F32), pltpu.SemaphoreType.DMA((2,))],
        compiler_params=_params("arbitrary"),
        name="out_proj_cross_q",
    )(*mix, x, w_out, w_cq, g_cross.reshape(1, D), g_q.reshape(1, dh))


def _hgrn_consts(C):
    nlev = int(np.log2(C))
    t = np.arange(C)[:, None]
    u = np.arange(C)[None, :]
    tri = (u <= t).astype(np.float32)
    mall2 = np.concatenate([tri, tri], axis=1)
    tt = np.arange(C)[:, None]
    ss = np.arange(C)[None, :]
    lev = np.full((C, C), nlev + 1, np.int32)
    x = tt ^ ss
    hb = np.zeros_like(x)
    for j in range(nlev):
        hb = np.where((x >> j) & 1, j, hb)
    lev = np.where(tt > ss, hb, lev)
    lev = np.where(tt == ss, nlev, lev)
    return jnp.asarray(mall2, BF16), jnp.asarray(lev, jnp.int32)


def _level_sums(lf, b, row, j, C):
    m = 1 << j
    second = (row & m) != 0
    if j == 0:
        return jnp.where(second, lf, 0.0)
    if j == 1:
        prev = pltpu.roll(lf, 1, axis=0)
        nxt = pltpu.roll(lf, C - 1, axis=0)
        first = jnp.where((row & 1) == 0, nxt, 0.0)
        return jnp.where(second, jnp.where((row & 1) != 0, lf + prev, lf), first)
    g = 2 * m
    d = b.shape[1]
    bsel = jnp.broadcast_to(b.reshape(C // g, g, d)[:, m - 1:m, :], (C // g, g, d)).reshape(C, d)
    return jnp.where(second, b - bsel, bsel - b)


def _hgrn_kernel(x_ref, mall_ref, lev_ref, loglb_ref, l1mlb_ref, ng_ref, o_ref, st_ref, *, C, nlev):
    @pl.when(pl.program_id(1) == 0)
    def _():
        st_ref[...] = jnp.zeros_like(st_ref)

    W = HG_WIDTH
    d = HG_DIM
    lev = lev_ref[...]
    mall = mall_ref[...]
    row = lax.broadcasted_iota(jnp.int32, (C, d), 0)
    nt = (((1,), (1,)), ((), ()))

    def side(xs):
        return jnp.concatenate(xs, axis=1)

    def block_diag(xa, xb):
        return jnp.concatenate([side([xa, jnp.zeros_like(xb)]), side([jnp.zeros_like(xa), xb])],
                               axis=0)

    for h0 in range(0, HG_HEADS, 2):
        q, lf, kk, vb, g = [], [], [], [], []
        for h in (h0, h0 + 1):
            sl = slice(h * d, (h + 1) * d)
            q.append(x_ref[0, :, h * d:(h + 1) * d])
            z = x_ref[0, :, W + h * d:W + (h + 1) * d]
            vb.append(x_ref[0, :, 2 * W + h * d:2 * W + (h + 1) * d].astype(BF16))
            g.append(x_ref[0, :, 3 * W + h * d:3 * W + (h + 1) * d])
            a = loglb_ref[:, sl]
            bt = l1mlb_ref[:, sl] + _log_sigmoid(z)
            lf.append(jnp.maximum(a, bt) + jnp.log1p(jnp.exp(-jnp.abs(a - bt))))
            kk.append(1.0 - jnp.exp(lf[-1]))
        hi = [x.astype(BF16) for x in lf]
        lo = [(x - y.astype(F32)).astype(BF16) for x, y in zip(lf, hi)]
        b2 = jnp.dot(mall, jnp.concatenate([side(hi), side(lo)], axis=0),
                     preferred_element_type=F32)
        b = [b2[:, :d], b2[:, d:]]
        st = [st_ref[h0], st_ref[h0 + 1]]
        o2 = lax.dot_general(side([(q[i] * jnp.exp(b[i])).astype(BF16) for i in (0, 1)]),
                             block_diag(st[0].astype(BF16), st[1].astype(BF16)), nt,
                             preferred_element_type=F32)
        att = [jnp.where(lev == nlev, jnp.sum(q[i] * kk[i], axis=-1, keepdims=True), 0.0)
               for i in (0, 1)]
        for j in range(nlev):
            second = (row & (1 << j)) != 0
            xq, xk = [], []
            for i in (0, 1):
                e = jnp.exp(_level_sums(lf[i], b[i], row, j, C))
                xq.append(jnp.where(second, q[i] * e, 0.0).astype(BF16))
                xk.append(jnp.where(second, 0.0, kk[i] * e).astype(BF16))
            am = lax.dot_general(side(xq), block_diag(xk[0], xk[1]), nt,
                                 preferred_element_type=F32)
            att = [jnp.where(lev == j, am[:, i * C:(i + 1) * C], att[i]) for i in (0, 1)]
        o2 = o2 + jnp.dot(side([x.astype(BF16) for x in att]), block_diag(vb[0], vb[1]),
                          preferred_element_type=F32)
        for i, h in enumerate((h0, h0 + 1)):
            bl = b[i][C - 1:C, :]
            kh = (kk[i] * jnp.exp(bl - b[i])).astype(BF16)
            st_ref[h] = st[i] * jnp.exp(bl) + lax.dot_general(
                vb[i], kh, (((0,), (0,)), ((), ())), preferred_element_type=F32)
            o = o2[:, i * d:(i + 1) * d]
            y = o * lax.rsqrt(jnp.mean(o * o, axis=-1, keepdims=True) + EPS) * ng_ref[...]
            o_ref[0, :, h * d:(h + 1) * d] = (y * (g[i] * _sigmoid(g[i]))).astype(o_ref.dtype)


def hgrn2(proj_a, loglb, l1mlb, norm_g):
    B, S, _ = proj_a.shape
    C = HG_CHUNK
    nlev = int(np.log2(C))
    mall, lev = _hgrn_consts(C)
    return pl.pallas_call(
        functools.partial(_hgrn_kernel, C=C, nlev=nlev),
        grid=(B, S // C),
        in_specs=[pl.BlockSpec((1, C, 4 * HG_WIDTH), lambda b, c: (b, c, 0)),
                  pl.BlockSpec(mall.shape, lambda b, c: (0, 0)),
                  pl.BlockSpec((C, C), lambda b, c: (0, 0)),
                  pl.BlockSpec((1, HG_WIDTH), lambda b, c: (0, 0)),
                  pl.BlockSpec((1, HG_WIDTH), lambda b, c: (0, 0)),
                  pl.BlockSpec((1, HG_DIM), lambda b, c: (0, 0))],
        out_specs=pl.BlockSpec((1, C, HG_WIDTH), lambda b, c: (b, c, 0)),
        out_shape=jax.ShapeDtypeStruct((B, S, HG_WIDTH), BF16),
        scratch_shapes=[pltpu.VMEM((HG_HEADS, HG_DIM, HG_DIM), F32)],
        compiler_params=_params("parallel", "arbitrary"),
        name="hgrn2",
    )(proj_a, mall, lev, loglb.reshape(1, HG_WIDTH), l1mlb.reshape(1, HG_WIDTH),
      norm_g.reshape(1, HG_DIM))


def _conv_kernel(b_ref, c_ref, h_ref, w_ref, o_ref):
    u = c_ref[0].astype(F32) * h_ref[0].astype(F32)
    row = lax.broadcasted_iota(jnp.int32, u.shape, 0)
    u1 = jnp.where(row >= 1, pltpu.roll(u, 1, axis=0), 0.0)
    u2 = jnp.where(row >= 2, pltpu.roll(u, 2, axis=0), 0.0)
    y = w_ref[0:1, :] * u2 + w_ref[1:2, :] * u1 + w_ref[2:3, :] * u
    o_ref[0] = (b_ref[0].astype(F32) * y).astype(o_ref.dtype)


def short_conv(proj_b, w):
    B, S, _ = proj_b.shape
    cw = CONV_WIDTH
    spec = lambda k: pl.BlockSpec((1, S, cw), lambda b: (b, 0, k))
    return pl.pallas_call(
        _conv_kernel,
        grid=(B,),
        in_specs=[spec(0), spec(1), spec(2), pl.BlockSpec((3, cw), lambda b: (0, 0))],
        out_specs=pl.BlockSpec((1, S, cw), lambda b: (b, 0, 0)),
        out_shape=jax.ShapeDtypeStruct((B, S, cw), BF16),
        compiler_params=_params("parallel"),
        name="short_conv",
    )(proj_b, proj_b, proj_b, w)


def _fox_gate_kernel(hn_ref, wf_ref, bias_ref, c_ref):
    gl = lax.dot_general(wf_ref[...].astype(BF16), hn_ref[0], (((1,), (1,)), ((), ())),
                         preferred_element_type=F32)
    c = _log_sigmoid(gl + bias_ref[...])
    S = c.shape[1]
    lane = lax.broadcasted_iota(jnp.int32, c.shape, 1)
    sh = 1
    while sh < S:
        c = c + jnp.where(lane >= sh, pltpu.roll(c, sh, axis=1), 0.0)
        sh *= 2
    c_ref[0] = c


def fox_gate(hn, w_t, layer, row0, bias):
    B, S, D = hn.shape
    H = bias.shape[0]
    assert row0 % H == 0
    return pl.pallas_call(
        _fox_gate_kernel,
        grid=(B,),
        in_specs=[pl.BlockSpec((1, S, D), lambda b: (b, 0, 0)),
                  pl.BlockSpec((None, H, D), lambda b: (layer, row0 // H, 0)),
                  pl.BlockSpec((H, 1), lambda b: (0, 0))],
        out_specs=pl.BlockSpec((1, H, S), lambda b: (b, 0, 0)),
        out_shape=jax.ShapeDtypeStruct((B, H, S), F32),
        compiler_params=_params("parallel"),
        name="fox_gate",
    )(hn, w_t, bias.reshape(H, 1))


def _fox_kernel(q_ref, k_ref, v_ref, c_ref, o_ref, *, tq, tk, nh):
    qi = pl.program_id(2)
    nt = (((1,), (1,)), ((), ()))
    ones = jnp.ones((tk, FOX_DIM), BF16)
    heads = [slice(h * FOX_DIM, (h + 1) * FOX_DIM) for h in range(nh)]
    qs = [q_ref[0, :, sl] for sl in heads]

    def step(ki, carry, diag):
        out = []
        for h, sl in enumerate(heads):
            m, acc = carry[h]
            k = k_ref[0, pl.ds(ki * tk, tk), sl]
            v1 = jnp.concatenate([v_ref[0, pl.ds(ki * tk, tk), sl], ones], axis=1)
            ck = c_ref[0, h, pl.ds(ki, 1), :]
            s = lax.dot_general(qs[h], k, nt, preferred_element_type=F32) - ck
            if diag is not None:
                r = lax.broadcasted_iota(jnp.int32, s.shape, 0)
                c = lax.broadcasted_iota(jnp.int32, s.shape, 1)
                s = jnp.where(c + diag * tk <= r, s, -jnp.inf)
            m_new = jnp.maximum(m, jnp.max(s, axis=-1, keepdims=True))
            alpha = jnp.exp(m - m_new)
            p = jnp.exp((s - m_new).astype(BF16))
            out.append((m_new, alpha * acc + jnp.dot(p, v1, preferred_element_type=F32)))
        return tuple(out)

    init = tuple((jnp.full((tq, 1), -jnp.inf, F32), jnp.zeros((tq, 2 * FOX_DIM), F32))
                 for _ in heads)
    n_full = qi * (tq // tk)
    carry = lax.fori_loop(0, n_full, lambda ki, cr: step(ki, cr, None), init)
    for d in range(tq // tk):
        carry = step(n_full + d, carry, d)
    for h, sl in enumerate(heads):
        acc = carry[h][1]
        o_ref[0, :, sl] = (acc[:, :FOX_DIM] / acc[:, FOX_DIM:]).astype(o_ref.dtype)


def fox_attention(proj_b, q_col0, k_col0, v_col0, c):
    B, S, _ = proj_b.shape
    tq, tk, nh = FOX_TQ, FOX_TK, FOX_HEADS_PER_STEP
    H = FOX_HEADS
    w = nh * FOX_DIM
    assert tq % tk == 0 and H % nh == 0 and all(c0 % w == 0 for c0 in (q_col0, k_col0, v_col0))
    qb, kb, vb = q_col0 // w, k_col0 // w, v_col0 // w
    c4 = c.reshape(B, H, S // tk, tk)
    return pl.pallas_call(
        functools.partial(_fox_kernel, tq=tq, tk=tk, nh=nh),
        grid=(B, H // nh, S // tq),
        in_specs=[pl.BlockSpec((1, tq, w), lambda b, h, i: (b, i, qb + h)),
                  pl.BlockSpec((1, S, w), lambda b, h, i: (b, 0, kb + h)),
                  pl.BlockSpec((1, S, w), lambda b, h, i: (b, 0, vb + h)),
                  pl.BlockSpec((1, nh, S // tk, tk), lambda b, h, i: (b, h, 0, 0))],
        out_specs=pl.BlockSpec((1, tq, w), lambda b, h, i: (b, i, h)),
        out_shape=jax.ShapeDtypeStruct((B, S, FOX_WIDTH), BF16),
        compiler_params=_params("parallel", "parallel", "arbitrary"),
        name="fox_attention",
    )(proj_b, proj_b, proj_b, c4)


def _cross_kernel(q_ref, k_ref, v_ref, x_ref, wo_hbm, gn_ref, wr_ref, br_ref,
                  xo_ref, hp_ref, lg_ref, wo_b, stg, sem, *, layer, dh):
    @pl.when((pl.program_id(0) == 0) & (pl.program_id(1) == 0))
    def _():
        _load_weight(wo_hbm, layer, wo_b, stg, sem)

    nt = (((1,), (1,)), ((), ()))
    heads = []
    for h in range(CROSS_HEADS):
        sl = slice(h * dh, (h + 1) * dh)
        s = lax.dot_general(q_ref[0, :, sl], k_ref[0, :, sl], nt, preferred_element_type=F32)
        m = jnp.max(s, axis=-1, keepdims=True)
        p = jnp.exp(s - m)
        l = jnp.sum(p, axis=-1, keepdims=True)
        o = jnp.dot(p.astype(BF16), v_ref[0, :, sl], preferred_element_type=F32)
        heads.append((o / l).astype(BF16))
    xn = x_ref[0] + jnp.dot(jnp.concatenate(heads, axis=1), wo_b[...], preferred_element_type=F32)
    xo_ref[0] = xn
    y = _rms(xn) * gn_ref[...]
    _store_token_tiles(hp_ref, y, y.shape[0])
    yh = y.astype(BF16)
    yl = (y - yh.astype(F32)).astype(BF16)
    wr = wr_ref[...]
    wh = wr.astype(BF16)
    wl = (wr - wh.astype(F32)).astype(BF16)
    lg_ref[...] = (jnp.dot(yh, wh, preferred_element_type=F32)
                   + jnp.dot(yl, wh, preferred_element_type=F32)
                   + jnp.dot(yh, wl, preferred_element_type=F32)) + br_ref[...]


def cross_attention_ffn_in(qn, kn, v, x, w_co, layer, g_ffn, wr, br):
    B, S, D = qn.shape
    M = kn.shape[1]
    ts = CROSS_TS
    ns = S // ts
    assert D == TOKEN_ROWS * TOKEN_LANES
    tile = pl.BlockSpec((1, ts, D), lambda b, i: (b, i, 0))
    memb = pl.BlockSpec((1, M, D), lambda b, i: (b, 0, 0))
    const = lambda shape: pl.BlockSpec(shape, lambda b, i: (0, 0))
    return pl.pallas_call(
        functools.partial(_cross_kernel, layer=layer, dh=D // CROSS_HEADS),
        grid=(B, ns),
        in_specs=[tile, memb, memb, tile, pl.BlockSpec(memory_space=pl.ANY),
                  const((1, D)), const((D, LANES)), const((1, LANES))],
        out_specs=[tile,
                   pl.BlockSpec((ts * TOKEN_ROWS, TOKEN_LANES), lambda b, i: (b * ns + i, 0)),
                   pl.BlockSpec((ts, LANES), lambda b, i: (b * ns + i, 0))],
        out_shape=[jax.ShapeDtypeStruct((B, S, D), F32),
                   jax.ShapeDtypeStruct((B * S * TOKEN_ROWS, TOKEN_LANES), F32),
                   jax.ShapeDtypeStruct((B * S, LANES), F32)],
        scratch_shapes=[pltpu.VMEM((D, D), BF16), pltpu.VMEM((2, W_CHUNK, D), F32),
                        pltpu.SemaphoreType.DMA((2,))],
        compiler_params=_params("arbitrary", "arbitrary"),
        name="cross_attention_ffn_in",
    )(qn, kn, v, x, w_co, g_ffn.reshape(1, D), wr, br.reshape(1, LANES))


def _route_kernel(lg_ref, tri_ref, meta_ref, cnt_ref, carry_ref):
    @pl.when(pl.program_id(0) == 0)
    def _():
        carry_ref[...] = jnp.zeros_like(carry_ref)

    x = lg_ref[...]
    lane = lax.broadcasted_iota(jnp.int32, x.shape, 1)
    ninf = -jnp.inf
    is_g = lane < MOE_GROUPS
    gl = jnp.where(is_g, x, ninf)
    gmax = jnp.max(gl, axis=-1, keepdims=True)
    gidx = jnp.min(jnp.where(gl == gmax, lane, LANES), axis=-1, keepdims=True)
    gw = 1.0 / jnp.sum(jnp.where(is_g, jnp.exp(x - gmax), 0.0), axis=-1, keepdims=True)
    e_lane = lane - ROUTER_LANE0
    in_grp = (e_lane >= 0) & (e_lane < N_EXPERTS) & ((e_lane >> 3) == gidx)
    el = jnp.where(in_grp, x, ninf)
    m1 = jnp.max(el, axis=-1, keepdims=True)
    i1 = jnp.min(jnp.where(el == m1, lane, LANES), axis=-1, keepdims=True)
    el2 = jnp.where(lane == i1, ninf, el)
    m2 = jnp.max(el2, axis=-1, keepdims=True)
    i2 = jnp.min(jnp.where(el2 == m2, lane, LANES), axis=-1, keepdims=True)
    r = jnp.exp(m2 - m1)
    p1 = 1.0 / (1.0 + r)
    p2 = r * p1
    oh = ((lane == i1) | (lane == i2)).astype(F32)
    prefix = jnp.dot(tri_ref[...], oh.astype(BF16), preferred_element_type=F32) + carry_ref[...]
    rank1 = jnp.sum(jnp.where(lane == i1, prefix, 0.0), axis=-1, keepdims=True)
    rank2 = jnp.sum(jnp.where(lane == i2, prefix, 0.0), axis=-1, keepdims=True)
    carry_ref[...] += jnp.sum(oh, axis=0, keepdims=True)
    cnt_ref[...] = carry_ref[...]
    cols = [i1.astype(F32), i2.astype(F32), gw * p1, gw * p2, rank1, rank2]
    meta = jnp.zeros(x.shape, F32)
    for k, col in enumerate(cols):
        meta = jnp.where(lane == k, col, meta)
    meta_ref[...] = meta


def route(logits):
    T = logits.shape[0]
    tt = ROUTE_ROWS
    tri = jnp.asarray(np.tril(np.ones((tt, tt), np.float32), -1), BF16)
    return pl.pallas_call(
        _route_kernel,
        grid=(T // tt,),
        in_specs=[pl.BlockSpec((tt, LANES), lambda i: (i, 0)),
                  pl.BlockSpec((tt, tt), lambda i: (0, 0))],
        out_specs=[pl.BlockSpec((tt, LANES), lambda i: (i, 0)),
                   pl.BlockSpec((1, LANES), lambda i: (0, 0))],
        out_shape=[jax.ShapeDtypeStruct((T, LANES), F32),
                   jax.ShapeDtypeStruct((1, LANES), F32)],
        scratch_shapes=[pltpu.VMEM((1, LANES), F32)],
        compiler_params=_params("arbitrary"),
        name="moe_route",
    )(logits, tri)


def _plan_kernel(meta_ref, cnt_ref, d_ref):
    cnt = jnp.broadcast_to(cnt_ref[...], (SUBLANES, LANES)).astype(I32)
    pad = (((cnt + (MOE_ROWS - 1)) >> MOE_ROWS_LOG2) << MOE_ROWS_LOG2).astype(F32)
    lane8 = lax.broadcasted_iota(I32, pad.shape, 1)
    incl = pad
    sh = 1
    while sh < LANES:
        incl = incl + jnp.where(lane8 >= sh, pltpu.roll(incl, sh, axis=1), 0.0)
        sh *= 2
    off = (incl - pad)[0:1, :]
    m = meta_ref[...]
    lane = lax.broadcasted_iota(I32, m.shape, 1)
    i1 = m[:, 0:1].astype(I32)
    i2 = m[:, 1:2].astype(I32)
    d1 = jnp.sum(jnp.where(lane == i1, off, 0.0), axis=-1, keepdims=True) + m[:, 4:5]
    d2 = jnp.sum(jnp.where(lane == i2, off, 0.0), axis=-1, keepdims=True) + m[:, 5:6]
    dm = jnp.where(lane == 0, d1, jnp.where(lane == 1, d2, 0.0))
    d_ref[...] = jnp.transpose(dm)[0:2, :].astype(I32)


def plan(meta, cnt):
    T = meta.shape[0]
    tt = ROUTE_ROWS
    return pl.pallas_call(
        _plan_kernel,
        grid=(T // tt,),
        in_specs=[pl.BlockSpec((tt, LANES), lambda i: (i, 0)),
                  pl.BlockSpec((1, LANES), lambda i: (0, 0))],
        out_specs=pl.BlockSpec((2, tt), lambda i: (0, i)),
        out_shape=jax.ShapeDtypeStruct((2, T), I32),
        compiler_params=_params("parallel"),
        name="moe_plan",
    )(meta, cnt)


def _invert_kernel(d1_ref, d2_ref, src_ref, *, T, R):
    def init(r, _):
        src_ref[r] = 0
        return 0
    lax.fori_loop(0, R, init, 0, unroll=32)

    def body(t, _):
        src_ref[d1_ref[t]] = t
        src_ref[d2_ref[t]] = t
        return 0
    lax.fori_loop(0, T, body, 0, unroll=8)


def invert(d1, d2, R):
    T = d1.shape[0]
    grid_spec = pltpu.PrefetchScalarGridSpec(
        num_scalar_prefetch=2, grid=(1,), in_specs=[],
        out_specs=pl.BlockSpec(memory_space=pltpu.SMEM))
    return pl.pallas_call(
        functools.partial(_invert_kernel, T=T, R=R),
        grid_spec=grid_spec,
        out_shape=jax.ShapeDtypeStruct((R,), I32),
        compiler_params=_params("arbitrary"),
        name="moe_invert",
    )(d1, d2)


GATHER_CHUNK = 8


def _tile_copy(src_hbm, row, dst, slot, r, sem):
    return pltpu.make_async_copy(src_hbm.at[pl.ds(row * TOKEN_ROWS, TOKEN_ROWS)],
                                 dst.at[slot, pl.ds(r * TOKEN_ROWS, TOKEN_ROWS)], sem.at[slot])


def _expert_kernel(te_ref, first_ref, nxt_ref, wsl_ref, nch_ref, src_ref,
                   hp_hbm, w1_hbm, w3_hbm, w2_hbm, o_ref,
                   xbuf, r1, r3, r2, w1b, w3b, w2b, gsem, wsem, *, tr, base):
    i = pl.program_id(0)
    n = pl.num_programs(0)
    slot = i % 2

    def weight_copies(e, ws):
        return [pltpu.make_async_copy(w1_hbm.at[base + e], r1.at[ws], wsem.at[ws, 0]),
                pltpu.make_async_copy(w3_hbm.at[base + e], r3.at[ws], wsem.at[ws, 1]),
                pltpu.make_async_copy(w2_hbm.at[base + e], r2.at[ws], wsem.at[ws, 2])]

    def issue_rows(tile, slot, c0, c1):
        def body(c, _):
            for u in range(GATHER_CHUNK):
                r = c * GATHER_CHUNK + u
                _tile_copy(hp_hbm, src_ref[tile * tr + r], xbuf, slot, r, gsem).start()
            return 0
        lax.fori_loop(c0, c1, body, 0)

    def wait_rows(tile, slot):
        rows = GATHER_CHUNK * TOKEN_ROWS
        def body(c, _):
            pltpu.make_async_copy(hp_hbm.at[pl.ds(0, rows)], xbuf.at[slot, pl.ds(0, rows)],
                                  gsem.at[slot]).wait()
            return 0
        lax.fori_loop(0, nch_ref[tile], body, 0)

    @pl.when(i == 0)
    def _():
        xbuf[...] = jnp.zeros_like(xbuf)
        for cp in weight_copies(te_ref[0], 0):
            cp.start(priority=1)
        issue_rows(0, 0, 0, nch_ref[0])

    wait_rows(i, slot)

    nxt_tile = jnp.minimum(i + 1, n - 1)
    issue_rows(nxt_tile, 1 - slot, 0, jnp.where(i + 1 < n, nch_ref[nxt_tile], 0))

    @pl.when(first_ref[i] == 1)
    def _():
        ws = wsl_ref[i]
        for cp in weight_copies(te_ref[i], ws):
            cp.wait()

        @pl.when(nxt_ref[i] >= 0)
        def _():
            for cp in weight_copies(nxt_ref[i], 1 - ws):
                cp.start(priority=1)

        w1b[...] = r1[ws].astype(BF16)
        w3b[...] = r3[ws].astype(BF16)
        w2b[...] = r2[ws].astype(BF16)

    def expert_mlp(rows):
        x = jnp.concatenate([p.astype(BF16) for p in _load_token_tiles(xbuf.at[slot], 0, rows)],
                            axis=1)
        h1 = jnp.dot(x, w1b[...], preferred_element_type=F32)
        h3 = jnp.dot(x, w3b[...], preferred_element_type=F32)
        hm = (h1 * _sigmoid(h1) * h3).astype(BF16)
        _store_token_tiles(o_ref, jnp.dot(hm, w2b[...], preferred_element_type=F32), rows)

    half_chunks = tr // (2 * GATHER_CHUNK)

    @pl.when(nch_ref[i] > half_chunks)
    def _():
        expert_mlp(tr)

    @pl.when((nch_ref[i] > 0) & (nch_ref[i] <= half_chunks))
    def _():
        expert_mlp(tr // 2)
        o_ref[tr // 2 * TOKEN_ROWS:, :] = jnp.zeros((tr // 2 * TOKEN_ROWS, TOKEN_LANES), F32)

    @pl.when(nch_ref[i] == 0)
    def _():
        o_ref[...] = jnp.zeros_like(o_ref)


def moe_experts(hp, w1, w3, w2, layer, tables, src):
    D, F = w1.shape[-2:]
    tr = MOE_ROWS
    te, first, nxt, wsl, nch = tables
    NT = te.shape[0]
    w1f = w1.reshape(-1, D, F)
    w3f = w3.reshape(-1, D, F)
    w2f = w2.reshape(-1, F, D)
    hbm = pl.BlockSpec(memory_space=pl.ANY)
    grid_spec = pltpu.PrefetchScalarGridSpec(
        num_scalar_prefetch=6,
        grid=(NT,),
        in_specs=[hbm, hbm, hbm, hbm],
        out_specs=pl.BlockSpec((tr * TOKEN_ROWS, TOKEN_LANES), lambda i, *_: (i, 0)),
        scratch_shapes=[pltpu.VMEM((2, tr * TOKEN_ROWS, TOKEN_LANES), F32),
                        pltpu.VMEM((2, D, F), F32),
                        pltpu.VMEM((2, D, F), F32),
                        pltpu.VMEM((2, F, D), F32),
                        pltpu.VMEM((D, F), BF16),
                        pltpu.VMEM((D, F), BF16),
                        pltpu.VMEM((F, D), BF16),
                        pltpu.SemaphoreType.DMA((2,)),
                        pltpu.SemaphoreType.DMA((2, 3))],
    )
    return pl.pallas_call(
        functools.partial(_expert_kernel, tr=tr, base=layer * N_EXPERTS),
        grid_spec=grid_spec,
        out_shape=jax.ShapeDtypeStruct((NT * tr * TOKEN_ROWS, TOKEN_LANES), F32),
        compiler_params=_params("arbitrary"),
        name="moe_experts",
    )(te, first, nxt, wsl, nch, src, hp, w1f, w3f, w2f)


def _combine_kernel(d1_ref, d2_ref, x_ref, meta_ref, ys_hbm, *rest, tt, with_norm):
    if with_norm:
        g_ref, o_ref, hn_ref, buf, sem = rest
    else:
        o_ref, buf, sem = rest
    i = pl.program_id(0)
    n = pl.num_programs(0)
    slot = i % 2

    def issue(tile, slot):
        def body(c, _):
            for u in range(GATHER_CHUNK):
                r = c * GATHER_CHUNK + u
                t = tile * tt + r
                _tile_copy(ys_hbm, d1_ref[t], buf, slot, r, sem).start(priority=0)
                _tile_copy(ys_hbm, d2_ref[t], buf, slot, tt + r, sem).start(priority=1)
            return 0
        lax.fori_loop(0, tt // GATHER_CHUNK, body, 0)

    @pl.when(i == 0)
    def _():
        issue(0, 0)

    pltpu.make_async_copy(ys_hbm.at[pl.ds(0, 2 * tt * TOKEN_ROWS)], buf.at[slot], sem.at[slot]).wait()

    @pl.when(i + 1 < n)
    def _():
        issue(i + 1, 1 - slot)

    w1 = meta_ref[:, 2:3]
    w2 = meta_ref[:, 3:4]
    pa = _load_token_tiles(buf.at[slot], 0, tt)
    pb = _load_token_tiles(buf.at[slot], tt * TOKEN_ROWS, tt)
    cols = [x_ref[:, j * TOKEN_LANES:(j + 1) * TOKEN_LANES] + w1 * pa[j] + w2 * pb[j]
            for j in range(TOKEN_ROWS)]
    xn = jnp.concatenate(cols, axis=1)
    o_ref[...] = xn
    if with_norm:
        hn_ref[...] = (_rms(xn) * g_ref[...]).astype(hn_ref.dtype)


def moe_combine(x, meta, ys, d1, d2, g_next=None):
    T, D = x.shape
    tt = COMB_ROWS
    with_norm = g_next is not None
    row = pl.BlockSpec((tt, D), lambda i, d1, d2: (i, 0))
    in_specs = [row, pl.BlockSpec((tt, LANES), lambda i, d1, d2: (i, 0)),
                pl.BlockSpec(memory_space=pl.ANY)]
    args = [d1, d2, x, meta, ys]
    out_specs, out_shape = row, jax.ShapeDtypeStruct((T, D), F32)
    if with_norm:
        in_specs.append(pl.BlockSpec((1, D), lambda i, d1, d2: (0, 0)))
        args.append(g_next.reshape(1, D))
        out_specs, out_shape = [row, row], [out_shape, jax.ShapeDtypeStruct((T, D), BF16)]
    grid_spec = pltpu.PrefetchScalarGridSpec(
        num_scalar_prefetch=2,
        grid=(T // tt,),
        in_specs=in_specs,
        out_specs=out_specs,
        scratch_shapes=[pltpu.VMEM((2, 2 * tt * TOKEN_ROWS, TOKEN_LANES), F32),
                        pltpu.SemaphoreType.DMA((2,))],
    )
    return pl.pallas_call(
        functools.partial(_combine_kernel, tt=tt, with_norm=with_norm),
        grid_spec=grid_spec,
        out_shape=out_shape,
        compiler_params=_params("arbitrary"),
        name="moe_combine",
    )(*args)


def _tile_tables(cnt, T):
    E = N_EXPERTS
    counts = cnt[0, ROUTER_LANE0:ROUTER_LANE0 + E].astype(I32)
    tiles_e = (counts + MOE_ROWS - 1) // MOE_ROWS
    tile_end = jnp.cumsum(tiles_e)
    tile_start = tile_end - tiles_e
    nact = tile_end[-1]
    NT = (2 * T) // MOE_ROWS + E
    tid = jnp.arange(NT, dtype=I32)
    te = jnp.sum((jnp.minimum(tid, nact - 1)[:, None] >= tile_end[None, :]).astype(I32), axis=1)
    te = jnp.minimum(te, E - 1)
    active = tid < nact
    first = (active & (tid == tile_start[te])).astype(I32)
    eid = jnp.arange(E, dtype=I32)
    later = (eid[None, :] > eid[:, None]) & (tiles_e[None, :] > 0)
    nxt_e = jnp.min(jnp.where(later, eid[None, :], E), axis=1)
    nxt_e = jnp.where(nxt_e == E, -1, nxt_e)
    ordinal = jnp.cumsum((tiles_e > 0).astype(I32)) - 1
    valid = jnp.clip(counts[te] - (tid - tile_start[te]) * MOE_ROWS, 0, MOE_ROWS)
    nch = jnp.where(active, (valid + GATHER_CHUNK - 1) // GATHER_CHUNK, 0)
    return (te, first, nxt_e[te], ordinal[te] % 2, nch), NT * MOE_ROWS


def _lower_bound_logs(lb_param, layer):
    p = jax.nn.softmax(lb_param.astype(F32), axis=0)
    c = jnp.cumsum(p, axis=0)
    lb = c[layer] - c[0]
    return jnp.log(lb), jnp.log1p(-lb)


def kernel(x, mem, norm_mix, w_in, hg_lower_bounds, hg_norm, conv_w, fox_f_bias, fox_q_norm,
           fox_k_norm, w_out, norm_cross, norm_mem, w_cq, w_ck, w_cv, w_co, cross_q_norm,
           cross_k_norm, norm_ffn, router_group_w, router_group_b, router_expert_w,
           router_expert_b, moe_w1, moe_w3, moe_w2):
    B, S, D = x.shape
    M = mem.shape[1]
    L = w_in.shape[0]
    T = B * S
    HGW = 4 * HG_WIDTH
    CONV3 = 3 * CONV_WIDTH
    FQ0 = CONV3
    FK0 = FQ0 + FOX_WIDTH
    FV0 = FK0 + FOX_WIDTH
    PB = CONV3 + 3 * FOX_WIDTH
    dhc = D // CROSS_HEADS
    x2 = x.reshape(T, D)
    mem2 = mem.reshape(B * M, D)
    w_in_t = jnp.swapaxes(w_in, 1, 2)
    hn = rmsnorm(x2, norm_mix[0])
    for l in range(L):
        proj_a = matmul([hn], w_in_t, l, 0, HGW, F32, w_t=True)
        proj_b = matmul([hn], w_in_t, l, HGW, PB, BF16, w_t=True, head_norms=(
            (FQ0, FK0, fox_q_norm[l], FOX_DIM ** -0.5), (FK0, FV0, fox_k_norm[l], 1.0)))
        loglb, l1mlb = _lower_bound_logs(hg_lower_bounds, l)
        y_hg = hgrn2(proj_a.reshape(B, S, HGW), loglb, l1mlb, hg_norm[l])
        pb3 = proj_b.reshape(B, S, PB)
        y_conv = short_conv(pb3, conv_w[l])
        c = fox_gate(hn.reshape(B, S, D), w_in_t, l, HGW + PB, fox_f_bias[l])
        y_fox = fox_attention(pb3, FQ0, FK0, FV0, c)
        mix = [y_hg.reshape(T, HG_WIDTH), y_conv.reshape(T, CONV_WIDTH), y_fox.reshape(T, FOX_WIDTH)]
        x2, qcn = out_proj_cross_q(mix, x2, w_out, w_cq, l, norm_cross[l], cross_q_norm[l])
        memn = rmsnorm(mem2, norm_mem[l])
        kcn = matmul([memn], w_ck, l, 0, D, BF16, head_norms=((0, D, cross_k_norm[l], 1.0),))
        vc = matmul([memn], w_cv, l, 0, D, BF16)
        wr = jnp.concatenate([router_group_w[l], router_expert_w[l]], axis=1)
        wr = jnp.pad(wr, ((0, 0), (0, LANES - wr.shape[1])))
        br = jnp.concatenate([router_group_b[l], router_expert_b[l]])
        br = jnp.pad(br, (0, LANES - br.shape[0]))
        x3, hp, logits = cross_attention_ffn_in(
            qcn.reshape(B, S, D), kcn.reshape(B, M, D), vc.reshape(B, M, D), x2.reshape(B, S, D),
            w_co, l, norm_ffn[l], wr, br)
        x2 = x3.reshape(T, D)
        meta, cnt = route(logits)
        d = plan(meta, cnt)
        tables, R = _tile_tables(cnt, T)
        src = invert(d[0], d[1], R)
        ys = moe_experts(hp, moe_w1, moe_w3, moe_w2, l, tables, src)
        if l + 1 < L:
            x2, hn = moe_combine(x2, meta, ys, d[0], d[1], norm_mix[l + 1])
        else:
            x2 = moe_combine(x2, meta, ys, d[0], d[1])
    return x2.reshape(B, S, D)
```

```python
import functools

import numpy as np
import jax
import jax.numpy as jnp
from jax import lax
from jax.experimental import pallas as pl
from jax.experimental.pallas import tpu as pltpu

F32 = jnp.float32
BF16 = jnp.bfloat16
I32 = jnp.int32
EPS = 1e-6

HG_HEADS = 4
HG_DIM = 128
HG_WIDTH = HG_HEADS * HG_DIM
CONV_WIDTH = 512
FOX_HEADS = 8
FOX_DIM = 128
FOX_WIDTH = FOX_HEADS * FOX_DIM
CROSS_HEADS = 4
MOE_GROUPS = 4
MOE_EXPERTS = 8
N_EXPERTS = MOE_GROUPS * MOE_EXPERTS
ROUTER_LANE0 = MOE_GROUPS

LANES = 128
SUBLANES = 8
VMEM_LIMIT = 56 * 1024 * 1024

NORM_ROWS = 256
MM_TM = 2048
MM_TN = 512
HG_CHUNK = 128
FOX_TQ = 512
FOX_TK = 512
FOX_HEADS_PER_STEP = 4
CROSS_TS = 512
ROUTE_ROWS = 512
MOE_ROWS_LOG2 = 8
MOE_ROWS = 1 << MOE_ROWS_LOG2
COMB_ROWS = 256


def _params(*sem):
    return pltpu.CompilerParams(dimension_semantics=sem, vmem_limit_bytes=VMEM_LIMIT)


def _sigmoid(x):
    return 1.0 / (1.0 + jnp.exp(-x))


def _log_sigmoid(x):
    return jnp.minimum(x, 0.0) - jnp.log1p(jnp.exp(-jnp.abs(x)))


TOKEN_ROWS = 16
TOKEN_LANES = 128


def _store_token_tiles(ref, y, rows):
    for j in range(TOKEN_ROWS):
        ref[pl.ds(j, rows, stride=TOKEN_ROWS), :] = y[:, j * TOKEN_LANES:(j + 1) * TOKEN_LANES]


def _load_token_tiles(ref, base, rows):
    return [ref[pl.ds(base + j, rows, stride=TOKEN_ROWS), :] for j in range(TOKEN_ROWS)]


def _rms(x):
    return x * lax.rsqrt(jnp.mean(x * x, axis=-1, keepdims=True) + EPS)


def _rmsnorm_kernel(x_ref, g_ref, o_ref):
    o_ref[...] = (_rms(x_ref[...]) * g_ref[...]).astype(o_ref.dtype)


def rmsnorm(x, g, out_dtype=BF16):
    R, D = x.shape
    return pl.pallas_call(
        _rmsnorm_kernel,
        grid=(R // NORM_ROWS,),
        in_specs=[pl.BlockSpec((NORM_ROWS, D), lambda i: (i, 0)),
                  pl.BlockSpec((1, D), lambda i: (0, 0))],
        out_specs=pl.BlockSpec((NORM_ROWS, D), lambda i: (i, 0)),
        out_shape=jax.ShapeDtypeStruct((R, D), out_dtype),
        compiler_params=_params("parallel"),
        name="rmsnorm",
    )(x, g.reshape(1, D))


def _head_rmsnorm(x, g, dh, scale):
    heads = [_rms(x[:, h * dh:(h + 1) * dh]) * g * scale for h in range(x.shape[1] // dh)]
    return heads[0] if len(heads) == 1 else jnp.concatenate(heads, axis=1)


def _matmul_kernel(*refs, n_parts, w_t, norm_tiles, dh):
    a_refs = refs[:n_parts]
    w_ref = refs[n_parts]
    if norm_tiles:
        g_ref, o_ref, wb_ref = refs[n_parts + 1:]
    else:
        o_ref, wb_ref = refs[n_parts + 1:]
    j = pl.program_id(0)

    @pl.when(pl.program_id(1) == 0)
    def _():
        wb_ref[...] = w_ref[...].astype(BF16)

    if n_parts == 1:
        a = a_refs[0][...]
    else:
        a = jnp.concatenate([r[...] for r in a_refs], axis=1)
    if w_t:
        acc = lax.dot_general(a, wb_ref[...], (((1,), (1,)), ((), ())), preferred_element_type=F32)
    else:
        acc = jnp.dot(a, wb_ref[...], preferred_element_type=F32)
    if not norm_tiles:
        o_ref[...] = acc.astype(o_ref.dtype)
        return
    plain = True
    for k, (j0, j1, scale) in enumerate(norm_tiles):
        hit = (j >= j0) & (j < j1)
        plain = plain & jnp.logical_not(hit)

        @pl.when(hit)
        def _(k=k, scale=scale):
            o_ref[...] = _head_rmsnorm(acc, g_ref[k:k + 1, :], dh, scale).astype(o_ref.dtype)

    @pl.when(plain)
    def _():
        o_ref[...] = acc.astype(o_ref.dtype)


def matmul(a_parts, w, layer, col0, n, out_dtype, w_t=False, head_norms=(), tm=MM_TM, tn=MM_TN):
    M = a_parts[0].shape[0]
    K = w.shape[2] if w_t else w.shape[1]
    assert sum(p.shape[1] for p in a_parts) == K
    tm = min(tm, M)
    assert col0 % tn == 0 and n % tn == 0 and M % tm == 0
    cb = col0 // tn
    in_specs = [pl.BlockSpec((tm, p.shape[1]), lambda j, i: (i, 0)) for p in a_parts]
    if w_t:
        in_specs.append(pl.BlockSpec((None, tn, K), lambda j, i: (layer, cb + j, 0)))
    else:
        in_specs.append(pl.BlockSpec((None, K, tn), lambda j, i: (layer, 0, cb + j)))
    args = list(a_parts) + [w]
    norm_tiles, dh = (), 0
    if head_norms:
        dh = head_norms[0][2].shape[0]
        assert all(c0 % tn == 0 and c1 % tn == 0 and g.shape[0] == dh and tn % dh == 0
                   for c0, c1, g, _ in head_norms)
        norm_tiles = tuple((c0 // tn, c1 // tn, s) for c0, c1, _, s in head_norms)
        gains = jnp.stack([g for _, _, g, _ in head_norms])
        in_specs.append(pl.BlockSpec(gains.shape, lambda j, i: (0, 0)))
        args.append(gains)
    return pl.pallas_call(
        functools.partial(_matmul_kernel, n_parts=len(a_parts), w_t=w_t, norm_tiles=norm_tiles,
                          dh=dh),
        grid=(n // tn, M // tm),
        in_specs=in_specs,
        out_specs=pl.BlockSpec((tm, tn), lambda j, i: (i, j)),
        out_shape=jax.ShapeDtypeStruct((M, n), out_dtype),
        scratch_shapes=[pltpu.VMEM((tn, K) if w_t else (K, tn), BF16)],
        compiler_params=_params("arbitrary", "arbitrary"),
        name="matmul",
    )(*args)


W_CHUNK = 256


def _load_weight(w_hbm, layer, wb, stg, sem):
    n_chunks = wb.shape[0] // W_CHUNK

    def copy(c):
        return pltpu.make_async_copy(w_hbm.at[layer, pl.ds(c * W_CHUNK, W_CHUNK)], stg.at[c % 2],
                                     sem.at[c % 2])
    copy(0).start()
    for c in range(n_chunks):
        if c + 1 < n_chunks:
            copy(c + 1).start()
        copy(c).wait()
        wb[c * W_CHUNK:(c + 1) * W_CHUNK, :] = stg[c % 2].astype(BF16)


def _outq_kernel(a1_ref, a2_ref, a3_ref, x_ref, wo_hbm, wq_hbm, gn_ref, gq_ref, xo_ref, q_ref,
                 wo_b, wq_b, stg, sem, *, layer, dh, scale):
    @pl.when(pl.program_id(0) == 0)
    def _():
        _load_weight(wo_hbm, layer, wo_b, stg, sem)
        _load_weight(wq_hbm, layer, wq_b, stg, sem)

    a = jnp.concatenate([a1_ref[...], a2_ref[...], a3_ref[...]], axis=1)
    xn = x_ref[...] + jnp.dot(a, wo_b[...], preferred_element_type=F32)
    xo_ref[...] = xn
    hc = (_rms(xn) * gn_ref[...]).astype(BF16)
    q = jnp.dot(hc, wq_b[...], preferred_element_type=F32)
    q_ref[...] = _head_rmsnorm(q, gq_ref[...], dh, scale).astype(q_ref.dtype)


def out_proj_cross_q(mix, x, w_out, w_cq, layer, g_cross, g_q, tm=512):
    T, D = x.shape
    dh = g_q.shape[0]
    row = lambda w: pl.BlockSpec((tm, w), lambda i: (i, 0))
    hbm = pl.BlockSpec(memory_space=pl.ANY)
    return pl.pallas_call(
        functools.partial(_outq_kernel, layer=layer, dh=dh, scale=dh ** -0.5),
        grid=(T // tm,),
        in_specs=[row(mix[0].shape[1]), row(mix[1].shape[1]), row(mix[2].shape[1]), row(D), hbm, hbm,
                  pl.BlockSpec((1, D), lambda i: (0, 0)), pl.BlockSpec((1, dh), lambda i: (0, 0))],
        out_specs=[row(D), row(D)],
        out_shape=[jax.ShapeDtypeStruct((T, D), F32), jax.ShapeDtypeStruct((T, D), BF16)],
        scratch_shapes=[pltpu.VMEM((D, D), BF16), pltpu.VMEM((D, D), BF16),
                        pltpu.VMEM((2, W_CHUNK, D), F32), pltpu.SemaphoreType.DMA((2,))],
        compiler_params=_params("arbitrary"),
        name="out_proj_cross_q",
    )(*mix, x, w_out, w_cq, g_cross.reshape(1, D), g_q.reshape(1, dh))


def _hgrn_consts(C):
    nlev = int(np.log2(C))
    t = np.arange(C)[:, None]
    u = np.arange(C)[None, :]
    tri = (u <= t).astype(np.float32)
    mall2 = np.concatenate([tri, tri], axis=1)
    tt = np.arange(C)[:, None]
    ss = np.arange(C)[None, :]
    lev = np.full((C, C), nlev + 1, np.int32)
    x = tt ^ ss
    hb = np.zeros_like(x)
    for j in range(nlev):
        hb = np.where((x >> j) & 1, j, hb)
    lev = np.where(tt > ss, hb, lev)
    lev = np.where(tt == ss, nlev, lev)
    return jnp.asarray(mall2, BF16), jnp.asarray(lev, jnp.int32)


def _level_sums(lf, b, row, j, C):
    m = 1 << j
    second = (row & m) != 0
    if j == 0:
        return jnp.where(second, lf, 0.0)
    if j == 1:
        prev = pltpu.roll(lf, 1, axis=0)
        nxt = pltpu.roll(lf, C - 1, axis=0)
        first = jnp.where((row & 1) == 0, nxt, 0.0)
        return jnp.where(second, jnp.where((row & 1) != 0, lf + prev, lf), first)
    g = 2 * m
    d = b.shape[1]
    bsel = jnp.broadcast_to(b.reshape(C // g, g, d)[:, m - 1:m, :], (C // g, g, d)).reshape(C, d)
    return jnp.where(second, b - bsel, bsel - b)


def _hgrn_kernel(x_ref, mall_ref, lev_ref, loglb_ref, l1mlb_ref, ng_ref, o_ref, st_ref, *, C, nlev):
    @pl.when(pl.program_id(1) == 0)
    def _():
        st_ref[...] = jnp.zeros_like(st_ref)

    W = HG_WIDTH
    d = HG_DIM
    lev = lev_ref[...]
    mall = mall_ref[...]
    row = lax.broadcasted_iota(jnp.int32, (C, d), 0)
    nt = (((1,), (1,)), ((), ()))

    def side(xs):
        return jnp.concatenate(xs, axis=1)

    def block_diag(xa, xb):
        return jnp.concatenate([side([xa, jnp.zeros_like(xb)]), side([jnp.zeros_like(xa), xb])],
                               axis=0)

    for h0 in range(0, HG_HEADS, 2):
        q, lf, kk, vb, g = [], [], [], [], []
        for h in (h0, h0 + 1):
            sl = slice(h * d, (h + 1) * d)
            q.append(x_ref[0, :, h * d:(h + 1) * d])
            z = x_ref[0, :, W + h * d:W + (h + 1) * d]
            vb.append(x_ref[0, :, 2 * W + h * d:2 * W + (h + 1) * d].astype(BF16))
            g.append(x_ref[0, :, 3 * W + h * d:3 * W + (h + 1) * d])
            a = loglb_ref[:, sl]
            bt = l1mlb_ref[:, sl] + _log_sigmoid(z)
            lf.append(jnp.maximum(a, bt) + jnp.log1p(jnp.exp(-jnp.abs(a - bt))))
            kk.append(1.0 - jnp.exp(lf[-1]))
        hi = [x.astype(BF16) for x in lf]
        lo = [(x - y.astype(F32)).astype(BF16) for x, y in zip(lf, hi)]
        b2 = jnp.dot(mall, jnp.concatenate([side(hi), side(lo)], axis=0),
                     preferred_element_type=F32)
        b = [b2[:, :d], b2[:, d:]]
        st = [st_ref[h0], st_ref[h0 + 1]]
        o2 = lax.dot_general(side([(q[i] * jnp.exp(b[i])).astype(BF16) for i in (0, 1)]),
                             block_diag(st[0].astype(BF16), st[1].astype(BF16)), nt,
                             preferred_element_type=F32)
        att = [jnp.where(lev == nlev, jnp.sum(q[i] * kk[i], axis=-1, keepdims=True), 0.0)
               for i in (0, 1)]
        for j in range(nlev):
            second = (row & (1 << j)) != 0
            xq, xk = [], []
            for i in (0, 1):
                e = jnp.exp(_level_sums(lf[i], b[i], row, j, C))
                xq.append(jnp.where(second, q[i] * e, 0.0).astype(BF16))
                xk.append(jnp.where(second, 0.0, kk[i] * e).astype(BF16))
            am = lax.dot_general(side(xq), block_diag(xk[0], xk[1]), nt,
                                 preferred_element_type=F32)
            att = [jnp.where(lev == j, am[:, i * C:(i + 1) * C], att[i]) for i in (0, 1)]
        o2 = o2 + jnp.dot(side([x.astype(BF16) for x in att]), block_diag(vb[0], vb[1]),
                          preferred_element_type=F32)
        for i, h in enumerate((h0, h0 + 1)):
            bl = b[i][C - 1:C, :]
            kh = (kk[i] * jnp.exp(bl - b[i])).astype(BF16)
            st_ref[h] = st[i] * jnp.exp(bl) + lax.dot_general(
                vb[i], kh, (((0,), (0,)), ((), ())), preferred_element_type=F32)
            o = o2[:, i * d:(i + 1) * d]
            y = o * lax.rsqrt(jnp.mean(o * o, axis=-1, keepdims=True) + EPS) * ng_ref[...]
            o_ref[0, :, h * d:(h + 1) * d] = (y * (g[i] * _sigmoid(g[i]))).astype(o_ref.dtype)


def hgrn2(proj_a, loglb, l1mlb, norm_g):
    B, S, _ = proj_a.shape
    C = HG_CHUNK
    nlev = int(np.log2(C))
    mall, lev = _hgrn_consts(C)
    return pl.pallas_call(
        functools.partial(_hgrn_kernel, C=C, nlev=nlev),
        grid=(B, S // C),
        in_specs=[pl.BlockSpec((1, C, 4 * HG_WIDTH), lambda b, c: (b, c, 0)),
                  pl.BlockSpec(mall.shape, lambda b, c: (0, 0)),
                  pl.BlockSpec((C, C), lambda b, c: (0, 0)),
                  pl.BlockSpec((1, HG_WIDTH), lambda b, c: (0, 0)),
                  pl.BlockSpec((1, HG_WIDTH), lambda b, c: (0, 0)),
                  pl.BlockSpec((1, HG_DIM), lambda b, c: (0, 0))],
        out_specs=pl.BlockSpec((1, C, HG_WIDTH), lambda b, c: (b, c, 0)),
        out_shape=jax.ShapeDtypeStruct((B, S, HG_WIDTH), BF16),
        scratch_shapes=[pltpu.VMEM((HG_HEADS, HG_DIM, HG_DIM), F32)],
        compiler_params=_params("parallel", "arbitrary"),
        name="hgrn2",
    )(proj_a, mall, lev, loglb.reshape(1, HG_WIDTH), l1mlb.reshape(1, HG_WIDTH),
      norm_g.reshape(1, HG_DIM))


def _conv_kernel(b_ref, c_ref, h_ref, w_ref, o_ref):
    u = c_ref[0].astype(F32) * h_ref[0].astype(F32)
    row = lax.broadcasted_iota(jnp.int32, u.shape, 0)
    u1 = jnp.where(row >= 1, pltpu.roll(u, 1, axis=0), 0.0)
    u2 = jnp.where(row >= 2, pltpu.roll(u, 2, axis=0), 0.0)
    y = w_ref[0:1, :] * u2 + w_ref[1:2, :] * u1 + w_ref[2:3, :] * u
    o_ref[0] = (b_ref[0].astype(F32) * y).astype(o_ref.dtype)


def short_conv(proj_b, w):
    B, S, _ = proj_b.shape
    cw = CONV_WIDTH
    spec = lambda k: pl.BlockSpec((1, S, cw), lambda b: (b, 0, k))
    return pl.pallas_call(
        _conv_kernel,
        grid=(B,),
        in_specs=[spec(0), spec(1), spec(2), pl.BlockSpec((3, cw), lambda b: (0, 0))],
        out_specs=pl.BlockSpec((1, S, cw), lambda b: (b, 0, 0)),
        out_shape=jax.ShapeDtypeStruct((B, S, cw), BF16),
        compiler_params=_params("parallel"),
        name="short_conv",
    )(proj_b, proj_b, proj_b, w)


def _fox_gate_kernel(hn_ref, wf_ref, bias_ref, c_ref):
    gl = lax.dot_general(wf_ref[...].astype(BF16), hn_ref[0], (((1,), (1,)), ((), ())),
                         preferred_element_type=F32)
    c = _log_sigmoid(gl + bias_ref[...])
    S = c.shape[1]
    lane = lax.broadcasted_iota(jnp.int32, c.shape, 1)
    sh = 1
    while sh < S:
        c = c + jnp.where(lane >= sh, pltpu.roll(c, sh, axis=1), 0.0)
        sh *= 2
    c_ref[0] = c


def fox_gate(hn, w_t, layer, row0, bias):
    B, S, D = hn.shape
    H = bias.shape[0]
    assert row0 % H == 0
    return pl.pallas_call(
        _fox_gate_kernel,
        grid=(B,),
        in_specs=[pl.BlockSpec((1, S, D), lambda b: (b, 0, 0)),
                  pl.BlockSpec((None, H, D), lambda b: (layer, row0 // H, 0)),
                  pl.BlockSpec((H, 1), lambda b: (0, 0))],
        out_specs=pl.BlockSpec((1, H, S), lambda b: (b, 0, 0)),
        out_shape=jax.ShapeDtypeStruct((B, H, S), F32),
        compiler_params=_params("parallel"),
        name="fox_gate",
    )(hn, w_t, bias.reshape(H, 1))


def _fox_kernel(q_ref, k_ref, v_ref, c_ref, o_ref, *, tq, tk, nh):
    qi = pl.program_id(2)
    nt = (((1,), (1,)), ((), ()))
    ones = jnp.ones((tk, FOX_DIM), BF16)
    heads = [slice(h * FOX_DIM, (h + 1) * FOX_DIM) for h in range(nh)]
    qs = [q_ref[0, :, sl] for sl in heads]

    def step(ki, carry, diag):
        out = []
        for h, sl in enumerate(heads):
            m, acc = carry[h]
            k = k_ref[0, pl.ds(ki * tk, tk), sl]
            v1 = jnp.concatenate([v_ref[0, pl.ds(ki * tk, tk), sl], ones], axis=1)
            ck = c_ref[0, h, pl.ds(ki, 1), :]
            s = lax.dot_general(qs[h], k, nt, preferred_element_type=F32) - ck
            if diag is not None:
                r = lax.broadcasted_iota(jnp.int32, s.shape, 0)
                c = lax.broadcasted_iota(jnp.int32, s.shape, 1)
                s = jnp.where(c + diag * tk <= r, s, -jnp.inf)
            m_new = jnp.maximum(m, jnp.max(s, axis=-1, keepdims=True))
            alpha = jnp.exp(m - m_new)
            p = jnp.exp((s - m_new).astype(BF16))
            out.append((m_new, alpha * acc + jnp.dot(p, v1, preferred_element_type=F32)))
        return tuple(out)

    init = tuple((jnp.full((tq, 1), -jnp.inf, F32), jnp.zeros((tq, 2 * FOX_DIM), F32))
                 for _ in heads)
    n_full = qi * (tq // tk)
    carry = lax.fori_loop(0, n_full, lambda ki, cr: step(ki, cr, None), init)
    for d in range(tq // tk):
        carry = step(n_full + d, carry, d)
    for h, sl in enumerate(heads):
        acc = carry[h][1]
        o_ref[0, :, sl] = (acc[:, :FOX_DIM] / acc[:, FOX_DIM:]).astype(o_ref.dtype)


def fox_attention(proj_b, q_col0, k_col0, v_col0, c):
    B, S, _ = proj_b.shape
    tq, tk, nh = FOX_TQ, FOX_TK, FOX_HEADS_PER_STEP
    H = FOX_HEADS
    w = nh * FOX_DIM
    assert tq % tk == 0 and H % nh == 0 and all(c0 % w == 0 for c0 in (q_col0, k_col0, v_col0))
    qb, kb, vb = q_col0 // w, k_col0 // w, v_col0 // w
    c4 = c.reshape(B, H, S // tk, tk)
    return pl.pallas_call(
        functools.partial(_fox_kernel, tq=tq, tk=tk, nh=nh),
        grid=(B, H // nh, S // tq),
        in_specs=[pl.BlockSpec((1, tq, w), lambda b, h, i: (b, i, qb + h)),
                  pl.BlockSpec((1, S, w), lambda b, h, i: (b, 0, kb + h)),
                  pl.BlockSpec((1, S, w), lambda b, h, i: (b, 0, vb + h)),
                  pl.BlockSpec((1, nh, S // tk, tk), lambda b, h, i: (b, h, 0, 0))],
        out_specs=pl.BlockSpec((1, tq, w), lambda b, h, i: (b, i, h)),
        out_shape=jax.ShapeDtypeStruct((B, S, FOX_WIDTH), BF16),
        compiler_params=_params("parallel", "parallel", "arbitrary"),
        name="fox_attention",
    )(proj_b, proj_b, proj_b, c4)


def _cross_kernel(q_ref, k_ref, v_ref, x_ref, wo_hbm, gn_ref, wr_ref, br_ref,
                  xo_ref, hp_ref, lg_ref, wo_b, stg, sem, *, layer, dh):
    @pl.when((pl.program_id(0) == 0) & (pl.program_id(1) == 0))
    def _():
        _load_weight(wo_hbm, layer, wo_b, stg, sem)

    nt = (((1,), (1,)), ((), ()))
    heads = []
    for h in range(CROSS_HEADS):
        sl = slice(h * dh, (h + 1) * dh)
        s = lax.dot_general(q_ref[0, :, sl], k_ref[0, :, sl], nt, preferred_element_type=F32)
        m = jnp.max(s, axis=-1, keepdims=True)
        p = jnp.exp(s - m)
        l = jnp.sum(p, axis=-1, keepdims=True)
        o = jnp.dot(p.astype(BF16), v_ref[0, :, sl], preferred_element_type=F32)
        heads.append((o / l).astype(BF16))
    xn = x_ref[0] + jnp.dot(jnp.concatenate(heads, axis=1), wo_b[...], preferred_element_type=F32)
    xo_ref[0] = xn
    y = _rms(xn) * gn_ref[...]
    _store_token_tiles(hp_ref, y, y.shape[0])
    yh = y.astype(BF16)
    yl = (y - yh.astype(F32)).astype(BF16)
    wr = wr_ref[...]
    wh = wr.astype(BF16)
    wl = (wr - wh.astype(F32)).astype(BF16)
    lg_ref[...] = (jnp.dot(yh, wh, preferred_element_type=F32)
                   + jnp.dot(yl, wh, preferred_element_type=F32)
                   + jnp.dot(yh, wl, preferred_element_type=F32)) + br_ref[...]


def cross_attention_ffn_in(qn, kn, v, x, w_co, layer, g_ffn, wr, br):
    B, S, D = qn.shape
    M = kn.shape[1]
    ts = CROSS_TS
    ns = S // ts
    assert D == TOKEN_ROWS * TOKEN_LANES
    tile = pl.BlockSpec((1, ts, D), lambda b, i: (b, i, 0))
    memb = pl.BlockSpec((1, M, D), lambda b, i: (b, 0, 0))
    const = lambda shape: pl.BlockSpec(shape, lambda b, i: (0, 0))
    return pl.pallas_call(
        functools.partial(_cross_kernel, layer=layer, dh=D // CROSS_HEADS),
        grid=(B, ns),
        in_specs=[tile, memb, memb, tile, pl.BlockSpec(memory_space=pl.ANY),
                  const((1, D)), const((D, LANES)), const((1, LANES))],
        out_specs=[tile,
                   pl.BlockSpec((ts * TOKEN_ROWS, TOKEN_LANES), lambda b, i: (b * ns + i, 0)),
                   pl.BlockSpec((ts, LANES), lambda b, i: (b * ns + i, 0))],
        out_shape=[jax.ShapeDtypeStruct((B, S, D), F32),
                   jax.ShapeDtypeStruct((B * S * TOKEN_ROWS, TOKEN_LANES), F32),
                   jax.ShapeDtypeStruct((B * S, LANES), F32)],
        scratch_shapes=[pltpu.VMEM((D, D), BF16), pltpu.VMEM((2, W_CHUNK, D), F32),
                        pltpu.SemaphoreType.DMA((2,))],
        compiler_params=_params("arbitrary", "arbitrary"),
        name="cross_attention_ffn_in",
    )(qn, kn, v, x, w_co, g_ffn.reshape(1, D), wr, br.reshape(1, LANES))


def _route_kernel(lg_ref, tri_ref, meta_ref, cnt_ref, carry_ref):
    @pl.when(pl.program_id(0) == 0)
    def _():
        carry_ref[...] = jnp.zeros_like(carry_ref)

    x = lg_ref[...]
    lane = lax.broadcasted_iota(jnp.int32, x.shape, 1)
    ninf = -jnp.inf
    is_g = lane < MOE_GROUPS
    gl = jnp.where(is_g, x, ninf)
    gmax = jnp.max(gl, axis=-1, keepdims=True)
    gidx = jnp.min(jnp.where(gl == gmax, lane, LANES), axis=-1, keepdims=True)
    gw = 1.0 / jnp.sum(jnp.where(is_g, jnp.exp(x - gmax), 0.0), axis=-1, keepdims=True)
    e_lane = lane - ROUTER_LANE0
    in_grp = (e_lane >= 0) & (e_lane < N_EXPERTS) & ((e_lane >> 3) == gidx)
    el = jnp.where(in_grp, x, ninf)
    m1 = jnp.max(el, axis=-1, keepdims=True)
    i1 = jnp.min(jnp.where(el == m1, lane, LANES), axis=-1, keepdims=True)
    el2 = jnp.where(lane == i1, ninf, el)
    m2 = jnp.max(el2, axis=-1, keepdims=True)
    i2 = jnp.min(jnp.where(el2 == m2, lane, LANES), axis=-1, keepdims=True)
    r = jnp.exp(m2 - m1)
    p1 = 1.0 / (1.0 + r)
    p2 = r * p1
    oh = ((lane == i1) | (lane == i2)).astype(F32)
    prefix = jnp.dot(tri_ref[...], oh.astype(BF16), preferred_element_type=F32) + carry_ref[...]
    rank1 = jnp.sum(jnp.where(lane == i1, prefix, 0.0), axis=-1, keepdims=True)
    rank2 = jnp.sum(jnp.where(lane == i2, prefix, 0.0), axis=-1, keepdims=True)
    carry_ref[...] += jnp.sum(oh, axis=0, keepdims=True)
    cnt_ref[...] = carry_ref[...]
    cols = [i1.astype(F32), i2.astype(F32), gw * p1, gw * p2, rank1, rank2]
    meta = jnp.zeros(x.shape, F32)
    for k, col in enumerate(cols):
        meta = jnp.where(lane == k, col, meta)
    meta_ref[...] = meta


def route(logits):
    T = logits.shape[0]
    tt = ROUTE_ROWS
    tri = jnp.asarray(np.tril(np.ones((tt, tt), np.float32), -1), BF16)
    return pl.pallas_call(
        _route_kernel,
        grid=(T // tt,),
        in_specs=[pl.BlockSpec((tt, LANES), lambda i: (i, 0)),
                  pl.BlockSpec((tt, tt), lambda i: (0, 0))],
        out_specs=[pl.BlockSpec((tt, LANES), lambda i: (i, 0)),
                   pl.BlockSpec((1, LANES), lambda i: (0, 0))],
        out_shape=[jax.ShapeDtypeStruct((T, LANES), F32),
                   jax.ShapeDtypeStruct((1, LANES), F32)],
        scratch_shapes=[pltpu.VMEM((1, LANES), F32)],
        compiler_params=_params("arbitrary"),
        name="moe_route",
    )(logits, tri)


def _plan_kernel(meta_ref, cnt_ref, d_ref):
    cnt = jnp.broadcast_to(cnt_ref[...], (SUBLANES, LANES)).astype(I32)
    pad = (((cnt + (MOE_ROWS - 1)) >> MOE_ROWS_LOG2) << MOE_ROWS_LOG2).astype(F32)
    lane8 = lax.broadcasted_iota(I32, pad.shape, 1)
    incl = pad
    sh = 1
    while sh < LANES:
        incl = incl + jnp.where(lane8 >= sh, pltpu.roll(incl, sh, axis=1), 0.0)
        sh *= 2
    off = (incl - pad)[0:1, :]
    m = meta_ref[...]
    lane = lax.broadcasted_iota(I32, m.shape, 1)
    i1 = m[:, 0:1].astype(I32)
    i2 = m[:, 1:2].astype(I32)
    d1 = jnp.sum(jnp.where(lane == i1, off, 0.0), axis=-1, keepdims=True) + m[:, 4:5]
    d2 = jnp.sum(jnp.where(lane == i2, off, 0.0), axis=-1, keepdims=True) + m[:, 5:6]
    dm = jnp.where(lane == 0, d1, jnp.where(lane == 1, d2, 0.0))
    d_ref[...] = jnp.transpose(dm)[0:2, :].astype(I32)


def plan(meta, cnt):
    T = meta.shape[0]
    tt = ROUTE_ROWS
    return pl.pallas_call(
        _plan_kernel,
        grid=(T // tt,),
        in_specs=[pl.BlockSpec((tt, LANES), lambda i: (i, 0)),
                  pl.BlockSpec((1, LANES), lambda i: (0, 0))],
        out_specs=pl.BlockSpec((2, tt), lambda i: (0, i)),
        out_shape=jax.ShapeDtypeStruct((2, T), I32),
        compiler_params=_params("parallel"),
        name="moe_plan",
    )(meta, cnt)


def _invert_kernel(d1_ref, d2_ref, src_ref, *, T, R):
    def init(r, _):
        src_ref[r] = 0
        return 0
    lax.fori_loop(0, R, init, 0, unroll=32)

    def body(t, _):
        src_ref[d1_ref[t]] = t
        src_ref[d2_ref[t]] = t
        return 0
    lax.fori_loop(0, T, body, 0, unroll=8)


def invert(d1, d2, R):
    T = d1.shape[0]
    grid_spec = pltpu.PrefetchScalarGridSpec(
        num_scalar_prefetch=2, grid=(1,), in_specs=[],
        out_specs=pl.BlockSpec(memory_space=pltpu.SMEM))
    return pl.pallas_call(
        functools.partial(_invert_kernel, T=T, R=R),
        grid_spec=grid_spec,
        out_shape=jax.ShapeDtypeStruct((R,), I32),
        compiler_params=_params("arbitrary"),
        name="moe_invert",
    )(d1, d2)


GATHER_CHUNK = 8


def _tile_copy(src_hbm, row, dst, slot, r, sem):
    return pltpu.make_async_copy(src_hbm.at[pl.ds(row * TOKEN_ROWS, TOKEN_ROWS)],
                                 dst.at[slot, pl.ds(r * TOKEN_ROWS, TOKEN_ROWS)], sem.at[slot])


def _expert_kernel(te_ref, first_ref, nxt_ref, wsl_ref, nch_ref, src_ref,
                   hp_hbm, w1_hbm, w3_hbm, w2_hbm, o_ref,
                   xbuf, r1, r3, r2, w1b, w3b, w2b, gsem, wsem, *, tr, base):
    i = pl.program_id(0)
    n = pl.num_programs(0)
    slot = i % 2

    def weight_copies(e, ws):
        return [pltpu.make_async_copy(w1_hbm.at[base + e], r1.at[ws], wsem.at[ws, 0]),
                pltpu.make_async_copy(w3_hbm.at[base + e], r3.at[ws], wsem.at[ws, 1]),
                pltpu.make_async_copy(w2_hbm.at[base + e], r2.at[ws], wsem.at[ws, 2])]

    def issue_rows(tile, slot, c0, c1):
        def body(c, _):
            for u in range(GATHER_CHUNK):
                r = c * GATHER_CHUNK + u
                _tile_copy(hp_hbm, src_ref[tile * tr + r], xbuf, slot, r, gsem).start()
            return 0
        lax.fori_loop(c0, c1, body, 0)

    def wait_rows(tile, slot):
        rows = GATHER_CHUNK * TOKEN_ROWS
        def body(c, _):
            pltpu.make_async_copy(hp_hbm.at[pl.ds(0, rows)], xbuf.at[slot, pl.ds(0, rows)],
                                  gsem.at[slot]).wait()
            return 0
        lax.fori_loop(0, nch_ref[tile], body, 0)

    @pl.when(i == 0)
    def _():
        xbuf[...] = jnp.zeros_like(xbuf)
        for cp in weight_copies(te_ref[0], 0):
            cp.start(priority=1)
        issue_rows(0, 0, 0, nch_ref[0])

    wait_rows(i, slot)

    nxt_tile = jnp.minimum(i + 1, n - 1)
    issue_rows(nxt_tile, 1 - slot, 0, jnp.where(i + 1 < n, nch_ref[nxt_tile], 0))

    @pl.when(first_ref[i] == 1)
    def _():
        ws = wsl_ref[i]
        for cp in weight_copies(te_ref[i], ws):
            cp.wait()

        @pl.when(nxt_ref[i] >= 0)
        def _():
            for cp in weight_copies(nxt_ref[i], 1 - ws):
                cp.start(priority=1)

        w1b[...] = r1[ws].astype(BF16)
        w3b[...] = r3[ws].astype(BF16)
        w2b[...] = r2[ws].astype(BF16)

    def expert_mlp(rows):
        x = jnp.concatenate([p.astype(BF16) for p in _load_token_tiles(xbuf.at[slot], 0, rows)],
                            axis=1)
        h1 = jnp.dot(x, w1b[...], preferred_element_type=F32)
        h3 = jnp.dot(x, w3b[...], preferred_element_type=F32)
        hm = (h1 * _sigmoid(h1) * h3).astype(BF16)
        _store_token_tiles(o_ref, jnp.dot(hm, w2b[...], preferred_element_type=F32), rows)

    half_chunks = tr // (2 * GATHER_CHUNK)

    @pl.when(nch_ref[i] > half_chunks)
    def _():
        expert_mlp(tr)

    @pl.when((nch_ref[i] > 0) & (nch_ref[i] <= half_chunks))
    def _():
        expert_mlp(tr // 2)
        o_ref[tr // 2 * TOKEN_ROWS:, :] = jnp.zeros((tr // 2 * TOKEN_ROWS, TOKEN_LANES), F32)

    @pl.when(nch_ref[i] == 0)
    def _():
        o_ref[...] = jnp.zeros_like(o_ref)


def moe_experts(hp, w1, w3, w2, layer, tables, src):
    D, F = w1.shape[-2:]
    tr = MOE_ROWS
    te, first, nxt, wsl, nch = tables
    NT = te.shape[0]
    w1f = w1.reshape(-1, D, F)
    w3f = w3.reshape(-1, D, F)
    w2f = w2.reshape(-1, F, D)
    hbm = pl.BlockSpec(memory_space=pl.ANY)
    grid_spec = pltpu.PrefetchScalarGridSpec(
        num_scalar_prefetch=6,
        grid=(NT,),
        in_specs=[hbm, hbm, hbm, hbm],
        out_specs=pl.BlockSpec((tr * TOKEN_ROWS, TOKEN_LANES), lambda i, *_: (i, 0)),
        scratch_shapes=[pltpu.VMEM((2, tr * TOKEN_ROWS, TOKEN_LANES), F32),
                        pltpu.VMEM((2, D, F), F32),
                        pltpu.VMEM((2, D, F), F32),
                        pltpu.VMEM((2, F, D), F32),
                        pltpu.VMEM((D, F), BF16),
                        pltpu.VMEM((D, F), BF16),
                        pltpu.VMEM((F, D), BF16),
                        pltpu.SemaphoreType.DMA((2,)),
                        pltpu.SemaphoreType.DMA((2, 3))],
    )
    return pl.pallas_call(
        functools.partial(_expert_kernel, tr=tr, base=layer * N_EXPERTS),
        grid_spec=grid_spec,
        out_shape=jax.ShapeDtypeStruct((NT * tr * TOKEN_ROWS, TOKEN_LANES), F32),
        compiler_params=_params("arbitrary"),
        name="moe_experts",
    )(te, first, nxt, wsl, nch, src, hp, w1f, w3f, w2f)


def _combine_kernel(d1_ref, d2_ref, x_ref, meta_ref, ys_hbm, *rest, tt, with_norm):
    if with_norm:
        g_ref, o_ref, hn_ref, buf, sem = rest
    else:
        o_ref, buf, sem = rest
    i = pl.program_id(0)
    n = pl.num_programs(0)
    slot = i % 2

    def issue(tile, slot):
        def body(c, _):
            for u in range(GATHER_CHUNK):
                r = c * GATHER_CHUNK + u
                t = tile * tt + r
                _tile_copy(ys_hbm, d1_ref[t], buf, slot, r, sem).start(priority=0)
                _tile_copy(ys_hbm, d2_ref[t], buf, slot, tt + r, sem).start(priority=1)
            return 0
        lax.fori_loop(0, tt // GATHER_CHUNK, body, 0)

    @pl.when(i == 0)
    def _():
        issue(0, 0)

    pltpu.make_async_copy(ys_hbm.at[pl.ds(0, 2 * tt * TOKEN_ROWS)], buf.at[slot], sem.at[slot]).wait()

    @pl.when(i + 1 < n)
    def _():
        issue(i + 1, 1 - slot)

    w1 = meta_ref[:, 2:3]
    w2 = meta_ref[:, 3:4]
    pa = _load_token_tiles(buf.at[slot], 0, tt)
    pb = _load_token_tiles(buf.at[slot], tt * TOKEN_ROWS, tt)
    cols = [x_ref[:, j * TOKEN_LANES:(j + 1) * TOKEN_LANES] + w1 * pa[j] + w2 * pb[j]
            for j in range(TOKEN_ROWS)]
    xn = jnp.concatenate(cols, axis=1)
    o_ref[...] = xn
    if with_norm:
        hn_ref[...] = (_rms(xn) * g_ref[...]).astype(hn_ref.dtype)


def moe_combine(x, meta, ys, d1, d2, g_next=None):
    T, D = x.shape
    tt = COMB_ROWS
    with_norm = g_next is not None
    row = pl.BlockSpec((tt, D), lambda i, d1, d2: (i, 0))
    in_specs = [row, pl.BlockSpec((tt, LANES), lambda i, d1, d2: (i, 0)),
                pl.BlockSpec(memory_space=pl.ANY)]
    args = [d1, d2, x, meta, ys]
    out_specs, out_shape = row, jax.ShapeDtypeStruct((T, D), F32)
    if with_norm:
        in_specs.append(pl.BlockSpec((1, D), lambda i, d1, d2: (0, 0)))
        args.append(g_next.reshape(1, D))
        out_specs, out_shape = [row, row], [out_shape, jax.ShapeDtypeStruct((T, D), BF16)]
    grid_spec = pltpu.PrefetchScalarGridSpec(
        num_scalar_prefetch=2,
        grid=(T // tt,),
        in_specs=in_specs,
        out_specs=out_specs,
        scratch_shapes=[pltpu.VMEM((2, 2 * tt * TOKEN_ROWS, TOKEN_LANES), F32),
                        pltpu.SemaphoreType.DMA((2,))],
    )
    return pl.pallas_call(
        functools.partial(_combine_kernel, tt=tt, with_norm=with_norm),
        grid_spec=grid_spec,
        out_shape=out_shape,
        compiler_params=_params("arbitrary"),
        name="moe_combine",
    )(*args)


def _tile_tables(cnt, T):
    E = N_EXPERTS
    counts = cnt[0, ROUTER_LANE0:ROUTER_LANE0 + E].astype(I32)
    tiles_e = (counts + MOE_ROWS - 1) // MOE_ROWS
    tile_end = jnp.cumsum(tiles_e)
    tile_start = tile_end - tiles_e
    nact = tile_end[-1]
    NT = (2 * T) // MOE_ROWS + E
    tid = jnp.arange(NT, dtype=I32)
    te = jnp.sum((jnp.minimum(tid, nact - 1)[:, None] >= tile_end[None, :]).astype(I32), axis=1)
    te = jnp.minimum(te, E - 1)
    active = tid < nact
    first = (active & (tid == tile_start[te])).astype(I32)
    eid = jnp.arange(E, dtype=I32)
    later = (eid[None, :] > eid[:, None]) & (tiles_e[None, :] > 0)
    nxt_e = jnp.min(jnp.where(later, eid[None, :], E), axis=1)
    nxt_e = jnp.where(nxt_e == E, -1, nxt_e)
    ordinal = jnp.cumsum((tiles_e > 0).astype(I32)) - 1
    valid = jnp.clip(counts[te] - (tid - tile_start[te]) * MOE_ROWS, 0, MOE_ROWS)
    nch = jnp.where(active, (valid + GATHER_CHUNK - 1) // GATHER_CHUNK, 0)
    return (te, first, nxt_e[te], ordinal[te] % 2, nch), NT * MOE_ROWS


def _lower_bound_logs(lb_param, layer):
    p = jax.nn.softmax(lb_param.astype(F32), axis=0)
    c = jnp.cumsum(p, axis=0)
    lb = c[layer] - c[0]
    return jnp.log(lb), jnp.log1p(-lb)


def kernel(x, mem, norm_mix, w_in, hg_lower_bounds, hg_norm, conv_w, fox_f_bias, fox_q_norm,
           fox_k_norm, w_out, norm_cross, norm_mem, w_cq, w_ck, w_cv, w_co, cross_q_norm,
           cross_k_norm, norm_ffn, router_group_w, router_group_b, router_expert_w,
           router_expert_b, moe_w1, moe_w3, moe_w2):
    B, S, D = x.shape
    M = mem.shape[1]
    L = w_in.shape[0]
    T = B * S
    HGW = 4 * HG_WIDTH
    CONV3 = 3 * CONV_WIDTH
    FQ0 = CONV3
    FK0 = FQ0 + FOX_WIDTH
    FV0 = FK0 + FOX_WIDTH
    PB = CONV3 + 3 * FOX_WIDTH
    dhc = D // CROSS_HEADS
    x2 = x.reshape(T, D)
    mem2 = mem.reshape(B * M, D)
    w_in_t = jnp.swapaxes(w_in, 1, 2)
    hn = rmsnorm(x2, norm_mix[0])
    for l in range(L):
        proj_a = matmul([hn], w_in_t, l, 0, HGW, F32, w_t=True)
        proj_b = matmul([hn], w_in_t, l, HGW, PB, BF16, w_t=True, head_norms=(
            (FQ0, FK0, fox_q_norm[l], FOX_DIM ** -0.5), (FK0, FV0, fox_k_norm[l], 1.0)))
        loglb, l1mlb = _lower_bound_logs(hg_lower_bounds, l)
        y_hg = hgrn2(proj_a.reshape(B, S, HGW), loglb, l1mlb, hg_norm[l])
        pb3 = proj_b.reshape(B, S, PB)
        y_conv = short_conv(pb3, conv_w[l])
        c = fox_gate(hn.reshape(B, S, D), w_in_t, l, HGW + PB, fox_f_bias[l])
        y_fox = fox_attention(pb3, FQ0, FK0, FV0, c)
        mix = [y_hg.reshape(T, HG_WIDTH), y_conv.reshape(T, CONV_WIDTH), y_fox.reshape(T, FOX_WIDTH)]
        x2, qcn = out_proj_cross_q(mix, x2, w_out, w_cq, l, norm_cross[l], cross_q_norm[l])
        memn = rmsnorm(mem2, norm_mem[l])
        kcn = matmul([memn], w_ck, l, 0, D, BF16, head_norms=((0, D, cross_k_norm[l], 1.0),))
        vc = matmul([memn], w_cv, l, 0, D, BF16)
        wr = jnp.concatenate([router_group_w[l], router_expert_w[l]], axis=1)
        wr = jnp.pad(wr, ((0, 0), (0, LANES - wr.shape[1])))
        br = jnp.concatenate([router_group_b[l], router_expert_b[l]])
        br = jnp.pad(br, (0, LANES - br.shape[0]))
        x3, hp, logits = cross_attention_ffn_in(
            qcn.reshape(B, S, D), kcn.reshape(B, M, D), vc.reshape(B, M, D), x2.reshape(B, S, D),
            w_co, l, norm_ffn[l], wr, br)
        x2 = x3.reshape(T, D)
        meta, cnt = route(logits)
        d = plan(meta, cnt)
        tables, R = _tile_tables(cnt, T)
        src = invert(d[0], d[1], R)
        ys = moe_experts(hp, moe_w1, moe_w3, moe_w2, l, tables, src)
        if l + 1 < L:
            x2, hn = moe_combine(x2, meta, ys, d[0], d[1], norm_mix[l + 1])
        else:
            x2 = moe_combine(x2, meta, ys, d[0], d[1])
    return x2.reshape(B, S, D)
```

```python
import functools

import numpy as np
import jax
import jax.numpy as jnp
from jax import lax
from jax.experimental import pallas as pl
from jax.experimental.pallas import tpu as pltpu

F32 = jnp.float32
BF16 = jnp.bfloat16
I32 = jnp.int32
EPS = 1e-6

HG_HEADS = 4
HG_DIM = 128
HG_WIDTH = HG_HEADS * HG_DIM
CONV_WIDTH = 512
FOX_HEADS = 8
FOX_DIM = 128
FOX_WIDTH = FOX_HEADS * FOX_DIM
CROSS_HEADS = 4
MOE_GROUPS = 4
MOE_EXPERTS = 8
N_EXPERTS = MOE_GROUPS * MOE_EXPERTS
ROUTER_LANE0 = MOE_GROUPS

LANES = 128
SUBLANES = 8
VMEM_LIMIT = 56 * 1024 * 1024

NORM_ROWS = 256
MM_TM = 2048
MM_TN = 512
HG_CHUNK = 128
FOX_TQ = 512
FOX_TK = 512
FOX_HEADS_PER_STEP = 8
CROSS_TS = 512
ROUTE_ROWS = 512
MOE_ROWS_LOG2 = 8
MOE_ROWS = 1 << MOE_ROWS_LOG2
COMB_ROWS = 256


def _params(*sem):
    return pltpu.CompilerParams(dimension_semantics=sem, vmem_limit_bytes=VMEM_LIMIT)


def _sigmoid(x):
    return 1.0 / (1.0 + jnp.exp(-x))


def _log_sigmoid(x):
    return jnp.minimum(x, 0.0) - jnp.log1p(jnp.exp(-jnp.abs(x)))


TOKEN_ROWS = 16
TOKEN_LANES = 128


def _store_token_tiles(ref, y, rows):
    for j in range(TOKEN_ROWS):
        ref[pl.ds(j, rows, stride=TOKEN_ROWS), :] = y[:, j * TOKEN_LANES:(j + 1) * TOKEN_LANES]


def _load_token_tiles(ref, base, rows):
    return [ref[pl.ds(base + j, rows, stride=TOKEN_ROWS), :] for j in range(TOKEN_ROWS)]


def _rms(x):
    return x * lax.rsqrt(jnp.mean(x * x, axis=-1, keepdims=True) + EPS)


def _rmsnorm_kernel(x_ref, g_ref, o_ref):
    o_ref[...] = (_rms(x_ref[...]) * g_ref[...]).astype(o_ref.dtype)


def rmsnorm(x, g, out_dtype=BF16):
    R, D = x.shape
    return pl.pallas_call(
        _rmsnorm_kernel,
        grid=(R // NORM_ROWS,),
        in_specs=[pl.BlockSpec((NORM_ROWS, D), lambda i: (i, 0)),
                  pl.BlockSpec((1, D), lambda i: (0, 0))],
        out_specs=pl.BlockSpec((NORM_ROWS, D), lambda i: (i, 0)),
        out_shape=jax.ShapeDtypeStruct((R, D), out_dtype),
        compiler_params=_params("parallel"),
        name="rmsnorm",
    )(x, g.reshape(1, D))


def _head_rmsnorm(x, g, dh, scale):
    heads = [_rms(x[:, h * dh:(h + 1) * dh]) * g * scale for h in range(x.shape[1] // dh)]
    return heads[0] if len(heads) == 1 else jnp.concatenate(heads, axis=1)


def _matmul_kernel(*refs, n_parts, w_t, norm_tiles, dh):
    a_refs = refs[:n_parts]
    w_ref = refs[n_parts]
    if norm_tiles:
        g_ref, o_ref, wb_ref = refs[n_parts + 1:]
    else:
        o_ref, wb_ref = refs[n_parts + 1:]
    j = pl.program_id(0)

    @pl.when(pl.program_id(1) == 0)
    def _():
        wb_ref[...] = w_ref[...].astype(BF16)

    if n_parts == 1:
        a = a_refs[0][...]
    else:
        a = jnp.concatenate([r[...] for r in a_refs], axis=1)
    if w_t:
        acc = lax.dot_general(a, wb_ref[...], (((1,), (1,)), ((), ())), preferred_element_type=F32)
    else:
        acc = jnp.dot(a, wb_ref[...], preferred_element_type=F32)
    if not norm_tiles:
        o_ref[...] = acc.astype(o_ref.dtype)
        return
    plain = True
    for k, (j0, j1, scale) in enumerate(norm_tiles):
        hit = (j >= j0) & (j < j1)
        plain = plain & jnp.logical_not(hit)

        @pl.when(hit)
        def _(k=k, scale=scale):
            o_ref[...] = _head_rmsnorm(acc, g_ref[k:k + 1, :], dh, scale).astype(o_ref.dtype)

    @pl.when(plain)
    def _():
        o_ref[...] = acc.astype(o_ref.dtype)


def matmul(a_parts, w, layer, col0, n, out_dtype, w_t=False, head_norms=(), rotate=0,
           tm=MM_TM, tn=MM_TN):
    M = a_parts[0].shape[0]
    K = w.shape[2] if w_t else w.shape[1]
    assert sum(p.shape[1] for p in a_parts) == K
    tm = min(tm, M)
    assert col0 % tn == 0 and n % tn == 0 and M % tm == 0 and rotate % tn == 0
    cb, nt, rot = col0 // tn, n // tn, rotate // tn
    in_specs = [pl.BlockSpec((tm, p.shape[1]), lambda j, i: (i, 0)) for p in a_parts]
    if w_t:
        in_specs.append(pl.BlockSpec((None, tn, K), lambda j, i: (layer, cb + (j + rot) % nt, 0)))
    else:
        in_specs.append(pl.BlockSpec((None, K, tn), lambda j, i: (layer, 0, cb + (j + rot) % nt)))
    args = list(a_parts) + [w]
    norm_tiles, dh = (), 0
    if head_norms:
        dh = head_norms[0][2].shape[0]
        assert all(c0 % tn == 0 and c1 % tn == 0 and g.shape[0] == dh and tn % dh == 0
                   for c0, c1, g, _ in head_norms)
        norm_tiles = tuple((c0 // tn, c1 // tn, s) for c0, c1, _, s in head_norms)
        gains = jnp.stack([g for _, _, g, _ in head_norms])
        in_specs.append(pl.BlockSpec(gains.shape, lambda j, i: (0, 0)))
        args.append(gains)
    return pl.pallas_call(
        functools.partial(_matmul_kernel, n_parts=len(a_parts), w_t=w_t, norm_tiles=norm_tiles,
                          dh=dh),
        grid=(n // tn, M // tm),
        in_specs=in_specs,
        out_specs=pl.BlockSpec((tm, tn), lambda j, i: (i, j)),
        out_shape=jax.ShapeDtypeStruct((M, n), out_dtype),
        scratch_shapes=[pltpu.VMEM((tn, K) if w_t else (K, tn), BF16)],
        compiler_params=_params("arbitrary", "arbitrary"),
        name="matmul",
    )(*args)


W_CHUNK = 256


def _load_weight(w_hbm, layer, wb, stg, sem):
    n_chunks = wb.shape[0] // W_CHUNK

    def copy(c):
        return pltpu.make_async_copy(w_hbm.at[layer, pl.ds(c * W_CHUNK, W_CHUNK)], stg.at[c % 2],
                                     sem.at[c % 2])
    copy(0).start()
    for c in range(n_chunks):
        if c + 1 < n_chunks:
            copy(c + 1).start()
        copy(c).wait()
        wb[c * W_CHUNK:(c + 1) * W_CHUNK, :] = stg[c % 2].astype(BF16)


def _outq_kernel(a1_ref, a2_ref, a3_ref, x_ref, wo_hbm, wq_hbm, gn_ref, gq_ref, xo_ref, q_ref,
                 wo_b, wq_b, stg, sem, *, layer, dh, scale):
    @pl.when(pl.program_id(0) == 0)
    def _():
        _load_weight(wo_hbm, layer, wo_b, stg, sem)
        _load_weight(wq_hbm, layer, wq_b, stg, sem)

    a = jnp.concatenate([a1_ref[...], a2_ref[...], a3_ref[...]], axis=1)
    xn = x_ref[...] + jnp.dot(a, wo_b[...], preferred_element_type=F32)
    xo_ref[...] = xn
    hc = (_rms(xn) * gn_ref[...]).astype(BF16)
    q = jnp.dot(hc, wq_b[...], preferred_element_type=F32)
    q_ref[...] = _head_rmsnorm(q, gq_ref[...], dh, scale).astype(q_ref.dtype)


def out_proj_cross_q(mix, x, w_out, w_cq, layer, g_cross, g_q, tm=512):
    T, D = x.shape
    dh = g_q.shape[0]
    row = lambda w: pl.BlockSpec((tm, w), lambda i: (i, 0))
    hbm = pl.BlockSpec(memory_space=pl.ANY)
    return pl.pallas_call(
        functools.partial(_outq_kernel, layer=layer, dh=dh, scale=dh ** -0.5),
        grid=(T // tm,),
        in_specs=[row(mix[0].shape[1]), row(mix[1].shape[1]), row(mix[2].shape[1]), row(D), hbm, hbm,
                  pl.BlockSpec((1, D), lambda i: (0, 0)), pl.BlockSpec((1, dh), lambda i: (0, 0))],
        out_specs=[row(D), row(D)],
        out_shape=[jax.ShapeDtypeStruct((T, D), F32), jax.ShapeDtypeStruct((T, D), BF16)],
        scratch_shapes=[pltpu.VMEM((D, D), BF16), pltpu.VMEM((D, D), BF16),
                        pltpu.VMEM((2, W_CHUNK, D), F32), pltpu.SemaphoreType.DMA((2,))],
        compiler_params=_params("arbitrary"),
        name="out_proj_cross_q",
    )(*mix, x, w_out, w_cq, g_cross.reshape(1, D), g_q.reshape(1, dh))


def _hgrn_consts(C):
    nlev = int(np.log2(C))
    t = np.arange(C)[:, None]
    u = np.arange(C)[None, :]
    tri = (u <= t).astype(np.float32)
    mall2 = np.concatenate([tri, tri], axis=1)
    tt = np.arange(C)[:, None]
    ss = np.arange(C)[None, :]
    lev = np.full((C, C), nlev + 1, np.int32)
    x = tt ^ ss
    hb = np.zeros_like(x)
    for j in range(nlev):
        hb = np.where((x >> j) & 1, j, hb)
    lev = np.where(tt > ss, hb, lev)
    lev = np.where(tt == ss, nlev, lev)
    return jnp.asarray(mall2, BF16), jnp.asarray(lev, jnp.int32)


def _level_sums(lf, b, row, j, C):
    m = 1 << j
    second = (row & m) != 0
    if j == 0:
        return jnp.where(second, lf, 0.0)
    if j == 1:
        prev = pltpu.roll(lf, 1, axis=0)
        nxt = pltpu.roll(lf, C - 1, axis=0)
        first = jnp.where((row & 1) == 0, nxt, 0.0)
        return jnp.where(second, jnp.where((row & 1) != 0, lf + prev, lf), first)
    g = 2 * m
    d = b.shape[1]
    bsel = jnp.broadcast_to(b.reshape(C // g, g, d)[:, m - 1:m, :], (C // g, g, d)).reshape(C, d)
    return jnp.where(second, b - bsel, bsel - b)


def _hgrn_kernel(x_ref, mall_ref, lev_ref, loglb_ref, l1mlb_ref, ng_ref, o_ref, st_ref, *, C, nlev):
    @pl.when(pl.program_id(1) == 0)
    def _():
        st_ref[...] = jnp.zeros_like(st_ref)

    W = HG_WIDTH
    d = HG_DIM
    lev = lev_ref[...]
    mall = mall_ref[...]
    row = lax.broadcasted_iota(jnp.int32, (C, d), 0)
    nt = (((1,), (1,)), ((), ()))

    def side(xs):
        return jnp.concatenate(xs, axis=1)

    def block_diag(xa, xb):
        return jnp.concatenate([side([xa, jnp.zeros_like(xb)]), side([jnp.zeros_like(xa), xb])],
                               axis=0)

    for h0 in range(0, HG_HEADS, 2):
        q, lf, kk, vb, g = [], [], [], [], []
        for h in (h0, h0 + 1):
            sl = slice(h * d, (h + 1) * d)
            q.append(x_ref[0, :, h * d:(h + 1) * d])
            z = x_ref[0, :, W + h * d:W + (h + 1) * d]
            vb.append(x_ref[0, :, 2 * W + h * d:2 * W + (h + 1) * d].astype(BF16))
            g.append(x_ref[0, :, 3 * W + h * d:3 * W + (h + 1) * d])
            a = loglb_ref[:, sl]
            bt = l1mlb_ref[:, sl] + _log_sigmoid(z)
            lf.append(jnp.maximum(a, bt) + jnp.log1p(jnp.exp(-jnp.abs(a - bt))))
            kk.append(1.0 - jnp.exp(lf[-1]))
        hi = [x.astype(BF16) for x in lf]
        lo = [(x - y.astype(F32)).astype(BF16) for x, y in zip(lf, hi)]
        b2 = jnp.dot(mall, jnp.concatenate([side(hi), side(lo)], axis=0),
                     preferred_element_type=F32)
        b = [b2[:, :d], b2[:, d:]]
        st = [st_ref[h0], st_ref[h0 + 1]]
        o2 = lax.dot_general(side([(q[i] * jnp.exp(b[i])).astype(BF16) for i in (0, 1)]),
                             block_diag(st[0].astype(BF16), st[1].astype(BF16)), nt,
                             preferred_element_type=F32)
        att = [jnp.where(lev == nlev, jnp.sum(q[i] * kk[i], axis=-1, keepdims=True), 0.0)
               for i in (0, 1)]
        for j in range(nlev):
            second = (row & (1 << j)) != 0
            xq, xk = [], []
            for i in (0, 1):
                e = jnp.exp(_level_sums(lf[i], b[i], row, j, C))
                xq.append(jnp.where(second, q[i] * e, 0.0).astype(BF16))
                xk.append(jnp.where(second, 0.0, kk[i] * e).astype(BF16))
            am = lax.dot_general(side(xq), block_diag(xk[0], xk[1]), nt,
                                 preferred_element_type=F32)
            att = [jnp.where(lev == j, am[:, i * C:(i + 1) * C], att[i]) for i in (0, 1)]
        o2 = o2 + jnp.dot(side([x.astype(BF16) for x in att]), block_diag(vb[0], vb[1]),
                          preferred_element_type=F32)
        for i, h in enumerate((h0, h0 + 1)):
            bl = b[i][C - 1:C, :]
            kh = (kk[i] * jnp.exp(bl - b[i])).astype(BF16)
            st_ref[h] = st[i] * jnp.exp(bl) + lax.dot_general(
                vb[i], kh, (((0,), (0,)), ((), ())), preferred_element_type=F32)
            o = o2[:, i * d:(i + 1) * d]
            y = o * lax.rsqrt(jnp.mean(o * o, axis=-1, keepdims=True) + EPS) * ng_ref[...]
            o_ref[0, :, h * d:(h + 1) * d] = (y * (g[i] * _sigmoid(g[i]))).astype(o_ref.dtype)


def hgrn2(proj_a, loglb, l1mlb, norm_g):
    B, S, _ = proj_a.shape
    C = HG_CHUNK
    nlev = int(np.log2(C))
    mall, lev = _hgrn_consts(C)
    return pl.pallas_call(
        functools.partial(_hgrn_kernel, C=C, nlev=nlev),
        grid=(B, S // C),
        in_specs=[pl.BlockSpec((1, C, 4 * HG_WIDTH), lambda b, c: (b, c, 0)),
                  pl.BlockSpec(mall.shape, lambda b, c: (0, 0)),
                  pl.BlockSpec((C, C), lambda b, c: (0, 0)),
                  pl.BlockSpec((1, HG_WIDTH), lambda b, c: (0, 0)),
                  pl.BlockSpec((1, HG_WIDTH), lambda b, c: (0, 0)),
                  pl.BlockSpec((1, HG_DIM), lambda b, c: (0, 0))],
        out_specs=pl.BlockSpec((1, C, HG_WIDTH), lambda b, c: (b, c, 0)),
        out_shape=jax.ShapeDtypeStruct((B, S, HG_WIDTH), BF16),
        scratch_shapes=[pltpu.VMEM((HG_HEADS, HG_DIM, HG_DIM), F32)],
        compiler_params=_params("parallel", "arbitrary"),
        name="hgrn2",
    )(proj_a, mall, lev, loglb.reshape(1, HG_WIDTH), l1mlb.reshape(1, HG_WIDTH),
      norm_g.reshape(1, HG_DIM))


def _conv_kernel(b_ref, c_ref, h_ref, w_ref, o_ref):
    u = c_ref[0].astype(F32) * h_ref[0].astype(F32)
    row = lax.broadcasted_iota(jnp.int32, u.shape, 0)
    u1 = jnp.where(row >= 1, pltpu.roll(u, 1, axis=0), 0.0)
    u2 = jnp.where(row >= 2, pltpu.roll(u, 2, axis=0), 0.0)
    y = w_ref[0:1, :] * u2 + w_ref[1:2, :] * u1 + w_ref[2:3, :] * u
    o_ref[0] = (b_ref[0].astype(F32) * y).astype(o_ref.dtype)


def short_conv(proj_b, col0, w):
    B, S, _ = proj_b.shape
    cw = CONV_WIDTH
    assert col0 % cw == 0
    spec = lambda k: pl.BlockSpec((1, S, cw), lambda b: (b, 0, col0 // cw + k))
    return pl.pallas_call(
        _conv_kernel,
        grid=(B,),
        in_specs=[spec(0), spec(1), spec(2), pl.BlockSpec((3, cw), lambda b: (0, 0))],
        out_specs=pl.BlockSpec((1, S, cw), lambda b: (b, 0, 0)),
        out_shape=jax.ShapeDtypeStruct((B, S, cw), BF16),
        compiler_params=_params("parallel"),
        name="short_conv",
    )(proj_b, proj_b, proj_b, w)


def _fox_gate_kernel(hn_ref, wf_ref, bias_ref, c_ref):
    gl = lax.dot_general(wf_ref[...].astype(BF16), hn_ref[0], (((1,), (1,)), ((), ())),
                         preferred_element_type=F32)
    c = _log_sigmoid(gl + bias_ref[...])
    S = c.shape[1]
    lane = lax.broadcasted_iota(jnp.int32, c.shape, 1)
    sh = 1
    while sh < S:
        c = c + jnp.where(lane >= sh, pltpu.roll(c, sh, axis=1), 0.0)
        sh *= 2
    c_ref[0] = c


def fox_gate(hn, w_t, layer, row0, bias):
    B, S, D = hn.shape
    H = bias.shape[0]
    assert row0 % H == 0
    return pl.pallas_call(
        _fox_gate_kernel,
        grid=(B,),
        in_specs=[pl.BlockSpec((1, S, D), lambda b: (b, 0, 0)),
                  pl.BlockSpec((None, H, D), lambda b: (layer, row0 // H, 0)),
                  pl.BlockSpec((H, 1), lambda b: (0, 0))],
        out_specs=pl.BlockSpec((1, H, S), lambda b: (b, 0, 0)),
        out_shape=jax.ShapeDtypeStruct((B, H, S), F32),
        compiler_params=_params("parallel"),
        name="fox_gate",
    )(hn, w_t, bias.reshape(H, 1))


def _fox_kernel(q_ref, k_ref, v_ref, c_ref, o_ref, *, tq, tk, nh):
    qi = pl.program_id(2)
    nt = (((1,), (1,)), ((), ()))
    ones = jnp.ones((tk, FOX_DIM), BF16)
    heads = [slice(h * FOX_DIM, (h + 1) * FOX_DIM) for h in range(nh)]
    qs = [q_ref[0, :, sl] for sl in heads]

    def step(ki, carry, diag):
        out = []
        for h, sl in enumerate(heads):
            m, acc = carry[h]
            k = k_ref[0, pl.ds(ki * tk, tk), sl]
            v1 = jnp.concatenate([v_ref[0, pl.ds(ki * tk, tk), sl], ones], axis=1)
            ck = c_ref[0, h, pl.ds(ki, 1), :]
            s = lax.dot_general(qs[h], k, nt, preferred_element_type=F32) - ck
            if diag is not None:
                r = lax.broadcasted_iota(jnp.int32, s.shape, 0)
                c = lax.broadcasted_iota(jnp.int32, s.shape, 1)
                s = jnp.where(c + diag * tk <= r, s, -jnp.inf)
            m_new = jnp.maximum(m, jnp.max(s, axis=-1, keepdims=True))
            alpha = jnp.exp(m - m_new)
            p = jnp.exp((s - m_new).astype(BF16))
            out.append((m_new, alpha * acc + jnp.dot(p, v1, preferred_element_type=F32)))
        return tuple(out)

    init = tuple((jnp.full((tq, 1), -jnp.inf, F32), jnp.zeros((tq, 2 * FOX_DIM), F32))
                 for _ in heads)
    n_full = qi * (tq // tk)
    carry = lax.fori_loop(0, n_full, lambda ki, cr: step(ki, cr, None), init)
    for d in range(tq // tk):
        carry = step(n_full + d, carry, d)
    for h, sl in enumerate(heads):
        acc = carry[h][1]
        o_ref[0, :, sl] = (acc[:, :FOX_DIM] / acc[:, FOX_DIM:]).astype(o_ref.dtype)


def fox_attention(proj_b, q_col0, k_col0, v_col0, c):
    B, S, _ = proj_b.shape
    tq, tk, nh = FOX_TQ, FOX_TK, FOX_HEADS_PER_STEP
    H = FOX_HEADS
    w = nh * FOX_DIM
    assert tq % tk == 0 and H % nh == 0 and all(c0 % w == 0 for c0 in (q_col0, k_col0, v_col0))
    qb, kb, vb = q_col0 // w, k_col0 // w, v_col0 // w
    c4 = c.reshape(B, H, S // tk, tk)
    return pl.pallas_call(
        functools.partial(_fox_kernel, tq=tq, tk=tk, nh=nh),
        grid=(B, H // nh, S // tq),
        in_specs=[pl.BlockSpec((1, tq, w), lambda b, h, i: (b, i, qb + h)),
                  pl.BlockSpec((1, S, w), lambda b, h, i: (b, 0, kb + h)),
                  pl.BlockSpec((1, S, w), lambda b, h, i: (b, 0, vb + h)),
                  pl.BlockSpec((1, nh, S // tk, tk), lambda b, h, i: (b, h, 0, 0))],
        out_specs=pl.BlockSpec((1, tq, w), lambda b, h, i: (b, i, h)),
        out_shape=jax.ShapeDtypeStruct((B, S, FOX_WIDTH), BF16),
        compiler_params=_params("parallel", "parallel", "arbitrary"),
        name="fox_attention",
    )(proj_b, proj_b, proj_b, c4)


def _cross_kernel(q_ref, k_ref, v_ref, x_ref, wo_hbm, gn_ref, wr_ref, br_ref,
                  xo_ref, hp_ref, lg_ref, wo_b, stg, sem, *, layer, dh):
    @pl.when((pl.program_id(0) == 0) & (pl.program_id(1) == 0))
    def _():
        _load_weight(wo_hbm, layer, wo_b, stg, sem)

    nt = (((1,), (1,)), ((), ()))
    heads = []
    for h in range(CROSS_HEADS):
        sl = slice(h * dh, (h + 1) * dh)
        s = lax.dot_general(q_ref[0, :, sl], k_ref[0, :, sl], nt, preferred_element_type=F32)
        m = jnp.max(s, axis=-1, keepdims=True)
        p = jnp.exp(s - m)
        l = jnp.sum(p, axis=-1, keepdims=True)
        o = jnp.dot(p.astype(BF16), v_ref[0, :, sl], preferred_element_type=F32)
        heads.append((o / l).astype(BF16))
    xn = x_ref[0] + jnp.dot(jnp.concatenate(heads, axis=1), wo_b[...], preferred_element_type=F32)
    xo_ref[0] = xn
    y = _rms(xn) * gn_ref[...]
    _store_token_tiles(hp_ref, y, y.shape[0])
    yh = y.astype(BF16)
    yl = (y - yh.astype(F32)).astype(BF16)
    wr = wr_ref[...]
    wh = wr.astype(BF16)
    wl = (wr - wh.astype(F32)).astype(BF16)
    lg_ref[...] = (jnp.dot(yh, wh, preferred_element_type=F32)
                   + jnp.dot(yl, wh, preferred_element_type=F32)
                   + jnp.dot(yh, wl, preferred_element_type=F32)) + br_ref[...]


def cross_attention_ffn_in(qn, kn, v, x, w_co, layer, g_ffn, wr, br):
    B, S, D = qn.shape
    M = kn.shape[1]
    ts = CROSS_TS
    ns = S // ts
    assert D == TOKEN_ROWS * TOKEN_LANES
    tile = pl.BlockSpec((1, ts, D), lambda b, i: (b, i, 0))
    memb = pl.BlockSpec((1, M, D), lambda b, i: (b, 0, 0))
    const = lambda shape: pl.BlockSpec(shape, lambda b, i: (0, 0))
    return pl.pallas_call(
        functools.partial(_cross_kernel, layer=layer, dh=D // CROSS_HEADS),
        grid=(B, ns),
        in_specs=[tile, memb, memb, tile, pl.BlockSpec(memory_space=pl.ANY),
                  const((1, D)), const((D, LANES)), const((1, LANES))],
        out_specs=[tile,
                   pl.BlockSpec((ts * TOKEN_ROWS, TOKEN_LANES), lambda b, i: (b * ns + i, 0)),
                   pl.BlockSpec((ts, LANES), lambda b, i: (b * ns + i, 0))],
        out_shape=[jax.ShapeDtypeStruct((B, S, D), F32),
                   jax.ShapeDtypeStruct((B * S * TOKEN_ROWS, TOKEN_LANES), F32),
                   jax.ShapeDtypeStruct((B * S, LANES), F32)],
        scratch_shapes=[pltpu.VMEM((D, D), BF16), pltpu.VMEM((2, W_CHUNK, D), F32),
                        pltpu.SemaphoreType.DMA((2,))],
        compiler_params=_params("arbitrary", "arbitrary"),
        name="cross_attention_ffn_in",
    )(qn, kn, v, x, w_co, g_ffn.reshape(1, D), wr, br.reshape(1, LANES))


def _route_kernel(lg_ref, tri_ref, meta_ref, cnt_ref, carry_ref):
    @pl.when(pl.program_id(0) == 0)
    def _():
        carry_ref[...] = jnp.zeros_like(carry_ref)

    x = lg_ref[...]
    lane = lax.broadcasted_iota(jnp.int32, x.shape, 1)
    ninf = -jnp.inf
    is_g = lane < MOE_GROUPS
    gl = jnp.where(is_g, x, ninf)
    gmax = jnp.max(gl, axis=-1, keepdims=True)
    gidx = jnp.min(jnp.where(gl == gmax, lane, LANES), axis=-1, keepdims=True)
    gw = 1.0 / jnp.sum(jnp.where(is_g, jnp.exp(x - gmax), 0.0), axis=-1, keepdims=True)
    e_lane = lane - ROUTER_LANE0
    in_grp = (e_lane >= 0) & (e_lane < N_EXPERTS) & ((e_lane >> 3) == gidx)
    el = jnp.where(in_grp, x, ninf)
    m1 = jnp.max(el, axis=-1, keepdims=True)
    i1 = jnp.min(jnp.where(el == m1, lane, LANES), axis=-1, keepdims=True)
    el2 = jnp.where(lane == i1, ninf, el)
    m2 = jnp.max(el2, axis=-1, keepdims=True)
    i2 = jnp.min(jnp.where(el2 == m2, lane, LANES), axis=-1, keepdims=True)
    r = jnp.exp(m2 - m1)
    p1 = 1.0 / (1.0 + r)
    p2 = r * p1
    oh = ((lane == i1) | (lane == i2)).astype(F32)
    prefix = jnp.dot(tri_ref[...], oh.astype(BF16), preferred_element_type=F32) + carry_ref[...]
    rank1 = jnp.sum(jnp.where(lane == i1, prefix, 0.0), axis=-1, keepdims=True)
    rank2 = jnp.sum(jnp.where(lane == i2, prefix, 0.0), axis=-1, keepdims=True)
    carry_ref[...] += jnp.sum(oh, axis=0, keepdims=True)
    cnt_ref[...] = carry_ref[...]
    cols = [i1.astype(F32), i2.astype(F32), gw * p1, gw * p2, rank1, rank2]
    meta = jnp.zeros(x.shape, F32)
    for k, col in enumerate(cols):
        meta = jnp.where(lane == k, col, meta)
    meta_ref[...] = meta


def route(logits):
    T = logits.shape[0]
    tt = ROUTE_ROWS
    tri = jnp.asarray(np.tril(np.ones((tt, tt), np.float32), -1), BF16)
    return pl.pallas_call(
        _route_kernel,
        grid=(T // tt,),
        in_specs=[pl.BlockSpec((tt, LANES), lambda i: (i, 0)),
                  pl.BlockSpec((tt, tt), lambda i: (0, 0))],
        out_specs=[pl.BlockSpec((tt, LANES), lambda i: (i, 0)),
                   pl.BlockSpec((1, LANES), lambda i: (0, 0))],
        out_shape=[jax.ShapeDtypeStruct((T, LANES), F32),
                   jax.ShapeDtypeStruct((1, LANES), F32)],
        scratch_shapes=[pltpu.VMEM((1, LANES), F32)],
        compiler_params=_params("arbitrary"),
        name="moe_route",
    )(logits, tri)


def _plan_kernel(meta_ref, cnt_ref, d_ref):
    cnt = jnp.broadcast_to(cnt_ref[...], (SUBLANES, LANES)).astype(I32)
    pad = (((cnt + (MOE_ROWS - 1)) >> MOE_ROWS_LOG2) << MOE_ROWS_LOG2).astype(F32)
    lane8 = lax.broadcasted_iota(I32, pad.shape, 1)
    incl = pad
    sh = 1
    while sh < LANES:
        incl = incl + jnp.where(lane8 >= sh, pltpu.roll(incl, sh, axis=1), 0.0)
        sh *= 2
    off = (incl - pad)[0:1, :]
    m = meta_ref[...]
    lane = lax.broadcasted_iota(I32, m.shape, 1)
    i1 = m[:, 0:1].astype(I32)
    i2 = m[:, 1:2].astype(I32)
    d1 = jnp.sum(jnp.where(lane == i1, off, 0.0), axis=-1, keepdims=True) + m[:, 4:5]
    d2 = jnp.sum(jnp.where(lane == i2, off, 0.0), axis=-1, keepdims=True) + m[:, 5:6]
    dm = jnp.where(lane == 0, d1, jnp.where(lane == 1, d2, 0.0))
    d_ref[...] = jnp.transpose(dm)[0:2, :].astype(I32)


def plan(meta, cnt):
    T = meta.shape[0]
    tt = ROUTE_ROWS
    return pl.pallas_call(
        _plan_kernel,
        grid=(T // tt,),
        in_specs=[pl.BlockSpec((tt, LANES), lambda i: (i, 0)),
                  pl.BlockSpec((1, LANES), lambda i: (0, 0))],
        out_specs=pl.BlockSpec((2, tt), lambda i: (0, i)),
        out_shape=jax.ShapeDtypeStruct((2, T), I32),
        compiler_params=_params("parallel"),
        name="moe_plan",
    )(meta, cnt)


def _invert_kernel(d1_ref, d2_ref, src_ref, *, T, R):
    def init(r, _):
        src_ref[r] = 0
        return 0
    lax.fori_loop(0, R, init, 0, unroll=32)

    def body(t, _):
        src_ref[d1_ref[t]] = t
        src_ref[d2_ref[t]] = t
        return 0
    lax.fori_loop(0, T, body, 0, unroll=8)


def invert(d1, d2, R):
    T = d1.shape[0]
    grid_spec = pltpu.PrefetchScalarGridSpec(
        num_scalar_prefetch=2, grid=(1,), in_specs=[],
        out_specs=pl.BlockSpec(memory_space=pltpu.SMEM))
    return pl.pallas_call(
        functools.partial(_invert_kernel, T=T, R=R),
        grid_spec=grid_spec,
        out_shape=jax.ShapeDtypeStruct((R,), I32),
        compiler_params=_params("arbitrary"),
        name="moe_invert",
    )(d1, d2)


GATHER_CHUNK = 8


def _tile_copy(src_hbm, row, dst, slot, r, sem):
    return pltpu.make_async_copy(src_hbm.at[pl.ds(row * TOKEN_ROWS, TOKEN_ROWS)],
                                 dst.at[slot, pl.ds(r * TOKEN_ROWS, TOKEN_ROWS)], sem.at[slot])


def _expert_kernel(te_ref, first_ref, nxt_ref, wsl_ref, nch_ref, src_ref,
                   hp_hbm, w1_hbm, w3_hbm, w2_hbm, o_ref,
                   xbuf, r1, r3, r2, w1b, w3b, w2b, gsem, wsem, *, tr, base):
    i = pl.program_id(0)
    n = pl.num_programs(0)
    slot = i % 2

    def weight_copies(e, ws):
        return [pltpu.make_async_copy(w1_hbm.at[base + e], r1.at[ws], wsem.at[ws, 0]),
                pltpu.make_async_copy(w3_hbm.at[base + e], r3.at[ws], wsem.at[ws, 1]),
                pltpu.make_async_copy(w2_hbm.at[base + e], r2.at[ws], wsem.at[ws, 2])]

    def issue_rows(tile, slot, c0, c1):
        def body(c, _):
            for u in range(GATHER_CHUNK):
                r = c * GATHER_CHUNK + u
                _tile_copy(hp_hbm, src_ref[tile * tr + r], xbuf, slot, r, gsem).start()
            return 0
        lax.fori_loop(c0, c1, body, 0)

    def wait_rows(tile, slot):
        rows = GATHER_CHUNK * TOKEN_ROWS
        def body(c, _):
            pltpu.make_async_copy(hp_hbm.at[pl.ds(0, rows)], xbuf.at[slot, pl.ds(0, rows)],
                                  gsem.at[slot]).wait()
            return 0
        lax.fori_loop(0, nch_ref[tile], body, 0)

    @pl.when(i == 0)
    def _():
        xbuf[...] = jnp.zeros_like(xbuf)
        for cp in weight_copies(te_ref[0], 0):
            cp.start(priority=1)
        issue_rows(0, 0, 0, nch_ref[0])

    wait_rows(i, slot)

    nxt_tile = jnp.minimum(i + 1, n - 1)
    issue_rows(nxt_tile, 1 - slot, 0, jnp.where(i + 1 < n, nch_ref[nxt_tile], 0))

    @pl.when(first_ref[i] == 1)
    def _():
        ws = wsl_ref[i]
        for cp in weight_copies(te_ref[i], ws):
            cp.wait()

        @pl.when(nxt_ref[i] >= 0)
        def _():
            for cp in weight_copies(nxt_ref[i], 1 - ws):
                cp.start(priority=1)

        w1b[...] = r1[ws].astype(BF16)
        w3b[...] = r3[ws].astype(BF16)
        w2b[...] = r2[ws].astype(BF16)

    def expert_mlp(rows):
        x = jnp.concatenate([p.astype(BF16) for p in _load_token_tiles(xbuf.at[slot], 0, rows)],
                            axis=1)
        h1 = jnp.dot(x, w1b[...], preferred_element_type=F32)
        h3 = jnp.dot(x, w3b[...], preferred_element_type=F32)
        hm = (h1 * _sigmoid(h1) * h3).astype(BF16)
        _store_token_tiles(o_ref, jnp.dot(hm, w2b[...], preferred_element_type=F32), rows)

    half_chunks = tr // (2 * GATHER_CHUNK)

    @pl.when(nch_ref[i] > half_chunks)
    def _():
        expert_mlp(tr)

    @pl.when((nch_ref[i] > 0) & (nch_ref[i] <= half_chunks))
    def _():
        expert_mlp(tr // 2)
        o_ref[tr // 2 * TOKEN_ROWS:, :] = jnp.zeros((tr // 2 * TOKEN_ROWS, TOKEN_LANES), F32)

    @pl.when(nch_ref[i] == 0)
    def _():
        o_ref[...] = jnp.zeros_like(o_ref)


def moe_experts(hp, w1, w3, w2, layer, tables, src):
    D, F = w1.shape[-2:]
    tr = MOE_ROWS
    te, first, nxt, wsl, nch = tables
    NT = te.shape[0]
    w1f = w1.reshape(-1, D, F)
    w3f = w3.reshape(-1, D, F)
    w2f = w2.reshape(-1, F, D)
    hbm = pl.BlockSpec(memory_space=pl.ANY)
    grid_spec = pltpu.PrefetchScalarGridSpec(
        num_scalar_prefetch=6,
        grid=(NT,),
        in_specs=[hbm, hbm, hbm, hbm],
        out_specs=pl.BlockSpec((tr * TOKEN_ROWS, TOKEN_LANES), lambda i, *_: (i, 0)),
        scratch_shapes=[pltpu.VMEM((2, tr * TOKEN_ROWS, TOKEN_LANES), F32),
                        pltpu.VMEM((2, D, F), F32),
                        pltpu.VMEM((2, D, F), F32),
                        pltpu.VMEM((2, F, D), F32),
                        pltpu.VMEM((D, F), BF16),
                        pltpu.VMEM((D, F), BF16),
                        pltpu.VMEM((F, D), BF16),
                        pltpu.SemaphoreType.DMA((2,)),
                        pltpu.SemaphoreType.DMA((2, 3))],
    )
    return pl.pallas_call(
        functools.partial(_expert_kernel, tr=tr, base=layer * N_EXPERTS),
        grid_spec=grid_spec,
        out_shape=jax.ShapeDtypeStruct((NT * tr * TOKEN_ROWS, TOKEN_LANES), F32),
        compiler_params=_params("arbitrary"),
        name="moe_experts",
    )(te, first, nxt, wsl, nch, src, hp, w1f, w3f, w2f)


def _combine_kernel(d1_ref, d2_ref, x_ref, meta_ref, ys_hbm, *rest, tt, with_norm):
    if with_norm:
        g_ref, o_ref, hn_ref, buf, sem = rest
    else:
        o_ref, buf, sem = rest
    i = pl.program_id(0)
    n = pl.num_programs(0)
    slot = i % 2

    def issue(tile, slot):
        def body(c, _):
            for u in range(GATHER_CHUNK):
                r = c * GATHER_CHUNK + u
                t = tile * tt + r
                _tile_copy(ys_hbm, d1_ref[t], buf, slot, r, sem).start(priority=0)
                _tile_copy(ys_hbm, d2_ref[t], buf, slot, tt + r, sem).start(priority=1)
            return 0
        lax.fori_loop(0, tt // GATHER_CHUNK, body, 0)

    @pl.when(i == 0)
    def _():
        issue(0, 0)

    pltpu.make_async_copy(ys_hbm.at[pl.ds(0, 2 * tt * TOKEN_ROWS)], buf.at[slot], sem.at[slot]).wait()

    @pl.when(i + 1 < n)
    def _():
        issue(i + 1, 1 - slot)

    w1 = meta_ref[:, 2:3]
    w2 = meta_ref[:, 3:4]
    pa = _load_token_tiles(buf.at[slot], 0, tt)
    pb = _load_token_tiles(buf.at[slot], tt * TOKEN_ROWS, tt)
    cols = [x_ref[:, j * TOKEN_LANES:(j + 1) * TOKEN_LANES] + w1 * pa[j] + w2 * pb[j]
            for j in range(TOKEN_ROWS)]
    xn = jnp.concatenate(cols, axis=1)
    o_ref[...] = xn
    if with_norm:
        hn_ref[...] = (_rms(xn) * g_ref[...]).astype(hn_ref.dtype)


def moe_combine(x, meta, ys, d1, d2, g_next=None):
    T, D = x.shape
    tt = COMB_ROWS
    with_norm = g_next is not None
    row = pl.BlockSpec((tt, D), lambda i, d1, d2: (i, 0))
    in_specs = [row, pl.BlockSpec((tt, LANES), lambda i, d1, d2: (i, 0)),
                pl.BlockSpec(memory_space=pl.ANY)]
    args = [d1, d2, x, meta, ys]
    out_specs, out_shape = row, jax.ShapeDtypeStruct((T, D), F32)
    if with_norm:
        in_specs.append(pl.BlockSpec((1, D), lambda i, d1, d2: (0, 0)))
        args.append(g_next.reshape(1, D))
        out_specs, out_shape = [row, row], [out_shape, jax.ShapeDtypeStruct((T, D), BF16)]
    grid_spec = pltpu.PrefetchScalarGridSpec(
        num_scalar_prefetch=2,
        grid=(T // tt,),
        in_specs=in_specs,
        out_specs=out_specs,
        scratch_shapes=[pltpu.VMEM((2, 2 * tt * TOKEN_ROWS, TOKEN_LANES), F32),
                        pltpu.SemaphoreType.DMA((2,))],
    )
    return pl.pallas_call(
        functools.partial(_combine_kernel, tt=tt, with_norm=with_norm),
        grid_spec=grid_spec,
        out_shape=out_shape,
        compiler_params=_params("arbitrary"),
        name="moe_combine",
    )(*args)


def _tile_tables(cnt, T):
    E = N_EXPERTS
    counts = cnt[0, ROUTER_LANE0:ROUTER_LANE0 + E].astype(I32)
    tiles_e = (counts + MOE_ROWS - 1) // MOE_ROWS
    tile_end = jnp.cumsum(tiles_e)
    tile_start = tile_end - tiles_e
    nact = tile_end[-1]
    NT = (2 * T) // MOE_ROWS + E
    tid = jnp.arange(NT, dtype=I32)
    te = jnp.sum((jnp.minimum(tid, nact - 1)[:, None] >= tile_end[None, :]).astype(I32), axis=1)
    te = jnp.minimum(te, E - 1)
    active = tid < nact
    first = (active & (tid == tile_start[te])).astype(I32)
    eid = jnp.arange(E, dtype=I32)
    later = (eid[None, :] > eid[:, None]) & (tiles_e[None, :] > 0)
    nxt_e = jnp.min(jnp.where(later, eid[None, :], E), axis=1)
    nxt_e = jnp.where(nxt_e == E, -1, nxt_e)
    ordinal = jnp.cumsum((tiles_e > 0).astype(I32)) - 1
    valid = jnp.clip(counts[te] - (tid - tile_start[te]) * MOE_ROWS, 0, MOE_ROWS)
    nch = jnp.where(active, (valid + GATHER_CHUNK - 1) // GATHER_CHUNK, 0)
    return (te, first, nxt_e[te], ordinal[te] % 2, nch), NT * MOE_ROWS


def _lower_bound_logs(lb_param, layer):
    p = jax.nn.softmax(lb_param.astype(F32), axis=0)
    c = jnp.cumsum(p, axis=0)
    lb = c[layer] - c[0]
    return jnp.log(lb), jnp.log1p(-lb)


def kernel(x, mem, norm_mix, w_in, hg_lower_bounds, hg_norm, conv_w, fox_f_bias, fox_q_norm,
           fox_k_norm, w_out, norm_cross, norm_mem, w_cq, w_ck, w_cv, w_co, cross_q_norm,
           cross_k_norm, norm_ffn, router_group_w, router_group_b, router_expert_w,
           router_expert_b, moe_w1, moe_w3, moe_w2):
    B, S, D = x.shape
    M = mem.shape[1]
    L = w_in.shape[0]
    T = B * S
    HGW = 4 * HG_WIDTH
    CONV3 = 3 * CONV_WIDTH
    PB = CONV3 + 3 * FOX_WIDTH
    FQ0, FK0, FV0 = 0, FOX_WIDTH, 2 * FOX_WIDTH
    CV0 = 3 * FOX_WIDTH
    dhc = D // CROSS_HEADS
    x2 = x.reshape(T, D)
    mem2 = mem.reshape(B * M, D)
    w_in_t = jnp.swapaxes(w_in, 1, 2)
    hn = rmsnorm(x2, norm_mix[0])
    for l in range(L):
        proj_a = matmul([hn], w_in_t, l, 0, HGW, F32, w_t=True)
        proj_b = matmul([hn], w_in_t, l, HGW, PB, BF16, w_t=True, rotate=CONV3, head_norms=(
            (FQ0, FK0, fox_q_norm[l], FOX_DIM ** -0.5), (FK0, FV0, fox_k_norm[l], 1.0)))
        loglb, l1mlb = _lower_bound_logs(hg_lower_bounds, l)
        y_hg = hgrn2(proj_a.reshape(B, S, HGW), loglb, l1mlb, hg_norm[l])
        pb3 = proj_b.reshape(B, S, PB)
        y_conv = short_conv(pb3, CV0, conv_w[l])
        c = fox_gate(hn.reshape(B, S, D), w_in_t, l, HGW + PB, fox_f_bias[l])
        y_fox = fox_attention(pb3, FQ0, FK0, FV0, c)
        mix = [y_hg.reshape(T, HG_WIDTH), y_conv.reshape(T, CONV_WIDTH), y_fox.reshape(T, FOX_WIDTH)]
        x2, qcn = out_proj_cross_q(mix, x2, w_out, w_cq, l, norm_cross[l], cross_q_norm[l])
        memn = rmsnorm(mem2, norm_mem[l])
        kcn = matmul([memn], w_ck, l, 0, D, BF16, head_norms=((0, D, cross_k_norm[l], 1.0),))
        vc = matmul([memn], w_cv, l, 0, D, BF16)
        wr = jnp.concatenate([router_group_w[l], router_expert_w[l]], axis=1)
        wr = jnp.pad(wr, ((0, 0), (0, LANES - wr.shape[1])))
        br = jnp.concatenate([router_group_b[l], router_expert_b[l]])
        br = jnp.pad(br, (0, LANES - br.shape[0]))
        x3, hp, logits = cross_attention_ffn_in(
            qcn.reshape(B, S, D), kcn.reshape(B, M, D), vc.reshape(B, M, D), x2.reshape(B, S, D),
            w_co, l, norm_ffn[l], wr, br)
        x2 = x3.reshape(T, D)
        meta, cnt = route(logits)
        d = plan(meta, cnt)
        tables, R = _tile_tables(cnt, T)
        src = invert(d[0], d[1], R)
        ys = moe_experts(hp, moe_w1, moe_w3, moe_w2, l, tables, src)
        if l + 1 < L:
            x2, hn = moe_combine(x2, meta, ys, d[0], d[1], norm_mix[l + 1])
        else:
            x2 = moe_combine(x2, meta, ys, d[0], d[1])
    return x2.reshape(B, S, D)
```

```python
import functools

import numpy as np
import jax
import jax.numpy as jnp
from jax import lax
from jax.experimental import pallas as pl
from jax.experimental.pallas import tpu as pltpu

F32 = jnp.float32
BF16 = jnp.bfloat16
I32 = jnp.int32
EPS = 1e-6

HG_HEADS = 4
HG_DIM = 128
HG_WIDTH = HG_HEADS * HG_DIM
CONV_WIDTH = 512
FOX_HEADS = 8
FOX_DIM = 128
FOX_WIDTH = FOX_HEADS * FOX_DIM
CROSS_HEADS = 4
MOE_GROUPS = 4
MOE_EXPERTS = 8
N_EXPERTS = MOE_GROUPS * MOE_EXPERTS
ROUTER_LANE0 = MOE_GROUPS

LANES = 128
SUBLANES = 8
VMEM_LIMIT = 56 * 1024 * 1024

NORM_ROWS = 512
MM_TM = 2048
MM_TN = 512
HG_CHUNK = 128
FOX_TQ = 512
FOX_TK = 512
FOX_HEADS_PER_STEP = 8
CROSS_TS = 512
ROUTE_ROWS = 1024
MOE_ROWS_LOG2 = 8
MOE_ROWS = 1 << MOE_ROWS_LOG2
COMB_ROWS = 256


def _params(*sem):
    return pltpu.CompilerParams(dimension_semantics=sem, vmem_limit_bytes=VMEM_LIMIT)


def _sigmoid(x):
    return 1.0 / (1.0 + jnp.exp(-x))


def _log_sigmoid(x):
    return jnp.minimum(x, 0.0) - jnp.log1p(jnp.exp(-jnp.abs(x)))


TOKEN_ROWS = 16
TOKEN_LANES = 128


def _store_token_tiles(ref, y, rows):
    for j in range(TOKEN_ROWS):
        ref[pl.ds(j, rows, stride=TOKEN_ROWS), :] = y[:, j * TOKEN_LANES:(j + 1) * TOKEN_LANES]


def _load_token_tiles(ref, base, rows):
    return [ref[pl.ds(base + j, rows, stride=TOKEN_ROWS), :] for j in range(TOKEN_ROWS)]


def _rms(x):
    return x * lax.rsqrt(jnp.mean(x * x, axis=-1, keepdims=True) + EPS)


def _rmsnorm_kernel(x_ref, g_ref, o_ref):
    o_ref[...] = (_rms(x_ref[...]) * g_ref[...]).astype(o_ref.dtype)


def rmsnorm(x, g, out_dtype=BF16):
    R, D = x.shape
    return pl.pallas_call(
        _rmsnorm_kernel,
        grid=(R // NORM_ROWS,),
        in_specs=[pl.BlockSpec((NORM_ROWS, D), lambda i: (i, 0)),
                  pl.BlockSpec((1, D), lambda i: (0, 0))],
        out_specs=pl.BlockSpec((NORM_ROWS, D), lambda i: (i, 0)),
        out_shape=jax.ShapeDtypeStruct((R, D), out_dtype),
        compiler_params=_params("parallel"),
        name="rmsnorm",
    )(x, g.reshape(1, D))


def _head_rmsnorm(x, g, dh, scale):
    heads = [_rms(x[:, h * dh:(h + 1) * dh]) * g * scale for h in range(x.shape[1] // dh)]
    return heads[0] if len(heads) == 1 else jnp.concatenate(heads, axis=1)


def _matmul_kernel(*refs, n_parts, w_t, norm_tiles, dh):
    a_refs = refs[:n_parts]
    w_ref = refs[n_parts]
    if norm_tiles:
        g_ref, o_ref, wb_ref = refs[n_parts + 1:]
    else:
        o_ref, wb_ref = refs[n_parts + 1:]
    j = pl.program_id(0)

    @pl.when(pl.program_id(1) == 0)
    def _():
        wb_ref[...] = w_ref[...].astype(BF16)

    if n_parts == 1:
        a = a_refs[0][...]
    else:
        a = jnp.concatenate([r[...] for r in a_refs], axis=1)
    if w_t:
        acc = lax.dot_general(a, wb_ref[...], (((1,), (1,)), ((), ())), preferred_element_type=F32)
    else:
        acc = jnp.dot(a, wb_ref[...], preferred_element_type=F32)
    if not norm_tiles:
        o_ref[...] = acc.astype(o_ref.dtype)
        return
    plain = True
    for k, (j0, j1, scale) in enumerate(norm_tiles):
        hit = (j >= j0) & (j < j1)
        plain = plain & jnp.logical_not(hit)

        @pl.when(hit)
        def _(k=k, scale=scale):
            o_ref[...] = _head_rmsnorm(acc, g_ref[k:k + 1, :], dh, scale).astype(o_ref.dtype)

    @pl.when(plain)
    def _():
        o_ref[...] = acc.astype(o_ref.dtype)


def matmul(a_parts, w, layer, col0, n, out_dtype, w_t=False, head_norms=(), rotate=0,
           tm=MM_TM, tn=MM_TN):
    M = a_parts[0].shape[0]
    K = w.shape[2] if w_t else w.shape[1]
    assert sum(p.shape[1] for p in a_parts) == K
    tm = min(tm, M)
    assert col0 % tn == 0 and n % tn == 0 and M % tm == 0 and rotate % tn == 0
    cb, nt, rot = col0 // tn, n // tn, rotate // tn
    in_specs = [pl.BlockSpec((tm, p.shape[1]), lambda j, i: (i, 0)) for p in a_parts]
    if w_t:
        in_specs.append(pl.BlockSpec((None, tn, K), lambda j, i: (layer, cb + (j + rot) % nt, 0)))
    else:
        in_specs.append(pl.BlockSpec((None, K, tn), lambda j, i: (layer, 0, cb + (j + rot) % nt)))
    args = list(a_parts) + [w]
    norm_tiles, dh = (), 0
    if head_norms:
        dh = head_norms[0][2].shape[0]
        assert all(c0 % tn == 0 and c1 % tn == 0 and g.shape[0] == dh and tn % dh == 0
                   for c0, c1, g, _ in head_norms)
        norm_tiles = tuple((c0 // tn, c1 // tn, s) for c0, c1, _, s in head_norms)
        gains = jnp.stack([g for _, _, g, _ in head_norms])
        in_specs.append(pl.BlockSpec(gains.shape, lambda j, i: (0, 0)))
        args.append(gains)
    return pl.pallas_call(
        functools.partial(_matmul_kernel, n_parts=len(a_parts), w_t=w_t, norm_tiles=norm_tiles,
                          dh=dh),
        grid=(n // tn, M // tm),
        in_specs=in_specs,
        out_specs=pl.BlockSpec((tm, tn), lambda j, i: (i, j)),
        out_shape=jax.ShapeDtypeStruct((M, n), out_dtype),
        scratch_shapes=[pltpu.VMEM((tn, K) if w_t else (K, tn), BF16)],
        compiler_params=_params("arbitrary", "arbitrary"),
        name="matmul",
    )(*args)


W_CHUNK = 256


def _load_weight(w_hbm, layer, wb, stg, sem):
    n_chunks = wb.shape[0] // W_CHUNK

    def copy(c):
        return pltpu.make_async_copy(w_hbm.at[layer, pl.ds(c * W_CHUNK, W_CHUNK)], stg.at[c % 2],
                                     sem.at[c % 2])
    copy(0).start()
    for c in range(n_chunks):
        if c + 1 < n_chunks:
            copy(c + 1).start()
        copy(c).wait()
        wb[c * W_CHUNK:(c + 1) * W_CHUNK, :] = stg[c % 2].astype(BF16)


def _outq_kernel(a1_ref, a2_ref, a3_ref, x_ref, wo_hbm, wq_hbm, gn_ref, gq_ref, xo_ref, q_ref,
                 wo_b, wq_b, stg, sem, *, layer, dh, scale):
    @pl.when(pl.program_id(0) == 0)
    def _():
        _load_weight(wo_hbm, layer, wo_b, stg, sem)
        _load_weight(wq_hbm, layer, wq_b, stg, sem)

    a = jnp.concatenate([a1_ref[...], a2_ref[...], a3_ref[...]], axis=1)
    xn = x_ref[...] + jnp.dot(a, wo_b[...], preferred_element_type=F32)
    xo_ref[...] = xn
    hc = (_rms(xn) * gn_ref[...]).astype(BF16)
    q = jnp.dot(hc, wq_b[...], preferred_element_type=F32)
    q_ref[...] = _head_rmsnorm(q, gq_ref[...], dh, scale).astype(q_ref.dtype)


def out_proj_cross_q(mix, x, w_out, w_cq, layer, g_cross, g_q, tm=512):
    T, D = x.shape
    dh = g_q.shape[0]
    row = lambda w: pl.BlockSpec((tm, w), lambda i: (i, 0))
    hbm = pl.BlockSpec(memory_space=pl.ANY)
    return pl.pallas_call(
        functools.partial(_outq_kernel, layer=layer, dh=dh, scale=dh ** -0.5),
        grid=(T // tm,),
        in_specs=[row(mix[0].shape[1]), row(mix[1].shape[1]), row(mix[2].shape[1]), row(D), hbm, hbm,
                  pl.BlockSpec((1, D), lambda i: (0, 0)), pl.BlockSpec((1, dh), lambda i: (0, 0))],
        out_specs=[row(D), row(D)],
        out_shape=[jax.ShapeDtypeStruct((T, D), F32), jax.ShapeDtypeStruct((T, D), BF16)],
        scratch_shapes=[pltpu.VMEM((D, D), BF16), pltpu.VMEM((D, D), BF16),
                        pltpu.VMEM((2, W_CHUNK, D), F32), pltpu.SemaphoreType.DMA((2,))],
        compiler_params=_params("arbitrary"),
        name="out_proj_cross_q",
    )(*mix, x, w_out, w_cq, g_cross.reshape(1, D), g_q.reshape(1, dh))


def _hgrn_consts(C):
    nlev = int(np.log2(C))
    t = np.arange(C)[:, None]
    u = np.arange(C)[None, :]
    tri = (u <= t).astype(np.float32)
    mall2 = np.concatenate([tri, tri], axis=1)
    tt = np.arange(C)[:, None]
    ss = np.arange(C)[None, :]
    lev = np.full((C, C), nlev + 1, np.int32)
    x = tt ^ ss
    hb = np.zeros_like(x)
    for j in range(nlev):
        hb = np.where((x >> j) & 1, j, hb)
    lev = np.where(tt > ss, hb, lev)
    lev = np.where(tt == ss, nlev, lev)
    return jnp.asarray(mall2, BF16), jnp.asarray(lev, jnp.int32)


def _level_sums(lf, b, row, j, C):
    m = 1 << j
    second = (row & m) != 0
    if j == 0:
        return jnp.where(second, lf, 0.0)
    if j == 1:
        prev = pltpu.roll(lf, 1, axis=0)
        nxt = pltpu.roll(lf, C - 1, axis=0)
        first = jnp.where((row & 1) == 0, nxt, 0.0)
        return jnp.where(second, jnp.where((row & 1) != 0, lf + prev, lf), first)
    g = 2 * m
    d = b.shape[1]
    bsel = jnp.broadcast_to(b.reshape(C // g, g, d)[:, m - 1:m, :], (C // g, g, d)).reshape(C, d)
    return jnp.where(second, b - bsel, bsel - b)


def _hgrn_kernel(x_ref, mall_ref, lev_ref, loglb_ref, l1mlb_ref, ng_ref, o_ref, st_ref, *, C, nlev):
    @pl.when(pl.program_id(1) == 0)
    def _():
        st_ref[...] = jnp.zeros_like(st_ref)

    W = HG_WIDTH
    d = HG_DIM
    lev = lev_ref[...]
    mall = mall_ref[...]
    row = lax.broadcasted_iota(jnp.int32, (C, d), 0)
    nt = (((1,), (1,)), ((), ()))

    def side(xs):
        return jnp.concatenate(xs, axis=1)

    def block_diag(xa, xb):
        return jnp.concatenate([side([xa, jnp.zeros_like(xb)]), side([jnp.zeros_like(xa), xb])],
                               axis=0)

    for h0 in range(0, HG_HEADS, 2):
        q, lf, kk, vb, g = [], [], [], [], []
        for h in (h0, h0 + 1):
            sl = slice(h * d, (h + 1) * d)
            q.append(x_ref[0, :, h * d:(h + 1) * d])
            z = x_ref[0, :, W + h * d:W + (h + 1) * d]
            vb.append(x_ref[0, :, 2 * W + h * d:2 * W + (h + 1) * d].astype(BF16))
            g.append(x_ref[0, :, 3 * W + h * d:3 * W + (h + 1) * d])
            a = loglb_ref[:, sl]
            bt = l1mlb_ref[:, sl] + _log_sigmoid(z)
            lf.append(jnp.maximum(a, bt) + jnp.log1p(jnp.exp(-jnp.abs(a - bt))))
            kk.append(1.0 - jnp.exp(lf[-1]))
        hi = [x.astype(BF16) for x in lf]
        lo = [(x - y.astype(F32)).astype(BF16) for x, y in zip(lf, hi)]
        b2 = jnp.dot(mall, jnp.concatenate([side(hi), side(lo)], axis=0),
                     preferred_element_type=F32)
        b = [b2[:, :d], b2[:, d:]]
        st = [st_ref[h0], st_ref[h0 + 1]]
        o2 = lax.dot_general(side([(q[i] * jnp.exp(b[i])).astype(BF16) for i in (0, 1)]),
                             block_diag(st[0].astype(BF16), st[1].astype(BF16)), nt,
                             preferred_element_type=F32)
        att = [jnp.where(lev == nlev, jnp.sum(q[i] * kk[i], axis=-1, keepdims=True), 0.0)
               for i in (0, 1)]
        for j in range(nlev):
            second = (row & (1 << j)) != 0
            xq, xk = [], []
            for i in (0, 1):
                e = jnp.exp(_level_sums(lf[i], b[i], row, j, C))
                xq.append(jnp.where(second, q[i] * e, 0.0).astype(BF16))
                xk.append(jnp.where(second, 0.0, kk[i] * e).astype(BF16))
            am = lax.dot_general(side(xq), block_diag(xk[0], xk[1]), nt,
                                 preferred_element_type=F32)
            att = [jnp.where(lev == j, am[:, i * C:(i + 1) * C], att[i]) for i in (0, 1)]
        o2 = o2 + jnp.dot(side([x.astype(BF16) for x in att]), block_diag(vb[0], vb[1]),
                          preferred_element_type=F32)
        for i, h in enumerate((h0, h0 + 1)):
            bl = b[i][C - 1:C, :]
            kh = (kk[i] * jnp.exp(bl - b[i])).astype(BF16)
            st_ref[h] = st[i] * jnp.exp(bl) + lax.dot_general(
                vb[i], kh, (((0,), (0,)), ((), ())), preferred_element_type=F32)
            o = o2[:, i * d:(i + 1) * d]
            y = o * lax.rsqrt(jnp.mean(o * o, axis=-1, keepdims=True) + EPS) * ng_ref[...]
            o_ref[0, :, h * d:(h + 1) * d] = (y * (g[i] * _sigmoid(g[i]))).astype(o_ref.dtype)


def hgrn2(proj_a, loglb, l1mlb, norm_g):
    B, S, _ = proj_a.shape
    C = HG_CHUNK
    nlev = int(np.log2(C))
    mall, lev = _hgrn_consts(C)
    return pl.pallas_call(
        functools.partial(_hgrn_kernel, C=C, nlev=nlev),
        grid=(B, S // C),
        in_specs=[pl.BlockSpec((1, C, 4 * HG_WIDTH), lambda b, c: (b, c, 0)),
                  pl.BlockSpec(mall.shape, lambda b, c: (0, 0)),
                  pl.BlockSpec((C, C), lambda b, c: (0, 0)),
                  pl.BlockSpec((1, HG_WIDTH), lambda b, c: (0, 0)),
                  pl.BlockSpec((1, HG_WIDTH), lambda b, c: (0, 0)),
                  pl.BlockSpec((1, HG_DIM), lambda b, c: (0, 0))],
        out_specs=pl.BlockSpec((1, C, HG_WIDTH), lambda b, c: (b, c, 0)),
        out_shape=jax.ShapeDtypeStruct((B, S, HG_WIDTH), BF16),
        scratch_shapes=[pltpu.VMEM((HG_HEADS, HG_DIM, HG_DIM), F32)],
        compiler_params=_params("parallel", "arbitrary"),
        name="hgrn2",
    )(proj_a, mall, lev, loglb.reshape(1, HG_WIDTH), l1mlb.reshape(1, HG_WIDTH),
      norm_g.reshape(1, HG_DIM))


def _conv_kernel(b_ref, c_ref, h_ref, w_ref, o_ref):
    u = c_ref[0].astype(F32) * h_ref[0].astype(F32)
    row = lax.broadcasted_iota(jnp.int32, u.shape, 0)
    u1 = jnp.where(row >= 1, pltpu.roll(u, 1, axis=0), 0.0)
    u2 = jnp.where(row >= 2, pltpu.roll(u, 2, axis=0), 0.0)
    y = w_ref[0:1, :] * u2 + w_ref[1:2, :] * u1 + w_ref[2:3, :] * u
    o_ref[0] = (b_ref[0].astype(F32) * y).astype(o_ref.dtype)


def short_conv(proj_b, col0, w):
    B, S, _ = proj_b.shape
    cw = CONV_WIDTH
    assert col0 % cw == 0
    spec = lambda k: pl.BlockSpec((1, S, cw), lambda b: (b, 0, col0 // cw + k))
    return pl.pallas_call(
        _conv_kernel,
        grid=(B,),
        in_specs=[spec(0), spec(1), spec(2), pl.BlockSpec((3, cw), lambda b: (0, 0))],
        out_specs=pl.BlockSpec((1, S, cw), lambda b: (b, 0, 0)),
        out_shape=jax.ShapeDtypeStruct((B, S, cw), BF16),
        compiler_params=_params("parallel"),
        name="short_conv",
    )(proj_b, proj_b, proj_b, w)


def _fox_gate_kernel(hn_ref, wf_ref, bias_ref, c_ref):
    gl = lax.dot_general(wf_ref[...].astype(BF16), hn_ref[0], (((1,), (1,)), ((), ())),
                         preferred_element_type=F32)
    c = _log_sigmoid(gl + bias_ref[...])
    S = c.shape[1]
    lane = lax.broadcasted_iota(jnp.int32, c.shape, 1)
    sh = 1
    while sh < S:
        c = c + jnp.where(lane >= sh, pltpu.roll(c, sh, axis=1), 0.0)
        sh *= 2
    c_ref[0] = c


def fox_gate(hn, w_t, layer, row0, bias):
    B, S, D = hn.shape
    H = bias.shape[0]
    assert row0 % H == 0
    return pl.pallas_call(
        _fox_gate_kernel,
        grid=(B,),
        in_specs=[pl.BlockSpec((1, S, D), lambda b: (b, 0, 0)),
                  pl.BlockSpec((None, H, D), lambda b: (layer, row0 // H, 0)),
                  pl.BlockSpec((H, 1), lambda b: (0, 0))],
        out_specs=pl.BlockSpec((1, H, S), lambda b: (b, 0, 0)),
        out_shape=jax.ShapeDtypeStruct((B, H, S), F32),
        compiler_params=_params("parallel"),
        name="fox_gate",
    )(hn, w_t, bias.reshape(H, 1))


def _fox_kernel(q_ref, k_ref, v_ref, c_ref, o_ref, *, tq, tk, nh):
    qi = pl.program_id(2)
    nt = (((1,), (1,)), ((), ()))
    ones = jnp.ones((tk, FOX_DIM), BF16)
    heads = [slice(h * FOX_DIM, (h + 1) * FOX_DIM) for h in range(nh)]
    qs = [q_ref[0, :, sl] for sl in heads]

    def step(ki, carry, diag):
        out = []
        for h, sl in enumerate(heads):
            m, acc = carry[h]
            k = k_ref[0, pl.ds(ki * tk, tk), sl]
            v1 = jnp.concatenate([v_ref[0, pl.ds(ki * tk, tk), sl], ones], axis=1)
            ck = c_ref[0, h, pl.ds(ki, 1), :]
            s = lax.dot_general(qs[h], k, nt, preferred_element_type=F32) - ck
            if diag is not None:
                r = lax.broadcasted_iota(jnp.int32, s.shape, 0)
                c = lax.broadcasted_iota(jnp.int32, s.shape, 1)
                s = jnp.where(c + diag * tk <= r, s, -jnp.inf)
            m_new = jnp.maximum(m, jnp.max(s, axis=-1, keepdims=True))
            alpha = jnp.exp(m - m_new)
            p = jnp.exp((s - m_new).astype(BF16))
            out.append((m_new, alpha * acc + jnp.dot(p, v1, preferred_element_type=F32)))
        return tuple(out)

    init = tuple((jnp.full((tq, 1), -jnp.inf, F32), jnp.zeros((tq, 2 * FOX_DIM), F32))
                 for _ in heads)
    n_full = qi * (tq // tk)
    carry = lax.fori_loop(0, n_full, lambda ki, cr: step(ki, cr, None), init)
    for d in range(tq // tk):
        carry = step(n_full + d, carry, d)
    for h, sl in enumerate(heads):
        acc = carry[h][1]
        o_ref[0, :, sl] = (acc[:, :FOX_DIM] / acc[:, FOX_DIM:]).astype(o_ref.dtype)


def fox_attention(proj_b, q_col0, k_col0, v_col0, c):
    B, S, _ = proj_b.shape
    tq, tk, nh = FOX_TQ, FOX_TK, FOX_HEADS_PER_STEP
    H = FOX_HEADS
    w = nh * FOX_DIM
    assert tq % tk == 0 and H % nh == 0 and all(c0 % w == 0 for c0 in (q_col0, k_col0, v_col0))
    qb, kb, vb = q_col0 // w, k_col0 // w, v_col0 // w
    c4 = c.reshape(B, H, S // tk, tk)
    return pl.pallas_call(
        functools.partial(_fox_kernel, tq=tq, tk=tk, nh=nh),
        grid=(B, H // nh, S // tq),
        in_specs=[pl.BlockSpec((1, tq, w), lambda b, h, i: (b, i, qb + h)),
                  pl.BlockSpec((1, S, w), lambda b, h, i: (b, 0, kb + h)),
                  pl.BlockSpec((1, S, w), lambda b, h, i: (b, 0, vb + h)),
                  pl.BlockSpec((1, nh, S // tk, tk), lambda b, h, i: (b, h, 0, 0))],
        out_specs=pl.BlockSpec((1, tq, w), lambda b, h, i: (b, i, h)),
        out_shape=jax.ShapeDtypeStruct((B, S, FOX_WIDTH), BF16),
        compiler_params=_params("parallel", "parallel", "arbitrary"),
        name="fox_attention",
    )(proj_b, proj_b, proj_b, c4)


def _cross_kernel(q_ref, k_ref, v_ref, x_ref, wo_hbm, gn_ref, wr_ref, br_ref,
                  xo_ref, hp_ref, lg_ref, wo_b, stg, sem, *, layer, dh):
    @pl.when((pl.program_id(0) == 0) & (pl.program_id(1) == 0))
    def _():
        _load_weight(wo_hbm, layer, wo_b, stg, sem)

    nt = (((1,), (1,)), ((), ()))
    heads = []
    for h in range(CROSS_HEADS):
        sl = slice(h * dh, (h + 1) * dh)
        s = lax.dot_general(q_ref[0, :, sl], k_ref[0, :, sl], nt, preferred_element_type=F32)
        m = jnp.max(s, axis=-1, keepdims=True)
        p = jnp.exp(s - m)
        l = jnp.sum(p, axis=-1, keepdims=True)
        o = jnp.dot(p.astype(BF16), v_ref[0, :, sl], preferred_element_type=F32)
        heads.append((o / l).astype(BF16))
    xn = x_ref[0] + jnp.dot(jnp.concatenate(heads, axis=1), wo_b[...], preferred_element_type=F32)
    xo_ref[0] = xn
    y = _rms(xn) * gn_ref[...]
    _store_token_tiles(hp_ref, y, y.shape[0])
    yh = y.astype(BF16)
    yl = (y - yh.astype(F32)).astype(BF16)
    wr = wr_ref[...]
    wh = wr.astype(BF16)
    wl = (wr - wh.astype(F32)).astype(BF16)
    lg_ref[...] = (jnp.dot(yh, wh, preferred_element_type=F32)
                   + jnp.dot(yl, wh, preferred_element_type=F32)
                   + jnp.dot(yh, wl, preferred_element_type=F32)) + br_ref[...]


def cross_attention_ffn_in(qn, kn, v, x, w_co, layer, g_ffn, wr, br):
    B, S, D = qn.shape
    M = kn.shape[1]
    ts = CROSS_TS
    ns = S // ts
    assert D == TOKEN_ROWS * TOKEN_LANES
    tile = pl.BlockSpec((1, ts, D), lambda b, i: (b, i, 0))
    memb = pl.BlockSpec((1, M, D), lambda b, i: (b, 0, 0))
    const = lambda shape: pl.BlockSpec(shape, lambda b, i: (0, 0))
    return pl.pallas_call(
        functools.partial(_cross_kernel, layer=layer, dh=D // CROSS_HEADS),
        grid=(B, ns),
        in_specs=[tile, memb, memb, tile, pl.BlockSpec(memory_space=pl.ANY),
                  const((1, D)), const((D, LANES)), const((1, LANES))],
        out_specs=[tile,
                   pl.BlockSpec((ts * TOKEN_ROWS, TOKEN_LANES), lambda b, i: (b * ns + i, 0)),
                   pl.BlockSpec((ts, LANES), lambda b, i: (b * ns + i, 0))],
        out_shape=[jax.ShapeDtypeStruct((B, S, D), F32),
                   jax.ShapeDtypeStruct((B * S * TOKEN_ROWS, TOKEN_LANES), F32),
                   jax.ShapeDtypeStruct((B * S, LANES), F32)],
        scratch_shapes=[pltpu.VMEM((D, D), BF16), pltpu.VMEM((2, W_CHUNK, D), F32),
                        pltpu.SemaphoreType.DMA((2,))],
        compiler_params=_params("arbitrary", "arbitrary"),
        name="cross_attention_ffn_in",
    )(qn, kn, v, x, w_co, g_ffn.reshape(1, D), wr, br.reshape(1, LANES))


def _route_kernel(lg_ref, tri_ref, meta_ref, cnt_ref, carry_ref):
    @pl.when(pl.program_id(0) == 0)
    def _():
        carry_ref[...] = jnp.zeros_like(carry_ref)

    x = lg_ref[...]
    lane = lax.broadcasted_iota(jnp.int32, x.shape, 1)
    ninf = -jnp.inf
    is_g = lane < MOE_GROUPS
    gl = jnp.where(is_g, x, ninf)
    gmax = jnp.max(gl, axis=-1, keepdims=True)
    gidx = jnp.min(jnp.where(gl == gmax, lane, LANES), axis=-1, keepdims=True)
    gw = 1.0 / jnp.sum(jnp.where(is_g, jnp.exp(x - gmax), 0.0), axis=-1, keepdims=True)
    e_lane = lane - ROUTER_LANE0
    in_grp = (e_lane >= 0) & (e_lane < N_EXPERTS) & ((e_lane >> 3) == gidx)
    el = jnp.where(in_grp, x, ninf)
    m1 = jnp.max(el, axis=-1, keepdims=True)
    i1 = jnp.min(jnp.where(el == m1, lane, LANES), axis=-1, keepdims=True)
    el2 = jnp.where(lane == i1, ninf, el)
    m2 = jnp.max(el2, axis=-1, keepdims=True)
    i2 = jnp.min(jnp.where(el2 == m2, lane, LANES), axis=-1, keepdims=True)
    r = jnp.exp(m2 - m1)
    p1 = 1.0 / (1.0 + r)
    p2 = r * p1
    oh = ((lane == i1) | (lane == i2)).astype(F32)
    prefix = jnp.dot(tri_ref[...], oh.astype(BF16), preferred_element_type=F32) + carry_ref[...]
    rank1 = jnp.sum(jnp.where(lane == i1, prefix, 0.0), axis=-1, keepdims=True)
    rank2 = jnp.sum(jnp.where(lane == i2, prefix, 0.0), axis=-1, keepdims=True)
    carry_ref[...] += jnp.sum(oh, axis=0, keepdims=True)
    cnt_ref[...] = carry_ref[...]
    cols = [i1.astype(F32), i2.astype(F32), gw * p1, gw * p2, rank1, rank2]
    meta = jnp.zeros(x.shape, F32)
    for k, col in enumerate(cols):
        meta = jnp.where(lane == k, col, meta)
    meta_ref[...] = meta


def route(logits):
    T = logits.shape[0]
    tt = ROUTE_ROWS
    tri = jnp.asarray(np.tril(np.ones((tt, tt), np.float32), -1), BF16)
    return pl.pallas_call(
        _route_kernel,
        grid=(T // tt,),
        in_specs=[pl.BlockSpec((tt, LANES), lambda i: (i, 0)),
                  pl.BlockSpec((tt, tt), lambda i: (0, 0))],
        out_specs=[pl.BlockSpec((tt, LANES), lambda i: (i, 0)),
                   pl.BlockSpec((1, LANES), lambda i: (0, 0))],
        out_shape=[jax.ShapeDtypeStruct((T, LANES), F32),
                   jax.ShapeDtypeStruct((1, LANES), F32)],
        scratch_shapes=[pltpu.VMEM((1, LANES), F32)],
        compiler_params=_params("arbitrary"),
        name="moe_route",
    )(logits, tri)


def _plan_kernel(meta_ref, cnt_ref, d_ref):
    cnt = jnp.broadcast_to(cnt_ref[...], (SUBLANES, LANES)).astype(I32)
    pad = (((cnt + (MOE_ROWS - 1)) >> MOE_ROWS_LOG2) << MOE_ROWS_LOG2).astype(F32)
    lane8 = lax.broadcasted_iota(I32, pad.shape, 1)
    incl = pad
    sh = 1
    while sh < LANES:
        incl = incl + jnp.where(lane8 >= sh, pltpu.roll(incl, sh, axis=1), 0.0)
        sh *= 2
    off = (incl - pad)[0:1, :]
    m = meta_ref[...]
    lane = lax.broadcasted_iota(I32, m.shape, 1)
    i1 = m[:, 0:1].astype(I32)
    i2 = m[:, 1:2].astype(I32)
    d1 = jnp.sum(jnp.where(lane == i1, off, 0.0), axis=-1, keepdims=True) + m[:, 4:5]
    d2 = jnp.sum(jnp.where(lane == i2, off, 0.0), axis=-1, keepdims=True) + m[:, 5:6]
    dm = jnp.where(lane == 0, d1, jnp.where(lane == 1, d2, 0.0))
    d_ref[...] = jnp.transpose(dm)[0:2, :].astype(I32)


def plan(meta, cnt):
    T = meta.shape[0]
    tt = ROUTE_ROWS
    return pl.pallas_call(
        _plan_kernel,
        grid=(T // tt,),
        in_specs=[pl.BlockSpec((tt, LANES), lambda i: (i, 0)),
                  pl.BlockSpec((1, LANES), lambda i: (0, 0))],
        out_specs=pl.BlockSpec((2, tt), lambda i: (0, i)),
        out_shape=jax.ShapeDtypeStruct((2, T), I32),
        compiler_params=_params("parallel"),
        name="moe_plan",
    )(meta, cnt)


def _invert_kernel(d1_ref, d2_ref, src_ref, zeros, sem, *, T):
    zeros[...] = jnp.zeros_like(zeros)
    fill = pltpu.make_async_copy(zeros, src_ref, sem)
    fill.start()
    fill.wait()

    def body(t, _):
        src_ref[d1_ref[t]] = t
        src_ref[d2_ref[t]] = t
        return 0
    lax.fori_loop(0, T, body, 0, unroll=8)


def invert(d1, d2, R):
    T = d1.shape[0]
    grid_spec = pltpu.PrefetchScalarGridSpec(
        num_scalar_prefetch=2, grid=(1,), in_specs=[],
        out_specs=pl.BlockSpec(memory_space=pltpu.SMEM),
        scratch_shapes=[pltpu.VMEM((R,), I32), pltpu.SemaphoreType.DMA(())])
    return pl.pallas_call(
        functools.partial(_invert_kernel, T=T),
        grid_spec=grid_spec,
        out_shape=jax.ShapeDtypeStruct((R,), I32),
        compiler_params=_params("arbitrary"),
        name="moe_invert",
    )(d1, d2)


GATHER_CHUNK = 8


def _tile_copy(src_hbm, row, dst, slot, r, sem):
    return pltpu.make_async_copy(src_hbm.at[pl.ds(row * TOKEN_ROWS, TOKEN_ROWS)],
                                 dst.at[slot, pl.ds(r * TOKEN_ROWS, TOKEN_ROWS)], sem.at[slot])


def _expert_kernel(te_ref, first_ref, nxt_ref, wsl_ref, nch_ref, src_ref,
                   hp_hbm, w1_hbm, w3_hbm, w2_hbm, o_ref,
                   xbuf, r1, r3, r2, w1b, w3b, w2b, gsem, wsem, *, tr, base):
    i = pl.program_id(0)
    n = pl.num_programs(0)
    slot = i % 2

    def weight_copies(e, ws):
        return [pltpu.make_async_copy(w1_hbm.at[base + e], r1.at[ws], wsem.at[ws, 0]),
                pltpu.make_async_copy(w3_hbm.at[base + e], r3.at[ws], wsem.at[ws, 1]),
                pltpu.make_async_copy(w2_hbm.at[base + e], r2.at[ws], wsem.at[ws, 2])]

    def issue_rows(tile, slot, c0, c1):
        def body(c, _):
            for u in range(GATHER_CHUNK):
                r = c * GATHER_CHUNK + u
                _tile_copy(hp_hbm, src_ref[tile * tr + r], xbuf, slot, r, gsem).start()
            return 0
        lax.fori_loop(c0, c1, body, 0)

    def wait_rows(tile, slot):
        rows = GATHER_CHUNK * TOKEN_ROWS
        def body(c, _):
            pltpu.make_async_copy(hp_hbm.at[pl.ds(0, rows)], xbuf.at[slot, pl.ds(0, rows)],
                                  gsem.at[slot]).wait()
            return 0
        lax.fori_loop(0, nch_ref[tile], body, 0)

    @pl.when(i == 0)
    def _():
        xbuf[...] = jnp.zeros_like(xbuf)
        for cp in weight_copies(te_ref[0], 0):
            cp.start(priority=1)
        issue_rows(0, 0, 0, nch_ref[0])

    wait_rows(i, slot)

    nxt_tile = jnp.minimum(i + 1, n - 1)
    issue_rows(nxt_tile, 1 - slot, 0, jnp.where(i + 1 < n, nch_ref[nxt_tile], 0))

    @pl.when(first_ref[i] == 1)
    def _():
        ws = wsl_ref[i]
        for cp in weight_copies(te_ref[i], ws):
            cp.wait()

        @pl.when(nxt_ref[i] >= 0)
        def _():
            for cp in weight_copies(nxt_ref[i], 1 - ws):
                cp.start(priority=1)

        w1b[...] = r1[ws].astype(BF16)
        w3b[...] = r3[ws].astype(BF16)
        w2b[...] = r2[ws].astype(BF16)

    def expert_mlp(rows):
        x = jnp.concatenate([p.astype(BF16) for p in _load_token_tiles(xbuf.at[slot], 0, rows)],
                            axis=1)
        h1 = jnp.dot(x, w1b[...], preferred_element_type=F32)
        h3 = jnp.dot(x, w3b[...], preferred_element_type=F32)
        hm = (h1 * _sigmoid(h1) * h3).astype(BF16)
        _store_token_tiles(o_ref, jnp.dot(hm, w2b[...], preferred_element_type=F32), rows)

    half_chunks = tr // (2 * GATHER_CHUNK)

    @pl.when(nch_ref[i] > half_chunks)
    def _():
        expert_mlp(tr)

    @pl.when((nch_ref[i] > 0) & (nch_ref[i] <= half_chunks))
    def _():
        expert_mlp(tr // 2)
        o_ref[tr // 2 * TOKEN_ROWS:, :] = jnp.zeros((tr // 2 * TOKEN_ROWS, TOKEN_LANES), F32)

    @pl.when(nch_ref[i] == 0)
    def _():
        o_ref[...] = jnp.zeros_like(o_ref)


def moe_experts(hp, w1, w3, w2, layer, tables, src):
    D, F = w1.shape[-2:]
    tr = MOE_ROWS
    te, first, nxt, wsl, nch = tables
    NT = te.shape[0]
    w1f = w1.reshape(-1, D, F)
    w3f = w3.reshape(-1, D, F)
    w2f = w2.reshape(-1, F, D)
    hbm = pl.BlockSpec(memory_space=pl.ANY)
    grid_spec = pltpu.PrefetchScalarGridSpec(
        num_scalar_prefetch=6,
        grid=(NT,),
        in_specs=[hbm, hbm, hbm, hbm],
        out_specs=pl.BlockSpec((tr * TOKEN_ROWS, TOKEN_LANES), lambda i, *_: (i, 0)),
        scratch_shapes=[pltpu.VMEM((2, tr * TOKEN_ROWS, TOKEN_LANES), F32),
                        pltpu.VMEM((2, D, F), F32),
                        pltpu.VMEM((2, D, F), F32),
                        pltpu.VMEM((2, F, D), F32),
                        pltpu.VMEM((D, F), BF16),
                        pltpu.VMEM((D, F), BF16),
                        pltpu.VMEM((F, D), BF16),
                        pltpu.SemaphoreType.DMA((2,)),
                        pltpu.SemaphoreType.DMA((2, 3))],
    )
    return pl.pallas_call(
        functools.partial(_expert_kernel, tr=tr, base=layer * N_EXPERTS),
        grid_spec=grid_spec,
        out_shape=jax.ShapeDtypeStruct((NT * tr * TOKEN_ROWS, TOKEN_LANES), F32),
        compiler_params=_params("arbitrary"),
        name="moe_experts",
    )(te, first, nxt, wsl, nch, src, hp, w1f, w3f, w2f)


def _combine_kernel(d1_ref, d2_ref, x_ref, meta_ref, ys_hbm, *rest, tt, with_norm):
    if with_norm:
        g_ref, o_ref, hn_ref, buf, sem = rest
    else:
        o_ref, buf, sem = rest
    i = pl.program_id(0)
    n = pl.num_programs(0)
    slot = i % 2

    def issue(tile, slot):
        def body(c, _):
            for u in range(GATHER_CHUNK):
                r = c * GATHER_CHUNK + u
                t = tile * tt + r
                _tile_copy(ys_hbm, d1_ref[t], buf, slot, r, sem).start(priority=0)
                _tile_copy(ys_hbm, d2_ref[t], buf, slot, tt + r, sem).start(priority=1)
            return 0
        lax.fori_loop(0, tt // GATHER_CHUNK, body, 0)

    @pl.when(i == 0)
    def _():
        issue(0, 0)

    pltpu.make_async_copy(ys_hbm.at[pl.ds(0, 2 * tt * TOKEN_ROWS)], buf.at[slot], sem.at[slot]).wait()

    @pl.when(i + 1 < n)
    def _():
        issue(i + 1, 1 - slot)

    w1 = meta_ref[:, 2:3]
    w2 = meta_ref[:, 3:4]
    pa = _load_token_tiles(buf.at[slot], 0, tt)
    pb = _load_token_tiles(buf.at[slot], tt * TOKEN_ROWS, tt)
    cols = [x_ref[:, j * TOKEN_LANES:(j + 1) * TOKEN_LANES] + w1 * pa[j] + w2 * pb[j]
            for j in range(TOKEN_ROWS)]
    xn = jnp.concatenate(cols, axis=1)
    o_ref[...] = xn
    if with_norm:
        hn_ref[...] = (_rms(xn) * g_ref[...]).astype(hn_ref.dtype)


def moe_combine(x, meta, ys, d1, d2, g_next=None):
    T, D = x.shape
    tt = COMB_ROWS
    with_norm = g_next is not None
    row = pl.BlockSpec((tt, D), lambda i, d1, d2: (i, 0))
    in_specs = [row, pl.BlockSpec((tt, LANES), lambda i, d1, d2: (i, 0)),
                pl.BlockSpec(memory_space=pl.ANY)]
    args = [d1, d2, x, meta, ys]
    out_specs, out_shape = row, jax.ShapeDtypeStruct((T, D), F32)
    if with_norm:
        in_specs.append(pl.BlockSpec((1, D), lambda i, d1, d2: (0, 0)))
        args.append(g_next.reshape(1, D))
        out_specs, out_shape = [row, row], [out_shape, jax.ShapeDtypeStruct((T, D), BF16)]
    grid_spec = pltpu.PrefetchScalarGridSpec(
        num_scalar_prefetch=2,
        grid=(T // tt,),
        in_specs=in_specs,
        out_specs=out_specs,
        scratch_shapes=[pltpu.VMEM((2, 2 * tt * TOKEN_ROWS, TOKEN_LANES), F32),
                        pltpu.SemaphoreType.DMA((2,))],
    )
    return pl.pallas_call(
        functools.partial(_combine_kernel, tt=tt, with_norm=with_norm),
        grid_spec=grid_spec,
        out_shape=out_shape,
        compiler_params=_params("arbitrary"),
        name="moe_combine",
    )(*args)


def _tile_tables(cnt, T):
    E = N_EXPERTS
    counts = cnt[0, ROUTER_LANE0:ROUTER_LANE0 + E].astype(I32)
    tiles_e = (counts + MOE_ROWS - 1) // MOE_ROWS
    tile_end = jnp.cumsum(tiles_e)
    tile_start = tile_end - tiles_e
    nact = tile_end[-1]
    NT = (2 * T) // MOE_ROWS + E
    tid = jnp.arange(NT, dtype=I32)
    te = jnp.sum((jnp.minimum(tid, nact - 1)[:, None] >= tile_end[None, :]).astype(I32), axis=1)
    te = jnp.minimum(te, E - 1)
    active = tid < nact
    first = (active & (tid == tile_start[te])).astype(I32)
    eid = jnp.arange(E, dtype=I32)
    later = (eid[None, :] > eid[:, None]) & (tiles_e[None, :] > 0)
    nxt_e = jnp.min(jnp.where(later, eid[None, :], E), axis=1)
    nxt_e = jnp.where(nxt_e == E, -1, nxt_e)
    ordinal = jnp.cumsum((tiles_e > 0).astype(I32)) - 1
    valid = jnp.clip(counts[te] - (tid - tile_start[te]) * MOE_ROWS, 0, MOE_ROWS)
    nch = jnp.where(active, (valid + GATHER_CHUNK - 1) // GATHER_CHUNK, 0)
    return (te, first, nxt_e[te], ordinal[te] % 2, nch), NT * MOE_ROWS


def _lower_bound_logs(lb_param, layer):
    p = jax.nn.softmax(lb_param.astype(F32), axis=0)
    c = jnp.cumsum(p, axis=0)
    lb = c[layer] - c[0]
    return jnp.log(lb), jnp.log1p(-lb)


def kernel(x, mem, norm_mix, w_in, hg_lower_bounds, hg_norm, conv_w, fox_f_bias, fox_q_norm,
           fox_k_norm, w_out, norm_cross, norm_mem, w_cq, w_ck, w_cv, w_co, cross_q_norm,
           cross_k_norm, norm_ffn, router_group_w, router_group_b, router_expert_w,
           router_expert_b, moe_w1, moe_w3, moe_w2):
    B, S, D = x.shape
    M = mem.shape[1]
    L = w_in.shape[0]
    T = B * S
    HGW = 4 * HG_WIDTH
    CONV3 = 3 * CONV_WIDTH
    PB = CONV3 + 3 * FOX_WIDTH
    FQ0, FK0, FV0 = 0, FOX_WIDTH, 2 * FOX_WIDTH
    CV0 = 3 * FOX_WIDTH
    dhc = D // CROSS_HEADS
    x2 = x.reshape(T, D)
    mem2 = mem.reshape(B * M, D)
    w_in_t = jnp.swapaxes(w_in, 1, 2)
    hn = rmsnorm(x2, norm_mix[0])
    for l in range(L):
        proj_a = matmul([hn], w_in_t, l, 0, HGW, F32, w_t=True)
        proj_b = matmul([hn], w_in_t, l, HGW, PB, BF16, w_t=True, rotate=CONV3, head_norms=(
            (FQ0, FK0, fox_q_norm[l], FOX_DIM ** -0.5), (FK0, FV0, fox_k_norm[l], 1.0)))
        loglb, l1mlb = _lower_bound_logs(hg_lower_bounds, l)
        y_hg = hgrn2(proj_a.reshape(B, S, HGW), loglb, l1mlb, hg_norm[l])
        pb3 = proj_b.reshape(B, S, PB)
        y_conv = short_conv(pb3, CV0, conv_w[l])
        c = fox_gate(hn.reshape(B, S, D), w_in_t, l, HGW + PB, fox_f_bias[l])
        y_fox = fox_attention(pb3, FQ0, FK0, FV0, c)
        mix = [y_hg.reshape(T, HG_WIDTH), y_conv.reshape(T, CONV_WIDTH), y_fox.reshape(T, FOX_WIDTH)]
        x2, qcn = out_proj_cross_q(mix, x2, w_out, w_cq, l, norm_cross[l], cross_q_norm[l])
        memn = rmsnorm(mem2, norm_mem[l])
        kcn = matmul([memn], w_ck, l, 0, D, BF16, head_norms=((0, D, cross_k_norm[l], 1.0),))
        vc = matmul([memn], w_cv, l, 0, D, BF16)
        wr = jnp.concatenate([router_group_w[l], router_expert_w[l]], axis=1)
        wr = jnp.pad(wr, ((0, 0), (0, LANES - wr.shape[1])))
        br = jnp.concatenate([router_group_b[l], router_expert_b[l]])
        br = jnp.pad(br, (0, LANES - br.shape[0]))
        x3, hp, logits = cross_attention_ffn_in(
            qcn.reshape(B, S, D), kcn.reshape(B, M, D), vc.reshape(B, M, D), x2.reshape(B, S, D),
            w_co, l, norm_ffn[l], wr, br)
        x2 = x3.reshape(T, D)
        meta, cnt = route(logits)
        d = plan(meta, cnt)
        tables, R = _tile_tables(cnt, T)
        src = invert(d[0], d[1], R)
        ys = moe_experts(hp, moe_w1, moe_w3, moe_w2, l, tables, src)
        if l + 1 < L:
            x2, hn = moe_combine(x2, meta, ys, d[0], d[1], norm_mix[l + 1])
        else:
            x2 = moe_combine(x2, meta, ys, d[0], d[1])
    return x2.reshape(B, S, D)
```

```python
import functools

import numpy as np
import jax
import jax.numpy as jnp
from jax import lax
from jax.experimental import pallas as pl
from jax.experimental.pallas import tpu as pltpu

F32 = jnp.float32
BF16 = jnp.bfloat16
I32 = jnp.int32
EPS = 1e-6

HG_HEADS = 4
HG_DIM = 128
HG_WIDTH = HG_HEADS * HG_DIM
CONV_WIDTH = 512
FOX_HEADS = 8
FOX_DIM = 128
FOX_WIDTH = FOX_HEADS * FOX_DIM
CROSS_HEADS = 4
MOE_GROUPS = 4
MOE_EXPERTS = 8
N_EXPERTS = MOE_GROUPS * MOE_EXPERTS
ROUTER_LANE0 = MOE_GROUPS

LANES = 128
SUBLANES = 8
VMEM_LIMIT = 56 * 1024 * 1024

NORM_ROWS = 512
MM_TM = 2048
MM_TN = 512
HG_CHUNK = 128
FOX_TQ = 512
FOX_TK = 512
FOX_HEADS_PER_STEP = 8
CROSS_TS = 512
ROUTE_ROWS = 1024
MOE_ROWS_LOG2 = 8
MOE_ROWS = 1 << MOE_ROWS_LOG2
COMB_ROWS = 512


def _params(*sem):
    return pltpu.CompilerParams(dimension_semantics=sem, vmem_limit_bytes=VMEM_LIMIT)


def _sigmoid(x):
    return 1.0 / (1.0 + jnp.exp(-x))


def _log_sigmoid(x):
    return jnp.minimum(x, 0.0) - jnp.log1p(jnp.exp(-jnp.abs(x)))


TOKEN_ROWS = 16
TOKEN_LANES = 128


def _store_token_tiles(ref, y, rows):
    for j in range(TOKEN_ROWS):
        ref[pl.ds(j, rows, stride=TOKEN_ROWS), :] = y[:, j * TOKEN_LANES:(j + 1) * TOKEN_LANES]


def _load_token_tiles(ref, base, rows):
    return [ref[pl.ds(base + j, rows, stride=TOKEN_ROWS), :] for j in range(TOKEN_ROWS)]


def _rms(x):
    return x * lax.rsqrt(jnp.mean(x * x, axis=-1, keepdims=True) + EPS)


def _rmsnorm_kernel(x_ref, g_ref, o_ref):
    o_ref[...] = (_rms(x_ref[...]) * g_ref[...]).astype(o_ref.dtype)


def rmsnorm(x, g, out_dtype=BF16):
    R, D = x.shape
    return pl.pallas_call(
        _rmsnorm_kernel,
        grid=(R // NORM_ROWS,),
        in_specs=[pl.BlockSpec((NORM_ROWS, D), lambda i: (i, 0)),
                  pl.BlockSpec((1, D), lambda i: (0, 0))],
        out_specs=pl.BlockSpec((NORM_ROWS, D), lambda i: (i, 0)),
        out_shape=jax.ShapeDtypeStruct((R, D), out_dtype),
        compiler_params=_params("parallel"),
        name="rmsnorm",
    )(x, g.reshape(1, D))


def _head_rmsnorm(x, g, dh, scale):
    heads = [_rms(x[:, h * dh:(h + 1) * dh]) * g * scale for h in range(x.shape[1] // dh)]
    return heads[0] if len(heads) == 1 else jnp.concatenate(heads, axis=1)


def _matmul_kernel(*refs, n_parts, w_t, norm_tiles, dh):
    a_refs = refs[:n_parts]
    w_ref = refs[n_parts]
    if norm_tiles:
        g_ref, o_ref, wb_ref = refs[n_parts + 1:]
    else:
        o_ref, wb_ref = refs[n_parts + 1:]
    j = pl.program_id(0)

    @pl.when(pl.program_id(1) == 0)
    def _():
        wb_ref[...] = w_ref[...].astype(BF16)

    if n_parts == 1:
        a = a_refs[0][...]
    else:
        a = jnp.concatenate([r[...] for r in a_refs], axis=1)
    if w_t:
        acc = lax.dot_general(a, wb_ref[...], (((1,), (1,)), ((), ())), preferred_element_type=F32)
    else:
        acc = jnp.dot(a, wb_ref[...], preferred_element_type=F32)
    if not norm_tiles:
        o_ref[...] = acc.astype(o_ref.dtype)
        return
    plain = True
    for k, (j0, j1, scale) in enumerate(norm_tiles):
        hit = (j >= j0) & (j < j1)
        plain = plain & jnp.logical_not(hit)

        @pl.when(hit)
        def _(k=k, scale=scale):
            o_ref[...] = _head_rmsnorm(acc, g_ref[k:k + 1, :], dh, scale).astype(o_ref.dtype)

    @pl.when(plain)
    def _():
        o_ref[...] = acc.astype(o_ref.dtype)


def matmul(a_parts, w, layer, col0, n, out_dtype, w_t=False, head_norms=(), rotate=0,
           tm=MM_TM, tn=MM_TN):
    M = a_parts[0].shape[0]
    K = w.shape[2] if w_t else w.shape[1]
    assert sum(p.shape[1] for p in a_parts) == K
    tm = min(tm, M)
    assert col0 % tn == 0 and n % tn == 0 and M % tm == 0 and rotate % tn == 0
    cb, nt, rot = col0 // tn, n // tn, rotate // tn
    in_specs = [pl.BlockSpec((tm, p.shape[1]), lambda j, i: (i, 0)) for p in a_parts]
    if w_t:
        in_specs.append(pl.BlockSpec((None, tn, K), lambda j, i: (layer, cb + (j + rot) % nt, 0)))
    else:
        in_specs.append(pl.BlockSpec((None, K, tn), lambda j, i: (layer, 0, cb + (j + rot) % nt)))
    args = list(a_parts) + [w]
    norm_tiles, dh = (), 0
    if head_norms:
        dh = head_norms[0][2].shape[0]
        assert all(c0 % tn == 0 and c1 % tn == 0 and g.shape[0] == dh and tn % dh == 0
                   for c0, c1, g, _ in head_norms)
        norm_tiles = tuple((c0 // tn, c1 // tn, s) for c0, c1, _, s in head_norms)
        gains = jnp.stack([g for _, _, g, _ in head_norms])
        in_specs.append(pl.BlockSpec(gains.shape, lambda j, i: (0, 0)))
        args.append(gains)
    return pl.pallas_call(
        functools.partial(_matmul_kernel, n_parts=len(a_parts), w_t=w_t, norm_tiles=norm_tiles,
                          dh=dh),
        grid=(n // tn, M // tm),
        in_specs=in_specs,
        out_specs=pl.BlockSpec((tm, tn), lambda j, i: (i, j)),
        out_shape=jax.ShapeDtypeStruct((M, n), out_dtype),
        scratch_shapes=[pltpu.VMEM((tn, K) if w_t else (K, tn), BF16)],
        compiler_params=_params("arbitrary", "arbitrary"),
        name="matmul",
    )(*args)


W_CHUNK = 256


def _load_weight(w_hbm, layer, wb, stg, sem):
    n_chunks = wb.shape[0] // W_CHUNK

    def copy(c):
        return pltpu.make_async_copy(w_hbm.at[layer, pl.ds(c * W_CHUNK, W_CHUNK)], stg.at[c % 2],
                                     sem.at[c % 2])
    copy(0).start()
    for c in range(n_chunks):
        if c + 1 < n_chunks:
            copy(c + 1).start()
        copy(c).wait()
        wb[c * W_CHUNK:(c + 1) * W_CHUNK, :] = stg[c % 2].astype(BF16)


def _outq_kernel(a1_ref, a2_ref, a3_ref, x_ref, wo_hbm, wq_hbm, gn_ref, gq_ref, xo_ref, q_ref,
                 wo_b, wq_b, stg, sem, *, layer, dh, scale):
    @pl.when(pl.program_id(0) == 0)
    def _():
        _load_weight(wo_hbm, layer, wo_b, stg, sem)
        _load_weight(wq_hbm, layer, wq_b, stg, sem)

    a = jnp.concatenate([a1_ref[...], a2_ref[...], a3_ref[...]], axis=1)
    xn = x_ref[...] + jnp.dot(a, wo_b[...], preferred_element_type=F32)
    xo_ref[...] = xn
    hc = (_rms(xn) * gn_ref[...]).astype(BF16)
    q = jnp.dot(hc, wq_b[...], preferred_element_type=F32)
    q_ref[...] = _head_rmsnorm(q, gq_ref[...], dh, scale).astype(q_ref.dtype)


def out_proj_cross_q(mix, x, w_out, w_cq, layer, g_cross, g_q, tm=512):
    T, D = x.shape
    dh = g_q.shape[0]
    row = lambda w: pl.BlockSpec((tm, w), lambda i: (i, 0))
    hbm = pl.BlockSpec(memory_space=pl.ANY)
    return pl.pallas_call(
        functools.partial(_outq_kernel, layer=layer, dh=dh, scale=dh ** -0.5),
        grid=(T // tm,),
        in_specs=[row(mix[0].shape[1]), row(mix[1].shape[1]), row(mix[2].shape[1]), row(D), hbm, hbm,
                  pl.BlockSpec((1, D), lambda i: (0, 0)), pl.BlockSpec((1, dh), lambda i: (0, 0))],
        out_specs=[row(D), row(D)],
        out_shape=[jax.ShapeDtypeStruct((T, D), F32), jax.ShapeDtypeStruct((T, D), BF16)],
        scratch_shapes=[pltpu.VMEM((D, D), BF16), pltpu.VMEM((D, D), BF16),
                        pltpu.VMEM((2, W_CHUNK, D), F32), pltpu.SemaphoreType.DMA((2,))],
        compiler_params=_params("arbitrary"),
        name="out_proj_cross_q",
    )(*mix, x, w_out, w_cq, g_cross.reshape(1, D), g_q.reshape(1, dh))


def _hgrn_consts(C):
    nlev = int(np.log2(C))
    t = np.arange(C)[:, None]
    u = np.arange(C)[None, :]
    tri = (u <= t).astype(np.float32)
    mall2 = np.concatenate([tri, tri], axis=1)
    tt = np.arange(C)[:, None]
    ss = np.arange(C)[None, :]
    lev = np.full((C, C), nlev + 1, np.int32)
    x = tt ^ ss
    hb = np.zeros_like(x)
    for j in range(nlev):
        hb = np.where((x >> j) & 1, j, hb)
    lev = np.where(tt > ss, hb, lev)
    lev = np.where(tt == ss, nlev, lev)
    return jnp.asarray(mall2, BF16), jnp.asarray(lev, jnp.int32)


def _level_sums(lf, b, row, j, C):
    m = 1 << j
    second = (row & m) != 0
    if j == 0:
        return jnp.where(second, lf, 0.0)
    if j == 1:
        prev = pltpu.roll(lf, 1, axis=0)
        nxt = pltpu.roll(lf, C - 1, axis=0)
        first = jnp.where((row & 1) == 0, nxt, 0.0)
        return jnp.where(second, jnp.where((row & 1) != 0, lf + prev, lf), first)
    g = 2 * m
    d = b.shape[1]
    bsel = jnp.broadcast_to(b.reshape(C // g, g, d)[:, m - 1:m, :], (C // g, g, d)).reshape(C, d)
    return jnp.where(second, b - bsel, bsel - b)


def _hgrn_kernel(x_ref, mall_ref, lev_ref, loglb_ref, l1mlb_ref, ng_ref, o_ref, st_ref, *, C, nlev):
    @pl.when(pl.program_id(1) == 0)
    def _():
        st_ref[...] = jnp.zeros_like(st_ref)

    W = HG_WIDTH
    d = HG_DIM
    lev = lev_ref[...]
    mall = mall_ref[...]
    row = lax.broadcasted_iota(jnp.int32, (C, d), 0)
    nt = (((1,), (1,)), ((), ()))

    def side(xs):
        return jnp.concatenate(xs, axis=1)

    def block_diag(xa, xb):
        return jnp.concatenate([side([xa, jnp.zeros_like(xb)]), side([jnp.zeros_like(xa), xb])],
                               axis=0)

    for h0 in range(0, HG_HEADS, 2):
        q, lf, kk, vb, g = [], [], [], [], []
        for h in (h0, h0 + 1):
            sl = slice(h * d, (h + 1) * d)
            q.append(x_ref[0, :, h * d:(h + 1) * d])
            z = x_ref[0, :, W + h * d:W + (h + 1) * d]
            vb.append(x_ref[0, :, 2 * W + h * d:2 * W + (h + 1) * d].astype(BF16))
            g.append(x_ref[0, :, 3 * W + h * d:3 * W + (h + 1) * d])
            a = loglb_ref[:, sl]
            bt = l1mlb_ref[:, sl] + _log_sigmoid(z)
            lf.append(jnp.maximum(a, bt) + jnp.log1p(jnp.exp(-jnp.abs(a - bt))))
            kk.append(1.0 - jnp.exp(lf[-1]))
        hi = [x.astype(BF16) for x in lf]
        lo = [(x - y.astype(F32)).astype(BF16) for x, y in zip(lf, hi)]
        b2 = jnp.dot(mall, jnp.concatenate([side(hi), side(lo)], axis=0),
                     preferred_element_type=F32)
        b = [b2[:, :d], b2[:, d:]]
        st = [st_ref[h0], st_ref[h0 + 1]]
        o2 = lax.dot_general(side([(q[i] * jnp.exp(b[i])).astype(BF16) for i in (0, 1)]),
                             block_diag(st[0].astype(BF16), st[1].astype(BF16)), nt,
                             preferred_element_type=F32)
        att = [jnp.where(lev == nlev, jnp.sum(q[i] * kk[i], axis=-1, keepdims=True), 0.0)
               for i in (0, 1)]
        for j in range(nlev):
            second = (row & (1 << j)) != 0
            xq, xk = [], []
            for i in (0, 1):
                e = jnp.exp(_level_sums(lf[i], b[i], row, j, C))
                xq.append(jnp.where(second, q[i] * e, 0.0).astype(BF16))
                xk.append(jnp.where(second, 0.0, kk[i] * e).astype(BF16))
            am = lax.dot_general(side(xq), block_diag(xk[0], xk[1]), nt,
                                 preferred_element_type=F32)
            att = [jnp.where(lev == j, am[:, i * C:(i + 1) * C], att[i]) for i in (0, 1)]
        o2 = o2 + jnp.dot(side([x.astype(BF16) for x in att]), block_diag(vb[0], vb[1]),
                          preferred_element_type=F32)
        for i, h in enumerate((h0, h0 + 1)):
            bl = b[i][C - 1:C, :]
            kh = (kk[i] * jnp.exp(bl - b[i])).astype(BF16)
            st_ref[h] = st[i] * jnp.exp(bl) + lax.dot_general(
                vb[i], kh, (((0,), (0,)), ((), ())), preferred_element_type=F32)
            o = o2[:, i * d:(i + 1) * d]
            y = o * lax.rsqrt(jnp.mean(o * o, axis=-1, keepdims=True) + EPS) * ng_ref[...]
            o_ref[0, :, h * d:(h + 1) * d] = (y * (g[i] * _sigmoid(g[i]))).astype(o_ref.dtype)


def hgrn2(proj_a, loglb, l1mlb, norm_g):
    B, S, _ = proj_a.shape
    C = HG_CHUNK
    nlev = int(np.log2(C))
    mall, lev = _hgrn_consts(C)
    return pl.pallas_call(
        functools.partial(_hgrn_kernel, C=C, nlev=nlev),
        grid=(B, S // C),
        in_specs=[pl.BlockSpec((1, C, 4 * HG_WIDTH), lambda b, c: (b, c, 0)),
                  pl.BlockSpec(mall.shape, lambda b, c: (0, 0)),
                  pl.BlockSpec((C, C), lambda b, c: (0, 0)),
                  pl.BlockSpec((1, HG_WIDTH), lambda b, c: (0, 0)),
                  pl.BlockSpec((1, HG_WIDTH), lambda b, c: (0, 0)),
                  pl.BlockSpec((1, HG_DIM), lambda b, c: (0, 0))],
        out_specs=pl.BlockSpec((1, C, HG_WIDTH), lambda b, c: (b, c, 0)),
        out_shape=jax.ShapeDtypeStruct((B, S, HG_WIDTH), BF16),
        scratch_shapes=[pltpu.VMEM((HG_HEADS, HG_DIM, HG_DIM), F32)],
        compiler_params=_params("parallel", "arbitrary"),
        name="hgrn2",
    )(proj_a, mall, lev, loglb.reshape(1, HG_WIDTH), l1mlb.reshape(1, HG_WIDTH),
      norm_g.reshape(1, HG_DIM))


def _conv_kernel(b_ref, c_ref, h_ref, w_ref, o_ref):
    u = c_ref[0].astype(F32) * h_ref[0].astype(F32)
    row = lax.broadcasted_iota(jnp.int32, u.shape, 0)
    u1 = jnp.where(row >= 1, pltpu.roll(u, 1, axis=0), 0.0)
    u2 = jnp.where(row >= 2, pltpu.roll(u, 2, axis=0), 0.0)
    y = w_ref[0:1, :] * u2 + w_ref[1:2, :] * u1 + w_ref[2:3, :] * u
    o_ref[0] = (b_ref[0].astype(F32) * y).astype(o_ref.dtype)


def short_conv(proj_b, col0, w):
    B, S, _ = proj_b.shape
    cw = CONV_WIDTH
    assert col0 % cw == 0
    spec = lambda k: pl.BlockSpec((1, S, cw), lambda b: (b, 0, col0 // cw + k))
    return pl.pallas_call(
        _conv_kernel,
        grid=(B,),
        in_specs=[spec(0), spec(1), spec(2), pl.BlockSpec((3, cw), lambda b: (0, 0))],
        out_specs=pl.BlockSpec((1, S, cw), lambda b: (b, 0, 0)),
        out_shape=jax.ShapeDtypeStruct((B, S, cw), BF16),
        compiler_params=_params("parallel"),
        name="short_conv",
    )(proj_b, proj_b, proj_b, w)


def _fox_gate_kernel(hn_ref, wf_ref, bias_ref, c_ref):
    gl = lax.dot_general(wf_ref[...].astype(BF16), hn_ref[0], (((1,), (1,)), ((), ())),
                         preferred_element_type=F32)
    c = _log_sigmoid(gl + bias_ref[...])
    S = c.shape[1]
    lane = lax.broadcasted_iota(jnp.int32, c.shape, 1)
    sh = 1
    while sh < S:
        c = c + jnp.where(lane >= sh, pltpu.roll(c, sh, axis=1), 0.0)
        sh *= 2
    c_ref[0] = c


def fox_gate(hn, w_t, layer, row0, bias):
    B, S, D = hn.shape
    H = bias.shape[0]
    assert row0 % H == 0
    return pl.pallas_call(
        _fox_gate_kernel,
        grid=(B,),
        in_specs=[pl.BlockSpec((1, S, D), lambda b: (b, 0, 0)),
                  pl.BlockSpec((None, H, D), lambda b: (layer, row0 // H, 0)),
                  pl.BlockSpec((H, 1), lambda b: (0, 0))],
        out_specs=pl.BlockSpec((1, H, S), lambda b: (b, 0, 0)),
        out_shape=jax.ShapeDtypeStruct((B, H, S), F32),
        compiler_params=_params("parallel"),
        name="fox_gate",
    )(hn, w_t, bias.reshape(H, 1))


def _fox_kernel(q_ref, k_ref, v_ref, c_ref, o_ref, *, tq, tk, nh):
    qi = pl.program_id(2)
    nt = (((1,), (1,)), ((), ()))
    ones = jnp.ones((tk, FOX_DIM), BF16)
    heads = [slice(h * FOX_DIM, (h + 1) * FOX_DIM) for h in range(nh)]
    qs = [q_ref[0, :, sl] for sl in heads]

    def step(ki, carry, diag):
        out = []
        for h, sl in enumerate(heads):
            m, acc = carry[h]
            k = k_ref[0, pl.ds(ki * tk, tk), sl]
            v1 = jnp.concatenate([v_ref[0, pl.ds(ki * tk, tk), sl], ones], axis=1)
            ck = c_ref[0, h, pl.ds(ki, 1), :]
            s = lax.dot_general(qs[h], k, nt, preferred_element_type=F32) - ck
            if diag is not None:
                r = lax.broadcasted_iota(jnp.int32, s.shape, 0)
                c = lax.broadcasted_iota(jnp.int32, s.shape, 1)
                s = jnp.where(c + diag * tk <= r, s, -jnp.inf)
            m_new = jnp.maximum(m, jnp.max(s, axis=-1, keepdims=True))
            alpha = jnp.exp(m - m_new)
            p = jnp.exp((s - m_new).astype(BF16))
            out.append((m_new, alpha * acc + jnp.dot(p, v1, preferred_element_type=F32)))
        return tuple(out)

    init = tuple((jnp.full((tq, 1), -jnp.inf, F32), jnp.zeros((tq, 2 * FOX_DIM), F32))
                 for _ in heads)
    n_full = qi * (tq // tk)
    carry = lax.fori_loop(0, n_full, lambda ki, cr: step(ki, cr, None), init)
    for d in range(tq // tk):
        carry = step(n_full + d, carry, d)
    for h, sl in enumerate(heads):
        acc = carry[h][1]
        o_ref[0, :, sl] = (acc[:, :FOX_DIM] / acc[:, FOX_DIM:]).astype(o_ref.dtype)


def fox_attention(proj_b, q_col0, k_col0, v_col0, c):
    B, S, _ = proj_b.shape
    tq, tk, nh = FOX_TQ, FOX_TK, FOX_HEADS_PER_STEP
    H = FOX_HEADS
    w = nh * FOX_DIM
    assert tq % tk == 0 and H % nh == 0 and all(c0 % w == 0 for c0 in (q_col0, k_col0, v_col0))
    qb, kb, vb = q_col0 // w, k_col0 // w, v_col0 // w
    c4 = c.reshape(B, H, S // tk, tk)
    return pl.pallas_call(
        functools.partial(_fox_kernel, tq=tq, tk=tk, nh=nh),
        grid=(B, H // nh, S // tq),
        in_specs=[pl.BlockSpec((1, tq, w), lambda b, h, i: (b, i, qb + h)),
                  pl.BlockSpec((1, S, w), lambda b, h, i: (b, 0, kb + h)),
                  pl.BlockSpec((1, S, w), lambda b, h, i: (b, 0, vb + h)),
                  pl.BlockSpec((1, nh, S // tk, tk), lambda b, h, i: (b, h, 0, 0))],
        out_specs=pl.BlockSpec((1, tq, w), lambda b, h, i: (b, i, h)),
        out_shape=jax.ShapeDtypeStruct((B, S, FOX_WIDTH), BF16),
        compiler_params=_params("parallel", "parallel", "arbitrary"),
        name="fox_attention",
    )(proj_b, proj_b, proj_b, c4)


def _cross_kernel(q_ref, k_ref, v_ref, x_ref, wo_hbm, gn_ref, wr_ref, br_ref,
                  xo_ref, hp_ref, lg_ref, wo_b, stg, sem, *, layer, dh):
    @pl.when((pl.program_id(0) == 0) & (pl.program_id(1) == 0))
    def _():
        _load_weight(wo_hbm, layer, wo_b, stg, sem)

    nt = (((1,), (1,)), ((), ()))
    heads = []
    for h in range(CROSS_HEADS):
        sl = slice(h * dh, (h + 1) * dh)
        s = lax.dot_general(q_ref[0, :, sl], k_ref[0, :, sl], nt, preferred_element_type=F32)
        m = jnp.max(s, axis=-1, keepdims=True)
        p = jnp.exp(s - m)
        l = jnp.sum(p, axis=-1, keepdims=True)
        o = jnp.dot(p.astype(BF16), v_ref[0, :, sl], preferred_element_type=F32)
        heads.append((o / l).astype(BF16))
    xn = x_ref[0] + jnp.dot(jnp.concatenate(heads, axis=1), wo_b[...], preferred_element_type=F32)
    xo_ref[0] = xn
    y = _rms(xn) * gn_ref[...]
    _store_token_tiles(hp_ref, y, y.shape[0])
    yh = y.astype(BF16)
    yl = (y - yh.astype(F32)).astype(BF16)
    wr = wr_ref[...]
    wh = wr.astype(BF16)
    wl = (wr - wh.astype(F32)).astype(BF16)
    lg_ref[...] = (jnp.dot(yh, wh, preferred_element_type=F32)
                   + jnp.dot(yl, wh, preferred_element_type=F32)
                   + jnp.dot(yh, wl, preferred_element_type=F32)) + br_ref[...]


def cross_attention_ffn_in(qn, kn, v, x, w_co, layer, g_ffn, wr, br):
    B, S, D = qn.shape
    M = kn.shape[1]
    ts = CROSS_TS
    ns = S // ts
    assert D == TOKEN_ROWS * TOKEN_LANES
    tile = pl.BlockSpec((1, ts, D), lambda b, i: (b, i, 0))
    memb = pl.BlockSpec((1, M, D), lambda b, i: (b, 0, 0))
    const = lambda shape: pl.BlockSpec(shape, lambda b, i: (0, 0))
    return pl.pallas_call(
        functools.partial(_cross_kernel, layer=layer, dh=D // CROSS_HEADS),
        grid=(B, ns),
        in_specs=[tile, memb, memb, tile, pl.BlockSpec(memory_space=pl.ANY),
                  const((1, D)), const((D, LANES)), const((1, LANES))],
        out_specs=[tile,
                   pl.BlockSpec((ts * TOKEN_ROWS, TOKEN_LANES), lambda b, i: (b * ns + i, 0)),
                   pl.BlockSpec((ts, LANES), lambda b, i: (b * ns + i, 0))],
        out_shape=[jax.ShapeDtypeStruct((B, S, D), F32),
                   jax.ShapeDtypeStruct((B * S * TOKEN_ROWS, TOKEN_LANES), F32),
                   jax.ShapeDtypeStruct((B * S, LANES), F32)],
        scratch_shapes=[pltpu.VMEM((D, D), BF16), pltpu.VMEM((2, W_CHUNK, D), F32),
                        pltpu.SemaphoreType.DMA((2,))],
        compiler_params=_params("arbitrary", "arbitrary"),
        name="cross_attention_ffn_in",
    )(qn, kn, v, x, w_co, g_ffn.reshape(1, D), wr, br.reshape(1, LANES))


def _route_kernel(lg_ref, tri_ref, meta_ref, cnt_ref, carry_ref):
    @pl.when(pl.program_id(0) == 0)
    def _():
        carry_ref[...] = jnp.zeros_like(carry_ref)

    x = lg_ref[...]
    lane = lax.broadcasted_iota(jnp.int32, x.shape, 1)
    ninf = -jnp.inf
    is_g = lane < MOE_GROUPS
    gl = jnp.where(is_g, x, ninf)
    gmax = jnp.max(gl, axis=-1, keepdims=True)
    gidx = jnp.min(jnp.where(gl == gmax, lane, LANES), axis=-1, keepdims=True)
    gw = 1.0 / jnp.sum(jnp.where(is_g, jnp.exp(x - gmax), 0.0), axis=-1, keepdims=True)
    e_lane = lane - ROUTER_LANE0
    in_grp = (e_lane >= 0) & (e_lane < N_EXPERTS) & ((e_lane >> 3) == gidx)
    el = jnp.where(in_grp, x, ninf)
    m1 = jnp.max(el, axis=-1, keepdims=True)
    i1 = jnp.min(jnp.where(el == m1, lane, LANES), axis=-1, keepdims=True)
    el2 = jnp.where(lane == i1, ninf, el)
    m2 = jnp.max(el2, axis=-1, keepdims=True)
    i2 = jnp.min(jnp.where(el2 == m2, lane, LANES), axis=-1, keepdims=True)
    r = jnp.exp(m2 - m1)
    p1 = 1.0 / (1.0 + r)
    p2 = r * p1
    oh = ((lane == i1) | (lane == i2)).astype(F32)
    prefix = jnp.dot(tri_ref[...], oh.astype(BF16), preferred_element_type=F32) + carry_ref[...]
    rank1 = jnp.sum(jnp.where(lane == i1, prefix, 0.0), axis=-1, keepdims=True)
    rank2 = jnp.sum(jnp.where(lane == i2, prefix, 0.0), axis=-1, keepdims=True)
    carry_ref[...] += jnp.sum(oh, axis=0, keepdims=True)
    cnt_ref[...] = carry_ref[...]
    cols = [i1.astype(F32), i2.astype(F32), gw * p1, gw * p2, rank1, rank2]
    meta = jnp.zeros(x.shape, F32)
    for k, col in enumerate(cols):
        meta = jnp.where(lane == k, col, meta)
    meta_ref[...] = meta


def route(logits):
    T = logits.shape[0]
    tt = ROUTE_ROWS
    tri = jnp.asarray(np.tril(np.ones((tt, tt), np.float32), -1), BF16)
    return pl.pallas_call(
        _route_kernel,
        grid=(T // tt,),
        in_specs=[pl.BlockSpec((tt, LANES), lambda i: (i, 0)),
                  pl.BlockSpec((tt, tt), lambda i: (0, 0))],
        out_specs=[pl.BlockSpec((tt, LANES), lambda i: (i, 0)),
                   pl.BlockSpec((1, LANES), lambda i: (0, 0))],
        out_shape=[jax.ShapeDtypeStruct((T, LANES), F32),
                   jax.ShapeDtypeStruct((1, LANES), F32)],
        scratch_shapes=[pltpu.VMEM((1, LANES), F32)],
        compiler_params=_params("arbitrary"),
        name="moe_route",
    )(logits, tri)


def _plan_kernel(meta_ref, cnt_ref, d_ref):
    cnt = jnp.broadcast_to(cnt_ref[...], (SUBLANES, LANES)).astype(I32)
    pad = (((cnt + (MOE_ROWS - 1)) >> MOE_ROWS_LOG2) << MOE_ROWS_LOG2).astype(F32)
    lane8 = lax.broadcasted_iota(I32, pad.shape, 1)
    incl = pad
    sh = 1
    while sh < LANES:
        incl = incl + jnp.where(lane8 >= sh, pltpu.roll(incl, sh, axis=1), 0.0)
        sh *= 2
    off = (incl - pad)[0:1, :]
    m = meta_ref[...]
    lane = lax.broadcasted_iota(I32, m.shape, 1)
    i1 = m[:, 0:1].astype(I32)
    i2 = m[:, 1:2].astype(I32)
    d1 = jnp.sum(jnp.where(lane == i1, off, 0.0), axis=-1, keepdims=True) + m[:, 4:5]
    d2 = jnp.sum(jnp.where(lane == i2, off, 0.0), axis=-1, keepdims=True) + m[:, 5:6]
    dm = jnp.where(lane == 0, d1, jnp.where(lane == 1, d2, 0.0))
    d_ref[...] = jnp.transpose(dm)[0:2, :].astype(I32)


def plan(meta, cnt):
    T = meta.shape[0]
    tt = ROUTE_ROWS
    return pl.pallas_call(
        _plan_kernel,
        grid=(T // tt,),
        in_specs=[pl.BlockSpec((tt, LANES), lambda i: (i, 0)),
                  pl.BlockSpec((1, LANES), lambda i: (0, 0))],
        out_specs=pl.BlockSpec((2, tt), lambda i: (0, i)),
        out_shape=jax.ShapeDtypeStruct((2, T), I32),
        compiler_params=_params("parallel"),
        name="moe_plan",
    )(meta, cnt)


def _invert_kernel(d1_ref, d2_ref, src_ref, zeros, sem, *, T):
    zeros[...] = jnp.zeros_like(zeros)
    fill = pltpu.make_async_copy(zeros, src_ref, sem)
    fill.start()
    fill.wait()

    def body(t, _):
        src_ref[d1_ref[t]] = t
        src_ref[d2_ref[t]] = t
        return 0
    lax.fori_loop(0, T, body, 0, unroll=8)


def invert(d1, d2, R):
    T = d1.shape[0]
    grid_spec = pltpu.PrefetchScalarGridSpec(
        num_scalar_prefetch=2, grid=(1,), in_specs=[],
        out_specs=pl.BlockSpec(memory_space=pltpu.SMEM),
        scratch_shapes=[pltpu.VMEM((R,), I32), pltpu.SemaphoreType.DMA(())])
    return pl.pallas_call(
        functools.partial(_invert_kernel, T=T),
        grid_spec=grid_spec,
        out_shape=jax.ShapeDtypeStruct((R,), I32),
        compiler_params=_params("arbitrary"),
        name="moe_invert",
    )(d1, d2)


GATHER_CHUNK = 8


def _tile_copy(src_hbm, row, dst, slot, r, sem):
    return pltpu.make_async_copy(src_hbm.at[pl.ds(row * TOKEN_ROWS, TOKEN_ROWS)],
                                 dst.at[slot, pl.ds(r * TOKEN_ROWS, TOKEN_ROWS)], sem.at[slot])


def _expert_kernel(te_ref, first_ref, nxt_ref, wsl_ref, nch_ref, src_ref,
                   hp_hbm, w1_hbm, w3_hbm, w2_hbm, o_ref,
                   xbuf, r1, r3, r2, w1b, w3b, w2b, gsem, wsem, *, tr, base):
    i = pl.program_id(0)
    n = pl.num_programs(0)
    slot = i % 2

    def weight_copies(e, ws):
        return [pltpu.make_async_copy(w1_hbm.at[base + e], r1.at[ws], wsem.at[ws, 0]),
                pltpu.make_async_copy(w3_hbm.at[base + e], r3.at[ws], wsem.at[ws, 1]),
                pltpu.make_async_copy(w2_hbm.at[base + e], r2.at[ws], wsem.at[ws, 2])]

    def issue_rows(tile, slot, c0, c1):
        def body(c, _):
            for u in range(GATHER_CHUNK):
                r = c * GATHER_CHUNK + u
                _tile_copy(hp_hbm, src_ref[tile * tr + r], xbuf, slot, r, gsem).start()
            return 0
        lax.fori_loop(c0, c1, body, 0)

    def wait_rows(tile, slot):
        rows = GATHER_CHUNK * TOKEN_ROWS
        def body(c, _):
            pltpu.make_async_copy(hp_hbm.at[pl.ds(0, rows)], xbuf.at[slot, pl.ds(0, rows)],
                                  gsem.at[slot]).wait()
            return 0
        lax.fori_loop(0, nch_ref[tile], body, 0)

    @pl.when(i == 0)
    def _():
        xbuf[...] = jnp.zeros_like(xbuf)
        for cp in weight_copies(te_ref[0], 0):
            cp.start(priority=1)
        issue_rows(0, 0, 0, nch_ref[0])

    wait_rows(i, slot)

    nxt_tile = jnp.minimum(i + 1, n - 1)
    issue_rows(nxt_tile, 1 - slot, 0, jnp.where(i + 1 < n, nch_ref[nxt_tile], 0))

    @pl.when(first_ref[i] == 1)
    def _():
        ws = wsl_ref[i]
        for cp in weight_copies(te_ref[i], ws):
            cp.wait()

        @pl.when(nxt_ref[i] >= 0)
        def _():
            for cp in weight_copies(nxt_ref[i], 1 - ws):
                cp.start(priority=1)

        w1b[...] = r1[ws].astype(BF16)
        w3b[...] = r3[ws].astype(BF16)
        w2b[...] = r2[ws].astype(BF16)

    def expert_mlp(rows):
        x = jnp.concatenate([p.astype(BF16) for p in _load_token_tiles(xbuf.at[slot], 0, rows)],
                            axis=1)
        h1 = jnp.dot(x, w1b[...], preferred_element_type=F32)
        h3 = jnp.dot(x, w3b[...], preferred_element_type=F32)
        hm = (h1 * _sigmoid(h1) * h3).astype(BF16)
        _store_token_tiles(o_ref, jnp.dot(hm, w2b[...], preferred_element_type=F32), rows)

    half_chunks = tr // (2 * GATHER_CHUNK)

    @pl.when(nch_ref[i] > half_chunks)
    def _():
        expert_mlp(tr)

    @pl.when((nch_ref[i] > 0) & (nch_ref[i] <= half_chunks))
    def _():
        expert_mlp(tr // 2)
        o_ref[tr // 2 * TOKEN_ROWS:, :] = jnp.zeros((tr // 2 * TOKEN_ROWS, TOKEN_LANES), F32)

    @pl.when(nch_ref[i] == 0)
    def _():
        o_ref[...] = jnp.zeros_like(o_ref)


def moe_experts(hp, w1, w3, w2, layer, tables, src):
    D, F = w1.shape[-2:]
    tr = MOE_ROWS
    te, first, nxt, wsl, nch = tables
    NT = te.shape[0]
    w1f = w1.reshape(-1, D, F)
    w3f = w3.reshape(-1, D, F)
    w2f = w2.reshape(-1, F, D)
    hbm = pl.BlockSpec(memory_space=pl.ANY)
    grid_spec = pltpu.PrefetchScalarGridSpec(
        num_scalar_prefetch=6,
        grid=(NT,),
        in_specs=[hbm, hbm, hbm, hbm],
        out_specs=pl.BlockSpec((tr * TOKEN_ROWS, TOKEN_LANES), lambda i, *_: (i, 0)),
        scratch_shapes=[pltpu.VMEM((2, tr * TOKEN_ROWS, TOKEN_LANES), F32),
                        pltpu.VMEM((2, D, F), F32),
                        pltpu.VMEM((2, D, F), F32),
                        pltpu.VMEM((2, F, D), F32),
                        pltpu.VMEM((D, F), BF16),
                        pltpu.VMEM((D, F), BF16),
                        pltpu.VMEM((F, D), BF16),
                        pltpu.SemaphoreType.DMA((2,)),
                        pltpu.SemaphoreType.DMA((2, 3))],
    )
    return pl.pallas_call(
        functools.partial(_expert_kernel, tr=tr, base=layer * N_EXPERTS),
        grid_spec=grid_spec,
        out_shape=jax.ShapeDtypeStruct((NT * tr * TOKEN_ROWS, TOKEN_LANES), F32),
        compiler_params=_params("arbitrary"),
        name="moe_experts",
    )(te, first, nxt, wsl, nch, src, hp, w1f, w3f, w2f)


def _combine_kernel(d1_ref, d2_ref, x_ref, meta_ref, ys_hbm, *rest, tt, with_norm):
    if with_norm:
        g_ref, o_ref, hn_ref, buf, sem = rest
    else:
        o_ref, buf, sem = rest
    i = pl.program_id(0)
    n = pl.num_programs(0)
    slot = i % 2

    def issue(tile, slot):
        def body(c, _):
            for u in range(GATHER_CHUNK):
                r = c * GATHER_CHUNK + u
                t = tile * tt + r
                _tile_copy(ys_hbm, d1_ref[t], buf, slot, r, sem).start(priority=0)
                _tile_copy(ys_hbm, d2_ref[t], buf, slot, tt + r, sem).start(priority=1)
            return 0
        lax.fori_loop(0, tt // GATHER_CHUNK, body, 0)

    @pl.when(i == 0)
    def _():
        issue(0, 0)

    pltpu.make_async_copy(ys_hbm.at[pl.ds(0, 2 * tt * TOKEN_ROWS)], buf.at[slot], sem.at[slot]).wait()

    @pl.when(i + 1 < n)
    def _():
        issue(i + 1, 1 - slot)

    w1 = meta_ref[:, 2:3]
    w2 = meta_ref[:, 3:4]
    pa = _load_token_tiles(buf.at[slot], 0, tt)
    pb = _load_token_tiles(buf.at[slot], tt * TOKEN_ROWS, tt)
    cols = [x_ref[:, j * TOKEN_LANES:(j + 1) * TOKEN_LANES] + w1 * pa[j] + w2 * pb[j]
            for j in range(TOKEN_ROWS)]
    xn = jnp.concatenate(cols, axis=1)
    o_ref[...] = xn
    if with_norm:
        hn_ref[...] = (_rms(xn) * g_ref[...]).astype(hn_ref.dtype)


def moe_combine(x, meta, ys, d1, d2, g_next=None):
    T, D = x.shape
    tt = COMB_ROWS
    with_norm = g_next is not None
    row = pl.BlockSpec((tt, D), lambda i, d1, d2: (i, 0))
    in_specs = [row, pl.BlockSpec((tt, LANES), lambda i, d1, d2: (i, 0)),
                pl.BlockSpec(memory_space=pl.ANY)]
    args = [d1, d2, x, meta, ys]
    out_specs, out_shape = row, jax.ShapeDtypeStruct((T, D), F32)
    if with_norm:
        in_specs.append(pl.BlockSpec((1, D), lambda i, d1, d2: (0, 0)))
        args.append(g_next.reshape(1, D))
        out_specs, out_shape = [row, row], [out_shape, jax.ShapeDtypeStruct((T, D), BF16)]
    grid_spec = pltpu.PrefetchScalarGridSpec(
        num_scalar_prefetch=2,
        grid=(T // tt,),
        in_specs=in_specs,
        out_specs=out_specs,
        scratch_shapes=[pltpu.VMEM((2, 2 * tt * TOKEN_ROWS, TOKEN_LANES), F32),
                        pltpu.SemaphoreType.DMA((2,))],
    )
    return pl.pallas_call(
        functools.partial(_combine_kernel, tt=tt, with_norm=with_norm),
        grid_spec=grid_spec,
        out_shape=out_shape,
        compiler_params=_params("arbitrary"),
        name="moe_combine",
    )(*args)


def _tile_tables(cnt, T):
    E = N_EXPERTS
    counts = cnt[0, ROUTER_LANE0:ROUTER_LANE0 + E].astype(I32)
    tiles_e = (counts + MOE_ROWS - 1) // MOE_ROWS
    tile_end = jnp.cumsum(tiles_e)
    tile_start = tile_end - tiles_e
    nact = tile_end[-1]
    NT = (2 * T) // MOE_ROWS + E
    tid = jnp.arange(NT, dtype=I32)
    te = jnp.sum((jnp.minimum(tid, nact - 1)[:, None] >= tile_end[None, :]).astype(I32), axis=1)
    te = jnp.minimum(te, E - 1)
    active = tid < nact
    first = (active & (tid == tile_start[te])).astype(I32)
    eid = jnp.arange(E, dtype=I32)
    later = (eid[None, :] > eid[:, None]) & (tiles_e[None, :] > 0)
    nxt_e = jnp.min(jnp.where(later, eid[None, :], E), axis=1)
    nxt_e = jnp.where(nxt_e == E, -1, nxt_e)
    ordinal = jnp.cumsum((tiles_e > 0).astype(I32)) - 1
    valid = jnp.clip(counts[te] - (tid - tile_start[te]) * MOE_ROWS, 0, MOE_ROWS)
    nch = jnp.where(active, (valid + GATHER_CHUNK - 1) // GATHER_CHUNK, 0)
    return (te, first, nxt_e[te], ordinal[te] % 2, nch), NT * MOE_ROWS


def _lower_bound_logs(lb_param, layer):
    p = jax.nn.softmax(lb_param.astype(F32), axis=0)
    c = jnp.cumsum(p, axis=0)
    lb = c[layer] - c[0]
    return jnp.log(lb), jnp.log1p(-lb)


def kernel(x, mem, norm_mix, w_in, hg_lower_bounds, hg_norm, conv_w, fox_f_bias, fox_q_norm,
           fox_k_norm, w_out, norm_cross, norm_mem, w_cq, w_ck, w_cv, w_co, cross_q_norm,
           cross_k_norm, norm_ffn, router_group_w, router_group_b, router_expert_w,
           router_expert_b, moe_w1, moe_w3, moe_w2):
    B, S, D = x.shape
    M = mem.shape[1]
    L = w_in.shape[0]
    T = B * S
    HGW = 4 * HG_WIDTH
    CONV3 = 3 * CONV_WIDTH
    PB = CONV3 + 3 * FOX_WIDTH
    FQ0, FK0, FV0 = 0, FOX_WIDTH, 2 * FOX_WIDTH
    CV0 = 3 * FOX_WIDTH
    dhc = D // CROSS_HEADS
    x2 = x.reshape(T, D)
    mem2 = mem.reshape(B * M, D)
    w_in_t = jnp.swapaxes(w_in, 1, 2)
    hn = rmsnorm(x2, norm_mix[0])
    for l in range(L):
        proj_a = matmul([hn], w_in_t, l, 0, HGW, F32, w_t=True)
        proj_b = matmul([hn], w_in_t, l, HGW, PB, BF16, w_t=True, rotate=CONV3, head_norms=(
            (FQ0, FK0, fox_q_norm[l], FOX_DIM ** -0.5), (FK0, FV0, fox_k_norm[l], 1.0)))
        loglb, l1mlb = _lower_bound_logs(hg_lower_bounds, l)
        y_hg = hgrn2(proj_a.reshape(B, S, HGW), loglb, l1mlb, hg_norm[l])
        pb3 = proj_b.reshape(B, S, PB)
        y_conv = short_conv(pb3, CV0, conv_w[l])
        c = fox_gate(hn.reshape(B, S, D), w_in_t, l, HGW + PB, fox_f_bias[l])
        y_fox = fox_attention(pb3, FQ0, FK0, FV0, c)
        mix = [y_hg.reshape(T, HG_WIDTH), y_conv.reshape(T, CONV_WIDTH), y_fox.reshape(T, FOX_WIDTH)]
        x2, qcn = out_proj_cross_q(mix, x2, w_out, w_cq, l, norm_cross[l], cross_q_norm[l])
        memn = rmsnorm(mem2, norm_mem[l])
        kcn = matmul([memn], w_ck, l, 0, D, BF16, head_norms=((0, D, cross_k_norm[l], 1.0),))
        vc = matmul([memn], w_cv, l, 0, D, BF16)
        wr = jnp.concatenate([router_group_w[l], router_expert_w[l]], axis=1)
        wr = jnp.pad(wr, ((0, 0), (0, LANES - wr.shape[1])))
        br = jnp.concatenate([router_group_b[l], router_expert_b[l]])
        br = jnp.pad(br, (0, LANES - br.shape[0]))
        x3, hp, logits = cross_attention_ffn_in(
            qcn.reshape(B, S, D), kcn.reshape(B, M, D), vc.reshape(B, M, D), x2.reshape(B, S, D),
            w_co, l, norm_ffn[l], wr, br)
        x2 = x3.reshape(T, D)
        meta, cnt = route(logits)
        d = plan(meta, cnt)
        tables, R = _tile_tables(cnt, T)
        src = invert(d[0], d[1], R)
        ys = moe_experts(hp, moe_w1, moe_w3, moe_w2, l, tables, src)
        if l + 1 < L:
            x2, hn = moe_combine(x2, meta, ys, d[0], d[1], norm_mix[l + 1])
        else:
            x2 = moe_combine(x2, meta, ys, d[0], d[1])
    return x2.reshape(B, S, D)
```

```python
import functools

import numpy as np
import jax
import jax.numpy as jnp
from jax import lax
from jax.experimental import pallas as pl
from jax.experimental.pallas import tpu as pltpu

F32 = jnp.float32
BF16 = jnp.bfloat16
I32 = jnp.int32
EPS = 1e-6

HG_HEADS = 4
HG_DIM = 128
HG_WIDTH = HG_HEADS * HG_DIM
CONV_WIDTH = 512
FOX_HEADS = 8
FOX_DIM = 128
FOX_WIDTH = FOX_HEADS * FOX_DIM
CROSS_HEADS = 4
MOE_GROUPS = 4
MOE_EXPERTS = 8
N_EXPERTS = MOE_GROUPS * MOE_EXPERTS
ROUTER_LANE0 = MOE_GROUPS

LANES = 128
SUBLANES = 8
VMEM_LIMIT = 56 * 1024 * 1024

NORM_ROWS = 512
MM_TM = 2048
MM_TN = 512
HG_CHUNK = 128
FOX_TQ = 512
FOX_TK = 512
FOX_HEADS_PER_STEP = 8
CROSS_TS = 512
ROUTE_ROWS = 1024
MOE_ROWS_LOG2 = 8
MOE_ROWS = 1 << MOE_ROWS_LOG2
COMB_ROWS = 128


def _params(*sem):
    return pltpu.CompilerParams(dimension_semantics=sem, vmem_limit_bytes=VMEM_LIMIT)


def _sigmoid(x):
    return 1.0 / (1.0 + jnp.exp(-x))


def _log_sigmoid(x):
    return jnp.minimum(x, 0.0) - jnp.log1p(jnp.exp(-jnp.abs(x)))


TOKEN_ROWS = 16
TOKEN_LANES = 128


def _store_token_tiles(ref, y, rows):
    for j in range(TOKEN_ROWS):
        ref[pl.ds(j, rows, stride=TOKEN_ROWS), :] = y[:, j * TOKEN_LANES:(j + 1) * TOKEN_LANES]


def _load_token_tiles(ref, base, rows):
    return [ref[pl.ds(base + j, rows, stride=TOKEN_ROWS), :] for j in range(TOKEN_ROWS)]


def _rms(x):
    return x * lax.rsqrt(jnp.mean(x * x, axis=-1, keepdims=True) + EPS)


def _rmsnorm_kernel(x_ref, g_ref, o_ref):
    o_ref[...] = (_rms(x_ref[...]) * g_ref[...]).astype(o_ref.dtype)


def rmsnorm(x, g, out_dtype=BF16):
    R, D = x.shape
    return pl.pallas_call(
        _rmsnorm_kernel,
        grid=(R // NORM_ROWS,),
        in_specs=[pl.BlockSpec((NORM_ROWS, D), lambda i: (i, 0)),
                  pl.BlockSpec((1, D), lambda i: (0, 0))],
        out_specs=pl.BlockSpec((NORM_ROWS, D), lambda i: (i, 0)),
        out_shape=jax.ShapeDtypeStruct((R, D), out_dtype),
        compiler_params=_params("parallel"),
        name="rmsnorm",
    )(x, g.reshape(1, D))


def _head_rmsnorm(x, g, dh, scale):
    heads = [_rms(x[:, h * dh:(h + 1) * dh]) * g * scale for h in range(x.shape[1] // dh)]
    return heads[0] if len(heads) == 1 else jnp.concatenate(heads, axis=1)


def _matmul_kernel(*refs, n_parts, w_t, norm_tiles, dh):
    a_refs = refs[:n_parts]
    w_ref = refs[n_parts]
    if norm_tiles:
        g_ref, o_ref, wb_ref = refs[n_parts + 1:]
    else:
        o_ref, wb_ref = refs[n_parts + 1:]
    j = pl.program_id(0)

    @pl.when(pl.program_id(1) == 0)
    def _():
        wb_ref[...] = w_ref[...].astype(BF16)

    if n_parts == 1:
        a = a_refs[0][...]
    else:
        a = jnp.concatenate([r[...] for r in a_refs], axis=1)
    if w_t:
        acc = lax.dot_general(a, wb_ref[...], (((1,), (1,)), ((), ())), preferred_element_type=F32)
    else:
        acc = jnp.dot(a, wb_ref[...], preferred_element_type=F32)
    if not norm_tiles:
        o_ref[...] = acc.astype(o_ref.dtype)
        return
    plain = True
    for k, (j0, j1, scale) in enumerate(norm_tiles):
        hit = (j >= j0) & (j < j1)
        plain = plain & jnp.logical_not(hit)

        @pl.when(hit)
        def _(k=k, scale=scale):
            o_ref[...] = _head_rmsnorm(acc, g_ref[k:k + 1, :], dh, scale).astype(o_ref.dtype)

    @pl.when(plain)
    def _():
        o_ref[...] = acc.astype(o_ref.dtype)


def matmul(a_parts, w, layer, col0, n, out_dtype, w_t=False, head_norms=(), rotate=0,
           tm=MM_TM, tn=MM_TN):
    M = a_parts[0].shape[0]
    K = w.shape[2] if w_t else w.shape[1]
    assert sum(p.shape[1] for p in a_parts) == K
    tm = min(tm, M)
    assert col0 % tn == 0 and n % tn == 0 and M % tm == 0 and rotate % tn == 0
    cb, nt, rot = col0 // tn, n // tn, rotate // tn
    in_specs = [pl.BlockSpec((tm, p.shape[1]), lambda j, i: (i, 0)) for p in a_parts]
    if w_t:
        in_specs.append(pl.BlockSpec((None, tn, K), lambda j, i: (layer, cb + (j + rot) % nt, 0)))
    else:
        in_specs.append(pl.BlockSpec((None, K, tn), lambda j, i: (layer, 0, cb + (j + rot) % nt)))
    args = list(a_parts) + [w]
    norm_tiles, dh = (), 0
    if head_norms:
        dh = head_norms[0][2].shape[0]
        assert all(c0 % tn == 0 and c1 % tn == 0 and g.shape[0] == dh and tn % dh == 0
                   for c0, c1, g, _ in head_norms)
        norm_tiles = tuple((c0 // tn, c1 // tn, s) for c0, c1, _, s in head_norms)
        gains = jnp.stack([g for _, _, g, _ in head_norms])
        in_specs.append(pl.BlockSpec(gains.shape, lambda j, i: (0, 0)))
        args.append(gains)
    return pl.pallas_call(
        functools.partial(_matmul_kernel, n_parts=len(a_parts), w_t=w_t, norm_tiles=norm_tiles,
                          dh=dh),
        grid=(n // tn, M // tm),
        in_specs=in_specs,
        out_specs=pl.BlockSpec((tm, tn), lambda j, i: (i, j)),
        out_shape=jax.ShapeDtypeStruct((M, n), out_dtype),
        scratch_shapes=[pltpu.VMEM((tn, K) if w_t else (K, tn), BF16)],
        compiler_params=_params("arbitrary", "arbitrary"),
        name="matmul",
    )(*args)


W_CHUNK = 256


def _load_weight(w_hbm, layer, wb, stg, sem):
    n_chunks = wb.shape[0] // W_CHUNK

    def copy(c):
        return pltpu.make_async_copy(w_hbm.at[layer, pl.ds(c * W_CHUNK, W_CHUNK)], stg.at[c % 2],
                                     sem.at[c % 2])
    copy(0).start()
    for c in range(n_chunks):
        if c + 1 < n_chunks:
            copy(c + 1).start()
        copy(c).wait()
        wb[c * W_CHUNK:(c + 1) * W_CHUNK, :] = stg[c % 2].astype(BF16)


def _outq_kernel(a1_ref, a2_ref, a3_ref, x_ref, wo_hbm, wq_hbm, gn_ref, gq_ref, xo_ref, q_ref,
                 wo_b, wq_b, stg, sem, *, layer, dh, scale):
    @pl.when(pl.program_id(0) == 0)
    def _():
        _load_weight(wo_hbm, layer, wo_b, stg, sem)
        _load_weight(wq_hbm, layer, wq_b, stg, sem)

    a = jnp.concatenate([a1_ref[...], a2_ref[...], a3_ref[...]], axis=1)
    xn = x_ref[...] + jnp.dot(a, wo_b[...], preferred_element_type=F32)
    xo_ref[...] = xn
    hc = (_rms(xn) * gn_ref[...]).astype(BF16)
    q = jnp.dot(hc, wq_b[...], preferred_element_type=F32)
    q_ref[...] = _head_rmsnorm(q, gq_ref[...], dh, scale).astype(q_ref.dtype)


def out_proj_cross_q(mix, x, w_out, w_cq, layer, g_cross, g_q, tm=512):
    T, D = x.shape
    dh = g_q.shape[0]
    row = lambda w: pl.BlockSpec((tm, w), lambda i: (i, 0))
    hbm = pl.BlockSpec(memory_space=pl.ANY)
    return pl.pallas_call(
        functools.partial(_outq_kernel, layer=layer, dh=dh, scale=dh ** -0.5),
        grid=(T // tm,),
        in_specs=[row(mix[0].shape[1]), row(mix[1].shape[1]), row(mix[2].shape[1]), row(D), hbm, hbm,
                  pl.BlockSpec((1, D), lambda i: (0, 0)), pl.BlockSpec((1, dh), lambda i: (0, 0))],
        out_specs=[row(D), row(D)],
        out_shape=[jax.ShapeDtypeStruct((T, D), F32), jax.ShapeDtypeStruct((T, D), BF16)],
        scratch_shapes=[pltpu.VMEM((D, D), BF16), pltpu.VMEM((D, D), BF16),
                        pltpu.VMEM((2, W_CHUNK, D), F32), pltpu.SemaphoreType.DMA((2,))],
        compiler_params=_params("arbitrary"),
        name="out_proj_cross_q",
    )(*mix, x, w_out, w_cq, g_cross.reshape(1, D), g_q.reshape(1, dh))


def _hgrn_consts(C):
    nlev = int(np.log2(C))
    t = np.arange(C)[:, None]
    u = np.arange(C)[None, :]
    tri = (u <= t).astype(np.float32)
    mall2 = np.concatenate([tri, tri], axis=1)
    tt = np.arange(C)[:, None]
    ss = np.arange(C)[None, :]
    lev = np.full((C, C), nlev + 1, np.int32)
    x = tt ^ ss
    hb = np.zeros_like(x)
    for j in range(nlev):
        hb = np.where((x >> j) & 1, j, hb)
    lev = np.where(tt > ss, hb, lev)
    lev = np.where(tt == ss, nlev, lev)
    return jnp.asarray(mall2, BF16), jnp.asarray(lev, jnp.int32)


def _level_sums(lf, b, row, j, C):
    m = 1 << j
    second = (row & m) != 0
    if j == 0:
        return jnp.where(second, lf, 0.0)
    if j == 1:
        prev = pltpu.roll(lf, 1, axis=0)
        nxt = pltpu.roll(lf, C - 1, axis=0)
        first = jnp.where((row & 1) == 0, nxt, 0.0)
        return jnp.where(second, jnp.where((row & 1) != 0, lf + prev, lf), first)
    g = 2 * m
    d = b.shape[1]
    bsel = jnp.broadcast_to(b.reshape(C // g, g, d)[:, m - 1:m, :], (C // g, g, d)).reshape(C, d)
    return jnp.where(second, b - bsel, bsel - b)


def _hgrn_kernel(x_ref, mall_ref, lev_ref, loglb_ref, l1mlb_ref, ng_ref, o_ref, st_ref, *, C, nlev):
    @pl.when(pl.program_id(1) == 0)
    def _():
        st_ref[...] = jnp.zeros_like(st_ref)

    W = HG_WIDTH
    d = HG_DIM
    lev = lev_ref[...]
    mall = mall_ref[...]
    row = lax.broadcasted_iota(jnp.int32, (C, d), 0)
    nt = (((1,), (1,)), ((), ()))

    def side(xs):
        return jnp.concatenate(xs, axis=1)

    def block_diag(xa, xb):
        return jnp.concatenate([side([xa, jnp.zeros_like(xb)]), side([jnp.zeros_like(xa), xb])],
                               axis=0)

    for h0 in range(0, HG_HEADS, 2):
        q, lf, kk, vb, g = [], [], [], [], []
        for h in (h0, h0 + 1):
            sl = slice(h * d, (h + 1) * d)
            q.append(x_ref[0, :, h * d:(h + 1) * d])
            z = x_ref[0, :, W + h * d:W + (h + 1) * d]
            vb.append(x_ref[0, :, 2 * W + h * d:2 * W + (h + 1) * d].astype(BF16))
            g.append(x_ref[0, :, 3 * W + h * d:3 * W + (h + 1) * d])
            a = loglb_ref[:, sl]
            bt = l1mlb_ref[:, sl] + _log_sigmoid(z)
            lf.append(jnp.maximum(a, bt) + jnp.log1p(jnp.exp(-jnp.abs(a - bt))))
            kk.append(1.0 - jnp.exp(lf[-1]))
        hi = [x.astype(BF16) for x in lf]
        lo = [(x - y.astype(F32)).astype(BF16) for x, y in zip(lf, hi)]
        b2 = jnp.dot(mall, jnp.concatenate([side(hi), side(lo)], axis=0),
                     preferred_element_type=F32)
        b = [b2[:, :d], b2[:, d:]]
        st = [st_ref[h0], st_ref[h0 + 1]]
        o2 = lax.dot_general(side([(q[i] * jnp.exp(b[i])).astype(BF16) for i in (0, 1)]),
                             block_diag(st[0].astype(BF16), st[1].astype(BF16)), nt,
                             preferred_element_type=F32)
        att = [jnp.where(lev == nlev, jnp.sum(q[i] * kk[i], axis=-1, keepdims=True), 0.0)
               for i in (0, 1)]
        for j in range(nlev):
            second = (row & (1 << j)) != 0
            xq, xk = [], []
            for i in (0, 1):
                e = jnp.exp(_level_sums(lf[i], b[i], row, j, C))
                xq.append(jnp.where(second, q[i] * e, 0.0).astype(BF16))
                xk.append(jnp.where(second, 0.0, kk[i] * e).astype(BF16))
            am = lax.dot_general(side(xq), block_diag(xk[0], xk[1]), nt,
                                 preferred_element_type=F32)
            att = [jnp.where(lev == j, am[:, i * C:(i + 1) * C], att[i]) for i in (0, 1)]
        o2 = o2 + jnp.dot(side([x.astype(BF16) for x in att]), block_diag(vb[0], vb[1]),
                          preferred_element_type=F32)
        for i, h in enumerate((h0, h0 + 1)):
            bl = b[i][C - 1:C, :]
            kh = (kk[i] * jnp.exp(bl - b[i])).astype(BF16)
            st_ref[h] = st[i] * jnp.exp(bl) + lax.dot_general(
                vb[i], kh, (((0,), (0,)), ((), ())), preferred_element_type=F32)
            o = o2[:, i * d:(i + 1) * d]
            y = o * lax.rsqrt(jnp.mean(o * o, axis=-1, keepdims=True) + EPS) * ng_ref[...]
            o_ref[0, :, h * d:(h + 1) * d] = (y * (g[i] * _sigmoid(g[i]))).astype(o_ref.dtype)


def hgrn2(proj_a, loglb, l1mlb, norm_g):
    B, S, _ = proj_a.shape
    C = HG_CHUNK
    nlev = int(np.log2(C))
    mall, lev = _hgrn_consts(C)
    return pl.pallas_call(
        functools.partial(_hgrn_kernel, C=C, nlev=nlev),
        grid=(B, S // C),
        in_specs=[pl.BlockSpec((1, C, 4 * HG_WIDTH), lambda b, c: (b, c, 0)),
                  pl.BlockSpec(mall.shape, lambda b, c: (0, 0)),
                  pl.BlockSpec((C, C), lambda b, c: (0, 0)),
                  pl.BlockSpec((1, HG_WIDTH), lambda b, c: (0, 0)),
                  pl.BlockSpec((1, HG_WIDTH), lambda b, c: (0, 0)),
                  pl.BlockSpec((1, HG_DIM), lambda b, c: (0, 0))],
        out_specs=pl.BlockSpec((1, C, HG_WIDTH), lambda b, c: (b, c, 0)),
        out_shape=jax.ShapeDtypeStruct((B, S, HG_WIDTH), BF16),
        scratch_shapes=[pltpu.VMEM((HG_HEADS, HG_DIM, HG_DIM), F32)],
        compiler_params=_params("parallel", "arbitrary"),
        name="hgrn2",
    )(proj_a, mall, lev, loglb.reshape(1, HG_WIDTH), l1mlb.reshape(1, HG_WIDTH),
      norm_g.reshape(1, HG_DIM))


def _conv_kernel(b_ref, c_ref, h_ref, w_ref, o_ref):
    u = c_ref[0].astype(F32) * h_ref[0].astype(F32)
    row = lax.broadcasted_iota(jnp.int32, u.shape, 0)
    u1 = jnp.where(row >= 1, pltpu.roll(u, 1, axis=0), 0.0)
    u2 = jnp.where(row >= 2, pltpu.roll(u, 2, axis=0), 0.0)
    y = w_ref[0:1, :] * u2 + w_ref[1:2, :] * u1 + w_ref[2:3, :] * u
    o_ref[0] = (b_ref[0].astype(F32) * y).astype(o_ref.dtype)


def short_conv(proj_b, col0, w):
    B, S, _ = proj_b.shape
    cw = CONV_WIDTH
    assert col0 % cw == 0
    spec = lambda k: pl.BlockSpec((1, S, cw), lambda b: (b, 0, col0 // cw + k))
    return pl.pallas_call(
        _conv_kernel,
        grid=(B,),
        in_specs=[spec(0), spec(1), spec(2), pl.BlockSpec((3, cw), lambda b: (0, 0))],
        out_specs=pl.BlockSpec((1, S, cw), lambda b: (b, 0, 0)),
        out_shape=jax.ShapeDtypeStruct((B, S, cw), BF16),
        compiler_params=_params("parallel"),
        name="short_conv",
    )(proj_b, proj_b, proj_b, w)


def _fox_gate_kernel(hn_ref, wf_ref, bias_ref, c_ref):
    gl = lax.dot_general(wf_ref[...].astype(BF16), hn_ref[0], (((1,), (1,)), ((), ())),
                         preferred_element_type=F32)
    c = _log_sigmoid(gl + bias_ref[...])
    S = c.shape[1]
    lane = lax.broadcasted_iota(jnp.int32, c.shape, 1)
    sh = 1
    while sh < S:
        c = c + jnp.where(lane >= sh, pltpu.roll(c, sh, axis=1), 0.0)
        sh *= 2
    c_ref[0] = c


def fox_gate(hn, w_t, layer, row0, bias):
    B, S, D = hn.shape
    H = bias.shape[0]
    assert row0 % H == 0
    return pl.pallas_call(
        _fox_gate_kernel,
        grid=(B,),
        in_specs=[pl.BlockSpec((1, S, D), lambda b: (b, 0, 0)),
                  pl.BlockSpec((None, H, D), lambda b: (layer, row0 // H, 0)),
                  pl.BlockSpec((H, 1), lambda b: (0, 0))],
        out_specs=pl.BlockSpec((1, H, S), lambda b: (b, 0, 0)),
        out_shape=jax.ShapeDtypeStruct((B, H, S), F32),
        compiler_params=_params("parallel"),
        name="fox_gate",
    )(hn, w_t, bias.reshape(H, 1))


def _fox_kernel(q_ref, k_ref, v_ref, c_ref, o_ref, *, tq, tk, nh):
    qi = pl.program_id(2)
    nt = (((1,), (1,)), ((), ()))
    ones = jnp.ones((tk, FOX_DIM), BF16)
    heads = [slice(h * FOX_DIM, (h + 1) * FOX_DIM) for h in range(nh)]
    qs = [q_ref[0, :, sl] for sl in heads]

    def step(ki, carry, diag):
        out = []
        for h, sl in enumerate(heads):
            m, acc = carry[h]
            k = k_ref[0, pl.ds(ki * tk, tk), sl]
            v1 = jnp.concatenate([v_ref[0, pl.ds(ki * tk, tk), sl], ones], axis=1)
            ck = c_ref[0, h, pl.ds(ki, 1), :]
            s = lax.dot_general(qs[h], k, nt, preferred_element_type=F32) - ck
            if diag is not None:
                r = lax.broadcasted_iota(jnp.int32, s.shape, 0)
                c = lax.broadcasted_iota(jnp.int32, s.shape, 1)
                s = jnp.where(c + diag * tk <= r, s, -jnp.inf)
            m_new = jnp.maximum(m, jnp.max(s, axis=-1, keepdims=True))
            alpha = jnp.exp(m - m_new)
            p = jnp.exp((s - m_new).astype(BF16))
            out.append((m_new, alpha * acc + jnp.dot(p, v1, preferred_element_type=F32)))
        return tuple(out)

    init = tuple((jnp.full((tq, 1), -jnp.inf, F32), jnp.zeros((tq, 2 * FOX_DIM), F32))
                 for _ in heads)
    n_full = qi * (tq // tk)
    carry = lax.fori_loop(0, n_full, lambda ki, cr: step(ki, cr, None), init)
    for d in range(tq // tk):
        carry = step(n_full + d, carry, d)
    for h, sl in enumerate(heads):
        acc = carry[h][1]
        o_ref[0, :, sl] = (acc[:, :FOX_DIM] / acc[:, FOX_DIM:]).astype(o_ref.dtype)


def fox_attention(proj_b, q_col0, k_col0, v_col0, c):
    B, S, _ = proj_b.shape
    tq, tk, nh = FOX_TQ, FOX_TK, FOX_HEADS_PER_STEP
    H = FOX_HEADS
    w = nh * FOX_DIM
    assert tq % tk == 0 and H % nh == 0 and all(c0 % w == 0 for c0 in (q_col0, k_col0, v_col0))
    qb, kb, vb = q_col0 // w, k_col0 // w, v_col0 // w
    c4 = c.reshape(B, H, S // tk, tk)
    return pl.pallas_call(
        functools.partial(_fox_kernel, tq=tq, tk=tk, nh=nh),
        grid=(B, H // nh, S // tq),
        in_specs=[pl.BlockSpec((1, tq, w), lambda b, h, i: (b, i, qb + h)),
                  pl.BlockSpec((1, S, w), lambda b, h, i: (b, 0, kb + h)),
                  pl.BlockSpec((1, S, w), lambda b, h, i: (b, 0, vb + h)),
                  pl.BlockSpec((1, nh, S // tk, tk), lambda b, h, i: (b, h, 0, 0))],
        out_specs=pl.BlockSpec((1, tq, w), lambda b, h, i: (b, i, h)),
        out_shape=jax.ShapeDtypeStruct((B, S, FOX_WIDTH), BF16),
        compiler_params=_params("parallel", "parallel", "arbitrary"),
        name="fox_attention",
    )(proj_b, proj_b, proj_b, c4)


def _cross_kernel(q_ref, k_ref, v_ref, x_ref, wo_hbm, gn_ref, wr_ref, br_ref,
                  xo_ref, hp_ref, lg_ref, wo_b, stg, sem, *, layer, dh):
    @pl.when((pl.program_id(0) == 0) & (pl.program_id(1) == 0))
    def _():
        _load_weight(wo_hbm, layer, wo_b, stg, sem)

    nt = (((1,), (1,)), ((), ()))
    heads = []
    for h in range(CROSS_HEADS):
        sl = slice(h * dh, (h + 1) * dh)
        s = lax.dot_general(q_ref[0, :, sl], k_ref[0, :, sl], nt, preferred_element_type=F32)
        m = jnp.max(s, axis=-1, keepdims=True)
        p = jnp.exp(s - m)
        l = jnp.sum(p, axis=-1, keepdims=True)
        o = jnp.dot(p.astype(BF16), v_ref[0, :, sl], preferred_element_type=F32)
        heads.append((o / l).astype(BF16))
    xn = x_ref[0] + jnp.dot(jnp.concatenate(heads, axis=1), wo_b[...], preferred_element_type=F32)
    xo_ref[0] = xn
    y = _rms(xn) * gn_ref[...]
    _store_token_tiles(hp_ref, y, y.shape[0])
    yh = y.astype(BF16)
    yl = (y - yh.astype(F32)).astype(BF16)
    wr = wr_ref[...]
    wh = wr.astype(BF16)
    wl = (wr - wh.astype(F32)).astype(BF16)
    lg_ref[...] = (jnp.dot(yh, wh, preferred_element_type=F32)
                   + jnp.dot(yl, wh, preferred_element_type=F32)
                   + jnp.dot(yh, wl, preferred_element_type=F32)) + br_ref[...]


def cross_attention_ffn_in(qn, kn, v, x, w_co, layer, g_ffn, wr, br):
    B, S, D = qn.shape
    M = kn.shape[1]
    ts = CROSS_TS
    ns = S // ts
    assert D == TOKEN_ROWS * TOKEN_LANES
    tile = pl.BlockSpec((1, ts, D), lambda b, i: (b, i, 0))
    memb = pl.BlockSpec((1, M, D), lambda b, i: (b, 0, 0))
    const = lambda shape: pl.BlockSpec(shape, lambda b, i: (0, 0))
    return pl.pallas_call(
        functools.partial(_cross_kernel, layer=layer, dh=D // CROSS_HEADS),
        grid=(B, ns),
        in_specs=[tile, memb, memb, tile, pl.BlockSpec(memory_space=pl.ANY),
                  const((1, D)), const((D, LANES)), const((1, LANES))],
        out_specs=[tile,
                   pl.BlockSpec((ts * TOKEN_ROWS, TOKEN_LANES), lambda b, i: (b * ns + i, 0)),
                   pl.BlockSpec((ts, LANES), lambda b, i: (b * ns + i, 0))],
        out_shape=[jax.ShapeDtypeStruct((B, S, D), F32),
                   jax.ShapeDtypeStruct((B * S * TOKEN_ROWS, TOKEN_LANES), F32),
                   jax.ShapeDtypeStruct((B * S, LANES), F32)],
        scratch_shapes=[pltpu.VMEM((D, D), BF16), pltpu.VMEM((2, W_CHUNK, D), F32),
                        pltpu.SemaphoreType.DMA((2,))],
        compiler_params=_params("arbitrary", "arbitrary"),
        name="cross_attention_ffn_in",
    )(qn, kn, v, x, w_co, g_ffn.reshape(1, D), wr, br.reshape(1, LANES))


def _route_kernel(lg_ref, tri_ref, meta_ref, cnt_ref, carry_ref):
    @pl.when(pl.program_id(0) == 0)
    def _():
        carry_ref[...] = jnp.zeros_like(carry_ref)

    x = lg_ref[...]
    lane = lax.broadcasted_iota(jnp.int32, x.shape, 1)
    ninf = -jnp.inf
    is_g = lane < MOE_GROUPS
    gl = jnp.where(is_g, x, ninf)
    gmax = jnp.max(gl, axis=-1, keepdims=True)
    gidx = jnp.min(jnp.where(gl == gmax, lane, LANES), axis=-1, keepdims=True)
    gw = 1.0 / jnp.sum(jnp.where(is_g, jnp.exp(x - gmax), 0.0), axis=-1, keepdims=True)
    e_lane = lane - ROUTER_LANE0
    in_grp = (e_lane >= 0) & (e_lane < N_EXPERTS) & ((e_lane >> 3) == gidx)
    el = jnp.where(in_grp, x, ninf)
    m1 = jnp.max(el, axis=-1, keepdims=True)
    i1 = jnp.min(jnp.where(el == m1, lane, LANES), axis=-1, keepdims=True)
    el2 = jnp.where(lane == i1, ninf, el)
    m2 = jnp.max(el2, axis=-1, keepdims=True)
    i2 = jnp.min(jnp.where(el2 == m2, lane, LANES), axis=-1, keepdims=True)
    r = jnp.exp(m2 - m1)
    p1 = 1.0 / (1.0 + r)
    p2 = r * p1
    oh = ((lane == i1) | (lane == i2)).astype(F32)
    prefix = jnp.dot(tri_ref[...], oh.astype(BF16), preferred_element_type=F32) + carry_ref[...]
    rank1 = jnp.sum(jnp.where(lane == i1, prefix, 0.0), axis=-1, keepdims=True)
    rank2 = jnp.sum(jnp.where(lane == i2, prefix, 0.0), axis=-1, keepdims=True)
    carry_ref[...] += jnp.sum(oh, axis=0, keepdims=True)
    cnt_ref[...] = carry_ref[...]
    cols = [i1.astype(F32), i2.astype(F32), gw * p1, gw * p2, rank1, rank2]
    meta = jnp.zeros(x.shape, F32)
    for k, col in enumerate(cols):
        meta = jnp.where(lane == k, col, meta)
    meta_ref[...] = meta


def route(logits):
    T = logits.shape[0]
    tt = ROUTE_ROWS
    tri = jnp.asarray(np.tril(np.ones((tt, tt), np.float32), -1), BF16)
    return pl.pallas_call(
        _route_kernel,
        grid=(T // tt,),
        in_specs=[pl.BlockSpec((tt, LANES), lambda i: (i, 0)),
                  pl.BlockSpec((tt, tt), lambda i: (0, 0))],
        out_specs=[pl.BlockSpec((tt, LANES), lambda i: (i, 0)),
                   pl.BlockSpec((1, LANES), lambda i: (0, 0))],
        out_shape=[jax.ShapeDtypeStruct((T, LANES), F32),
                   jax.ShapeDtypeStruct((1, LANES), F32)],
        scratch_shapes=[pltpu.VMEM((1, LANES), F32)],
        compiler_params=_params("arbitrary"),
        name="moe_route",
    )(logits, tri)


def _plan_kernel(meta_ref, cnt_ref, d_ref):
    cnt = jnp.broadcast_to(cnt_ref[...], (SUBLANES, LANES)).astype(I32)
    pad = (((cnt + (MOE_ROWS - 1)) >> MOE_ROWS_LOG2) << MOE_ROWS_LOG2).astype(F32)
    lane8 = lax.broadcasted_iota(I32, pad.shape, 1)
    incl = pad
    sh = 1
    while sh < LANES:
        incl = incl + jnp.where(lane8 >= sh, pltpu.roll(incl, sh, axis=1), 0.0)
        sh *= 2
    off = (incl - pad)[0:1, :]
    m = meta_ref[...]
    lane = lax.broadcasted_iota(I32, m.shape, 1)
    i1 = m[:, 0:1].astype(I32)
    i2 = m[:, 1:2].astype(I32)
    d1 = jnp.sum(jnp.where(lane == i1, off, 0.0), axis=-1, keepdims=True) + m[:, 4:5]
    d2 = jnp.sum(jnp.where(lane == i2, off, 0.0), axis=-1, keepdims=True) + m[:, 5:6]
    dm = jnp.where(lane == 0, d1, jnp.where(lane == 1, d2, 0.0))
    d_ref[...] = jnp.transpose(dm)[0:2, :].astype(I32)


def plan(meta, cnt):
    T = meta.shape[0]
    tt = ROUTE_ROWS
    return pl.pallas_call(
        _plan_kernel,
        grid=(T // tt,),
        in_specs=[pl.BlockSpec((tt, LANES), lambda i: (i, 0)),
                  pl.BlockSpec((1, LANES), lambda i: (0, 0))],
        out_specs=pl.BlockSpec((2, tt), lambda i: (0, i)),
        out_shape=jax.ShapeDtypeStruct((2, T), I32),
        compiler_params=_params("parallel"),
        name="moe_plan",
    )(meta, cnt)


def _invert_kernel(d1_ref, d2_ref, src_ref, zeros, sem, *, T):
    zeros[...] = jnp.zeros_like(zeros)
    fill = pltpu.make_async_copy(zeros, src_ref, sem)
    fill.start()
    fill.wait()

    def body(t, _):
        src_ref[d1_ref[t]] = t
        src_ref[d2_ref[t]] = t
        return 0
    lax.fori_loop(0, T, body, 0, unroll=8)


def invert(d1, d2, R):
    T = d1.shape[0]
    grid_spec = pltpu.PrefetchScalarGridSpec(
        num_scalar_prefetch=2, grid=(1,), in_specs=[],
        out_specs=pl.BlockSpec(memory_space=pltpu.SMEM),
        scratch_shapes=[pltpu.VMEM((R,), I32), pltpu.SemaphoreType.DMA(())])
    return pl.pallas_call(
        functools.partial(_invert_kernel, T=T),
        grid_spec=grid_spec,
        out_shape=jax.ShapeDtypeStruct((R,), I32),
        compiler_params=_params("arbitrary"),
        name="moe_invert",
    )(d1, d2)


GATHER_CHUNK = 8


def _tile_copy(src_hbm, row, dst, slot, r, sem):
    return pltpu.make_async_copy(src_hbm.at[pl.ds(row * TOKEN_ROWS, TOKEN_ROWS)],
                                 dst.at[slot, pl.ds(r * TOKEN_ROWS, TOKEN_ROWS)], sem.at[slot])


def _expert_kernel(te_ref, first_ref, nxt_ref, wsl_ref, nch_ref, src_ref,
                   hp_hbm, w1_hbm, w3_hbm, w2_hbm, o_ref,
                   xbuf, r1, r3, r2, w1b, w3b, w2b, gsem, wsem, *, tr, base):
    i = pl.program_id(0)
    n = pl.num_programs(0)
    slot = i % 2

    def weight_copies(e, ws):
        return [pltpu.make_async_copy(w1_hbm.at[base + e], r1.at[ws], wsem.at[ws, 0]),
                pltpu.make_async_copy(w3_hbm.at[base + e], r3.at[ws], wsem.at[ws, 1]),
                pltpu.make_async_copy(w2_hbm.at[base + e], r2.at[ws], wsem.at[ws, 2])]

    def issue_rows(tile, slot, c0, c1):
        def body(c, _):
            for u in range(GATHER_CHUNK):
                r = c * GATHER_CHUNK + u
                _tile_copy(hp_hbm, src_ref[tile * tr + r], xbuf, slot, r, gsem).start()
            return 0
        lax.fori_loop(c0, c1, body, 0)

    def wait_rows(tile, slot):
        rows = GATHER_CHUNK * TOKEN_ROWS
        def body(c, _):
            pltpu.make_async_copy(hp_hbm.at[pl.ds(0, rows)], xbuf.at[slot, pl.ds(0, rows)],
                                  gsem.at[slot]).wait()
            return 0
        lax.fori_loop(0, nch_ref[tile], body, 0)

    @pl.when(i == 0)
    def _():
        xbuf[...] = jnp.zeros_like(xbuf)
        for cp in weight_copies(te_ref[0], 0):
            cp.start(priority=1)
        issue_rows(0, 0, 0, nch_ref[0])

    wait_rows(i, slot)

    nxt_tile = jnp.minimum(i + 1, n - 1)
    issue_rows(nxt_tile, 1 - slot, 0, jnp.where(i + 1 < n, nch_ref[nxt_tile], 0))

    @pl.when(first_ref[i] == 1)
    def _():
        ws = wsl_ref[i]
        for cp in weight_copies(te_ref[i], ws):
            cp.wait()

        @pl.when(nxt_ref[i] >= 0)
        def _():
            for cp in weight_copies(nxt_ref[i], 1 - ws):
                cp.start(priority=1)

        w1b[...] = r1[ws].astype(BF16)
        w3b[...] = r3[ws].astype(BF16)
        w2b[...] = r2[ws].astype(BF16)

    def expert_mlp(rows):
        x = jnp.concatenate([p.astype(BF16) for p in _load_token_tiles(xbuf.at[slot], 0, rows)],
                            axis=1)
        h1 = jnp.dot(x, w1b[...], preferred_element_type=F32)
        h3 = jnp.dot(x, w3b[...], preferred_element_type=F32)
        hm = (h1 * _sigmoid(h1) * h3).astype(BF16)
        _store_token_tiles(o_ref, jnp.dot(hm, w2b[...], preferred_element_type=F32), rows)

    half_chunks = tr // (2 * GATHER_CHUNK)

    @pl.when(nch_ref[i] > half_chunks)
    def _():
        expert_mlp(tr)

    @pl.when((nch_ref[i] > 0) & (nch_ref[i] <= half_chunks))
    def _():
        expert_mlp(tr // 2)
        o_ref[tr // 2 * TOKEN_ROWS:, :] = jnp.zeros((tr // 2 * TOKEN_ROWS, TOKEN_LANES), F32)

    @pl.when(nch_ref[i] == 0)
    def _():
        o_ref[...] = jnp.zeros_like(o_ref)


def moe_experts(hp, w1, w3, w2, layer, tables, src):
    D, F = w1.shape[-2:]
    tr = MOE_ROWS
    te, first, nxt, wsl, nch = tables
    NT = te.shape[0]
    w1f = w1.reshape(-1, D, F)
    w3f = w3.reshape(-1, D, F)
    w2f = w2.reshape(-1, F, D)
    hbm = pl.BlockSpec(memory_space=pl.ANY)
    grid_spec = pltpu.PrefetchScalarGridSpec(
        num_scalar_prefetch=6,
        grid=(NT,),
        in_specs=[hbm, hbm, hbm, hbm],
        out_specs=pl.BlockSpec((tr * TOKEN_ROWS, TOKEN_LANES), lambda i, *_: (i, 0)),
        scratch_shapes=[pltpu.VMEM((2, tr * TOKEN_ROWS, TOKEN_LANES), F32),
                        pltpu.VMEM((2, D, F), F32),
                        pltpu.VMEM((2, D, F), F32),
                        pltpu.VMEM((2, F, D), F32),
                        pltpu.VMEM((D, F), BF16),
                        pltpu.VMEM((D, F), BF16),
                        pltpu.VMEM((F, D), BF16),
                        pltpu.SemaphoreType.DMA((2,)),
                        pltpu.SemaphoreType.DMA((2, 3))],
    )
    return pl.pallas_call(
        functools.partial(_expert_kernel, tr=tr, base=layer * N_EXPERTS),
        grid_spec=grid_spec,
        out_shape=jax.ShapeDtypeStruct((NT * tr * TOKEN_ROWS, TOKEN_LANES), F32),
        compiler_params=_params("arbitrary"),
        name="moe_experts",
    )(te, first, nxt, wsl, nch, src, hp, w1f, w3f, w2f)


def _combine_kernel(d1_ref, d2_ref, x_ref, meta_ref, ys_hbm, *rest, tt, with_norm):
    if with_norm:
        g_ref, o_ref, hn_ref, buf, sem = rest
    else:
        o_ref, buf, sem = rest
    i = pl.program_id(0)
    n = pl.num_programs(0)
    slot = i % 2

    def issue(tile, slot):
        def body(c, _):
            for u in range(GATHER_CHUNK):
                r = c * GATHER_CHUNK + u
                t = tile * tt + r
                _tile_copy(ys_hbm, d1_ref[t], buf, slot, r, sem).start(priority=0)
                _tile_copy(ys_hbm, d2_ref[t], buf, slot, tt + r, sem).start(priority=1)
            return 0
        lax.fori_loop(0, tt // GATHER_CHUNK, body, 0)

    @pl.when(i == 0)
    def _():
        issue(0, 0)

    pltpu.make_async_copy(ys_hbm.at[pl.ds(0, 2 * tt * TOKEN_ROWS)], buf.at[slot], sem.at[slot]).wait()

    @pl.when(i + 1 < n)
    def _():
        issue(i + 1, 1 - slot)

    w1 = meta_ref[:, 2:3]
    w2 = meta_ref[:, 3:4]
    pa = _load_token_tiles(buf.at[slot], 0, tt)
    pb = _load_token_tiles(buf.at[slot], tt * TOKEN_ROWS, tt)
    cols = [x_ref[:, j * TOKEN_LANES:(j + 1) * TOKEN_LANES] + w1 * pa[j] + w2 * pb[j]
            for j in range(TOKEN_ROWS)]
    xn = jnp.concatenate(cols, axis=1)
    o_ref[...] = xn
    if with_norm:
        hn_ref[...] = (_rms(xn) * g_ref[...]).astype(hn_ref.dtype)


def moe_combine(x, meta, ys, d1, d2, g_next=None):
    T, D = x.shape
    tt = COMB_ROWS
    with_norm = g_next is not None
    row = pl.BlockSpec((tt, D), lambda i, d1, d2: (i, 0))
    in_specs = [row, pl.BlockSpec((tt, LANES), lambda i, d1, d2: (i, 0)),
                pl.BlockSpec(memory_space=pl.ANY)]
    args = [d1, d2, x, meta, ys]
    out_specs, out_shape = row, jax.ShapeDtypeStruct((T, D), F32)
    if with_norm:
        in_specs.append(pl.BlockSpec((1, D), lambda i, d1, d2: (0, 0)))
        args.append(g_next.reshape(1, D))
        out_specs, out_shape = [row, row], [out_shape, jax.ShapeDtypeStruct((T, D), BF16)]
    grid_spec = pltpu.PrefetchScalarGridSpec(
        num_scalar_prefetch=2,
        grid=(T // tt,),
        in_specs=in_specs,
        out_specs=out_specs,
        scratch_shapes=[pltpu.VMEM((2, 2 * tt * TOKEN_ROWS, TOKEN_LANES), F32),
                        pltpu.SemaphoreType.DMA((2,))],
    )
    return pl.pallas_call(
        functools.partial(_combine_kernel, tt=tt, with_norm=with_norm),
        grid_spec=grid_spec,
        out_shape=out_shape,
        compiler_params=_params("arbitrary"),
        name="moe_combine",
    )(*args)


def _tile_tables(cnt, T):
    E = N_EXPERTS
    counts = cnt[0, ROUTER_LANE0:ROUTER_LANE0 + E].astype(I32)
    tiles_e = (counts + MOE_ROWS - 1) // MOE_ROWS
    tile_end = jnp.cumsum(tiles_e)
    tile_start = tile_end - tiles_e
    nact = tile_end[-1]
    NT = (2 * T) // MOE_ROWS + E
    tid = jnp.arange(NT, dtype=I32)
    te = jnp.sum((jnp.minimum(tid, nact - 1)[:, None] >= tile_end[None, :]).astype(I32), axis=1)
    te = jnp.minimum(te, E - 1)
    active = tid < nact
    first = (active & (tid == tile_start[te])).astype(I32)
    eid = jnp.arange(E, dtype=I32)
    later = (eid[None, :] > eid[:, None]) & (tiles_e[None, :] > 0)
    nxt_e = jnp.min(jnp.where(later, eid[None, :], E), axis=1)
    nxt_e = jnp.where(nxt_e == E, -1, nxt_e)
    ordinal = jnp.cumsum((tiles_e > 0).astype(I32)) - 1
    valid = jnp.clip(counts[te] - (tid - tile_start[te]) * MOE_ROWS, 0, MOE_ROWS)
    nch = jnp.where(active, (valid + GATHER_CHUNK - 1) // GATHER_CHUNK, 0)
    return (te, first, nxt_e[te], ordinal[te] % 2, nch), NT * MOE_ROWS


def _lower_bound_logs(lb_param, layer):
    p = jax.nn.softmax(lb_param.astype(F32), axis=0)
    c = jnp.cumsum(p, axis=0)
    lb = c[layer] - c[0]
    return jnp.log(lb), jnp.log1p(-lb)


def kernel(x, mem, norm_mix, w_in, hg_lower_bounds, hg_norm, conv_w, fox_f_bias, fox_q_norm,
           fox_k_norm, w_out, norm_cross, norm_mem, w_cq, w_ck, w_cv, w_co, cross_q_norm,
           cross_k_norm, norm_ffn, router_group_w, router_group_b, router_expert_w,
           router_expert_b, moe_w1, moe_w3, moe_w2):
    B, S, D = x.shape
    M = mem.shape[1]
    L = w_in.shape[0]
    T = B * S
    HGW = 4 * HG_WIDTH
    CONV3 = 3 * CONV_WIDTH
    PB = CONV3 + 3 * FOX_WIDTH
    FQ0, FK0, FV0 = 0, FOX_WIDTH, 2 * FOX_WIDTH
    CV0 = 3 * FOX_WIDTH
    dhc = D // CROSS_HEADS
    x2 = x.reshape(T, D)
    mem2 = mem.reshape(B * M, D)
    w_in_t = jnp.swapaxes(w_in, 1, 2)
    hn = rmsnorm(x2, norm_mix[0])
    for l in range(L):
        proj_a = matmul([hn], w_in_t, l, 0, HGW, F32, w_t=True)
        proj_b = matmul([hn], w_in_t, l, HGW, PB, BF16, w_t=True, rotate=CONV3, head_norms=(
            (FQ0, FK0, fox_q_norm[l], FOX_DIM ** -0.5), (FK0, FV0, fox_k_norm[l], 1.0)))
        loglb, l1mlb = _lower_bound_logs(hg_lower_bounds, l)
        y_hg = hgrn2(proj_a.reshape(B, S, HGW), loglb, l1mlb, hg_norm[l])
        pb3 = proj_b.reshape(B, S, PB)
        y_conv = short_conv(pb3, CV0, conv_w[l])
        c = fox_gate(hn.reshape(B, S, D), w_in_t, l, HGW + PB, fox_f_bias[l])
        y_fox = fox_attention(pb3, FQ0, FK0, FV0, c)
        mix = [y_hg.reshape(T, HG_WIDTH), y_conv.reshape(T, CONV_WIDTH), y_fox.reshape(T, FOX_WIDTH)]
        x2, qcn = out_proj_cross_q(mix, x2, w_out, w_cq, l, norm_cross[l], cross_q_norm[l])
        memn = rmsnorm(mem2, norm_mem[l])
        kcn = matmul([memn], w_ck, l, 0, D, BF16, head_norms=((0, D, cross_k_norm[l], 1.0),))
        vc = matmul([memn], w_cv, l, 0, D, BF16)
        wr = jnp.concatenate([router_group_w[l], router_expert_w[l]], axis=1)
        wr = jnp.pad(wr, ((0, 0), (0, LANES - wr.shape[1])))
        br = jnp.concatenate([router_group_b[l], router_expert_b[l]])
        br = jnp.pad(br, (0, LANES - br.shape[0]))
        x3, hp, logits = cross_attention_ffn_in(
            qcn.reshape(B, S, D), kcn.reshape(B, M, D), vc.reshape(B, M, D), x2.reshape(B, S, D),
            w_co, l, norm_ffn[l], wr, br)
        x2 = x3.reshape(T, D)
        meta, cnt = route(logits)
        d = plan(meta, cnt)
        tables, R = _tile_tables(cnt, T)
        src = invert(d[0], d[1], R)
        ys = moe_experts(hp, moe_w1, moe_w3, moe_w2, l, tables, src)
        if l + 1 < L:
            x2, hn = moe_combine(x2, meta, ys, d[0], d[1], norm_mix[l + 1])
        else:
            x2 = moe_combine(x2, meta, ys, d[0], d[1])
    return x2.reshape(B, S, D)
```

```python
import functools

import numpy as np
import jax
import jax.numpy as jnp
from jax import lax
from jax.experimental import pallas as pl
from jax.experimental.pallas import tpu as pltpu

F32 = jnp.float32
BF16 = jnp.bfloat16
I32 = jnp.int32
EPS = 1e-6

HG_HEADS = 4
HG_DIM = 128
HG_WIDTH = HG_HEADS * HG_DIM
CONV_WIDTH = 512
FOX_HEADS = 8
FOX_DIM = 128
FOX_WIDTH = FOX_HEADS * FOX_DIM
CROSS_HEADS = 4
MOE_GROUPS = 4
MOE_EXPERTS = 8
N_EXPERTS = MOE_GROUPS * MOE_EXPERTS
ROUTER_LANE0 = MOE_GROUPS

LANES = 128
SUBLANES = 8
VMEM_LIMIT = 56 * 1024 * 1024

NORM_ROWS = 512
MM_TM = 2048
MM_TN = 512
HG_CHUNK = 128
FOX_TQ = 512
FOX_TK = 512
FOX_HEADS_PER_STEP = 8
CROSS_TS = 512
ROUTE_ROWS = 1024
MOE_ROWS_LOG2 = 8
MOE_ROWS = 1 << MOE_ROWS_LOG2
COMB_ROWS = 256


def _params(*sem):
    return pltpu.CompilerParams(dimension_semantics=sem, vmem_limit_bytes=VMEM_LIMIT)


def _sigmoid(x):
    return 1.0 / (1.0 + jnp.exp(-x))


def _log_sigmoid(x):
    return jnp.minimum(x, 0.0) - jnp.log1p(jnp.exp(-jnp.abs(x)))


TOKEN_ROWS = 16
TOKEN_LANES = 128


def _store_token_tiles(ref, y, rows):
    for j in range(TOKEN_ROWS):
        ref[pl.ds(j, rows, stride=TOKEN_ROWS), :] = y[:, j * TOKEN_LANES:(j + 1) * TOKEN_LANES]


def _load_token_tiles(ref, base, rows):
    return [ref[pl.ds(base + j, rows, stride=TOKEN_ROWS), :] for j in range(TOKEN_ROWS)]


def _rms(x):
    return x * lax.rsqrt(jnp.mean(x * x, axis=-1, keepdims=True) + EPS)


def _rmsnorm_kernel(x_ref, g_ref, o_ref):
    o_ref[...] = (_rms(x_ref[...]) * g_ref[...]).astype(o_ref.dtype)


def rmsnorm(x, g, out_dtype=BF16):
    R, D = x.shape
    return pl.pallas_call(
        _rmsnorm_kernel,
        grid=(R // NORM_ROWS,),
        in_specs=[pl.BlockSpec((NORM_ROWS, D), lambda i: (i, 0)),
                  pl.BlockSpec((1, D), lambda i: (0, 0))],
        out_specs=pl.BlockSpec((NORM_ROWS, D), lambda i: (i, 0)),
        out_shape=jax.ShapeDtypeStruct((R, D), out_dtype),
        compiler_params=_params("parallel"),
        name="rmsnorm",
    )(x, g.reshape(1, D))


def _head_rmsnorm(x, g, dh, scale):
    heads = [_rms(x[:, h * dh:(h + 1) * dh]) * g * scale for h in range(x.shape[1] // dh)]
    return heads[0] if len(heads) == 1 else jnp.concatenate(heads, axis=1)


def _matmul_kernel(*refs, n_parts, w_t, norm_tiles, dh):
    a_refs = refs[:n_parts]
    w_ref = refs[n_parts]
    if norm_tiles:
        g_ref, o_ref, wb_ref = refs[n_parts + 1:]
    else:
        o_ref, wb_ref = refs[n_parts + 1:]
    j = pl.program_id(0)

    @pl.when(pl.program_id(1) == 0)
    def _():
        wb_ref[...] = w_ref[...].astype(BF16)

    if n_parts == 1:
        a = a_refs[0][...]
    else:
        a = jnp.concatenate([r[...] for r in a_refs], axis=1)
    if w_t:
        acc = lax.dot_general(a, wb_ref[...], (((1,), (1,)), ((), ())), preferred_element_type=F32)
    else:
        acc = jnp.dot(a, wb_ref[...], preferred_element_type=F32)
    if not norm_tiles:
        o_ref[...] = acc.astype(o_ref.dtype)
        return
    plain = True
    for k, (j0, j1, scale) in enumerate(norm_tiles):
        hit = (j >= j0) & (j < j1)
        plain = plain & jnp.logical_not(hit)

        @pl.when(hit)
        def _(k=k, scale=scale):
            o_ref[...] = _head_rmsnorm(acc, g_ref[k:k + 1, :], dh, scale).astype(o_ref.dtype)

    @pl.when(plain)
    def _():
        o_ref[...] = acc.astype(o_ref.dtype)


def matmul(a_parts, w, layer, col0, n, out_dtype, w_t=False, head_norms=(), rotate=0,
           tm=MM_TM, tn=MM_TN):
    M = a_parts[0].shape[0]
    K = w.shape[2] if w_t else w.shape[1]
    assert sum(p.shape[1] for p in a_parts) == K
    tm = min(tm, M)
    assert col0 % tn == 0 and n % tn == 0 and M % tm == 0 and rotate % tn == 0
    cb, nt, rot = col0 // tn, n // tn, rotate // tn
    in_specs = [pl.BlockSpec((tm, p.shape[1]), lambda j, i: (i, 0)) for p in a_parts]
    if w_t:
        in_specs.append(pl.BlockSpec((None, tn, K), lambda j, i: (layer, cb + (j + rot) % nt, 0)))
    else:
        in_specs.append(pl.BlockSpec((None, K, tn), lambda j, i: (layer, 0, cb + (j + rot) % nt)))
    args = list(a_parts) + [w]
    norm_tiles, dh = (), 0
    if head_norms:
        dh = head_norms[0][2].shape[0]
        assert all(c0 % tn == 0 and c1 % tn == 0 and g.shape[0] == dh and tn % dh == 0
                   for c0, c1, g, _ in head_norms)
        norm_tiles = tuple((c0 // tn, c1 // tn, s) for c0, c1, _, s in head_norms)
        gains = jnp.stack([g for _, _, g, _ in head_norms])
        in_specs.append(pl.BlockSpec(gains.shape, lambda j, i: (0, 0)))
        args.append(gains)
    return pl.pallas_call(
        functools.partial(_matmul_kernel, n_parts=len(a_parts), w_t=w_t, norm_tiles=norm_tiles,
                          dh=dh),
        grid=(n // tn, M // tm),
        in_specs=in_specs,
        out_specs=pl.BlockSpec((tm, tn), lambda j, i: (i, j)),
        out_shape=jax.ShapeDtypeStruct((M, n), out_dtype),
        scratch_shapes=[pltpu.VMEM((tn, K) if w_t else (K, tn), BF16)],
        compiler_params=_params("arbitrary", "arbitrary"),
        name="matmul",
    )(*args)


W_CHUNK = 256


def _load_weight(w_hbm, layer, wb, stg, sem):
    n_chunks = wb.shape[0] // W_CHUNK

    def copy(c):
        return pltpu.make_async_copy(w_hbm.at[layer, pl.ds(c * W_CHUNK, W_CHUNK)], stg.at[c % 2],
                                     sem.at[c % 2])
    copy(0).start()
    for c in range(n_chunks):
        if c + 1 < n_chunks:
            copy(c + 1).start()
        copy(c).wait()
        wb[c * W_CHUNK:(c + 1) * W_CHUNK, :] = stg[c % 2].astype(BF16)


def _outq_kernel(a1_ref, a2_ref, a3_ref, x_ref, wo_hbm, wq_hbm, gn_ref, gq_ref, xo_ref, q_ref,
                 wo_b, wq_b, stg, sem, *, layer, dh, scale):
    @pl.when(pl.program_id(0) == 0)
    def _():
        _load_weight(wo_hbm, layer, wo_b, stg, sem)
        _load_weight(wq_hbm, layer, wq_b, stg, sem)

    a = jnp.concatenate([a1_ref[...], a2_ref[...], a3_ref[...]], axis=1)
    xn = x_ref[...] + jnp.dot(a, wo_b[...], preferred_element_type=F32)
    xo_ref[...] = xn
    hc = (_rms(xn) * gn_ref[...]).astype(BF16)
    q = jnp.dot(hc, wq_b[...], preferred_element_type=F32)
    q_ref[...] = _head_rmsnorm(q, gq_ref[...], dh, scale).astype(q_ref.dtype)


def out_proj_cross_q(mix, x, w_out, w_cq, layer, g_cross, g_q, tm=512):
    T, D = x.shape
    dh = g_q.shape[0]
    row = lambda w: pl.BlockSpec((tm, w), lambda i: (i, 0))
    hbm = pl.BlockSpec(memory_space=pl.ANY)
    return pl.pallas_call(
        functools.partial(_outq_kernel, layer=layer, dh=dh, scale=dh ** -0.5),
        grid=(T // tm,),
        in_specs=[row(mix[0].shape[1]), row(mix[1].shape[1]), row(mix[2].shape[1]), row(D), hbm, hbm,
                  pl.BlockSpec((1, D), lambda i: (0, 0)), pl.BlockSpec((1, dh), lambda i: (0, 0))],
        out_specs=[row(D), row(D)],
        out_shape=[jax.ShapeDtypeStruct((T, D), F32), jax.ShapeDtypeStruct((T, D), BF16)],
        scratch_shapes=[pltpu.VMEM((D, D), BF16), pltpu.VMEM((D, D), BF16),
                        pltpu.VMEM((2, W_CHUNK, D), F32), pltpu.SemaphoreType.DMA((2,))],
        compiler_params=_params("arbitrary"),
        name="out_proj_cross_q",
    )(*mix, x, w_out, w_cq, g_cross.reshape(1, D), g_q.reshape(1, dh))


def _hgrn_consts(C):
    nlev = int(np.log2(C))
    t = np.arange(C)[:, None]
    u = np.arange(C)[None, :]
    tri = (u <= t).astype(np.float32)
    mall2 = np.concatenate([tri, tri], axis=1)
    tt = np.arange(C)[:, None]
    ss = np.arange(C)[None, :]
    lev = np.full((C, C), nlev + 1, np.int32)
    x = tt ^ ss
    hb = np.zeros_like(x)
    for j in range(nlev):
        hb = np.where((x >> j) & 1, j, hb)
    lev = np.where(tt > ss, hb, lev)
    lev = np.where(tt == ss, nlev, lev)
    return jnp.asarray(mall2, BF16), jnp.asarray(lev, jnp.int32)


def _level_sums(lf, b, row, j, C):
    m = 1 << j
    second = (row & m) != 0
    if j == 0:
        return jnp.where(second, lf, 0.0)
    if j == 1:
        prev = pltpu.roll(lf, 1, axis=0)
        nxt = pltpu.roll(lf, C - 1, axis=0)
        first = jnp.where((row & 1) == 0, nxt, 0.0)
        return jnp.where(second, jnp.where((row & 1) != 0, lf + prev, lf), first)
    g = 2 * m
    d = b.shape[1]
    bsel = jnp.broadcast_to(b.reshape(C // g, g, d)[:, m - 1:m, :], (C // g, g, d)).reshape(C, d)
    return jnp.where(second, b - bsel, bsel - b)


def _hgrn_kernel(x_ref, mall_ref, lev_ref, loglb_ref, l1mlb_ref, ng_ref, o_ref, st_ref, *, C, nlev):
    @pl.when(pl.program_id(1) == 0)
    def _():
        st_ref[...] = jnp.zeros_like(st_ref)

    W = HG_WIDTH
    d = HG_DIM
    lev = lev_ref[...]
    mall = mall_ref[...]
    row = lax.broadcasted_iota(jnp.int32, (C, d), 0)
    nt = (((1,), (1,)), ((), ()))

    def side(xs):
        return jnp.concatenate(xs, axis=1)

    def block_diag(xa, xb):
        return jnp.concatenate([side([xa, jnp.zeros_like(xb)]), side([jnp.zeros_like(xa), xb])],
                               axis=0)

    for h0 in range(0, HG_HEADS, 2):
        q, lf, kk, vb, g = [], [], [], [], []
        for h in (h0, h0 + 1):
            sl = slice(h * d, (h + 1) * d)
            q.append(x_ref[0, :, h * d:(h + 1) * d])
            z = x_ref[0, :, W + h * d:W + (h + 1) * d]
            vb.append(x_ref[0, :, 2 * W + h * d:2 * W + (h + 1) * d].astype(BF16))
            g.append(x_ref[0, :, 3 * W + h * d:3 * W + (h + 1) * d])
            a = loglb_ref[:, sl]
            bt = l1mlb_ref[:, sl] + _log_sigmoid(z)
            lf.append(jnp.maximum(a, bt) + jnp.log1p(jnp.exp(-jnp.abs(a - bt))))
            kk.append(1.0 - jnp.exp(lf[-1]))
        hi = [x.astype(BF16) for x in lf]
        lo = [(x - y.astype(F32)).astype(BF16) for x, y in zip(lf, hi)]
        b2 = jnp.dot(mall, jnp.concatenate([side(hi), side(lo)], axis=0),
                     preferred_element_type=F32)
        b = [b2[:, :d], b2[:, d:]]
        st = [st_ref[h0], st_ref[h0 + 1]]
        o2 = lax.dot_general(side([(q[i] * jnp.exp(b[i])).astype(BF16) for i in (0, 1)]),
                             block_diag(st[0].astype(BF16), st[1].astype(BF16)), nt,
                             preferred_element_type=F32)
        att = [jnp.where(lev == nlev, jnp.sum(q[i] * kk[i], axis=-1, keepdims=True), 0.0)
               for i in (0, 1)]
        for j in range(nlev):
            second = (row & (1 << j)) != 0
            xq, xk = [], []
            for i in (0, 1):
                e = jnp.exp(_level_sums(lf[i], b[i], row, j, C))
                xq.append(jnp.where(second, q[i] * e, 0.0).astype(BF16))
                xk.append(jnp.where(second, 0.0, kk[i] * e).astype(BF16))
            am = lax.dot_general(side(xq), block_diag(xk[0], xk[1]), nt,
                                 preferred_element_type=F32)
            att = [jnp.where(lev == j, am[:, i * C:(i + 1) * C], att[i]) for i in (0, 1)]
        o2 = o2 + jnp.dot(side([x.astype(BF16) for x in att]), block_diag(vb[0], vb[1]),
                          preferred_element_type=F32)
        for i, h in enumerate((h0, h0 + 1)):
            bl = b[i][C - 1:C, :]
            kh = (kk[i] * jnp.exp(bl - b[i])).astype(BF16)
            st_ref[h] = st[i] * jnp.exp(bl) + lax.dot_general(
                vb[i], kh, (((0,), (0,)), ((), ())), preferred_element_type=F32)
            o = o2[:, i * d:(i + 1) * d]
            y = o * lax.rsqrt(jnp.mean(o * o, axis=-1, keepdims=True) + EPS) * ng_ref[...]
            o_ref[0, :, h * d:(h + 1) * d] = (y * (g[i] * _sigmoid(g[i]))).astype(o_ref.dtype)


def hgrn2(proj_a, loglb, l1mlb, norm_g):
    B, S, _ = proj_a.shape
    C = HG_CHUNK
    nlev = int(np.log2(C))
    mall, lev = _hgrn_consts(C)
    return pl.pallas_call(
        functools.partial(_hgrn_kernel, C=C, nlev=nlev),
        grid=(B, S // C),
        in_specs=[pl.BlockSpec((1, C, 4 * HG_WIDTH), lambda b, c: (b, c, 0)),
                  pl.BlockSpec(mall.shape, lambda b, c: (0, 0)),
                  pl.BlockSpec((C, C), lambda b, c: (0, 0)),
                  pl.BlockSpec((1, HG_WIDTH), lambda b, c: (0, 0)),
                  pl.BlockSpec((1, HG_WIDTH), lambda b, c: (0, 0)),
                  pl.BlockSpec((1, HG_DIM), lambda b, c: (0, 0))],
        out_specs=pl.BlockSpec((1, C, HG_WIDTH), lambda b, c: (b, c, 0)),
        out_shape=jax.ShapeDtypeStruct((B, S, HG_WIDTH), BF16),
        scratch_shapes=[pltpu.VMEM((HG_HEADS, HG_DIM, HG_DIM), F32)],
        compiler_params=_params("parallel", "arbitrary"),
        name="hgrn2",
    )(proj_a, mall, lev, loglb.reshape(1, HG_WIDTH), l1mlb.reshape(1, HG_WIDTH),
      norm_g.reshape(1, HG_DIM))


def _conv_kernel(b_ref, c_ref, h_ref, w_ref, o_ref):
    u = c_ref[0].astype(F32) * h_ref[0].astype(F32)
    row = lax.broadcasted_iota(jnp.int32, u.shape, 0)
    u1 = jnp.where(row >= 1, pltpu.roll(u, 1, axis=0), 0.0)
    u2 = jnp.where(row >= 2, pltpu.roll(u, 2, axis=0), 0.0)
    y = w_ref[0:1, :] * u2 + w_ref[1:2, :] * u1 + w_ref[2:3, :] * u
    o_ref[0] = (b_ref[0].astype(F32) * y).astype(o_ref.dtype)


def short_conv(proj_b, col0, w):
    B, S, _ = proj_b.shape
    cw = CONV_WIDTH
    assert col0 % cw == 0
    spec = lambda k: pl.BlockSpec((1, S, cw), lambda b: (b, 0, col0 // cw + k))
    return pl.pallas_call(
        _conv_kernel,
        grid=(B,),
        in_specs=[spec(0), spec(1), spec(2), pl.BlockSpec((3, cw), lambda b: (0, 0))],
        out_specs=pl.BlockSpec((1, S, cw), lambda b: (b, 0, 0)),
        out_shape=jax.ShapeDtypeStruct((B, S, cw), BF16),
        compiler_params=_params("parallel"),
        name="short_conv",
    )(proj_b, proj_b, proj_b, w)


def _fox_gate_kernel(hn_ref, wf_ref, bias_ref, c_ref):
    gl = lax.dot_general(wf_ref[...].astype(BF16), hn_ref[0], (((1,), (1,)), ((), ())),
                         preferred_element_type=F32)
    c = _log_sigmoid(gl + bias_ref[...])
    S = c.shape[1]
    lane = lax.broadcasted_iota(jnp.int32, c.shape, 1)
    sh = 1
    while sh < S:
        c = c + jnp.where(lane >= sh, pltpu.roll(c, sh, axis=1), 0.0)
        sh *= 2
    c_ref[0] = c


def fox_gate(hn, w_t, layer, row0, bias):
    B, S, D = hn.shape
    H = bias.shape[0]
    assert row0 % H == 0
    return pl.pallas_call(
        _fox_gate_kernel,
        grid=(B,),
        in_specs=[pl.BlockSpec((1, S, D), lambda b: (b, 0, 0)),
                  pl.BlockSpec((None, H, D), lambda b: (layer, row0 // H, 0)),
                  pl.BlockSpec((H, 1), lambda b: (0, 0))],
        out_specs=pl.BlockSpec((1, H, S), lambda b: (b, 0, 0)),
        out_shape=jax.ShapeDtypeStruct((B, H, S), F32),
        compiler_params=_params("parallel"),
        name="fox_gate",
    )(hn, w_t, bias.reshape(H, 1))


def _fox_kernel(q_ref, k_ref, v_ref, c_ref, o_ref, *, tq, tk, nh):
    qi = pl.program_id(2)
    nt = (((1,), (1,)), ((), ()))
    ones = jnp.ones((tk, FOX_DIM), BF16)
    heads = [slice(h * FOX_DIM, (h + 1) * FOX_DIM) for h in range(nh)]
    qs = [q_ref[0, :, sl] for sl in heads]

    def step(ki, carry, diag):
        out = []
        for h, sl in enumerate(heads):
            m, acc = carry[h]
            k = k_ref[0, pl.ds(ki * tk, tk), sl]
            v1 = jnp.concatenate([v_ref[0, pl.ds(ki * tk, tk), sl], ones], axis=1)
            ck = c_ref[0, h, pl.ds(ki, 1), :]
            s = lax.dot_general(qs[h], k, nt, preferred_element_type=F32) - ck
            if diag is not None:
                r = lax.broadcasted_iota(jnp.int32, s.shape, 0)
                c = lax.broadcasted_iota(jnp.int32, s.shape, 1)
                s = jnp.where(c + diag * tk <= r, s, -jnp.inf)
            m_new = jnp.maximum(m, jnp.max(s, axis=-1, keepdims=True))
            alpha = jnp.exp(m - m_new)
            p = jnp.exp((s - m_new).astype(BF16))
            out.append((m_new, alpha * acc + jnp.dot(p, v1, preferred_element_type=F32)))
        return tuple(out)

    init = tuple((jnp.full((tq, 1), -jnp.inf, F32), jnp.zeros((tq, 2 * FOX_DIM), F32))
                 for _ in heads)
    n_full = qi * (tq // tk)
    carry = lax.fori_loop(0, n_full, lambda ki, cr: step(ki, cr, None), init)
    for d in range(tq // tk):
        carry = step(n_full + d, carry, d)
    for h, sl in enumerate(heads):
        acc = carry[h][1]
        o_ref[0, :, sl] = (acc[:, :FOX_DIM] / acc[:, FOX_DIM:]).astype(o_ref.dtype)


def fox_attention(proj_b, q_col0, k_col0, v_col0, c):
    B, S, _ = proj_b.shape
    tq, tk, nh = FOX_TQ, FOX_TK, FOX_HEADS_PER_STEP
    H = FOX_HEADS
    w = nh * FOX_DIM
    assert tq % tk == 0 and H % nh == 0 and all(c0 % w == 0 for c0 in (q_col0, k_col0, v_col0))
    qb, kb, vb = q_col0 // w, k_col0 // w, v_col0 // w
    c4 = c.reshape(B, H, S // tk, tk)
    return pl.pallas_call(
        functools.partial(_fox_kernel, tq=tq, tk=tk, nh=nh),
        grid=(B, H // nh, S // tq),
        in_specs=[pl.BlockSpec((1, tq, w), lambda b, h, i: (b, i, qb + h)),
                  pl.BlockSpec((1, S, w), lambda b, h, i: (b, 0, kb + h)),
                  pl.BlockSpec((1, S, w), lambda b, h, i: (b, 0, vb + h)),
                  pl.BlockSpec((1, nh, S // tk, tk), lambda b, h, i: (b, h, 0, 0))],
        out_specs=pl.BlockSpec((1, tq, w), lambda b, h, i: (b, i, h)),
        out_shape=jax.ShapeDtypeStruct((B, S, FOX_WIDTH), BF16),
        compiler_params=_params("parallel", "parallel", "arbitrary"),
        name="fox_attention",
    )(proj_b, proj_b, proj_b, c4)


def _cross_kernel(q_ref, k_ref, v_ref, x_ref, wo_hbm, gn_ref, wr_ref, br_ref,
                  xo_ref, hp_ref, lg_ref, wo_b, stg, sem, *, layer, dh):
    @pl.when((pl.program_id(0) == 0) & (pl.program_id(1) == 0))
    def _():
        _load_weight(wo_hbm, layer, wo_b, stg, sem)

    nt = (((1,), (1,)), ((), ()))
    heads = []
    for h in range(CROSS_HEADS):
        sl = slice(h * dh, (h + 1) * dh)
        s = lax.dot_general(q_ref[0, :, sl], k_ref[0, :, sl], nt, preferred_element_type=F32)
        m = jnp.max(s, axis=-1, keepdims=True)
        p = jnp.exp(s - m)
        l = jnp.sum(p, axis=-1, keepdims=True)
        o = jnp.dot(p.astype(BF16), v_ref[0, :, sl], preferred_element_type=F32)
        heads.append((o / l).astype(BF16))
    xn = x_ref[0] + jnp.dot(jnp.concatenate(heads, axis=1), wo_b[...], preferred_element_type=F32)
    xo_ref[0] = xn
    y = _rms(xn) * gn_ref[...]
    _store_token_tiles(hp_ref, y, y.shape[0])
    yh = y.astype(BF16)
    yl = (y - yh.astype(F32)).astype(BF16)
    wr = wr_ref[...]
    wh = wr.astype(BF16)
    wl = (wr - wh.astype(F32)).astype(BF16)
    lg_ref[...] = (jnp.dot(yh, wh, preferred_element_type=F32)
                   + jnp.dot(yl, wh, preferred_element_type=F32)
                   + jnp.dot(yh, wl, preferred_element_type=F32)) + br_ref[...]


def cross_attention_ffn_in(qn, kn, v, x, w_co, layer, g_ffn, wr, br):
    B, S, D = qn.shape
    M = kn.shape[1]
    ts = CROSS_TS
    ns = S // ts
    assert D == TOKEN_ROWS * TOKEN_LANES
    tile = pl.BlockSpec((1, ts, D), lambda b, i: (b, i, 0))
    memb = pl.BlockSpec((1, M, D), lambda b, i: (b, 0, 0))
    const = lambda shape: pl.BlockSpec(shape, lambda b, i: (0, 0))
    return pl.pallas_call(
        functools.partial(_cross_kernel, layer=layer, dh=D // CROSS_HEADS),
        grid=(B, ns),
        in_specs=[tile, memb, memb, tile, pl.BlockSpec(memory_space=pl.ANY),
                  const((1, D)), const((D, LANES)), const((1, LANES))],
        out_specs=[tile,
                   pl.BlockSpec((ts * TOKEN_ROWS, TOKEN_LANES), lambda b, i: (b * ns + i, 0)),
                   pl.BlockSpec((ts, LANES), lambda b, i: (b * ns + i, 0))],
        out_shape=[jax.ShapeDtypeStruct((B, S, D), F32),
                   jax.ShapeDtypeStruct((B * S * TOKEN_ROWS, TOKEN_LANES), F32),
                   jax.ShapeDtypeStruct((B * S, LANES), F32)],
        scratch_shapes=[pltpu.VMEM((D, D), BF16), pltpu.VMEM((2, W_CHUNK, D), F32),
                        pltpu.SemaphoreType.DMA((2,))],
        compiler_params=_params("arbitrary", "arbitrary"),
        name="cross_attention_ffn_in",
    )(qn, kn, v, x, w_co, g_ffn.reshape(1, D), wr, br.reshape(1, LANES))


def _route_kernel(lg_ref, tri_ref, meta_ref, cnt_ref, carry_ref):
    @pl.when(pl.program_id(0) == 0)
    def _():
        carry_ref[...] = jnp.zeros_like(carry_ref)

    x = lg_ref[...]
    lane = lax.broadcasted_iota(jnp.int32, x.shape, 1)
    ninf = -jnp.inf
    is_g = lane < MOE_GROUPS
    gl = jnp.where(is_g, x, ninf)
    gmax = jnp.max(gl, axis=-1, keepdims=True)
    gidx = jnp.min(jnp.where(gl == gmax, lane, LANES), axis=-1, keepdims=True)
    gw = 1.0 / jnp.sum(jnp.where(is_g, jnp.exp(x - gmax), 0.0), axis=-1, keepdims=True)
    e_lane = lane - ROUTER_LANE0
    in_grp = (e_lane >= 0) & (e_lane < N_EXPERTS) & ((e_lane >> 3) == gidx)
    el = jnp.where(in_grp, x, ninf)
    m1 = jnp.max(el, axis=-1, keepdims=True)
    i1 = jnp.min(jnp.where(el == m1, lane, LANES), axis=-1, keepdims=True)
    el2 = jnp.where(lane == i1, ninf, el)
    m2 = jnp.max(el2, axis=-1, keepdims=True)
    i2 = jnp.min(jnp.where(el2 == m2, lane, LANES), axis=-1, keepdims=True)
    r = jnp.exp(m2 - m1)
    p1 = 1.0 / (1.0 + r)
    p2 = r * p1
    oh = ((lane == i1) | (lane == i2)).astype(F32)
    prefix = jnp.dot(tri_ref[...], oh.astype(BF16), preferred_element_type=F32) + carry_ref[...]
    rank1 = jnp.sum(jnp.where(lane == i1, prefix, 0.0), axis=-1, keepdims=True)
    rank2 = jnp.sum(jnp.where(lane == i2, prefix, 0.0), axis=-1, keepdims=True)
    carry_ref[...] += jnp.sum(oh, axis=0, keepdims=True)
    cnt_ref[...] = carry_ref[...]
    cols = [i1.astype(F32), i2.astype(F32), gw * p1, gw * p2, rank1, rank2]
    meta = jnp.zeros(x.shape, F32)
    for k, col in enumerate(cols):
        meta = jnp.where(lane == k, col, meta)
    meta_ref[...] = meta


def route(logits):
    T = logits.shape[0]
    tt = ROUTE_ROWS
    tri = jnp.asarray(np.tril(np.ones((tt, tt), np.float32), -1), BF16)
    return pl.pallas_call(
        _route_kernel,
        grid=(T // tt,),
        in_specs=[pl.BlockSpec((tt, LANES), lambda i: (i, 0)),
                  pl.BlockSpec((tt, tt), lambda i: (0, 0))],
        out_specs=[pl.BlockSpec((tt, LANES), lambda i: (i, 0)),
                   pl.BlockSpec((1, LANES), lambda i: (0, 0))],
        out_shape=[jax.ShapeDtypeStruct((T, LANES), F32),
                   jax.ShapeDtypeStruct((1, LANES), F32)],
        scratch_shapes=[pltpu.VMEM((1, LANES), F32)],
        compiler_params=_params("arbitrary"),
        name="moe_route",
    )(logits, tri)


def _plan_kernel(meta_ref, cnt_ref, d_ref):
    cnt = jnp.broadcast_to(cnt_ref[...], (SUBLANES, LANES)).astype(I32)
    pad = (((cnt + (MOE_ROWS - 1)) >> MOE_ROWS_LOG2) << MOE_ROWS_LOG2).astype(F32)
    lane8 = lax.broadcasted_iota(I32, pad.shape, 1)
    incl = pad
    sh = 1
    while sh < LANES:
        incl = incl + jnp.where(lane8 >= sh, pltpu.roll(incl, sh, axis=1), 0.0)
        sh *= 2
    off = (incl - pad)[0:1, :]
    m = meta_ref[...]
    lane = lax.broadcasted_iota(I32, m.shape, 1)
    i1 = m[:, 0:1].astype(I32)
    i2 = m[:, 1:2].astype(I32)
    d1 = jnp.sum(jnp.where(lane == i1, off, 0.0), axis=-1, keepdims=True) + m[:, 4:5]
    d2 = jnp.sum(jnp.where(lane == i2, off, 0.0), axis=-1, keepdims=True) + m[:, 5:6]
    dm = jnp.where(lane == 0, d1, jnp.where(lane == 1, d2, 0.0))
    d_ref[...] = jnp.transpose(dm)[0:2, :].astype(I32)


def plan(meta, cnt):
    T = meta.shape[0]
    tt = ROUTE_ROWS
    return pl.pallas_call(
        _plan_kernel,
        grid=(T // tt,),
        in_specs=[pl.BlockSpec((tt, LANES), lambda i: (i, 0)),
                  pl.BlockSpec((1, LANES), lambda i: (0, 0))],
        out_specs=pl.BlockSpec((2, tt), lambda i: (0, i)),
        out_shape=jax.ShapeDtypeStruct((2, T), I32),
        compiler_params=_params("parallel"),
        name="moe_plan",
    )(meta, cnt)


def _invert_kernel(d1_ref, d2_ref, src_ref, zeros, sem, *, T):
    zeros[...] = jnp.zeros_like(zeros)
    fill = pltpu.make_async_copy(zeros, src_ref, sem)
    fill.start()
    fill.wait()

    def body(t, _):
        src_ref[d1_ref[t]] = t
        src_ref[d2_ref[t]] = t
        return 0
    lax.fori_loop(0, T, body, 0, unroll=8)


def invert(d1, d2, R):
    T = d1.shape[0]
    grid_spec = pltpu.PrefetchScalarGridSpec(
        num_scalar_prefetch=2, grid=(1,), in_specs=[],
        out_specs=pl.BlockSpec(memory_space=pltpu.SMEM),
        scratch_shapes=[pltpu.VMEM((R,), I32), pltpu.SemaphoreType.DMA(())])
    return pl.pallas_call(
        functools.partial(_invert_kernel, T=T),
        grid_spec=grid_spec,
        out_shape=jax.ShapeDtypeStruct((R,), I32),
        compiler_params=_params("arbitrary"),
        name="moe_invert",
    )(d1, d2)


GATHER_CHUNK = 8


def _tile_copy(src_hbm, row, dst, slot, r, sem):
    return pltpu.make_async_copy(src_hbm.at[pl.ds(row * TOKEN_ROWS, TOKEN_ROWS)],
                                 dst.at[slot, pl.ds(r * TOKEN_ROWS, TOKEN_ROWS)], sem.at[slot])


def _expert_kernel(te_ref, first_ref, nxt_ref, wsl_ref, nch_ref, src_ref,
                   hp_hbm, w1_hbm, w3_hbm, w2_hbm, o_ref,
                   xbuf, r1, r3, r2, w1b, w3b, w2b, gsem, wsem, *, tr, base):
    i = pl.program_id(0)
    n = pl.num_programs(0)
    slot = i % 2

    def weight_copies(e, ws):
        return [pltpu.make_async_copy(w1_hbm.at[base + e], r1.at[ws], wsem.at[ws, 0]),
                pltpu.make_async_copy(w3_hbm.at[base + e], r3.at[ws], wsem.at[ws, 1]),
                pltpu.make_async_copy(w2_hbm.at[base + e], r2.at[ws], wsem.at[ws, 2])]

    def issue_rows(tile, slot, c0, c1):
        def body(c, _):
            for u in range(GATHER_CHUNK):
                r = c * GATHER_CHUNK + u
                _tile_copy(hp_hbm, src_ref[tile * tr + r], xbuf, slot, r, gsem).start(priority=u % 2)
            return 0
        lax.fori_loop(c0, c1, body, 0)

    def wait_rows(tile, slot):
        rows = GATHER_CHUNK * TOKEN_ROWS
        def body(c, _):
            pltpu.make_async_copy(hp_hbm.at[pl.ds(0, rows)], xbuf.at[slot, pl.ds(0, rows)],
                                  gsem.at[slot]).wait()
            return 0
        lax.fori_loop(0, nch_ref[tile], body, 0)

    @pl.when(i == 0)
    def _():
        xbuf[...] = jnp.zeros_like(xbuf)
        for cp in weight_copies(te_ref[0], 0):
            cp.start(priority=1)
        issue_rows(0, 0, 0, nch_ref[0])

    wait_rows(i, slot)

    nxt_tile = jnp.minimum(i + 1, n - 1)
    issue_rows(nxt_tile, 1 - slot, 0, jnp.where(i + 1 < n, nch_ref[nxt_tile], 0))

    @pl.when(first_ref[i] == 1)
    def _():
        ws = wsl_ref[i]
        for cp in weight_copies(te_ref[i], ws):
            cp.wait()

        @pl.when(nxt_ref[i] >= 0)
        def _():
            for cp in weight_copies(nxt_ref[i], 1 - ws):
                cp.start(priority=1)

        w1b[...] = r1[ws].astype(BF16)
        w3b[...] = r3[ws].astype(BF16)
        w2b[...] = r2[ws].astype(BF16)

    def expert_mlp(rows):
        x = jnp.concatenate([p.astype(BF16) for p in _load_token_tiles(xbuf.at[slot], 0, rows)],
                            axis=1)
        h1 = jnp.dot(x, w1b[...], preferred_element_type=F32)
        h3 = jnp.dot(x, w3b[...], preferred_element_type=F32)
        hm = (h1 * _sigmoid(h1) * h3).astype(BF16)
        _store_token_tiles(o_ref, jnp.dot(hm, w2b[...], preferred_element_type=F32), rows)

    half_chunks = tr // (2 * GATHER_CHUNK)

    @pl.when(nch_ref[i] > half_chunks)
    def _():
        expert_mlp(tr)

    @pl.when((nch_ref[i] > 0) & (nch_ref[i] <= half_chunks))
    def _():
        expert_mlp(tr // 2)
        o_ref[tr // 2 * TOKEN_ROWS:, :] = jnp.zeros((tr // 2 * TOKEN_ROWS, TOKEN_LANES), F32)

    @pl.when(nch_ref[i] == 0)
    def _():
        o_ref[...] = jnp.zeros_like(o_ref)


def moe_experts(hp, w1, w3, w2, layer, tables, src):
    D, F = w1.shape[-2:]
    tr = MOE_ROWS
    te, first, nxt, wsl, nch = tables
    NT = te.shape[0]
    w1f = w1.reshape(-1, D, F)
    w3f = w3.reshape(-1, D, F)
    w2f = w2.reshape(-1, F, D)
    hbm = pl.BlockSpec(memory_space=pl.ANY)
    grid_spec = pltpu.PrefetchScalarGridSpec(
        num_scalar_prefetch=6,
        grid=(NT,),
        in_specs=[hbm, hbm, hbm, hbm],
        out_specs=pl.BlockSpec((tr * TOKEN_ROWS, TOKEN_LANES), lambda i, *_: (i, 0)),
        scratch_shapes=[pltpu.VMEM((2, tr * TOKEN_ROWS, TOKEN_LANES), F32),
                        pltpu.VMEM((2, D, F), F32),
                        pltpu.VMEM((2, D, F), F32),
                        pltpu.VMEM((2, F, D), F32),
                        pltpu.VMEM((D, F), BF16),
                        pltpu.VMEM((D, F), BF16),
                        pltpu.VMEM((F, D), BF16),
                        pltpu.SemaphoreType.DMA((2,)),
                        pltpu.SemaphoreType.DMA((2, 3))],
    )
    return pl.pallas_call(
        functools.partial(_expert_kernel, tr=tr, base=layer * N_EXPERTS),
        grid_spec=grid_spec,
        out_shape=jax.ShapeDtypeStruct((NT * tr * TOKEN_ROWS, TOKEN_LANES), F32),
        compiler_params=_params("arbitrary"),
        name="moe_experts",
    )(te, first, nxt, wsl, nch, src, hp, w1f, w3f, w2f)


def _combine_kernel(d1_ref, d2_ref, x_ref, meta_ref, ys_hbm, *rest, tt, with_norm):
    if with_norm:
        g_ref, o_ref, hn_ref, buf, sem = rest
    else:
        o_ref, buf, sem = rest
    i = pl.program_id(0)
    n = pl.num_programs(0)
    slot = i % 2

    def issue(tile, slot):
        def body(c, _):
            for u in range(GATHER_CHUNK):
                r = c * GATHER_CHUNK + u
                t = tile * tt + r
                _tile_copy(ys_hbm, d1_ref[t], buf, slot, r, sem).start(priority=0)
                _tile_copy(ys_hbm, d2_ref[t], buf, slot, tt + r, sem).start(priority=1)
            return 0
        lax.fori_loop(0, tt // GATHER_CHUNK, body, 0)

    @pl.when(i == 0)
    def _():
        issue(0, 0)

    pltpu.make_async_copy(ys_hbm.at[pl.ds(0, 2 * tt * TOKEN_ROWS)], buf.at[slot], sem.at[slot]).wait()

    @pl.when(i + 1 < n)
    def _():
        issue(i + 1, 1 - slot)

    w1 = meta_ref[:, 2:3]
    w2 = meta_ref[:, 3:4]
    pa = _load_token_tiles(buf.at[slot], 0, tt)
    pb = _load_token_tiles(buf.at[slot], tt * TOKEN_ROWS, tt)
    cols = [x_ref[:, j * TOKEN_LANES:(j + 1) * TOKEN_LANES] + w1 * pa[j] + w2 * pb[j]
            for j in range(TOKEN_ROWS)]
    xn = jnp.concatenate(cols, axis=1)
    o_ref[...] = xn
    if with_norm:
        hn_ref[...] = (_rms(xn) * g_ref[...]).astype(hn_ref.dtype)


def moe_combine(x, meta, ys, d1, d2, g_next=None):
    T, D = x.shape
    tt = COMB_ROWS
    with_norm = g_next is not None
    row = pl.BlockSpec((tt, D), lambda i, d1, d2: (i, 0))
    in_specs = [row, pl.BlockSpec((tt, LANES), lambda i, d1, d2: (i, 0)),
                pl.BlockSpec(memory_space=pl.ANY)]
    args = [d1, d2, x, meta, ys]
    out_specs, out_shape = row, jax.ShapeDtypeStruct((T, D), F32)
    if with_norm:
        in_specs.append(pl.BlockSpec((1, D), lambda i, d1, d2: (0, 0)))
        args.append(g_next.reshape(1, D))
        out_specs, out_shape = [row, row], [out_shape, jax.ShapeDtypeStruct((T, D), BF16)]
    grid_spec = pltpu.PrefetchScalarGridSpec(
        num_scalar_prefetch=2,
        grid=(T // tt,),
        in_specs=in_specs,
        out_specs=out_specs,
        scratch_shapes=[pltpu.VMEM((2, 2 * tt * TOKEN_ROWS, TOKEN_LANES), F32),
                        pltpu.SemaphoreType.DMA((2,))],
    )
    return pl.pallas_call(
        functools.partial(_combine_kernel, tt=tt, with_norm=with_norm),
        grid_spec=grid_spec,
        out_shape=out_shape,
        compiler_params=_params("arbitrary"),
        name="moe_combine",
    )(*args)


def _tile_tables(cnt, T):
    E = N_EXPERTS
    counts = cnt[0, ROUTER_LANE0:ROUTER_LANE0 + E].astype(I32)
    tiles_e = (counts + MOE_ROWS - 1) // MOE_ROWS
    tile_end = jnp.cumsum(tiles_e)
    tile_start = tile_end - tiles_e
    nact = tile_end[-1]
    NT = (2 * T) // MOE_ROWS + E
    tid = jnp.arange(NT, dtype=I32)
    te = jnp.sum((jnp.minimum(tid, nact - 1)[:, None] >= tile_end[None, :]).astype(I32), axis=1)
    te = jnp.minimum(te, E - 1)
    active = tid < nact
    first = (active & (tid == tile_start[te])).astype(I32)
    eid = jnp.arange(E, dtype=I32)
    later = (eid[None, :] > eid[:, None]) & (tiles_e[None, :] > 0)
    nxt_e = jnp.min(jnp.where(later, eid[None, :], E), axis=1)
    nxt_e = jnp.where(nxt_e == E, -1, nxt_e)
    ordinal = jnp.cumsum((tiles_e > 0).astype(I32)) - 1
    valid = jnp.clip(counts[te] - (tid - tile_start[te]) * MOE_ROWS, 0, MOE_ROWS)
    nch = jnp.where(active, (valid + GATHER_CHUNK - 1) // GATHER_CHUNK, 0)
    return (te, first, nxt_e[te], ordinal[te] % 2, nch), NT * MOE_ROWS


def _lower_bound_logs(lb_param, layer):
    p = jax.nn.softmax(lb_param.astype(F32), axis=0)
    c = jnp.cumsum(p, axis=0)
    lb = c[layer] - c[0]
    return jnp.log(lb), jnp.log1p(-lb)


def kernel(x, mem, norm_mix, w_in, hg_lower_bounds, hg_norm, conv_w, fox_f_bias, fox_q_norm,
           fox_k_norm, w_out, norm_cross, norm_mem, w_cq, w_ck, w_cv, w_co, cross_q_norm,
           cross_k_norm, norm_ffn, router_group_w, router_group_b, router_expert_w,
           router_expert_b, moe_w1, moe_w3, moe_w2):
    B, S, D = x.shape
    M = mem.shape[1]
    L = w_in.shape[0]
    T = B * S
    HGW = 4 * HG_WIDTH
    CONV3 = 3 * CONV_WIDTH
    PB = CONV3 + 3 * FOX_WIDTH
    FQ0, FK0, FV0 = 0, FOX_WIDTH, 2 * FOX_WIDTH
    CV0 = 3 * FOX_WIDTH
    dhc = D // CROSS_HEADS
    x2 = x.reshape(T, D)
    mem2 = mem.reshape(B * M, D)
    w_in_t = jnp.swapaxes(w_in, 1, 2)
    hn = rmsnorm(x2, norm_mix[0])
    for l in range(L):
        proj_a = matmul([hn], w_in_t, l, 0, HGW, F32, w_t=True)
        proj_b = matmul([hn], w_in_t, l, HGW, PB, BF16, w_t=True, rotate=CONV3, head_norms=(
            (FQ0, FK0, fox_q_norm[l], FOX_DIM ** -0.5), (FK0, FV0, fox_k_norm[l], 1.0)))
        loglb, l1mlb = _lower_bound_logs(hg_lower_bounds, l)
        y_hg = hgrn2(proj_a.reshape(B, S, HGW), loglb, l1mlb, hg_norm[l])
        pb3 = proj_b.reshape(B, S, PB)
        y_conv = short_conv(pb3, CV0, conv_w[l])
        c = fox_gate(hn.reshape(B, S, D), w_in_t, l, HGW + PB, fox_f_bias[l])
        y_fox = fox_attention(pb3, FQ0, FK0, FV0, c)
        mix = [y_hg.reshape(T, HG_WIDTH), y_conv.reshape(T, CONV_WIDTH), y_fox.reshape(T, FOX_WIDTH)]
        x2, qcn = out_proj_cross_q(mix, x2, w_out, w_cq, l, norm_cross[l], cross_q_norm[l])
        memn = rmsnorm(mem2, norm_mem[l])
        kcn = matmul([memn], w_ck, l, 0, D, BF16, head_norms=((0, D, cross_k_norm[l], 1.0),))
        vc = matmul([memn], w_cv, l, 0, D, BF16)
        wr = jnp.concatenate([router_group_w[l], router_expert_w[l]], axis=1)
        wr = jnp.pad(wr, ((0, 0), (0, LANES - wr.shape[1])))
        br = jnp.concatenate([router_group_b[l], router_expert_b[l]])
        br = jnp.pad(br, (0, LANES - br.shape[0]))
        x3, hp, logits = cross_attention_ffn_in(
            qcn.reshape(B, S, D), kcn.reshape(B, M, D), vc.reshape(B, M, D), x2.reshape(B, S, D),
            w_co, l, norm_ffn[l], wr, br)
        x2 = x3.reshape(T, D)
        meta, cnt = route(logits)
        d = plan(meta, cnt)
        tables, R = _tile_tables(cnt, T)
        src = invert(d[0], d[1], R)
        ys = moe_experts(hp, moe_w1, moe_w3, moe_w2, l, tables, src)
        if l + 1 < L:
            x2, hn = moe_combine(x2, meta, ys, d[0], d[1], norm_mix[l + 1])
        else:
            x2 = moe_combine(x2, meta, ys, d[0], d[1])
    return x2.reshape(B, S, D)
```

```python
import functools

import numpy as np
import jax
import jax.numpy as jnp
from jax import lax
from jax.experimental import pallas as pl
from jax.experimental.pallas import tpu as pltpu

F32 = jnp.float32
BF16 = jnp.bfloat16
I32 = jnp.int32
EPS = 1e-6

HG_HEADS = 4
HG_DIM = 128
HG_WIDTH = HG_HEADS * HG_DIM
CONV_WIDTH = 512
FOX_HEADS = 8
FOX_DIM = 128
FOX_WIDTH = FOX_HEADS * FOX_DIM
CROSS_HEADS = 4
MOE_GROUPS = 4
MOE_EXPERTS = 8
N_EXPERTS = MOE_GROUPS * MOE_EXPERTS
ROUTER_LANE0 = MOE_GROUPS

LANES = 128
SUBLANES = 8
VMEM_LIMIT = 56 * 1024 * 1024

NORM_ROWS = 512
MM_TM = 2048
MM_TN = 512
HG_CHUNK = 128
FOX_TQ = 512
FOX_TK = 512
FOX_HEADS_PER_STEP = 8
CROSS_TS = 512
CROSS_SLAB = 512
ROUTE_ROWS = 1024
MOE_ROWS_LOG2 = 8
MOE_ROWS = 1 << MOE_ROWS_LOG2
COMB_ROWS = 256


def _params(*sem):
    return pltpu.CompilerParams(dimension_semantics=sem, vmem_limit_bytes=VMEM_LIMIT)


def _sigmoid(x):
    return 1.0 / (1.0 + jnp.exp(-x))


def _log_sigmoid(x):
    return jnp.minimum(x, 0.0) - jnp.log1p(jnp.exp(-jnp.abs(x)))


TOKEN_ROWS = 16
TOKEN_LANES = 128


def _store_token_tiles(ref, y, rows):
    for j in range(TOKEN_ROWS):
        ref[pl.ds(j, rows, stride=TOKEN_ROWS), :] = y[:, j * TOKEN_LANES:(j + 1) * TOKEN_LANES]


def _load_token_tiles(ref, base, rows):
    return [ref[pl.ds(base + j, rows, stride=TOKEN_ROWS), :] for j in range(TOKEN_ROWS)]


def _rms(x):
    return x * lax.rsqrt(jnp.mean(x * x, axis=-1, keepdims=True) + EPS)


def _rmsnorm_kernel(x_ref, g_ref, o_ref):
    o_ref[...] = (_rms(x_ref[...]) * g_ref[...]).astype(o_ref.dtype)


def rmsnorm(x, g, out_dtype=BF16):
    R, D = x.shape
    return pl.pallas_call(
        _rmsnorm_kernel,
        grid=(R // NORM_ROWS,),
        in_specs=[pl.BlockSpec((NORM_ROWS, D), lambda i: (i, 0)),
                  pl.BlockSpec((1, D), lambda i: (0, 0))],
        out_specs=pl.BlockSpec((NORM_ROWS, D), lambda i: (i, 0)),
        out_shape=jax.ShapeDtypeStruct((R, D), out_dtype),
        compiler_params=_params("parallel"),
        name="rmsnorm",
    )(x, g.reshape(1, D))


def _head_rmsnorm(x, g, dh, scale):
    heads = [_rms(x[:, h * dh:(h + 1) * dh]) * g * scale for h in range(x.shape[1] // dh)]
    return heads[0] if len(heads) == 1 else jnp.concatenate(heads, axis=1)


def _matmul_kernel(*refs, n_parts, w_t, norm_tiles, dh):
    a_refs = refs[:n_parts]
    w_ref = refs[n_parts]
    if norm_tiles:
        g_ref, o_ref, wb_ref = refs[n_parts + 1:]
    else:
        o_ref, wb_ref = refs[n_parts + 1:]
    j = pl.program_id(0)

    @pl.when(pl.program_id(1) == 0)
    def _():
        wb_ref[...] = w_ref[...].astype(BF16)

    if n_parts == 1:
        a = a_refs[0][...]
    else:
        a = jnp.concatenate([r[...] for r in a_refs], axis=1)
    if w_t:
        acc = lax.dot_general(a, wb_ref[...], (((1,), (1,)), ((), ())), preferred_element_type=F32)
    else:
        acc = jnp.dot(a, wb_ref[...], preferred_element_type=F32)
    if not norm_tiles:
        o_ref[...] = acc.astype(o_ref.dtype)
        return
    plain = True
    for k, (j0, j1, scale) in enumerate(norm_tiles):
        hit = (j >= j0) & (j < j1)
        plain = plain & jnp.logical_not(hit)

        @pl.when(hit)
        def _(k=k, scale=scale):
            o_ref[...] = _head_rmsnorm(acc, g_ref[k:k + 1, :], dh, scale).astype(o_ref.dtype)

    @pl.when(plain)
    def _():
        o_ref[...] = acc.astype(o_ref.dtype)


def matmul(a_parts, w, layer, col0, n, out_dtype, w_t=False, head_norms=(), rotate=0,
           tm=MM_TM, tn=MM_TN):
    M = a_parts[0].shape[0]
    K = w.shape[2] if w_t else w.shape[1]
    assert sum(p.shape[1] for p in a_parts) == K
    tm = min(tm, M)
    assert col0 % tn == 0 and n % tn == 0 and M % tm == 0 and rotate % tn == 0
    cb, nt, rot = col0 // tn, n // tn, rotate // tn
    in_specs = [pl.BlockSpec((tm, p.shape[1]), lambda j, i: (i, 0)) for p in a_parts]
    if w_t:
        in_specs.append(pl.BlockSpec((None, tn, K), lambda j, i: (layer, cb + (j + rot) % nt, 0)))
    else:
        in_specs.append(pl.BlockSpec((None, K, tn), lambda j, i: (layer, 0, cb + (j + rot) % nt)))
    args = list(a_parts) + [w]
    norm_tiles, dh = (), 0
    if head_norms:
        dh = head_norms[0][2].shape[0]
        assert all(c0 % tn == 0 and c1 % tn == 0 and g.shape[0] == dh and tn % dh == 0
                   for c0, c1, g, _ in head_norms)
        norm_tiles = tuple((c0 // tn, c1 // tn, s) for c0, c1, _, s in head_norms)
        gains = jnp.stack([g for _, _, g, _ in head_norms])
        in_specs.append(pl.BlockSpec(gains.shape, lambda j, i: (0, 0)))
        args.append(gains)
    return pl.pallas_call(
        functools.partial(_matmul_kernel, n_parts=len(a_parts), w_t=w_t, norm_tiles=norm_tiles,
                          dh=dh),
        grid=(n // tn, M // tm),
        in_specs=in_specs,
        out_specs=pl.BlockSpec((tm, tn), lambda j, i: (i, j)),
        out_shape=jax.ShapeDtypeStruct((M, n), out_dtype),
        scratch_shapes=[pltpu.VMEM((tn, K) if w_t else (K, tn), BF16)],
        compiler_params=_params("arbitrary", "arbitrary"),
        name="matmul",
    )(*args)


W_CHUNK = 256


def _load_weight(w_hbm, layer, wb, stg, sem):
    n_chunks = wb.shape[0] // W_CHUNK

    def copy(c):
        return pltpu.make_async_copy(w_hbm.at[layer, pl.ds(c * W_CHUNK, W_CHUNK)], stg.at[c % 2],
                                     sem.at[c % 2])
    copy(0).start()
    for c in range(n_chunks):
        if c + 1 < n_chunks:
            copy(c + 1).start()
        copy(c).wait()
        wb[c * W_CHUNK:(c + 1) * W_CHUNK, :] = stg[c % 2].astype(BF16)


def _outq_kernel(a1_ref, a2_ref, a3_ref, x_ref, wo_hbm, wq_hbm, gn_ref, gq_ref, xo_ref, q_ref,
                 wo_b, wq_b, stg, sem, *, layer, dh, scale):
    @pl.when(pl.program_id(0) == 0)
    def _():
        _load_weight(wo_hbm, layer, wo_b, stg, sem)
        _load_weight(wq_hbm, layer, wq_b, stg, sem)

    a = jnp.concatenate([a1_ref[...], a2_ref[...], a3_ref[...]], axis=1)
    xn = x_ref[...] + jnp.dot(a, wo_b[...], preferred_element_type=F32)
    xo_ref[...] = xn
    hc = (_rms(xn) * gn_ref[...]).astype(BF16)
    q = jnp.dot(hc, wq_b[...], preferred_element_type=F32)
    q_ref[...] = _head_rmsnorm(q, gq_ref[...], dh, scale).astype(q_ref.dtype)


def out_proj_cross_q(mix, x, w_out, w_cq, layer, g_cross, g_q, tm=512):
    T, D = x.shape
    dh = g_q.shape[0]
    row = lambda w: pl.BlockSpec((tm, w), lambda i: (i, 0))
    hbm = pl.BlockSpec(memory_space=pl.ANY)
    return pl.pallas_call(
        functools.partial(_outq_kernel, layer=layer, dh=dh, scale=dh ** -0.5),
        grid=(T // tm,),
        in_specs=[row(mix[0].shape[1]), row(mix[1].shape[1]), row(mix[2].shape[1]), row(D), hbm, hbm,
                  pl.BlockSpec((1, D), lambda i: (0, 0)), pl.BlockSpec((1, dh), lambda i: (0, 0))],
        out_specs=[row(D), row(D)],
        out_shape=[jax.ShapeDtypeStruct((T, D), F32), jax.ShapeDtypeStruct((T, D), BF16)],
        scratch_shapes=[pltpu.VMEM((D, D), BF16), pltpu.VMEM((D, D), BF16),
                        pltpu.VMEM((2, W_CHUNK, D), F32), pltpu.SemaphoreType.DMA((2,))],
        compiler_params=_params("arbitrary"),
        name="out_proj_cross_q",
    )(*mix, x, w_out, w_cq, g_cross.reshape(1, D), g_q.reshape(1, dh))


def _hgrn_consts(C):
    nlev = int(np.log2(C))
    t = np.arange(C)[:, None]
    u = np.arange(C)[None, :]
    tri = (u <= t).astype(np.float32)
    mall2 = np.concatenate([tri, tri], axis=1)
    tt = np.arange(C)[:, None]
    ss = np.arange(C)[None, :]
    lev = np.full((C, C), nlev + 1, np.int32)
    x = tt ^ ss
    hb = np.zeros_like(x)
    for j in range(nlev):
        hb = np.where((x >> j) & 1, j, hb)
    lev = np.where(tt > ss, hb, lev)
    lev = np.where(tt == ss, nlev, lev)
    return jnp.asarray(mall2, BF16), jnp.asarray(lev, jnp.int32)


def _level_sums(lf, b, row, j, C):
    m = 1 << j
    second = (row & m) != 0
    if j == 0:
        return jnp.where(second, lf, 0.0)
    if j == 1:
        prev = pltpu.roll(lf, 1, axis=0)
        nxt = pltpu.roll(lf, C - 1, axis=0)
        first = jnp.where((row & 1) == 0, nxt, 0.0)
        return jnp.where(second, jnp.where((row & 1) != 0, lf + prev, lf), first)
    g = 2 * m
    d = b.shape[1]
    bsel = jnp.broadcast_to(b.reshape(C // g, g, d)[:, m - 1:m, :], (C // g, g, d)).reshape(C, d)
    return jnp.where(second, b - bsel, bsel - b)


def _hgrn_kernel(x_ref, mall_ref, lev_ref, loglb_ref, l1mlb_ref, ng_ref, o_ref, st_ref, *, C, nlev):
    @pl.when(pl.program_id(1) == 0)
    def _():
        st_ref[...] = jnp.zeros_like(st_ref)

    W = HG_WIDTH
    d = HG_DIM
    lev = lev_ref[...]
    mall = mall_ref[...]
    row = lax.broadcasted_iota(jnp.int32, (C, d), 0)
    nt = (((1,), (1,)), ((), ()))

    def side(xs):
        return jnp.concatenate(xs, axis=1)

    def block_diag(xa, xb):
        return jnp.concatenate([side([xa, jnp.zeros_like(xb)]), side([jnp.zeros_like(xa), xb])],
                               axis=0)

    for h0 in range(0, HG_HEADS, 2):
        q, lf, kk, vb, g = [], [], [], [], []
        for h in (h0, h0 + 1):
            sl = slice(h * d, (h + 1) * d)
            q.append(x_ref[0, :, h * d:(h + 1) * d])
            z = x_ref[0, :, W + h * d:W + (h + 1) * d]
            vb.append(x_ref[0, :, 2 * W + h * d:2 * W + (h + 1) * d].astype(BF16))
            g.append(x_ref[0, :, 3 * W + h * d:3 * W + (h + 1) * d])
            a = loglb_ref[:, sl]
            bt = l1mlb_ref[:, sl] + _log_sigmoid(z)
            lf.append(jnp.maximum(a, bt) + jnp.log1p(jnp.exp(-jnp.abs(a - bt))))
            kk.append(1.0 - jnp.exp(lf[-1]))
        hi = [x.astype(BF16) for x in lf]
        lo = [(x - y.astype(F32)).astype(BF16) for x, y in zip(lf, hi)]
        b2 = jnp.dot(mall, jnp.concatenate([side(hi), side(lo)], axis=0),
                     preferred_element_type=F32)
        b = [b2[:, :d], b2[:, d:]]
        st = [st_ref[h0], st_ref[h0 + 1]]
        o2 = lax.dot_general(side([(q[i] * jnp.exp(b[i])).astype(BF16) for i in (0, 1)]),
                             block_diag(st[0].astype(BF16), st[1].astype(BF16)), nt,
                             preferred_element_type=F32)
        att = [jnp.where(lev == nlev, jnp.sum(q[i] * kk[i], axis=-1, keepdims=True), 0.0)
               for i in (0, 1)]
        for j in range(nlev):
            second = (row & (1 << j)) != 0
            xq, xk = [], []
            for i in (0, 1):
                e = jnp.exp(_level_sums(lf[i], b[i], row, j, C))
                xq.append(jnp.where(second, q[i] * e, 0.0).astype(BF16))
                xk.append(jnp.where(second, 0.0, kk[i] * e).astype(BF16))
            am = lax.dot_general(side(xq), block_diag(xk[0], xk[1]), nt,
                                 preferred_element_type=F32)
            att = [jnp.where(lev == j, am[:, i * C:(i + 1) * C], att[i]) for i in (0, 1)]
        o2 = o2 + jnp.dot(side([x.astype(BF16) for x in att]), block_diag(vb[0], vb[1]),
                          preferred_element_type=F32)
        for i, h in enumerate((h0, h0 + 1)):
            bl = b[i][C - 1:C, :]
            kh = (kk[i] * jnp.exp(bl - b[i])).astype(BF16)
            st_ref[h] = st[i] * jnp.exp(bl) + lax.dot_general(
                vb[i], kh, (((0,), (0,)), ((), ())), preferred_element_type=F32)
            o = o2[:, i * d:(i + 1) * d]
            y = o * lax.rsqrt(jnp.mean(o * o, axis=-1, keepdims=True) + EPS) * ng_ref[...]
            o_ref[0, :, h * d:(h + 1) * d] = (y * (g[i] * _sigmoid(g[i]))).astype(o_ref.dtype)


def hgrn2(proj_a, loglb, l1mlb, norm_g):
    B, S, _ = proj_a.shape
    C = HG_CHUNK
    nlev = int(np.log2(C))
    mall, lev = _hgrn_consts(C)
    return pl.pallas_call(
        functools.partial(_hgrn_kernel, C=C, nlev=nlev),
        grid=(B, S // C),
        in_specs=[pl.BlockSpec((1, C, 4 * HG_WIDTH), lambda b, c: (b, c, 0)),
                  pl.BlockSpec(mall.shape, lambda b, c: (0, 0)),
                  pl.BlockSpec((C, C), lambda b, c: (0, 0)),
                  pl.BlockSpec((1, HG_WIDTH), lambda b, c: (0, 0)),
                  pl.BlockSpec((1, HG_WIDTH), lambda b, c: (0, 0)),
                  pl.BlockSpec((1, HG_DIM), lambda b, c: (0, 0))],
        out_specs=pl.BlockSpec((1, C, HG_WIDTH), lambda b, c: (b, c, 0)),
        out_shape=jax.ShapeDtypeStruct((B, S, HG_WIDTH), BF16),
        scratch_shapes=[pltpu.VMEM((HG_HEADS, HG_DIM, HG_DIM), F32)],
        compiler_params=_params("parallel", "arbitrary"),
        name="hgrn2",
    )(proj_a, mall, lev, loglb.reshape(1, HG_WIDTH), l1mlb.reshape(1, HG_WIDTH),
      norm_g.reshape(1, HG_DIM))


def _conv_kernel(b_ref, c_ref, h_ref, w_ref, o_ref):
    u = c_ref[0].astype(F32) * h_ref[0].astype(F32)
    row = lax.broadcasted_iota(jnp.int32, u.shape, 0)
    u1 = jnp.where(row >= 1, pltpu.roll(u, 1, axis=0), 0.0)
    u2 = jnp.where(row >= 2, pltpu.roll(u, 2, axis=0), 0.0)
    y = w_ref[0:1, :] * u2 + w_ref[1:2, :] * u1 + w_ref[2:3, :] * u
    o_ref[0] = (b_ref[0].astype(F32) * y).astype(o_ref.dtype)


def short_conv(proj_b, col0, w):
    B, S, _ = proj_b.shape
    cw = CONV_WIDTH
    assert col0 % cw == 0
    spec = lambda k: pl.BlockSpec((1, S, cw), lambda b: (b, 0, col0 // cw + k))
    return pl.pallas_call(
        _conv_kernel,
        grid=(B,),
        in_specs=[spec(0), spec(1), spec(2), pl.BlockSpec((3, cw), lambda b: (0, 0))],
        out_specs=pl.BlockSpec((1, S, cw), lambda b: (b, 0, 0)),
        out_shape=jax.ShapeDtypeStruct((B, S, cw), BF16),
        compiler_params=_params("parallel"),
        name="short_conv",
    )(proj_b, proj_b, proj_b, w)


def _fox_gate_kernel(hn_ref, wf_ref, bias_ref, c_ref):
    gl = lax.dot_general(wf_ref[...].astype(BF16), hn_ref[0], (((1,), (1,)), ((), ())),
                         preferred_element_type=F32)
    c = _log_sigmoid(gl + bias_ref[...])
    S = c.shape[1]
    lane = lax.broadcasted_iota(jnp.int32, c.shape, 1)
    sh = 1
    while sh < S:
        c = c + jnp.where(lane >= sh, pltpu.roll(c, sh, axis=1), 0.0)
        sh *= 2
    c_ref[0] = c


def fox_gate(hn, w_t, layer, row0, bias):
    B, S, D = hn.shape
    H = bias.shape[0]
    assert row0 % H == 0
    return pl.pallas_call(
        _fox_gate_kernel,
        grid=(B,),
        in_specs=[pl.BlockSpec((1, S, D), lambda b: (b, 0, 0)),
                  pl.BlockSpec((None, H, D), lambda b: (layer, row0 // H, 0)),
                  pl.BlockSpec((H, 1), lambda b: (0, 0))],
        out_specs=pl.BlockSpec((1, H, S), lambda b: (b, 0, 0)),
        out_shape=jax.ShapeDtypeStruct((B, H, S), F32),
        compiler_params=_params("parallel"),
        name="fox_gate",
    )(hn, w_t, bias.reshape(H, 1))


def _fox_kernel(q_ref, k_ref, v_ref, c_ref, o_ref, *, tq, tk, nh):
    qi = pl.program_id(2)
    nt = (((1,), (1,)), ((), ()))
    ones = jnp.ones((tk, FOX_DIM), BF16)
    heads = [slice(h * FOX_DIM, (h + 1) * FOX_DIM) for h in range(nh)]
    qs = [q_ref[0, :, sl] for sl in heads]

    def step(ki, carry, diag):
        out = []
        for h, sl in enumerate(heads):
            m, acc = carry[h]
            k = k_ref[0, pl.ds(ki * tk, tk), sl]
            v1 = jnp.concatenate([v_ref[0, pl.ds(ki * tk, tk), sl], ones], axis=1)
            ck = c_ref[0, h, pl.ds(ki, 1), :]
            s = lax.dot_general(qs[h], k, nt, preferred_element_type=F32) - ck
            if diag is not None:
                r = lax.broadcasted_iota(jnp.int32, s.shape, 0)
                c = lax.broadcasted_iota(jnp.int32, s.shape, 1)
                s = jnp.where(c + diag * tk <= r, s, -jnp.inf)
            m_new = jnp.maximum(m, jnp.max(s, axis=-1, keepdims=True))
            alpha = jnp.exp(m - m_new)
            p = jnp.exp((s - m_new).astype(BF16))
            out.append((m_new, alpha * acc + jnp.dot(p, v1, preferred_element_type=F32)))
        return tuple(out)

    init = tuple((jnp.full((tq, 1), -jnp.inf, F32), jnp.zeros((tq, 2 * FOX_DIM), F32))
                 for _ in heads)
    n_full = qi * (tq // tk)
    carry = lax.fori_loop(0, n_full, lambda ki, cr: step(ki, cr, None), init)
    for d in range(tq // tk):
        carry = step(n_full + d, carry, d)
    for h, sl in enumerate(heads):
        acc = carry[h][1]
        o_ref[0, :, sl] = (acc[:, :FOX_DIM] / acc[:, FOX_DIM:]).astype(o_ref.dtype)


def fox_attention(proj_b, q_col0, k_col0, v_col0, c):
    B, S, _ = proj_b.shape
    tq, tk, nh = FOX_TQ, FOX_TK, FOX_HEADS_PER_STEP
    H = FOX_HEADS
    w = nh * FOX_DIM
    assert tq % tk == 0 and H % nh == 0 and all(c0 % w == 0 for c0 in (q_col0, k_col0, v_col0))
    qb, kb, vb = q_col0 // w, k_col0 // w, v_col0 // w
    c4 = c.reshape(B, H, S // tk, tk)
    return pl.pallas_call(
        functools.partial(_fox_kernel, tq=tq, tk=tk, nh=nh),
        grid=(B, H // nh, S // tq),
        in_specs=[pl.BlockSpec((1, tq, w), lambda b, h, i: (b, i, qb + h)),
                  pl.BlockSpec((1, S, w), lambda b, h, i: (b, 0, kb + h)),
                  pl.BlockSpec((1, S, w), lambda b, h, i: (b, 0, vb + h)),
                  pl.BlockSpec((1, nh, S // tk, tk), lambda b, h, i: (b, h, 0, 0))],
        out_specs=pl.BlockSpec((1, tq, w), lambda b, h, i: (b, i, h)),
        out_shape=jax.ShapeDtypeStruct((B, S, FOX_WIDTH), BF16),
        compiler_params=_params("parallel", "parallel", "arbitrary"),
        name="fox_attention",
    )(proj_b, proj_b, proj_b, c4)


def _cross_kernel(q_ref, k_ref, v_ref, x_ref, wo_hbm, gn_ref, wr_ref, br_ref,
                  xo_ref, hp_ref, lg_ref, wo_b, stg, sem, *, layer, dh):
    @pl.when((pl.program_id(0) == 0) & (pl.program_id(1) == 0))
    def _():
        _load_weight(wo_hbm, layer, wo_b, stg, sem)

    nt = (((1,), (1,)), ((), ()))
    heads = []
    for h in range(CROSS_HEADS):
        sl = slice(h * dh, (h + 1) * dh)
        s = lax.dot_general(q_ref[0, :, sl], k_ref[0, :, sl], nt, preferred_element_type=F32)
        m = jnp.max(s, axis=-1, keepdims=True)
        p = jnp.exp(s - m)
        l = jnp.sum(p, axis=-1, keepdims=True)
        o = jnp.dot(p.astype(BF16), v_ref[0, :, sl], preferred_element_type=F32)
        heads.append((o / l).astype(BF16))
    oc = jnp.concatenate(heads, axis=1)
    D = oc.shape[1]
    ssq = jnp.zeros((oc.shape[0], 1), F32)
    for c in range(0, D, CROSS_SLAB):
        cs = slice(c, c + CROSS_SLAB)
        xc = x_ref[0, :, cs] + jnp.dot(oc, wo_b[:, cs], preferred_element_type=F32)
        xo_ref[0, :, cs] = xc
        ssq = ssq + jnp.sum(xc * xc, axis=-1, keepdims=True)
    y = xo_ref[0] * lax.rsqrt(ssq / D + EPS) * gn_ref[...]
    _store_token_tiles(hp_ref, y, y.shape[0])
    yh = y.astype(BF16)
    yl = (y - yh.astype(F32)).astype(BF16)
    wr = wr_ref[...]
    wh = wr.astype(BF16)
    wl = (wr - wh.astype(F32)).astype(BF16)
    lg_ref[...] = (jnp.dot(yh, wh, preferred_element_type=F32)
                   + jnp.dot(yl, wh, preferred_element_type=F32)
                   + jnp.dot(yh, wl, preferred_element_type=F32)) + br_ref[...]


def cross_attention_ffn_in(qn, kn, v, x, w_co, layer, g_ffn, wr, br):
    B, S, D = qn.shape
    M = kn.shape[1]
    ts = CROSS_TS
    ns = S // ts
    assert D == TOKEN_ROWS * TOKEN_LANES
    tile = pl.BlockSpec((1, ts, D), lambda b, i: (b, i, 0))
    memb = pl.BlockSpec((1, M, D), lambda b, i: (b, 0, 0))
    const = lambda shape: pl.BlockSpec(shape, lambda b, i: (0, 0))
    return pl.pallas_call(
        functools.partial(_cross_kernel, layer=layer, dh=D // CROSS_HEADS),
        grid=(B, ns),
        in_specs=[tile, memb, memb, tile, pl.BlockSpec(memory_space=pl.ANY),
                  const((1, D)), const((D, LANES)), const((1, LANES))],
        out_specs=[tile,
                   pl.BlockSpec((ts * TOKEN_ROWS, TOKEN_LANES), lambda b, i: (b * ns + i, 0)),
                   pl.BlockSpec((ts, LANES), lambda b, i: (b * ns + i, 0))],
        out_shape=[jax.ShapeDtypeStruct((B, S, D), F32),
                   jax.ShapeDtypeStruct((B * S * TOKEN_ROWS, TOKEN_LANES), F32),
                   jax.ShapeDtypeStruct((B * S, LANES), F32)],
        scratch_shapes=[pltpu.VMEM((D, D), BF16), pltpu.VMEM((2, W_CHUNK, D), F32),
                        pltpu.SemaphoreType.DMA((2,))],
        compiler_params=_params("arbitrary", "arbitrary"),
        name="cross_attention_ffn_in",
    )(qn, kn, v, x, w_co, g_ffn.reshape(1, D), wr, br.reshape(1, LANES))


def _route_kernel(lg_ref, tri_ref, meta_ref, cnt_ref, carry_ref):
    @pl.when(pl.program_id(0) == 0)
    def _():
        carry_ref[...] = jnp.zeros_like(carry_ref)

    x = lg_ref[...]
    lane = lax.broadcasted_iota(jnp.int32, x.shape, 1)
    ninf = -jnp.inf
    is_g = lane < MOE_GROUPS
    gl = jnp.where(is_g, x, ninf)
    gmax = jnp.max(gl, axis=-1, keepdims=True)
    gidx = jnp.min(jnp.where(gl == gmax, lane, LANES), axis=-1, keepdims=True)
    gw = 1.0 / jnp.sum(jnp.where(is_g, jnp.exp(x - gmax), 0.0), axis=-1, keepdims=True)
    e_lane = lane - ROUTER_LANE0
    in_grp = (e_lane >= 0) & (e_lane < N_EXPERTS) & ((e_lane >> 3) == gidx)
    el = jnp.where(in_grp, x, ninf)
    m1 = jnp.max(el, axis=-1, keepdims=True)
    i1 = jnp.min(jnp.where(el == m1, lane, LANES), axis=-1, keepdims=True)
    el2 = jnp.where(lane == i1, ninf, el)
    m2 = jnp.max(el2, axis=-1, keepdims=True)
    i2 = jnp.min(jnp.where(el2 == m2, lane, LANES), axis=-1, keepdims=True)
    r = jnp.exp(m2 - m1)
    p1 = 1.0 / (1.0 + r)
    p2 = r * p1
    oh = ((lane == i1) | (lane == i2)).astype(F32)
    prefix = jnp.dot(tri_ref[...], oh.astype(BF16), preferred_element_type=F32) + carry_ref[...]
    rank1 = jnp.sum(jnp.where(lane == i1, prefix, 0.0), axis=-1, keepdims=True)
    rank2 = jnp.sum(jnp.where(lane == i2, prefix, 0.0), axis=-1, keepdims=True)
    carry_ref[...] += jnp.sum(oh, axis=0, keepdims=True)
    cnt_ref[...] = carry_ref[...]
    cols = [i1.astype(F32), i2.astype(F32), gw * p1, gw * p2, rank1, rank2]
    meta = jnp.zeros(x.shape, F32)
    for k, col in enumerate(cols):
        meta = jnp.where(lane == k, col, meta)
    meta_ref[...] = meta


def route(logits):
    T = logits.shape[0]
    tt = ROUTE_ROWS
    tri = jnp.asarray(np.tril(np.ones((tt, tt), np.float32), -1), BF16)
    return pl.pallas_call(
        _route_kernel,
        grid=(T // tt,),
        in_specs=[pl.BlockSpec((tt, LANES), lambda i: (i, 0)),
                  pl.BlockSpec((tt, tt), lambda i: (0, 0))],
        out_specs=[pl.BlockSpec((tt, LANES), lambda i: (i, 0)),
                   pl.BlockSpec((1, LANES), lambda i: (0, 0))],
        out_shape=[jax.ShapeDtypeStruct((T, LANES), F32),
                   jax.ShapeDtypeStruct((1, LANES), F32)],
        scratch_shapes=[pltpu.VMEM((1, LANES), F32)],
        compiler_params=_params("arbitrary"),
        name="moe_route",
    )(logits, tri)


def _plan_kernel(meta_ref, cnt_ref, d_ref):
    cnt = jnp.broadcast_to(cnt_ref[...], (SUBLANES, LANES)).astype(I32)
    pad = (((cnt + (MOE_ROWS - 1)) >> MOE_ROWS_LOG2) << MOE_ROWS_LOG2).astype(F32)
    lane8 = lax.broadcasted_iota(I32, pad.shape, 1)
    incl = pad
    sh = 1
    while sh < LANES:
        incl = incl + jnp.where(lane8 >= sh, pltpu.roll(incl, sh, axis=1), 0.0)
        sh *= 2
    off = (incl - pad)[0:1, :]
    m = meta_ref[...]
    lane = lax.broadcasted_iota(I32, m.shape, 1)
    i1 = m[:, 0:1].astype(I32)
    i2 = m[:, 1:2].astype(I32)
    d1 = jnp.sum(jnp.where(lane == i1, off, 0.0), axis=-1, keepdims=True) + m[:, 4:5]
    d2 = jnp.sum(jnp.where(lane == i2, off, 0.0), axis=-1, keepdims=True) + m[:, 5:6]
    dm = jnp.where(lane == 0, d1, jnp.where(lane == 1, d2, 0.0))
    d_ref[...] = jnp.transpose(dm)[0:2, :].astype(I32)


def plan(meta, cnt):
    T = meta.shape[0]
    tt = ROUTE_ROWS
    return pl.pallas_call(
        _plan_kernel,
        grid=(T // tt,),
        in_specs=[pl.BlockSpec((tt, LANES), lambda i: (i, 0)),
                  pl.BlockSpec((1, LANES), lambda i: (0, 0))],
        out_specs=pl.BlockSpec((2, tt), lambda i: (0, i)),
        out_shape=jax.ShapeDtypeStruct((2, T), I32),
        compiler_params=_params("parallel"),
        name="moe_plan",
    )(meta, cnt)


def _invert_kernel(d1_ref, d2_ref, src_ref, zeros, sem, *, T):
    zeros[...] = jnp.zeros_like(zeros)
    fill = pltpu.make_async_copy(zeros, src_ref, sem)
    fill.start()
    fill.wait()

    def body(t, _):
        src_ref[d1_ref[t]] = t
        src_ref[d2_ref[t]] = t
        return 0
    lax.fori_loop(0, T, body, 0, unroll=8)


def invert(d1, d2, R):
    T = d1.shape[0]
    grid_spec = pltpu.PrefetchScalarGridSpec(
        num_scalar_prefetch=2, grid=(1,), in_specs=[],
        out_specs=pl.BlockSpec(memory_space=pltpu.SMEM),
        scratch_shapes=[pltpu.VMEM((R,), I32), pltpu.SemaphoreType.DMA(())])
    return pl.pallas_call(
        functools.partial(_invert_kernel, T=T),
        grid_spec=grid_spec,
        out_shape=jax.ShapeDtypeStruct((R,), I32),
        compiler_params=_params("arbitrary"),
        name="moe_invert",
    )(d1, d2)


GATHER_CHUNK = 8


def _tile_copy(src_hbm, row, dst, slot, r, sem):
    return pltpu.make_async_copy(src_hbm.at[pl.ds(row * TOKEN_ROWS, TOKEN_ROWS)],
                                 dst.at[slot, pl.ds(r * TOKEN_ROWS, TOKEN_ROWS)], sem.at[slot])


def _expert_kernel(te_ref, first_ref, nxt_ref, wsl_ref, nch_ref, src_ref,
                   hp_hbm, w1_hbm, w3_hbm, w2_hbm, o_ref,
                   xbuf, r1, r3, r2, w1b, w3b, w2b, gsem, wsem, *, tr, base):
    i = pl.program_id(0)
    n = pl.num_programs(0)
    slot = i % 2

    def weight_copies(e, ws):
        return [pltpu.make_async_copy(w1_hbm.at[base + e], r1.at[ws], wsem.at[ws, 0]),
                pltpu.make_async_copy(w3_hbm.at[base + e], r3.at[ws], wsem.at[ws, 1]),
                pltpu.make_async_copy(w2_hbm.at[base + e], r2.at[ws], wsem.at[ws, 2])]

    def issue_rows(tile, slot, c0, c1):
        def body(c, _):
            for u in range(GATHER_CHUNK):
                r = c * GATHER_CHUNK + u
                _tile_copy(hp_hbm, src_ref[tile * tr + r], xbuf, slot, r, gsem).start(priority=u % 2)
            return 0
        lax.fori_loop(c0, c1, body, 0)

    def wait_rows(tile, slot):
        rows = GATHER_CHUNK * TOKEN_ROWS
        def body(c, _):
            pltpu.make_async_copy(hp_hbm.at[pl.ds(0, rows)], xbuf.at[slot, pl.ds(0, rows)],
                                  gsem.at[slot]).wait()
            return 0
        lax.fori_loop(0, nch_ref[tile], body, 0)

    @pl.when(i == 0)
    def _():
        xbuf[...] = jnp.zeros_like(xbuf)
        for cp in weight_copies(te_ref[0], 0):
            cp.start(priority=1)
        issue_rows(0, 0, 0, nch_ref[0])

    wait_rows(i, slot)

    nxt_tile = jnp.minimum(i + 1, n - 1)
    issue_rows(nxt_tile, 1 - slot, 0, jnp.where(i + 1 < n, nch_ref[nxt_tile], 0))

    @pl.when(first_ref[i] == 1)
    def _():
        ws = wsl_ref[i]
        for cp in weight_copies(te_ref[i], ws):
            cp.wait()

        @pl.when(nxt_ref[i] >= 0)
        def _():
            for cp in weight_copies(nxt_ref[i], 1 - ws):
                cp.start(priority=1)

        w1b[...] = r1[ws].astype(BF16)
        w3b[...] = r3[ws].astype(BF16)
        w2b[...] = r2[ws].astype(BF16)

    def expert_mlp(rows):
        x = jnp.concatenate([p.astype(BF16) for p in _load_token_tiles(xbuf.at[slot], 0, rows)],
                            axis=1)
        h1 = jnp.dot(x, w1b[...], preferred_element_type=F32)
        h3 = jnp.dot(x, w3b[...], preferred_element_type=F32)
        hm = (h1 * _sigmoid(h1) * h3).astype(BF16)
        _store_token_tiles(o_ref, jnp.dot(hm, w2b[...], preferred_element_type=F32), rows)

    half_chunks = tr // (2 * GATHER_CHUNK)

    @pl.when(nch_ref[i] > half_chunks)
    def _():
        expert_mlp(tr)

    @pl.when((nch_ref[i] > 0) & (nch_ref[i] <= half_chunks))
    def _():
        expert_mlp(tr // 2)
        o_ref[tr // 2 * TOKEN_ROWS:, :] = jnp.zeros((tr // 2 * TOKEN_ROWS, TOKEN_LANES), F32)

    @pl.when(nch_ref[i] == 0)
    def _():
        o_ref[...] = jnp.zeros_like(o_ref)


def moe_experts(hp, w1, w3, w2, layer, tables, src):
    D, F = w1.shape[-2:]
    tr = MOE_ROWS
    te, first, nxt, wsl, nch = tables
    NT = te.shape[0]
    w1f = w1.reshape(-1, D, F)
    w3f = w3.reshape(-1, D, F)
    w2f = w2.reshape(-1, F, D)
    hbm = pl.BlockSpec(memory_space=pl.ANY)
    grid_spec = pltpu.PrefetchScalarGridSpec(
        num_scalar_prefetch=6,
        grid=(NT,),
        in_specs=[hbm, hbm, hbm, hbm],
        out_specs=pl.BlockSpec((tr * TOKEN_ROWS, TOKEN_LANES), lambda i, *_: (i, 0)),
        scratch_shapes=[pltpu.VMEM((2, tr * TOKEN_ROWS, TOKEN_LANES), F32),
                        pltpu.VMEM((2, D, F), F32),
                        pltpu.VMEM((2, D, F), F32),
                        pltpu.VMEM((2, F, D), F32),
                        pltpu.VMEM((D, F), BF16),
                        pltpu.VMEM((D, F), BF16),
                        pltpu.VMEM((F, D), BF16),
                        pltpu.SemaphoreType.DMA((2,)),
                        pltpu.SemaphoreType.DMA((2, 3))],
    )
    return pl.pallas_call(
        functools.partial(_expert_kernel, tr=tr, base=layer * N_EXPERTS),
        grid_spec=grid_spec,
        out_shape=jax.ShapeDtypeStruct((NT * tr * TOKEN_ROWS, TOKEN_LANES), F32),
        compiler_params=_params("arbitrary"),
        name="moe_experts",
    )(te, first, nxt, wsl, nch, src, hp, w1f, w3f, w2f)


def _combine_kernel(d1_ref, d2_ref, x_ref, meta_ref, ys_hbm, *rest, tt, with_norm):
    if with_norm:
        g_ref, o_ref, hn_ref, buf, sem = rest
    else:
        o_ref, buf, sem = rest
    i = pl.program_id(0)
    n = pl.num_programs(0)
    slot = i % 2

    def issue(tile, slot):
        def body(c, _):
            for u in range(GATHER_CHUNK):
                r = c * GATHER_CHUNK + u
                t = tile * tt + r
                _tile_copy(ys_hbm, d1_ref[t], buf, slot, r, sem).start(priority=0)
                _tile_copy(ys_hbm, d2_ref[t], buf, slot, tt + r, sem).start(priority=1)
            return 0
        lax.fori_loop(0, tt // GATHER_CHUNK, body, 0)

    @pl.when(i == 0)
    def _():
        issue(0, 0)

    pltpu.make_async_copy(ys_hbm.at[pl.ds(0, 2 * tt * TOKEN_ROWS)], buf.at[slot], sem.at[slot]).wait()

    @pl.when(i + 1 < n)
    def _():
        issue(i + 1, 1 - slot)

    w1 = meta_ref[:, 2:3]
    w2 = meta_ref[:, 3:4]
    pa = _load_token_tiles(buf.at[slot], 0, tt)
    pb = _load_token_tiles(buf.at[slot], tt * TOKEN_ROWS, tt)
    cols = [x_ref[:, j * TOKEN_LANES:(j + 1) * TOKEN_LANES] + w1 * pa[j] + w2 * pb[j]
            for j in range(TOKEN_ROWS)]
    xn = jnp.concatenate(cols, axis=1)
    o_ref[...] = xn
    if with_norm:
        hn_ref[...] = (_rms(xn) * g_ref[...]).astype(hn_ref.dtype)


def moe_combine(x, meta, ys, d1, d2, g_next=None):
    T, D = x.shape
    tt = COMB_ROWS
    with_norm = g_next is not None
    row = pl.BlockSpec((tt, D), lambda i, d1, d2: (i, 0))
    in_specs = [row, pl.BlockSpec((tt, LANES), lambda i, d1, d2: (i, 0)),
                pl.BlockSpec(memory_space=pl.ANY)]
    args = [d1, d2, x, meta, ys]
    out_specs, out_shape = row, jax.ShapeDtypeStruct((T, D), F32)
    if with_norm:
        in_specs.append(pl.BlockSpec((1, D), lambda i, d1, d2: (0, 0)))
        args.append(g_next.reshape(1, D))
        out_specs, out_shape = [row, row], [out_shape, jax.ShapeDtypeStruct((T, D), BF16)]
    grid_spec = pltpu.PrefetchScalarGridSpec(
        num_scalar_prefetch=2,
        grid=(T // tt,),
        in_specs=in_specs,
        out_specs=out_specs,
        scratch_shapes=[pltpu.VMEM((2, 2 * tt * TOKEN_ROWS, TOKEN_LANES), F32),
                        pltpu.SemaphoreType.DMA((2,))],
    )
    return pl.pallas_call(
        functools.partial(_combine_kernel, tt=tt, with_norm=with_norm),
        grid_spec=grid_spec,
        out_shape=out_shape,
        compiler_params=_params("arbitrary"),
        name="moe_combine",
    )(*args)


def _tile_tables(cnt, T):
    E = N_EXPERTS
    counts = cnt[0, ROUTER_LANE0:ROUTER_LANE0 + E].astype(I32)
    tiles_e = (counts + MOE_ROWS - 1) // MOE_ROWS
    tile_end = jnp.cumsum(tiles_e)
    tile_start = tile_end - tiles_e
    nact = tile_end[-1]
    NT = (2 * T) // MOE_ROWS + E
    tid = jnp.arange(NT, dtype=I32)
    te = jnp.sum((jnp.minimum(tid, nact - 1)[:, None] >= tile_end[None, :]).astype(I32), axis=1)
    te = jnp.minimum(te, E - 1)
    active = tid < nact
    first = (active & (tid == tile_start[te])).astype(I32)
    eid = jnp.arange(E, dtype=I32)
    later = (eid[None, :] > eid[:, None]) & (tiles_e[None, :] > 0)
    nxt_e = jnp.min(jnp.where(later, eid[None, :], E), axis=1)
    nxt_e = jnp.where(nxt_e == E, -1, nxt_e)
    ordinal = jnp.cumsum((tiles_e > 0).astype(I32)) - 1
    valid = jnp.clip(counts[te] - (tid - tile_start[te]) * MOE_ROWS, 0, MOE_ROWS)
    nch = jnp.where(active, (valid + GATHER_CHUNK - 1) // GATHER_CHUNK, 0)
    return (te, first, nxt_e[te], ordinal[te] % 2, nch), NT * MOE_ROWS


def _lower_bound_logs(lb_param, layer):
    p = jax.nn.softmax(lb_param.astype(F32), axis=0)
    c = jnp.cumsum(p, axis=0)
    lb = c[layer] - c[0]
    return jnp.log(lb), jnp.log1p(-lb)


def kernel(x, mem, norm_mix, w_in, hg_lower_bounds, hg_norm, conv_w, fox_f_bias, fox_q_norm,
           fox_k_norm, w_out, norm_cross, norm_mem, w_cq, w_ck, w_cv, w_co, cross_q_norm,
           cross_k_norm, norm_ffn, router_group_w, router_group_b, router_expert_w,
           router_expert_b, moe_w1, moe_w3, moe_w2):
    B, S, D = x.shape
    M = mem.shape[1]
    L = w_in.shape[0]
    T = B * S
    HGW = 4 * HG_WIDTH
    CONV3 = 3 * CONV_WIDTH
    PB = CONV3 + 3 * FOX_WIDTH
    FQ0, FK0, FV0 = 0, FOX_WIDTH, 2 * FOX_WIDTH
    CV0 = 3 * FOX_WIDTH
    dhc = D // CROSS_HEADS
    x2 = x.reshape(T, D)
    mem2 = mem.reshape(B * M, D)
    w_in_t = jnp.swapaxes(w_in, 1, 2)
    hn = rmsnorm(x2, norm_mix[0])
    for l in range(L):
        proj_a = matmul([hn], w_in_t, l, 0, HGW, F32, w_t=True)
        proj_b = matmul([hn], w_in_t, l, HGW, PB, BF16, w_t=True, rotate=CONV3, head_norms=(
            (FQ0, FK0, fox_q_norm[l], FOX_DIM ** -0.5), (FK0, FV0, fox_k_norm[l], 1.0)))
        loglb, l1mlb = _lower_bound_logs(hg_lower_bounds, l)
        y_hg = hgrn2(proj_a.reshape(B, S, HGW), loglb, l1mlb, hg_norm[l])
        pb3 = proj_b.reshape(B, S, PB)
        y_conv = short_conv(pb3, CV0, conv_w[l])
        c = fox_gate(hn.reshape(B, S, D), w_in_t, l, HGW + PB, fox_f_bias[l])
        y_fox = fox_attention(pb3, FQ0, FK0, FV0, c)
        mix = [y_hg.reshape(T, HG_WIDTH), y_conv.reshape(T, CONV_WIDTH), y_fox.reshape(T, FOX_WIDTH)]
        x2, qcn = out_proj_cross_q(mix, x2, w_out, w_cq, l, norm_cross[l], cross_q_norm[l])
        memn = rmsnorm(mem2, norm_mem[l])
        kcn = matmul([memn], w_ck, l, 0, D, BF16, head_norms=((0, D, cross_k_norm[l], 1.0),))
        vc = matmul([memn], w_cv, l, 0, D, BF16)
        wr = jnp.concatenate([router_group_w[l], router_expert_w[l]], axis=1)
        wr = jnp.pad(wr, ((0, 0), (0, LANES - wr.shape[1])))
        br = jnp.concatenate([router_group_b[l], router_expert_b[l]])
        br = jnp.pad(br, (0, LANES - br.shape[0]))
        x3, hp, logits = cross_attention_ffn_in(
            qcn.reshape(B, S, D), kcn.reshape(B, M, D), vc.reshape(B, M, D), x2.reshape(B, S, D),
            w_co, l, norm_ffn[l], wr, br)
        x2 = x3.reshape(T, D)
        meta, cnt = route(logits)
        d = plan(meta, cnt)
        tables, R = _tile_tables(cnt, T)
        src = invert(d[0], d[1], R)
        ys = moe_experts(hp, moe_w1, moe_w3, moe_w2, l, tables, src)
        if l + 1 < L:
            x2, hn = moe_combine(x2, meta, ys, d[0], d[1], norm_mix[l + 1])
        else:
            x2 = moe_combine(x2, meta, ys, d[0], d[1])
    return x2.reshape(B, S, D)
```
